```python
import jax, jax.numpy as jnp
from jax import lax
import numpy as np

D_MODEL = 2048
BATCH = 2
SEQ = 4096
DEPTH = 4

HEAD_DIM = 64
N_MIX_HEADS = D_MODEL // HEAD_DIM
A_HEADS = N_MIX_HEADS // 4
A_KV_HEADS = A_HEADS // 4
B_HEADS = (N_MIX_HEADS - A_HEADS) // 2
B_KV_HEADS = 2
C_HEADS = N_MIX_HEADS - A_HEADS - B_HEADS
A_WIDTH = A_HEADS * HEAD_DIM
B_WIDTH = B_HEADS * HEAD_DIM
C_WIDTH = C_HEADS * HEAD_DIM
MIX_WIDTH = A_WIDTH + B_WIDTH + C_WIDTH
A_KV_WIDTH = A_KV_HEADS * HEAD_DIM
B_KV_WIDTH = B_KV_HEADS * HEAD_DIM
N_GATES_B = 3
PROJ_WIDTHS = (A_WIDTH, A_KV_WIDTH, A_KV_WIDTH,
               B_WIDTH, B_KV_WIDTH, B_KV_WIDTH, B_KV_WIDTH, B_KV_WIDTH, B_KV_WIDTH, B_KV_WIDTH,
               B_HEADS * N_GATES_B,
               C_WIDTH, C_WIDTH, C_WIDTH, C_HEADS)
N_PROJ = A_WIDTH + 2 * A_KV_WIDTH + B_WIDTH + 6 * B_KV_WIDTH + B_HEADS * N_GATES_B + 3 * C_WIDTH + C_HEADS

Q_BLOCK = 128
WINDOW_A = 128
CMP_LEN = 32
CMP_STRIDE = 16
CMP_HIDDEN = 2 * HEAD_DIM
SEL_BLOCK = 64
N_SELECT = 16
WINDOW_B = 512
D_FF = 7 * D_MODEL // 2
N_EXPERTS = 8
TOP_K = 2
MOE_BLOCK = 128
N_DENSE = (DEPTH + 1) // 2
N_MOE = DEPTH // 2
RMS_EPS = 1e-6
NEG_INF = -1e30
FORCE_BONUS = 1e4
FOX_BIAS_MEAN = 3.0

kernel_name = "hymba_style_swa_nsa_fox_moe_trunk"


def rmsnorm(x, gain):
    xf = x.astype(jnp.float32)
    y = xf * lax.rsqrt(jnp.mean(xf * xf, axis=-1, keepdims=True) + RMS_EPS)
    return (y * gain.astype(jnp.float32)).astype(x.dtype)


def alibi_slopes(n_heads):
    return 2.0 ** (-8.0 * jnp.arange(1, n_heads + 1, dtype=jnp.float32) / n_heads)


def split_heads(t, n_heads):
    return t.reshape(t.shape[0], t.shape[1], n_heads, HEAD_DIM)


def swiglu(h, w_gate, w_up, w_down):
    return (jax.nn.silu(h @ w_gate) * (h @ w_up)) @ w_down


def banded_gqa_attention(q, k, v, window, slopes, sinks=None):
    b, s_len, n_h, dh = q.shape
    n_kv = k.shape[2]
    g = n_h // n_kv
    nb = s_len // Q_BLOCK
    n_kb = window // Q_BLOCK + 1
    qb = q.reshape(b, nb, Q_BLOCK, n_kv, g, dh)
    pad = ((0, 0), ((n_kb - 1) * Q_BLOCK, 0), (0, 0), (0, 0))
    kp = jnp.pad(k, pad).reshape(b, nb + n_kb - 1, Q_BLOCK, n_kv, dh)
    vp = jnp.pad(v, pad).reshape(b, nb + n_kb - 1, Q_BLOCK, n_kv, dh)
    kband = jnp.concatenate([kp[:, i:i + nb] for i in range(n_kb)], axis=2)
    vband = jnp.concatenate([vp[:, i:i + nb] for i in range(n_kb)], axis=2)
    qpos = jnp.arange(s_len).reshape(nb, Q_BLOCK)
    kpos = (jnp.arange(nb)[:, None] - (n_kb - 1)) * Q_BLOCK + jnp.arange(n_kb * Q_BLOCK)[None, :]
    rel = qpos[:, :, None] - kpos[:, None, :]
    mask = (rel >= 0) & (rel < window) & (kpos[:, None, :] >= 0)
    scores = jnp.einsum('bnqhgd,bnkhd->bnhgqk', qb, kband).astype(jnp.float32) * dh ** -0.5
    scores = scores - slopes[None, None, :, :, None, None] * rel.astype(jnp.float32)[None, :, None, None]
    scores = jnp.where(mask[None, :, None, None], scores, NEG_INF)
    if sinks is not None:
        sink_col = jnp.broadcast_to(sinks.astype(jnp.float32)[None, None, :, :, None, None],
                                    scores.shape[:-1] + (1,))
        probs = jax.nn.softmax(jnp.concatenate([scores, sink_col], axis=-1), axis=-1)[..., :-1]
    else:
        probs = jax.nn.softmax(scores, axis=-1)
    out = jnp.einsum('bnhgqk,bnkhd->bnqhgd', probs.astype(v.dtype), vband)
    return out.reshape(b, s_len, n_h, dh)


def compress_tokens(t, pe, w1, w2):
    b, s_len, n_kv, dh = t.shape
    n_cmp = (s_len - CMP_LEN) // CMP_STRIDE + 1
    idx = jnp.arange(n_cmp)[:, None] * CMP_STRIDE + jnp.arange(CMP_LEN)[None, :]
    blocks = t[:, idx] + pe[None, None, :, None, :]
    flat = blocks.transpose(0, 1, 3, 2, 4).reshape(b, n_cmp, n_kv, CMP_LEN * dh)
    return jax.nn.silu(flat @ w1) @ w2


def native_sparse_attention(q, k_cmp, v_cmp, k_slc, v_slc, k_win, v_win, gate_logits,
                            cmp_pe, cmp_w1, cmp_w2, slopes):
    b, s_len, n_h, dh = q.shape
    n_kv = k_cmp.shape[2]
    g = n_h // n_kv
    scale = dh ** -0.5
    f32 = jnp.float32
    qg = q.reshape(b, s_len, n_kv, g, dh)
    t_pos = jnp.arange(s_len)

    kc = compress_tokens(k_cmp, cmp_pe[0], cmp_w1[0], cmp_w2[0])
    vc = compress_tokens(v_cmp, cmp_pe[1], cmp_w1[1], cmp_w2[1])
    n_cmp = kc.shape[1]
    c_start = jnp.arange(n_cmp) * CMP_STRIDE
    rel_c = t_pos[:, None] - (c_start + CMP_LEN - 1)[None, :]
    mask_c = rel_c >= 0
    s_c = jnp.einsum('bthgd,bnhd->bhgtn', qg, kc).astype(f32) * scale
    s_c = s_c - slopes[None, :, :, None, None] * rel_c.astype(f32)
    s_c = jnp.where(mask_c, s_c, NEG_INF)
    p_c = jnp.where(mask_c, jax.nn.softmax(s_c, axis=-1), 0.0)
    o_cmp = jnp.einsum('bhgtn,bnhd->bthgd', p_c.astype(vc.dtype), vc)

    n_slc = s_len // SEL_BLOCK
    n_sel = min(N_SELECT, n_slc)
    s_start = jnp.arange(n_slc) * SEL_BLOCK
    inter = jnp.maximum(jnp.minimum(c_start[:, None] + CMP_LEN, s_start[None, :] + SEL_BLOCK)
                        - jnp.maximum(c_start[:, None], s_start[None, :]), 0).astype(f32) / CMP_LEN
    importance = jnp.einsum('bhtn,nj->bhtj', p_c.sum(axis=2), inter)
    cur = t_pos // SEL_BLOCK
    j = jnp.arange(n_slc)
    valid = s_start[None, :] <= t_pos[:, None]
    forced = (j[None, :] == 0) | (j[None, :] == cur[:, None]) | (j[None, :] == cur[:, None] - 1)
    sel_score = jnp.where(valid, importance + FORCE_BONUS * forced, NEG_INF)
    _, sel_idx = lax.top_k(sel_score, n_sel)

    ks_blk = k_slc.reshape(b, n_slc, SEL_BLOCK, n_kv, dh).transpose(0, 3, 1, 2, 4)
    vs_blk = v_slc.reshape(b, n_slc, SEL_BLOCK, n_kv, dh).transpose(0, 3, 1, 2, 4)
    nq = s_len // Q_BLOCK
    q_blocks = qg.reshape(b, nq, Q_BLOCK, n_kv, g, dh).transpose(1, 0, 2, 3, 4, 5)
    idx_blocks = sel_idx.reshape(b, n_kv, nq, Q_BLOCK, n_sel).transpose(2, 0, 3, 1, 4)
    t_blocks = t_pos.reshape(nq, Q_BLOCK)
    bi = jnp.arange(b)[:, None, None, None]
    hi = jnp.arange(n_kv)[None, None, :, None]
    in_blk = jnp.arange(SEL_BLOCK)

    def selected_block(args):
        q_i, idx_i, t_i = args
        k_g = ks_blk[bi, hi, idx_i]
        v_g = vs_blk[bi, hi, idx_i].reshape(b, Q_BLOCK, n_kv, n_sel * SEL_BLOCK, dh)
        rel = t_i[None, :, None, None, None] - (idx_i[..., None] * SEL_BLOCK + in_blk)
        s = jnp.einsum('bqhgd,bqhnld->bqhgnl', q_i, k_g).astype(f32) * scale
        s = s - slopes[None, None, :, :, None, None] * rel[:, :, :, None].astype(f32)
        s = jnp.where(rel[:, :, :, None] >= 0, s, NEG_INF).reshape(b, Q_BLOCK, n_kv, g, n_sel * SEL_BLOCK)
        p = jax.nn.softmax(s, axis=-1)
        return jnp.einsum('bqhgk,bqhkd->bqhgd', p.astype(v_g.dtype), v_g)

    o_slc = lax.map(selected_block, (q_blocks, idx_blocks, t_blocks))
    o_slc = o_slc.transpose(1, 0, 2, 3, 4, 5).reshape(b, s_len, n_kv, g, dh)

    o_win = banded_gqa_attention(q, k_win, v_win, WINDOW_B, slopes).reshape(b, s_len, n_kv, g, dh)

    gates = jax.nn.sigmoid(gate_logits.astype(f32)).reshape(b, s_len, n_kv, g, N_GATES_B).astype(q.dtype)
    out = gates[..., 0:1] * o_cmp + gates[..., 1:2] * o_slc + gates[..., 2:3] * o_win
    return out.reshape(b, s_len, n_h * dh)


def forgetting_attention(q, k, v, f_logits, f_bias):
    b, s_len, n_h, dh = q.shape
    f32 = jnp.float32
    scale = dh ** -0.5
    log_f = jax.nn.log_sigmoid(f_logits.astype(f32) + f_bias.astype(f32))
    cum = jnp.cumsum(log_f, axis=1).transpose(0, 2, 1)
    nq = s_len // Q_BLOCK
    q_blocks = q.reshape(b, nq, Q_BLOCK, n_h, dh).transpose(1, 0, 2, 3, 4)
    c_blocks = cum.reshape(b, n_h, nq, Q_BLOCK).transpose(2, 0, 1, 3)
    t_blocks = jnp.arange(s_len).reshape(nq, Q_BLOCK)
    k_pos = jnp.arange(s_len)

    def query_block(args):
        q_i, c_i, t_i = args
        s = jnp.einsum('bqhd,bkhd->bhqk', q_i, k).astype(f32) * scale
        s = s + (c_i[:, :, :, None] - cum[:, :, None, :])
        s = jnp.where(t_i[:, None] >= k_pos[None, :], s, NEG_INF)
        p = jax.nn.softmax(s, axis=-1)
        return jnp.einsum('bhqk,bkhd->bqhd', p.astype(v.dtype), v)

    out = lax.map(query_block, (q_blocks, c_blocks, t_blocks))
    return out.transpose(1, 0, 2, 3, 4).reshape(b, s_len, n_h * dh)


def moe_swiglu(h2d, w_router, w_gate, w_up, w_down):
    n_tok, d = h2d.shape
    logits = h2d.astype(jnp.float32) @ w_router.astype(jnp.float32)
    top_val, top_idx = lax.top_k(logits, TOP_K)
    gates = jax.nn.softmax(top_val, axis=-1).astype(h2d.dtype)
    n_slots = n_tok * TOP_K
    e_flat = top_idx.reshape(-1)
    tok_flat = jnp.repeat(jnp.arange(n_tok, dtype=jnp.int32), TOP_K)
    order = jnp.argsort(e_flat)
    e_sorted = e_flat[order]
    counts = jnp.bincount(e_flat, length=N_EXPERTS)
    starts = jnp.cumsum(counts) - counts
    padded = (counts + MOE_BLOCK - 1) // MOE_BLOCK * MOE_BLOCK
    pad_ends = jnp.cumsum(padded)
    pad_starts = pad_ends - padded
    dest = pad_starts[e_sorted] + jnp.arange(n_slots) - starts[e_sorted]
    n_blocks = -(-n_slots // MOE_BLOCK) + N_EXPERTS
    cap = n_blocks * MOE_BLOCK
    slot_tok = jnp.full((cap,), n_tok, jnp.int32).at[dest].set(tok_flat[order])
    slot_gate = jnp.zeros((cap,), h2d.dtype).at[dest].set(gates.reshape(-1)[order])
    block_expert = jnp.minimum(
        jnp.sum((jnp.arange(n_blocks) * MOE_BLOCK)[:, None] >= pad_ends[None, :], axis=1), N_EXPERTS - 1)
    x_pad = jnp.concatenate([h2d, jnp.zeros((1, d), h2d.dtype)], axis=0)
    xb = x_pad[slot_tok].reshape(n_blocks, MOE_BLOCK, d)

    def expert_block(args):
        x_blk, e = args
        return swiglu(x_blk, w_gate[e], w_up[e], w_down[e])

    yb = lax.map(expert_block, (xb, block_expert)).reshape(cap, d) * slot_gate[:, None]
    return jax.ops.segment_sum(yb, slot_tok, num_segments=n_tok + 1)[:n_tok]


def setup_inputs(seed: int = 0) -> dict:
    key = jax.random.key(seed)
    ks = jax.random.split(key, 20)
    f32 = jnp.float32

    def nrm(k, shape, scale):
        return jax.random.normal(k, shape, f32) * scale

    return {
        'x': nrm(ks[0], (BATCH, SEQ, D_MODEL), 1.0),
        'w_in': nrm(ks[1], (DEPTH, D_MODEL, N_PROJ), D_MODEL ** -0.5),
        'w_out': nrm(ks[2], (DEPTH, MIX_WIDTH, D_MODEL), MIX_WIDTH ** -0.5),
        'norm_mix': 1.0 + nrm(ks[3], (DEPTH, D_MODEL), 0.02),
        'mix_out_norm': 1.0 + nrm(ks[4], (DEPTH, MIX_WIDTH), 0.02),
        'attn_sinks': nrm(ks[5], (DEPTH, A_HEADS), 0.5),
        'nsa_cmp_pe': nrm(ks[6], (DEPTH, 2, CMP_LEN, HEAD_DIM), 0.1),
        'nsa_cmp_w1': nrm(ks[7], (DEPTH, 2, CMP_LEN * HEAD_DIM, CMP_HIDDEN), (CMP_LEN * HEAD_DIM) ** -0.5),
        'nsa_cmp_w2': nrm(ks[8], (DEPTH, 2, CMP_HIDDEN, HEAD_DIM), CMP_HIDDEN ** -0.5),
        'fox_f_bias': FOX_BIAS_MEAN + nrm(ks[9], (DEPTH, C_HEADS), 0.5),
        'norm_ffn': 1.0 + nrm(ks[10], (DEPTH, D_MODEL), 0.02),
        'ffn_w_gate': nrm(ks[11], (N_DENSE, D_MODEL, D_FF), D_MODEL ** -0.5),
        'ffn_w_up': nrm(ks[12], (N_DENSE, D_MODEL, D_FF), D_MODEL ** -0.5),
        'ffn_w_down': nrm(ks[13], (N_DENSE, D_FF, D_MODEL), D_FF ** -0.5),
        'moe_router': nrm(ks[14], (N_MOE, D_MODEL, N_EXPERTS), D_MODEL ** -0.5),
        'moe_w_gate': nrm(ks[15], (N_MOE, N_EXPERTS, D_MODEL, D_FF), D_MODEL ** -0.5),
        'moe_w_up': nrm(ks[16], (N_MOE, N_EXPERTS, D_MODEL, D_FF), D_MODEL ** -0.5),
        'moe_w_down': nrm(ks[17], (N_MOE, N_EXPERTS, D_FF, D_MODEL), D_FF ** -0.5),
        'norm_final': 1.0 + nrm(ks[18], (D_MODEL,), 0.02),
    }


def reference(x, w_in, w_out, norm_mix, mix_out_norm, attn_sinks, nsa_cmp_pe, nsa_cmp_w1,
              nsa_cmp_w2, fox_f_bias, norm_ffn, ffn_w_gate, ffn_w_up, ffn_w_down, moe_router,
              moe_w_gate, moe_w_up, moe_w_down, norm_final):
    b, s_len, d = x.shape
    slopes_a = alibi_slopes(A_HEADS).reshape(A_KV_HEADS, -1)
    slopes_b = alibi_slopes(B_HEADS).reshape(B_KV_HEADS, -1)
    proj_offsets = np.cumsum(PROJ_WIDTHS)[:-1].tolist()
    group_offsets = [A_WIDTH, A_WIDTH + B_WIDTH]
    for layer in range(DEPTH):
        h = rmsnorm(x, norm_mix[layer])
        (qa, ka, va, qb, kbc, vbc, kbs, vbs, kbw, vbw, gb, qc, kc, vc, fc) = jnp.split(
            h @ w_in[layer], proj_offsets, axis=-1)
        o_a = banded_gqa_attention(split_heads(qa, A_HEADS), split_heads(ka, A_KV_HEADS),
                                   split_heads(va, A_KV_HEADS), WINDOW_A, slopes_a,
                                   attn_sinks[layer].reshape(A_KV_HEADS, -1)).reshape(b, s_len, A_WIDTH)
        o_b = native_sparse_attention(split_heads(qb, B_HEADS),
                                      split_heads(kbc, B_KV_HEADS), split_heads(vbc, B_KV_HEADS),
                                      split_heads(kbs, B_KV_HEADS), split_heads(vbs, B_KV_HEADS),
                                      split_heads(kbw, B_KV_HEADS), split_heads(vbw, B_KV_HEADS),
                                      gb, nsa_cmp_pe[layer], nsa_cmp_w1[layer], nsa_cmp_w2[layer], slopes_b)
        o_c = forgetting_attention(split_heads(qc, C_HEADS), split_heads(kc, C_HEADS),
                                   split_heads(vc, C_HEADS), fc, fox_f_bias[layer])
        g_a, g_b, g_c = jnp.split(mix_out_norm[layer], group_offsets)
        mixed = jnp.concatenate([rmsnorm(o_a, g_a), rmsnorm(o_b, g_b), rmsnorm(o_c, g_c)], axis=-1)
        x = x + mixed @ w_out[layer]
        h = rmsnorm(x, norm_ffn[layer])
        if layer % 2 == 0:
            i = layer // 2
            x = x + swiglu(h, ffn_w_gate[i], ffn_w_up[i], ffn_w_down[i])
        else:
            i = layer // 2
            x = x + moe_swiglu(h.reshape(b * s_len, d), moe_router[i], moe_w_gate[i],
                               moe_w_up[i], moe_w_down[i]).reshape(b, s_len, d)
    return rmsnorm(x, norm_final)
```

```python
import functools

import jax
import jax.numpy as jnp
import numpy as np
from jax import lax
from jax.experimental import pallas as pl
from jax.experimental.pallas import tpu as pltpu

F32 = jnp.float32
BF16 = jnp.bfloat16

HEAD_DIM = 64
A_HEADS, A_KV_HEADS = 8, 2
B_HEADS, B_KV_HEADS = 12, 2
C_HEADS = 12
B_GROUP = B_HEADS // B_KV_HEADS
A_WIDTH, B_WIDTH, C_WIDTH = A_HEADS * HEAD_DIM, B_HEADS * HEAD_DIM, C_HEADS * HEAD_DIM
KV_WIDTH = 2 * HEAD_DIM
N_GATE_COLS = B_HEADS * 3
WINDOW_A = 128
WINDOW_B = 512
CMP_LEN, CMP_STRIDE = 32, 16
SEL_BLOCK, N_SELECT = 64, 16
N_EXPERTS = 8
RMS_EPS = 1e-6
NEG_INF = -1e30
M_INIT = -1e20
FORCE_BONUS = 1e4
LANES = 128
VMEM_LIMIT = 56 * 1024 * 1024

OFF_QA, OFF_KA, OFF_VA = 0, 512, 640
OFF_QB = 768
OFF_KBC, OFF_VBC, OFF_KBS, OFF_VBS, OFF_KBW, OFF_VBW = 1536, 1664, 1792, 1920, 2048, 2176
OFF_QC, OFF_KC, OFF_VC = 2304, 3072, 3840
MAIN_WIDTH = 4608
TAIL_FC = N_GATE_COLS


def _alibi(n):
    return [float(2.0 ** (-8.0 * i / n)) for i in range(1, n + 1)]


SLOPES_A = _alibi(A_HEADS)
SLOPES_B = _alibi(B_HEADS)


def _cparams(sem):
    return pltpu.CompilerParams(dimension_semantics=sem, vmem_limit_bytes=VMEM_LIMIT)


def _dot(a, b):
    return jnp.dot(a, b, preferred_element_type=F32)


def _dot_nt(a, b):
    return lax.dot_general(a, b, (((1,), (1,)), ((), ())), preferred_element_type=F32)


def _split3(x):
    hi = x.astype(BF16)
    r1 = x - hi.astype(F32)
    mid = r1.astype(BF16)
    lo = (r1 - mid.astype(F32)).astype(BF16)
    return hi, mid, lo


def _rms(x, gain):
    return x * lax.rsqrt(jnp.mean(x * x, axis=-1, keepdims=True) + RMS_EPS) * gain


def _sigmoid(x):
    return 1.0 / (1.0 + jnp.exp(-x))


def _norm_kernel(x_ref, g_ref, o_ref):
    o_ref[...] = _rms(x_ref[...], g_ref[...]).astype(o_ref.dtype)


def rmsnorm_rows(x, gain, out_dtype, tm=512):
    n, d = x.shape
    return pl.pallas_call(
        _norm_kernel,
        grid=(n // tm,),
        in_specs=[pl.BlockSpec((tm, d), lambda i: (i, 0)), pl.BlockSpec((1, d), lambda i: (0, 0))],
        out_specs=pl.BlockSpec((tm, d), lambda i: (i, 0)),
        out_shape=jax.ShapeDtypeStruct((n, d), out_dtype),
        compiler_params=_cparams(("parallel",)),
        name="rmsnorm",
    )(x, gain.reshape(1, d))


def _norm_matmul_kernel(x_ref, g_ref, w_ref, o_ref, h_sc):
    @pl.when(pl.program_id(1) == 0)
    def _():
        h_sc[...] = _rms(x_ref[...], g_ref[...]).astype(BF16)

    o_ref[...] = _dot(h_sc[...], w_ref[...]).astype(o_ref.dtype)


def norm_matmul(x, gain, w, out_dtype, tm=512, tn=512):
    n, d = x.shape
    nn = w.shape[1]
    tn = min(tn, nn)
    return pl.pallas_call(
        _norm_matmul_kernel,
        grid=(n // tm, nn // tn),
        in_specs=[pl.BlockSpec((tm, d), lambda i, j: (i, 0)),
                  pl.BlockSpec((1, d), lambda i, j: (0, 0)),
                  pl.BlockSpec((d, tn), lambda i, j: (0, j))],
        out_specs=pl.BlockSpec((tm, tn), lambda i, j: (i, j)),
        out_shape=jax.ShapeDtypeStruct((n, nn), out_dtype),
        scratch_shapes=[pltpu.VMEM((tm, d), BF16)],
        compiler_params=_cparams(("parallel", "arbitrary")),
        name="norm_matmul",
    )(x, gain.reshape(1, d), w)


def _swa_kernel(sink_ref, q_ref, kp_ref, kc_ref, vp_ref, vc_ref, o_ref):
    n = pl.program_id(1)
    tq = q_ref.shape[0]
    q = q_ref[...]
    k = jnp.concatenate([kp_ref[...], kc_ref[...]], axis=0)
    v = jnp.concatenate([vp_ref[...], vc_ref[...]], axis=0)
    i = lax.broadcasted_iota(jnp.int32, (tq, 2 * tq), 0)
    j = lax.broadcasted_iota(jnp.int32, (tq, 2 * tq), 1)
    rel = tq + i - j
    mask = (rel >= 0) & (rel < WINDOW_A) & ((j >= tq) | (n > 0))
    relf = rel.astype(F32)
    group = A_HEADS // A_KV_HEADS
    for h in range(A_HEADS):
        kv = h // group
        qh = q[:, h * HEAD_DIM:(h + 1) * HEAD_DIM]
        kh = k[:, kv * HEAD_DIM:(kv + 1) * HEAD_DIM]
        vh = v[:, kv * HEAD_DIM:(kv + 1) * HEAD_DIM]
        s = _dot_nt(qh, kh) * HEAD_DIM ** -0.5 - SLOPES_A[h] * relf
        s = jnp.where(mask, s, NEG_INF)
        sink = sink_ref[h]
        m = jnp.maximum(jnp.max(s, axis=-1, keepdims=True), sink)
        p = jnp.exp(s - m)
        l = jnp.sum(p, axis=-1, keepdims=True) + jnp.exp(sink - m)
        o_ref[:, h * HEAD_DIM:(h + 1) * HEAD_DIM] = _dot(p.astype(BF16), vh) / l


def swa_attention(proj, sinks, batch, seq):
    tq = WINDOW_A
    nb = seq // tq
    kcol, vcol = OFF_KA // KV_WIDTH, OFF_VA // KV_WIDTH

    def cur(col):
        return pl.BlockSpec((tq, KV_WIDTH), lambda b, n: (b * nb + n, col))

    def prev(col):
        return pl.BlockSpec((tq, KV_WIDTH), lambda b, n: (b * nb + jnp.maximum(n - 1, 0), col))

    return pl.pallas_call(
        _swa_kernel,
        grid=(batch, nb),
        in_specs=[pl.BlockSpec(memory_space=pltpu.SMEM),
                  pl.BlockSpec((tq, A_WIDTH), lambda b, n: (b * nb + n, OFF_QA // A_WIDTH)),
                  prev(kcol), cur(kcol), prev(vcol), cur(vcol)],
        out_specs=pl.BlockSpec((tq, A_WIDTH), lambda b, n: (b * nb + n, 0)),
        out_shape=jax.ShapeDtypeStruct((batch * seq, A_WIDTH), F32),
        compiler_params=_cparams(("parallel", "parallel")),
        name="swa_attention",
    )(sinks, proj, proj, proj, proj, proj)


def _fox_cum_kernel(z_ref, b_ref, o_ref, carry_sc):
    @pl.when(pl.program_id(1) == 0)
    def _():
        carry_sc[...] = jnp.zeros_like(carry_sc)

    z = z_ref[...] + b_ref[...]
    log_f = -(jnp.maximum(-z, 0.0) + jnp.log1p(jnp.exp(-jnp.abs(z))))
    ts = z.shape[0]
    r = lax.broadcasted_iota(jnp.int32, (ts, ts), 0)
    c = lax.broadcasted_iota(jnp.int32, (ts, ts), 1)
    tri = (c <= r).astype(BF16)
    hi, mid, lo = _split3(log_f)
    cum = _dot(tri, hi) + _dot(tri, mid) + _dot(tri, lo) + carry_sc[0:1, :]
    o_ref[...] = cum
    carry_sc[0:1, :] = cum[ts - 1:ts, :]


def fox_cumulative_log_forget(tail, bias_row, batch, seq, ts=512):
    nt = seq // ts
    return pl.pallas_call(
        _fox_cum_kernel,
        grid=(batch, nt),
        in_specs=[pl.BlockSpec((ts, LANES), lambda b, i: (b * nt + i, 0)),
                  pl.BlockSpec((1, LANES), lambda b, i: (0, 0))],
        out_specs=pl.BlockSpec((ts, LANES), lambda b, i: (b * nt + i, 0)),
        out_shape=jax.ShapeDtypeStruct((batch * seq, LANES), F32),
        scratch_shapes=[pltpu.VMEM((8, LANES), F32)],
        compiler_params=_cparams(("parallel", "arbitrary")),
        name="fox_cum",
    )(tail, bias_row)


def _fox_kernel(q_ref, k_ref, v_ref, cq_ref, ck_ref, o_ref, m_sc, l_sc, acc_sc, *, tq, tk):
    iq, ik = pl.program_id(1), pl.program_id(2)
    last_ik = (iq * tq + tq - 1) // tk

    @pl.when(ik == 0)
    def _():
        m_sc[...] = jnp.full_like(m_sc, M_INIT)
        l_sc[...] = jnp.zeros_like(l_sc)
        acc_sc[...] = jnp.zeros_like(acc_sc)

    @pl.when(ik <= last_ik)
    def _():
        row = iq * tq + lax.broadcasted_iota(jnp.int32, (tq, tk), 0)
        col = ik * tk + lax.broadcasted_iota(jnp.int32, (tq, tk), 1)
        causal = row >= col
        for h in range(C_HEADS):
            sl = slice(h * HEAD_DIM, (h + 1) * HEAD_DIM)
            s = _dot_nt(q_ref[:, sl], k_ref[:, sl]) * HEAD_DIM ** -0.5
            s = s + (cq_ref[:, TAIL_FC + h:TAIL_FC + h + 1] - ck_ref[0, h:h + 1, :])
            s = jnp.where(causal, s, NEG_INF)
            m_prev = m_sc[h]
            m_new = jnp.maximum(m_prev, jnp.max(s, axis=-1, keepdims=True))
            alpha = jnp.exp(m_prev - m_new)
            p = jnp.exp(s - m_new)
            l_sc[h] = alpha * l_sc[h] + jnp.sum(p, axis=-1, keepdims=True)
            acc_sc[h] = alpha * acc_sc[h] + _dot(p.astype(BF16), v_ref[:, sl])
            m_sc[h] = m_new

    @pl.when(ik == last_ik)
    def _():
        for h in range(C_HEADS):
            o_ref[:, h * HEAD_DIM:(h + 1) * HEAD_DIM] = acc_sc[h] / l_sc[h]


def fox_attention(proj, cum, cum_t, batch, seq, tq=512, tk=512):
    nq, nk = seq // tq, seq // tk

    def kv_block(col):
        return pl.BlockSpec(
            (tk, C_WIDTH), lambda b, iq, ik: (b * nk + jnp.minimum(ik, (iq * tq + tq - 1) // tk), col))

    return pl.pallas_call(
        functools.partial(_fox_kernel, tq=tq, tk=tk),
        grid=(batch, nq, nk),
        in_specs=[pl.BlockSpec((tq, C_WIDTH), lambda b, iq, ik: (b * nq + iq, OFF_QC // C_WIDTH)),
                  kv_block(OFF_KC // C_WIDTH), kv_block(OFF_VC // C_WIDTH),
                  pl.BlockSpec((tq, LANES), lambda b, iq, ik: (b * nq + iq, 0)),
                  pl.BlockSpec((1, 16, tk),
                               lambda b, iq, ik: (b, 0, jnp.minimum(ik, (iq * tq + tq - 1) // tk)))],
        out_specs=pl.BlockSpec((tq, C_WIDTH), lambda b, iq, ik: (b * nq + iq, 0)),
        out_shape=jax.ShapeDtypeStruct((batch * seq, C_WIDTH), F32),
        scratch_shapes=[pltpu.VMEM((C_HEADS, tq, 1), F32), pltpu.VMEM((C_HEADS, tq, 1), F32),
                        pltpu.VMEM((C_HEADS, tq, HEAD_DIM), F32)],
        compiler_params=_cparams(("parallel", "parallel", "arbitrary")),
        name="fox_attention",
    )(proj, proj, proj, cum, cum_t)


def _compress_kernel(t_ref, pe_ref, w1_ref, w2_ref, o_ref):
    w1 = w1_ref[0]
    hid = _dot(t_ref[0, 0, 0], w1) + _dot(pe_ref[0], w1)[0:1, :]
    act = hid * _sigmoid(hid)
    o_ref[0, 0, 0] = _dot(act.astype(BF16), w2_ref[0]).astype(o_ref.dtype)


def nsa_compress(flat, pe_rows, w1, w2):
    _, batch, n_kv, n_chunks, width = flat.shape
    hidden = w1.shape[-1]
    return pl.pallas_call(
        _compress_kernel,
        grid=(2, batch, n_kv),
        in_specs=[pl.BlockSpec((1, 1, 1, n_chunks, width), lambda s, b, h: (s, b, h, 0, 0)),
                  pl.BlockSpec((1, 8, width), lambda s, b, h: (s, 0, 0)),
                  pl.BlockSpec((1, width, hidden), lambda s, b, h: (s, 0, 0)),
                  pl.BlockSpec((1, hidden, HEAD_DIM), lambda s, b, h: (s, 0, 0))],
        out_specs=pl.BlockSpec((1, 1, 1, n_chunks, HEAD_DIM), lambda s, b, h: (s, b, h, 0, 0)),
        out_shape=jax.ShapeDtypeStruct((2, batch, n_kv, n_chunks, HEAD_DIM), BF16),
        compiler_params=_cparams(("parallel", "parallel", "parallel")),
        name="nsa_compress",
    )(flat, pe_rows, w1, w2)


def _cmp_attn_kernel(q_ref, kc_ref, vc_ref, inter_ref, o_ref, sel_ref, *, tq):
    iq = pl.program_id(1)
    n_chunks = kc_ref.shape[3]
    n_slc = inter_ref.shape[1]
    t = iq * tq + lax.broadcasted_iota(jnp.int32, (tq, n_chunks), 0)
    n = lax.broadcasted_iota(jnp.int32, (tq, n_chunks), 1)
    rel = t - (n * CMP_STRIDE + CMP_LEN - 1)
    mask = rel >= 0
    relf = rel.astype(F32)
    t_s = iq * tq + lax.broadcasted_iota(jnp.int32, (tq, n_slc), 0)
    jj = lax.broadcasted_iota(jnp.int32, (tq, n_slc), 1)
    cur = t_s >> (SEL_BLOCK.bit_length() - 1)
    valid = jj * SEL_BLOCK <= t_s
    forced = (jj == 0) | (jj == cur) | (jj == cur - 1)
    jf = jj.astype(F32)
    inter = inter_ref[...]
    for h in range(B_KV_HEADS):
        kc = kc_ref[0, 0, h]
        vc = vc_ref[0, 0, h]
        p_sum = jnp.zeros((tq, n_chunks), F32)
        for g in range(B_GROUP):
            hd = h * B_GROUP + g
            s = _dot_nt(q_ref[:, hd * HEAD_DIM:(hd + 1) * HEAD_DIM], kc) * HEAD_DIM ** -0.5
            s = jnp.where(mask, s - SLOPES_B[hd] * relf, NEG_INF)
            m = jnp.max(s, axis=-1, keepdims=True)
            e = jnp.where(mask, jnp.exp(s - m), 0.0)
            l = jnp.sum(e, axis=-1, keepdims=True)
            p = e / jnp.where(l > 0.0, l, 1.0)
            p_sum = p_sum + p
            o_ref[:, hd * HEAD_DIM:(hd + 1) * HEAD_DIM] = _dot(p.astype(BF16), vc)
        hi, mid, lo = _split3(p_sum)
        importance = _dot(hi, inter) + _dot(mid, inter) + _dot(lo, inter)
        score = jnp.where(valid, importance + jnp.where(forced, FORCE_BONUS, 0.0), NEG_INF)
        chosen = jnp.zeros((tq, n_slc), F32)
        for _ in range(min(N_SELECT, n_slc)):
            best = jnp.max(score, axis=-1, keepdims=True)
            first = jnp.min(jnp.where(score == best, jf, float(n_slc)), axis=-1, keepdims=True)
            pick = jf == first
            chosen = jnp.where(pick, 1.0, chosen)
            score = jnp.where(pick, -jnp.inf, score)
        sel_ref[0, h] = jnp.where(valid, chosen, 0.0).astype(sel_ref.dtype)


def nsa_compressed_attention(proj, kvc, inter, batch, seq, tq=256):
    nq = seq // tq
    n_chunks = kvc.shape[3]
    n_slc = seq // SEL_BLOCK
    return pl.pallas_call(
        functools.partial(_cmp_attn_kernel, tq=tq),
        grid=(batch, nq),
        in_specs=[pl.BlockSpec((tq, B_WIDTH), lambda b, i: (b * nq + i, OFF_QB // B_WIDTH)),
                  pl.BlockSpec((1, 1, B_KV_HEADS, n_chunks, HEAD_DIM), lambda b, i: (0, b, 0, 0, 0)),
                  pl.BlockSpec((1, 1, B_KV_HEADS, n_chunks, HEAD_DIM), lambda b, i: (1, b, 0, 0, 0)),
                  pl.BlockSpec((n_chunks, n_slc), lambda b, i: (0, 0))],
        out_specs=[pl.BlockSpec((tq, B_WIDTH), lambda b, i: (b * nq + i, 0)),
                   pl.BlockSpec((1, B_KV_HEADS, tq, n_slc), lambda b, i: (b, 0, i, 0))],
        out_shape=[jax.ShapeDtypeStruct((batch * seq, B_WIDTH), F32),
                   jax.ShapeDtypeStruct((batch, B_KV_HEADS, seq, n_slc), BF16)],
        compiler_params=_cparams(("parallel", "parallel")),
        name="nsa_cmp_attention",
    )(proj, kvc, kvc, inter)


def _nsa_flash_kernel(*refs, mode, tq, tk):
    if mode == "slc":
        q_ref, k_ref, v_ref, sel_ref, exp_ref, o_ref, qs_sc, m_sc, l_sc, acc_sc = refs
    else:
        q_ref, k_ref, v_ref, o_ref, qs_sc, m_sc, l_sc, acc_sc = refs
    iq, j = pl.program_id(1), pl.program_id(2)
    if mode == "slc":
        ik = j
        last_j = (iq * tq + tq - 1) // tk
        active = j <= last_j
    else:
        n_band = pl.num_programs(2)
        ik = iq - (n_band - 1) + j
        last_j = n_band - 1
        active = ik >= 0

    @pl.when(j == 0)
    def _():
        m_sc[...] = jnp.full_like(m_sc, M_INIT)
        l_sc[...] = jnp.zeros_like(l_sc)
        acc_sc[...] = jnp.zeros_like(acc_sc)
        for h in range(B_KV_HEADS):
            for g in range(B_GROUP):
                hd = h * B_GROUP + g
                qs_sc[h, g * tq:(g + 1) * tq, :] = q_ref[:, hd * HEAD_DIM:(hd + 1) * HEAD_DIM] * HEAD_DIM ** -0.5

    @pl.when(active)
    def _():
        t = iq * tq + lax.broadcasted_iota(jnp.int32, (tq, tk), 0)
        s_pos = ik * tk + lax.broadcasted_iota(jnp.int32, (tq, tk), 1)
        rel = t - s_pos
        relf = rel.astype(F32)
        for h in range(B_KV_HEADS):
            if mode == "slc":
                ok = (_dot(sel_ref[0, h], exp_ref[...]) > 0.5) & (rel >= 0)
            else:
                ok = (rel >= 0) & (rel < WINDOW_B)
            kh = k_ref[:, h * HEAD_DIM:(h + 1) * HEAD_DIM]
            vh = v_ref[:, h * HEAD_DIM:(h + 1) * HEAD_DIM]
            s_all = _dot_nt(qs_sc[h], kh)
            p_parts = []
            for g in range(B_GROUP):
                rows = slice(g * tq, (g + 1) * tq)
                s = jnp.where(ok, s_all[rows] - SLOPES_B[h * B_GROUP + g] * relf, NEG_INF)
                m_prev = m_sc[h, rows]
                m_new = jnp.maximum(m_prev, jnp.max(s, axis=-1, keepdims=True))
                alpha = jnp.exp(m_prev - m_new)
                p = jnp.exp(s - m_new)
                l_sc[h, rows] = alpha * l_sc[h, rows] + jnp.sum(p, axis=-1, keepdims=True)
                acc_sc[h, rows] = alpha * acc_sc[h, rows]
                m_sc[h, rows] = m_new
                p_parts.append(p.astype(BF16))
            acc_sc[h] = acc_sc[h] + _dot(jnp.concatenate(p_parts, axis=0), vh)

    @pl.when(j == last_j)
    def _():
        for h in range(B_KV_HEADS):
            o = acc_sc[h] / l_sc[h]
            for g in range(B_GROUP):
                hd = h * B_GROUP + g
                o_ref[:, hd * HEAD_DIM:(hd + 1) * HEAD_DIM] = o[g * tq:(g + 1) * tq]


def nsa_flash_attention(proj, mode, batch, seq, sel=None, expand=None, tq=256, tk=256):
    nq, nkb = seq // tq, seq // tk
    if mode == "slc":
        kcol, vcol = OFF_KBS // KV_WIDTH, OFF_VBS // KV_WIDTH
        n_steps = nkb

        def kblock(iq, j):
            return jnp.minimum(j, (iq * tq + tq - 1) // tk)
    else:
        assert tq == tk
        kcol, vcol = OFF_KBW // KV_WIDTH, OFF_VBW // KV_WIDTH
        n_steps = WINDOW_B // tk + 1

        def kblock(iq, j):
            return jnp.maximum(iq - (n_steps - 1) + j, 0)

    in_specs = [pl.BlockSpec((tq, B_WIDTH), lambda b, iq, j: (b * nq + iq, OFF_QB // B_WIDTH)),
                pl.BlockSpec((tk, KV_WIDTH), lambda b, iq, j: (b * nkb + kblock(iq, j), kcol)),
                pl.BlockSpec((tk, KV_WIDTH), lambda b, iq, j: (b * nkb + kblock(iq, j), vcol))]
    args = [proj, proj, proj]
    if mode == "slc":
        n_slc = seq // SEL_BLOCK
        in_specs += [pl.BlockSpec((1, B_KV_HEADS, tq, n_slc), lambda b, iq, j: (b, 0, iq, 0)),
                     pl.BlockSpec((n_slc, tk), lambda b, iq, j: (0, kblock(iq, j)))]
        args += [sel, expand]
    return pl.pallas_call(
        functools.partial(_nsa_flash_kernel, mode=mode, tq=tq, tk=tk),
        grid=(batch, nq, n_steps),
        in_specs=in_specs,
        out_specs=pl.BlockSpec((tq, B_WIDTH), lambda b, iq, j: (b * nq + iq, 0)),
        out_shape=jax.ShapeDtypeStruct((batch * seq, B_WIDTH), F32),
        scratch_shapes=[pltpu.VMEM((B_KV_HEADS, B_GROUP * tq, HEAD_DIM), BF16),
                        pltpu.VMEM((B_KV_HEADS, B_GROUP * tq, 1), F32),
                        pltpu.VMEM((B_KV_HEADS, B_GROUP * tq, 1), F32),
                        pltpu.VMEM((B_KV_HEADS, B_GROUP * tq, HEAD_DIM), F32)],
        compiler_params=_cparams(("parallel", "parallel", "arbitrary")),
        name="nsa_" + mode + "_attention",
    )(*args)


def _mix_out_kernel(oa_ref, ocmp_ref, oslc_ref, owin_ref, oc_ref, tail_ref, g_ref, w_ref, x_ref, o_ref, mixed_sc):
    @pl.when(pl.program_id(1) == 0)
    def _():
        gates = _sigmoid(tail_ref[...])
        parts = []
        for hd in range(B_HEADS):
            sl = slice(hd * HEAD_DIM, (hd + 1) * HEAD_DIM)
            parts.append(gates[:, 3 * hd:3 * hd + 1] * ocmp_ref[:, sl]
                         + gates[:, 3 * hd + 1:3 * hd + 2] * oslc_ref[:, sl]
                         + gates[:, 3 * hd + 2:3 * hd + 3] * owin_ref[:, sl])
        o_b = jnp.concatenate(parts, axis=-1)
        b0, c0 = A_WIDTH, A_WIDTH + B_WIDTH
        mixed_sc[:, 0:b0] = _rms(oa_ref[...], g_ref[:, 0:b0]).astype(BF16)
        mixed_sc[:, b0:c0] = _rms(o_b, g_ref[:, b0:c0]).astype(BF16)
        mixed_sc[:, c0:] = _rms(oc_ref[...], g_ref[:, c0:]).astype(BF16)

    o_ref[...] = x_ref[...] + _dot(mixed_sc[...], w_ref[...])


def mix_out(o_a, o_cmp, o_slc, o_win, o_c, tail, gain, w_out, x, tm=256, tn=512):
    n, d = x.shape
    width = w_out.shape[0]

    def rows(w):
        return pl.BlockSpec((tm, w), lambda i, j: (i, 0))

    return pl.pallas_call(
        _mix_out_kernel,
        grid=(n // tm, d // tn),
        in_specs=[rows(A_WIDTH), rows(B_WIDTH), rows(B_WIDTH), rows(B_WIDTH), rows(C_WIDTH), rows(LANES),
                  pl.BlockSpec((1, width), lambda i, j: (0, 0)),
                  pl.BlockSpec((width, tn), lambda i, j: (0, j)),
                  pl.BlockSpec((tm, tn), lambda i, j: (i, j))],
        out_specs=pl.BlockSpec((tm, tn), lambda i, j: (i, j)),
        out_shape=jax.ShapeDtypeStruct((n, d), F32),
        scratch_shapes=[pltpu.VMEM((tm, width), BF16)],
        compiler_params=_cparams(("parallel", "arbitrary")),
        name="mix_out",
    )(o_a, o_cmp, o_slc, o_win, o_c, tail, gain.reshape(1, width), w_out, x)


def _ffn_up_kernel(be_ref, nv_ref, x_ref, wg_ref, wu_ref, o_ref):
    i = pl.program_id(1)

    @pl.when(i < nv_ref[0])
    def _():
        x = x_ref[...].astype(BF16)
        gate = _dot(x, wg_ref[0])
        up = _dot(x, wu_ref[0])
        o_ref[...] = (gate * _sigmoid(gate) * up).astype(o_ref.dtype)

    @pl.when(i >= nv_ref[0])
    def _():
        o_ref[...] = jnp.zeros_like(o_ref)


def _ffn_down_kernel(be_ref, nv_ref, h_ref, wd_ref, *rest):
    o_ref = rest[-1]
    i = pl.program_id(1)

    @pl.when(i < nv_ref[0])
    def _():
        y = _dot(h_ref[...], wd_ref[0])
        if len(rest) == 2:
            y = y + rest[0][...]
        o_ref[...] = y

    @pl.when(i >= nv_ref[0])
    def _():
        o_ref[...] = jnp.zeros_like(o_ref)


def grouped_swiglu(xs, block_expert, n_valid, w_gate, w_up, w_down, residual=None, tm=256, tf=512, tn=512):
    rows, d = xs.shape
    f = w_gate.shape[-1]
    nblk = rows // tm

    def xrow(i, nv):
        return jnp.minimum(i, nv[0] - 1)

    hidden = pl.pallas_call(
        _ffn_up_kernel,
        grid_spec=pltpu.PrefetchScalarGridSpec(
            num_scalar_prefetch=2,
            grid=(f // tf, nblk),
            in_specs=[pl.BlockSpec((tm, d), lambda j, i, be, nv: (xrow(i, nv), 0)),
                      pl.BlockSpec((1, d, tf), lambda j, i, be, nv: (be[i], 0, j)),
                      pl.BlockSpec((1, d, tf), lambda j, i, be, nv: (be[i], 0, j))],
            out_specs=pl.BlockSpec((tm, tf), lambda j, i, be, nv: (i, j))),
        out_shape=jax.ShapeDtypeStruct((rows, f), BF16),
        compiler_params=_cparams(("parallel", "arbitrary")),
        name="ffn_up",
    )(block_expert, n_valid, xs, w_gate, w_up)

    in_specs = [pl.BlockSpec((tm, f), lambda j, i, be, nv: (xrow(i, nv), 0)),
                pl.BlockSpec((1, f, tn), lambda j, i, be, nv: (be[i], 0, j))]
    args = [hidden, w_down]
    if residual is not None:
        in_specs.append(pl.BlockSpec((tm, tn), lambda j, i, be, nv: (i, j)))
        args.append(residual)
    return pl.pallas_call(
        _ffn_down_kernel,
        grid_spec=pltpu.PrefetchScalarGridSpec(
            num_scalar_prefetch=2,
            grid=(d // tn, nblk),
            in_specs=in_specs,
            out_specs=pl.BlockSpec((tm, tn), lambda j, i, be, nv: (i, j))),
        out_shape=jax.ShapeDtypeStruct((rows, d), F32),
        compiler_params=_cparams(("parallel", "arbitrary")),
        name="ffn_down",
    )(block_expert, n_valid, *args)


def _router_kernel(x_ref, g_ref, wr_ref, h_ref, ri_ref, rf_ref, cnt_ref, carry_sc):
    @pl.when(pl.program_id(0) == 0)
    def _():
        carry_sc[...] = jnp.zeros_like(carry_sc)

    tm = x_ref.shape[0]
    h = _rms(x_ref[...], g_ref[...])
    h_ref[...] = h
    h1, h2, h3 = _split3(h)
    w1, w2, w3 = wr_ref[0], wr_ref[1], wr_ref[2]
    logits = (_dot(h1, w1) + _dot(h1, w2) + _dot(h2, w1)) + (_dot(h1, w3) + _dot(h2, w2) + _dot(h3, w1))
    lane_i = lax.broadcasted_iota(jnp.int32, (tm, LANES), 1)
    lane = lane_i.astype(F32)
    logits = jnp.where(lane_i < N_EXPERTS, logits, -jnp.inf)
    v1 = jnp.max(logits, axis=-1, keepdims=True)
    e1 = jnp.min(jnp.where(logits == v1, lane, float(LANES)), axis=-1, keepdims=True)
    rest = jnp.where(lane == e1, -jnp.inf, logits)
    v2 = jnp.max(rest, axis=-1, keepdims=True)
    e2 = jnp.min(jnp.where(rest == v2, lane, float(LANES)), axis=-1, keepdims=True)
    z = jnp.exp(v2 - v1)
    g1 = 1.0 / (1.0 + z)
    g2 = z / (1.0 + z)
    chosen = (lane == e1) | (lane == e2)
    onehot = jnp.where(chosen, 1.0, 0.0)
    r = lax.broadcasted_iota(jnp.int32, (tm, tm), 0)
    c = lax.broadcasted_iota(jnp.int32, (tm, tm), 1)
    before = _dot((c < r).astype(BF16), onehot.astype(BF16)) + carry_sc[0:1, :]
    r1 = jnp.sum(jnp.where(lane == e1, before, 0.0), axis=-1, keepdims=True)
    r2 = jnp.sum(jnp.where(lane == e2, before, 0.0), axis=-1, keepdims=True)
    carry = carry_sc[0:1, :] + jnp.sum(onehot, axis=0, keepdims=True)
    carry_sc[0:1, :] = carry
    cnt_ref[...] = jnp.broadcast_to(carry, cnt_ref.shape)
    packed = jnp.where(lane_i == 0, e1, jnp.where(lane_i == 1, e2, jnp.where(
        lane_i == 2, r1, jnp.where(lane_i == 3, r2, 0.0))))
    ri_ref[...] = packed.astype(jnp.int32)
    rf_ref[...] = jnp.where(lane_i == 0, g1, jnp.where(lane_i == 1, g2, 0.0))


def moe_route(x, gain, w_router3, tm=512):
    n, d = x.shape
    return pl.pallas_call(
        _router_kernel,
        grid=(n // tm,),
        in_specs=[pl.BlockSpec((tm, d), lambda i: (i, 0)),
                  pl.BlockSpec((1, d), lambda i: (0, 0)),
                  pl.BlockSpec((3, d, LANES), lambda i: (0, 0, 0))],
        out_specs=[pl.BlockSpec((tm, d), lambda i: (i, 0)),
                   pl.BlockSpec((tm, LANES), lambda i: (i, 0)),
                   pl.BlockSpec((tm, LANES), lambda i: (i, 0)),
                   pl.BlockSpec((8, LANES), lambda i: (0, 0))],
        out_shape=[jax.ShapeDtypeStruct((n, d), F32),
                   jax.ShapeDtypeStruct((n, LANES), jnp.int32),
                   jax.ShapeDtypeStruct((n, LANES), F32),
                   jax.ShapeDtypeStruct((8, LANES), F32)],
        scratch_shapes=[pltpu.VMEM((8, LANES), F32)],
        compiler_params=_cparams(("arbitrary",)),
        name="moe_route",
    )(x, gain.reshape(1, d), w_router3)


def _row_copy(src_ref, src_row, dst_ref, dst_row, sem):
    return pltpu.make_async_copy(src_ref.at[pl.ds(src_row, 1)], dst_ref.at[pl.ds(dst_row, 1)], sem)


def _dispatch_kernel(dest_ref, h_ref, zeros_ref, xs_ref, sem):
    del zeros_ref
    td = h_ref.shape[0]
    base = pl.program_id(0) * td

    def start(r, carry):
        for k in range(2):
            _row_copy(h_ref, r, xs_ref, dest_ref[2 * (base + r) + k], sem).start()
        return carry

    def wait(r, carry):
        for k in range(2):
            _row_copy(h_ref, r, xs_ref, dest_ref[2 * (base + r) + k], sem).wait()
        return carry

    lax.fori_loop(0, td, start, 0)
    lax.fori_loop(0, td, wait, 0)


def moe_dispatch(h, dest, cap, td=128):
    n, d = h.shape
    return pl.pallas_call(
        _dispatch_kernel,
        grid_spec=pltpu.PrefetchScalarGridSpec(
            num_scalar_prefetch=1,
            grid=(n // td,),
            in_specs=[pl.BlockSpec((td, d), lambda i, dest: (i, 0)),
                      pl.BlockSpec(memory_space=pl.ANY)],
            out_specs=pl.BlockSpec(memory_space=pl.ANY),
            scratch_shapes=[pltpu.SemaphoreType.DMA(())]),
        out_shape=jax.ShapeDtypeStruct((cap, d), h.dtype),
        input_output_aliases={2: 0},
        compiler_params=_cparams(("arbitrary",)),
        name="moe_dispatch",
    )(dest, h, jnp.zeros((cap, d), h.dtype))


def _combine_kernel(dest_ref, x_ref, gate_ref, y_ref, o_ref, ya_sc, yb_sc, sem):
    tc = x_ref.shape[0]
    base = pl.program_id(0) * tc

    def start(r, carry):
        _row_copy(y_ref, dest_ref[2 * (base + r)], ya_sc, r, sem).start()
        _row_copy(y_ref, dest_ref[2 * (base + r) + 1], yb_sc, r, sem).start()
        return carry

    def wait(r, carry):
        _row_copy(y_ref, dest_ref[2 * (base + r)], ya_sc, r, sem).wait()
        _row_copy(y_ref, dest_ref[2 * (base + r) + 1], yb_sc, r, sem).wait()
        return carry

    lax.fori_loop(0, tc, start, 0)
    lax.fori_loop(0, tc, wait, 0)
    gates = gate_ref[...]
    o_ref[...] = x_ref[...] + (gates[:, 0:1] * ya_sc[...] + gates[:, 1:2] * yb_sc[...])


def moe_combine(x, gates, y, dest, tc=128):
    n, d = x.shape
    return pl.pallas_call(
        _combine_kernel,
        grid_spec=pltpu.PrefetchScalarGridSpec(
            num_scalar_prefetch=1,
            grid=(n // tc,),
            in_specs=[pl.BlockSpec((tc, d), lambda i, dest: (i, 0)),
                      pl.BlockSpec((tc, LANES), lambda i, dest: (i, 0)),
                      pl.BlockSpec(memory_space=pl.ANY)],
            out_specs=pl.BlockSpec((tc, d), lambda i, dest: (i, 0)),
            scratch_shapes=[pltpu.VMEM((tc, d), F32), pltpu.VMEM((tc, d), F32),
                            pltpu.SemaphoreType.DMA(())]),
        out_shape=jax.ShapeDtypeStruct((n, d), F32),
        compiler_params=_cparams(("arbitrary",)),
        name="moe_combine",
    )(dest, x, gates, y)


def moe_layer(x, gain, w_router, w_gate, w_up, w_down, tm=256):
    n, d = x.shape
    wr = jnp.pad(w_router.astype(F32), ((0, 0), (0, LANES - N_EXPERTS)))
    w_hi = wr.astype(BF16)
    w_r1 = wr - w_hi.astype(F32)
    w_mid = w_r1.astype(BF16)
    w_lo = (w_r1 - w_mid.astype(F32)).astype(BF16)
    h, info, gates, counts = moe_route(x, gain, jnp.stack([w_hi, w_mid, w_lo]))
    counts = counts[0, :N_EXPERTS].astype(jnp.int32)
    padded = (counts + tm - 1) // tm * tm
    pad_ends = jnp.cumsum(padded)
    pad_starts = pad_ends - padded
    experts, ranks = info[:, 0:2], info[:, 2:4]
    dest = (pad_starts[experts] + ranks).reshape(-1).astype(jnp.int32)
    nblk = (2 * n) // tm + N_EXPERTS
    n_valid = (pad_ends[-1] // tm).astype(jnp.int32).reshape(1)
    blk = jnp.minimum(jnp.arange(nblk, dtype=jnp.int32), n_valid[0] - 1) * tm
    block_expert = jnp.minimum(jnp.sum(blk[:, None] >= pad_ends[None, :], axis=1), N_EXPERTS - 1).astype(jnp.int32)
    xs = moe_dispatch(h, dest, nblk * tm)
    y = grouped_swiglu(xs, block_expert, n_valid, w_gate, w_up, w_down, tm=tm)
    return moe_combine(x, gates, y, dest)


def _project_weights(w):
    g0 = OFF_QC
    c0 = g0 + N_GATE_COLS
    c1 = c0 + 3 * C_WIDTH
    main = jnp.concatenate([w[:, :g0], w[:, c0:c1]], axis=1).astype(BF16)
    tail = jnp.concatenate([w[:, g0:c0], w[:, c1:]], axis=1)
    tail = jnp.pad(tail, ((0, 0), (0, LANES - tail.shape[1]))).astype(BF16)
    return main, tail


def _chunk_blocks(cols, batch, seq):
    n_chunks = seq // CMP_STRIDE
    t = cols.reshape(batch, seq, B_KV_HEADS, HEAD_DIM).transpose(0, 2, 1, 3)
    t = t.reshape(batch, B_KV_HEADS, n_chunks, CMP_STRIDE * HEAD_DIM)
    nxt = jnp.concatenate([t[:, :, 1:], jnp.zeros_like(t[:, :, :1])], axis=2)
    return jnp.concatenate([t, nxt], axis=-1)


def _overlap_matrix(seq):
    n_chunks, n_slc = seq // CMP_STRIDE, seq // SEL_BLOCK
    c_start = np.arange(n_chunks) * CMP_STRIDE
    s_start = np.arange(n_slc) * SEL_BLOCK
    inter = np.maximum(np.minimum(c_start[:, None] + CMP_LEN, s_start[None, :] + SEL_BLOCK)
                       - np.maximum(c_start[:, None], s_start[None, :]), 0) / CMP_LEN
    inter[(seq - CMP_LEN) // CMP_STRIDE + 1:] = 0.0
    expand = (np.arange(seq)[None, :] // SEL_BLOCK == np.arange(n_slc)[:, None])
    return jnp.asarray(inter, BF16), jnp.asarray(expand, BF16)


def mixer_layer(x, batch, seq, w_in, w_out, norm_mix, mix_out_norm, sinks, cmp_pe, cmp_w1, cmp_w2, f_bias):
    w_main, w_tail = _project_weights(w_in)
    proj = norm_matmul(x, norm_mix, w_main, BF16)
    tail = norm_matmul(x, norm_mix, w_tail, F32)
    o_a = swa_attention(proj, sinks.astype(F32), batch, seq)

    bias_row = jnp.zeros((1, LANES), F32).at[0, TAIL_FC:TAIL_FC + C_HEADS].set(f_bias.astype(F32))
    cum = fox_cumulative_log_forget(tail, bias_row, batch, seq)
    cum_t = cum[:, TAIL_FC:TAIL_FC + C_HEADS].reshape(batch, seq, C_HEADS).transpose(0, 2, 1)
    cum_t = jnp.pad(cum_t, ((0, 0), (0, 16 - C_HEADS), (0, 0)))
    o_c = fox_attention(proj, cum, cum_t, batch, seq)

    flat = jnp.stack([_chunk_blocks(proj[:, OFF_KBC:OFF_KBC + KV_WIDTH], batch, seq),
                      _chunk_blocks(proj[:, OFF_VBC:OFF_VBC + KV_WIDTH], batch, seq)])
    pe_rows = jnp.broadcast_to(cmp_pe.reshape(2, 1, CMP_LEN * HEAD_DIM), (2, 8, CMP_LEN * HEAD_DIM)).astype(BF16)
    kvc = nsa_compress(flat, pe_rows, cmp_w1.astype(BF16), cmp_w2.astype(BF16))
    inter, expand = _overlap_matrix(seq)
    o_cmp, sel = nsa_compressed_attention(proj, kvc, inter, batch, seq)
    o_slc = nsa_flash_attention(proj, "slc", batch, seq, sel=sel, expand=expand)
    o_win = nsa_flash_attention(proj, "win", batch, seq)
    return mix_out(o_a, o_cmp, o_slc, o_win, o_c, tail, mix_out_norm, w_out.astype(BF16), x)


def dense_layer(x, gain, w_gate, w_up, w_down, tm=256):
    n = x.shape[0]
    h = rmsnorm_rows(x, gain, BF16)
    nblk = n // tm
    return grouped_swiglu(h, jnp.zeros((nblk,), jnp.int32), jnp.full((1,), nblk, jnp.int32),
                          w_gate[None].astype(BF16), w_up[None].astype(BF16), w_down[None].astype(BF16),
                          residual=x, tm=tm)


def kernel(x, w_in, w_out, norm_mix, mix_out_norm, attn_sinks, nsa_cmp_pe, nsa_cmp_w1, nsa_cmp_w2, fox_f_bias,
           norm_ffn, ffn_w_gate, ffn_w_up, ffn_w_down, moe_router, moe_w_gate, moe_w_up, moe_w_down, norm_final):
    batch, seq, d = x.shape
    depth = w_in.shape[0]
    xf = x.reshape(batch * seq, d).astype(F32)
    for layer in range(depth):
        xf = mixer_layer(xf, batch, seq, w_in[layer], w_out[layer], norm_mix[layer], mix_out_norm[layer],
                         attn_sinks[layer], nsa_cmp_pe[layer], nsa_cmp_w1[layer], nsa_cmp_w2[layer],
                         fox_f_bias[layer])
        i = layer // 2
        if layer % 2 == 0:
            xf = dense_layer(xf, norm_ffn[layer], ffn_w_gate[i], ffn_w_up[i], ffn_w_down[i])
        else:
            xf = moe_layer(xf, norm_ffn[layer], moe_router[i], moe_w_gate[i].astype(BF16),
                           moe_w_up[i].astype(BF16), moe_w_down[i].astype(BF16))
    return rmsnorm_rows(xf, norm_final, x.dtype).reshape(batch, seq, d)
```

```python
import functools

import jax
import jax.numpy as jnp
import numpy as np
from jax import lax
from jax.experimental import pallas as pl
from jax.experimental.pallas import tpu as pltpu

F32 = jnp.float32
BF16 = jnp.bfloat16

HEAD_DIM = 64
A_HEADS, A_KV_HEADS = 8, 2
B_HEADS, B_KV_HEADS = 12, 2
C_HEADS = 12
B_GROUP = B_HEADS // B_KV_HEADS
A_WIDTH, B_WIDTH, C_WIDTH = A_HEADS * HEAD_DIM, B_HEADS * HEAD_DIM, C_HEADS * HEAD_DIM
KV_WIDTH = 2 * HEAD_DIM
N_GATE_COLS = B_HEADS * 3
WINDOW_A = 128
WINDOW_B = 512
CMP_LEN, CMP_STRIDE = 32, 16
SEL_BLOCK, N_SELECT = 64, 16
N_EXPERTS = 8
RMS_EPS = 1e-6
NEG_INF = -1e30
M_INIT = -1e20
FORCE_BONUS = 1e4
LANES = 128
VMEM_LIMIT = 56 * 1024 * 1024
Q_SCALE = HEAD_DIM ** -0.5

OFF_KC, OFF_KA, OFF_VA, OFF_QA = 0, 768, 896, 1024
OFF_KBC, OFF_VBC, OFF_KBS, OFF_KBW = 1536, 1664, 1792, 1920
ROW_WIDTH = 2048
OFF_QBT, OFF_QCT, OFF_VCT, OFF_VBST, OFF_VBWT = 0, 768, 1536, 2304, 2432
T_WIDTH = 2560
TAIL_FC = N_GATE_COLS
ONE_LANE = LANES - 1
N_AUG = HEAD_DIM


def _alibi(n):
    return [float(2.0 ** (-8.0 * i / n)) for i in range(1, n + 1)]


SLOPES_A = _alibi(A_HEADS)
SLOPES_B = _alibi(B_HEADS)


def _cparams(sem):
    return pltpu.CompilerParams(dimension_semantics=sem, vmem_limit_bytes=VMEM_LIMIT)


def _dot(a, b):
    return jnp.dot(a, b, preferred_element_type=F32)


def _dot_nt(a, b):
    return lax.dot_general(a, b, (((1,), (1,)), ((), ())), preferred_element_type=F32)


def _split3(x):
    hi = x.astype(BF16)
    r1 = x - hi.astype(F32)
    mid = r1.astype(BF16)
    lo = (r1 - mid.astype(F32)).astype(BF16)
    return hi, mid, lo


def _rms(x, gain):
    return x * lax.rsqrt(jnp.mean(x * x, axis=-1, keepdims=True) + RMS_EPS) * gain


def _sigmoid(x):
    return 1.0 / (1.0 + jnp.exp(-x))


def _norm_kernel(x_ref, g_ref, o_ref):
    o_ref[...] = _rms(x_ref[...], g_ref[...]).astype(o_ref.dtype)


def rmsnorm_rows(x, gain, out_dtype, tm=512):
    n, d = x.shape
    return pl.pallas_call(
        _norm_kernel,
        grid=(n // tm,),
        in_specs=[pl.BlockSpec((tm, d), lambda i: (i, 0)), pl.BlockSpec((1, d), lambda i: (0, 0))],
        out_specs=pl.BlockSpec((tm, d), lambda i: (i, 0)),
        out_shape=jax.ShapeDtypeStruct((n, d), out_dtype),
        compiler_params=_cparams(("parallel",)),
        name="rmsnorm",
    )(x, gain.reshape(1, d))


def _norm_matmul_kernel(x_ref, g_ref, w_ref, o_ref, h_sc, *, transposed):
    @pl.when(pl.program_id(1) == 0)
    def _():
        h_sc[...] = _rms(x_ref[...], g_ref[...]).astype(BF16)

    if transposed:
        o_ref[...] = _dot_nt(w_ref[...], h_sc[...]).astype(o_ref.dtype)
    else:
        o_ref[...] = _dot(h_sc[...], w_ref[...]).astype(o_ref.dtype)


def norm_matmul(x, gain, w, out_dtype, transposed=False, tm=512, tn=512):
    n, d = x.shape
    nn = w.shape[0] if transposed else w.shape[1]
    tn = min(tn, nn)
    if transposed:
        w_spec = pl.BlockSpec((tn, d), lambda i, j: (j, 0))
        o_spec = pl.BlockSpec((tn, tm), lambda i, j: (j, i))
        o_shape = (nn, n)
    else:
        w_spec = pl.BlockSpec((d, tn), lambda i, j: (0, j))
        o_spec = pl.BlockSpec((tm, tn), lambda i, j: (i, j))
        o_shape = (n, nn)
    return pl.pallas_call(
        functools.partial(_norm_matmul_kernel, transposed=transposed),
        grid=(n // tm, nn // tn),
        in_specs=[pl.BlockSpec((tm, d), lambda i, j: (i, 0)),
                  pl.BlockSpec((1, d), lambda i, j: (0, 0)),
                  w_spec],
        out_specs=o_spec,
        out_shape=jax.ShapeDtypeStruct(o_shape, out_dtype),
        scratch_shapes=[pltpu.VMEM((tm, d), BF16)],
        compiler_params=_cparams(("parallel", "arbitrary")),
        name="norm_matmul_t" if transposed else "norm_matmul",
    )(x, gain.reshape(1, d), w)


def _swa_kernel(sink_ref, q_ref, kp_ref, kc_ref, vp_ref, vc_ref, o_ref):
    n = pl.program_id(1)
    tq = q_ref.shape[0]
    q = q_ref[...]
    k = jnp.concatenate([kp_ref[...], kc_ref[...]], axis=0)
    v = jnp.concatenate([vp_ref[...], vc_ref[...]], axis=0)
    i = lax.broadcasted_iota(jnp.int32, (tq, 2 * tq), 0)
    j = lax.broadcasted_iota(jnp.int32, (tq, 2 * tq), 1)
    rel = tq + i - j
    mask = (rel >= 0) & (rel < WINDOW_A) & ((j >= tq) | (n > 0))
    relf = rel.astype(F32)
    group = A_HEADS // A_KV_HEADS
    for h in range(A_HEADS):
        kv = h // group
        qh = q[:, h * HEAD_DIM:(h + 1) * HEAD_DIM]
        kh = k[:, kv * HEAD_DIM:(kv + 1) * HEAD_DIM]
        vh = v[:, kv * HEAD_DIM:(kv + 1) * HEAD_DIM]
        s = _dot_nt(qh, kh) * Q_SCALE - SLOPES_A[h] * relf
        s = jnp.where(mask, s, NEG_INF)
        sink = sink_ref[h]
        m = jnp.maximum(jnp.max(s, axis=-1, keepdims=True), sink)
        p = jnp.exp(s - m)
        l = jnp.sum(p, axis=-1, keepdims=True) + jnp.exp(sink - m)
        o_ref[:, h * HEAD_DIM:(h + 1) * HEAD_DIM] = _dot(p.astype(BF16), vh) / l


def swa_attention(proj, sinks, batch, seq):
    tq = WINDOW_A
    nb = seq // tq
    kcol, vcol = OFF_KA // KV_WIDTH, OFF_VA // KV_WIDTH

    def cur(col):
        return pl.BlockSpec((tq, KV_WIDTH), lambda b, n: (b * nb + n, col))

    def prev(col):
        return pl.BlockSpec((tq, KV_WIDTH), lambda b, n: (b * nb + jnp.maximum(n - 1, 0), col))

    return pl.pallas_call(
        _swa_kernel,
        grid=(batch, nb),
        in_specs=[pl.BlockSpec(memory_space=pltpu.SMEM),
                  pl.BlockSpec((tq, A_WIDTH), lambda b, n: (b * nb + n, OFF_QA // A_WIDTH)),
                  prev(kcol), cur(kcol), prev(vcol), cur(vcol)],
        out_specs=pl.BlockSpec((tq, A_WIDTH), lambda b, n: (b * nb + n, 0)),
        out_shape=jax.ShapeDtypeStruct((batch * seq, A_WIDTH), F32),
        compiler_params=_cparams(("parallel", "parallel")),
        name="swa_attention",
    )(sinks, proj, proj, proj, proj, proj)


def _fox_aug_kernel(z_ref, b_ref, pk_ref, pq_ref, ka_ref, qa_ref, carry_sc):
    @pl.when(pl.program_id(1) == 0)
    def _():
        carry_sc[...] = jnp.zeros_like(carry_sc)

    z = z_ref[...] + b_ref[...]
    log_f = -(jnp.maximum(-z, 0.0) + jnp.log1p(jnp.exp(-jnp.abs(z))))
    ts = z.shape[0]
    r = lax.broadcasted_iota(jnp.int32, (ts, ts), 0)
    c = lax.broadcasted_iota(jnp.int32, (ts, ts), 1)
    tri = (c <= r).astype(BF16)
    hi, mid, lo = _split3(log_f)
    cum = _dot(tri, hi) + _dot(tri, mid) + _dot(tri, lo) + carry_sc[0:1, :]
    carry_sc[0:1, :] = cum[ts - 1:ts, :]
    hi, mid, lo = _split3(cum)
    lane = lax.broadcasted_iota(jnp.int32, (ts, LANES), 1)
    hi = jnp.where(lane == ONE_LANE, 1.0, hi).astype(BF16)
    ka_ref[...] = (_dot(hi, pk_ref[0]) + _dot(mid, pk_ref[1]) + _dot(lo, pk_ref[2])).astype(BF16)
    qa_ref[...] = (_dot_nt(pq_ref[0], hi) + _dot_nt(pq_ref[1], mid) + _dot_nt(pq_ref[2], lo)).astype(BF16)


def _aug_lane(h):
    return HEAD_DIM if h % 2 == 0 else 0


def _fox_placement():
    pk = np.zeros((3, LANES, C_HEADS * LANES), np.float32)
    pq = np.zeros((3, C_HEADS * N_AUG, LANES), np.float32)
    for h in range(C_HEADS):
        src, kbase, qbase = TAIL_FC + h, h * LANES + _aug_lane(h), h * N_AUG
        for piece in range(3):
            pk[piece, src, kbase + piece] = -1.0
            pq[piece, qbase + 3 + piece, src] = 1.0
            pk[0, ONE_LANE, kbase + 3 + piece] = 1.0
            pq[0, qbase + piece, ONE_LANE] = 1.0
    return jnp.asarray(pk, BF16), jnp.asarray(pq, BF16)


def fox_augmentation(tail, bias_row, batch, seq, ts=512):
    nt = seq // ts
    pk, pq = _fox_placement()
    kw, qw = C_HEADS * LANES, C_HEADS * N_AUG
    return pl.pallas_call(
        _fox_aug_kernel,
        grid=(batch, nt),
        in_specs=[pl.BlockSpec((ts, LANES), lambda b, i: (b * nt + i, 0)),
                  pl.BlockSpec((1, LANES), lambda b, i: (0, 0)),
                  pl.BlockSpec((3, LANES, kw), lambda b, i: (0, 0, 0)),
                  pl.BlockSpec((3, qw, LANES), lambda b, i: (0, 0, 0))],
        out_specs=[pl.BlockSpec((ts, kw), lambda b, i: (b * nt + i, 0)),
                   pl.BlockSpec((qw, ts), lambda b, i: (0, b * nt + i))],
        out_shape=[jax.ShapeDtypeStruct((batch * seq, kw), BF16),
                   jax.ShapeDtypeStruct((qw, batch * seq), BF16)],
        scratch_shapes=[pltpu.VMEM((8, LANES), F32)],
        compiler_params=_cparams(("parallel", "arbitrary")),
        name="fox_augmentation",
    )(tail, bias_row, pk, pq)


def _alibi_augmentation(seq):
    pos = np.arange(seq)
    ka = np.zeros((seq, LANES), np.float32)
    for base in (0, HEAD_DIM):
        ka[:, base:base + 3] = 1.0
        ka[:, base + 3:base + 6] = ((pos >> 8) << 8)[:, None]
        ka[:, base + 6:base + 9] = (pos & 255)[:, None]
    slopes = jnp.asarray(SLOPES_B, F32)
    st = _split3(-(slopes[:, None] * jnp.asarray(pos, F32)[None, :]))
    sl = _split3(slopes)
    qa = jnp.zeros((B_HEADS, N_AUG, seq), BF16)
    for piece in range(3):
        qa = qa.at[:, piece, :].set(st[piece])
        qa = qa.at[:, 3 + piece, :].set(sl[piece][:, None])
        qa = qa.at[:, 6 + piece, :].set(sl[piece][:, None])
    return jnp.asarray(ka, BF16), qa.reshape(B_HEADS * N_AUG, seq)


def _flash_t_kernel(*refs, mode, tq, tk, n_steps):
    if mode == "slc":
        k_ref, ka_ref, qt_ref, qat_ref, vt_ref, selt_ref, exp_ref, o_ref, m_sc, l_sc, acc_sc = refs
    else:
        k_ref, ka_ref, qt_ref, qat_ref, vt_ref, o_ref, m_sc, l_sc, acc_sc = refs
    n_heads = qt_ref.shape[0] // HEAD_DIM
    group = 1 if mode == "fox" else B_GROUP
    iq, j = pl.program_id(1), pl.program_id(2)
    if mode == "win":
        ik = iq - (n_steps - 1) + j
        last_j = n_steps - 1
        active = ik >= 0
    else:
        ik = j
        last_j = (iq * tq + tq - 1) // tk
        active = j <= last_j

    @pl.when(j == 0)
    def _():
        m_sc[...] = jnp.full_like(m_sc, M_INIT)
        l_sc[...] = jnp.zeros_like(l_sc)
        acc_sc[...] = jnp.zeros_like(acc_sc)

    def step(masked):
        ok = None
        if masked:
            rel = ((iq * tq + lax.broadcasted_iota(jnp.int32, (tk, tq), 1))
                   - (ik * tk + lax.broadcasted_iota(jnp.int32, (tk, tq), 0)))
            ok = rel >= 0
            if mode == "win":
                ok = ok & (rel < WINDOW_B)
        lane = lax.broadcasted_iota(jnp.int32, (tk, LANES), 1)
        ka = None
        for h in range(n_heads):
            kv = h // group
            rows = slice(h * HEAD_DIM, (h + 1) * HEAD_DIM)
            if h % group == 0:
                pair = slice((kv // 2) * LANES, (kv // 2 + 1) * LANES)
                aug = ka_ref[:, kv * LANES:(kv + 1) * LANES] if mode == "fox" else ka_ref[...]
                own = (lane < HEAD_DIM) if kv % 2 == 0 else (lane >= HEAD_DIM)
                ka = jnp.where(own, k_ref[:, pair], aug)
                ok_h = ok
                if mode == "slc":
                    ok_h = ok & (_dot(exp_ref[...], selt_ref[0, kv]) > 0.5)
            halves = [qt_ref[rows, :], qat_ref[rows, :]]
            qa = jnp.concatenate(halves if kv % 2 == 0 else halves[::-1], axis=0)
            s = _dot(ka, qa)
            if ok_h is not None:
                s = jnp.where(ok_h, s, NEG_INF)
            m_prev = m_sc[h:h + 1, :]
            m_new = jnp.maximum(m_prev, jnp.max(s, axis=0, keepdims=True))
            alpha = jnp.exp(m_prev - m_new)
            p = jnp.exp(s - m_new)
            l_sc[h:h + 1, :] = alpha * l_sc[h:h + 1, :] + jnp.sum(p, axis=0, keepdims=True)
            vrows = slice(kv * HEAD_DIM, (kv + 1) * HEAD_DIM)
            acc_sc[rows, :] = alpha * acc_sc[rows, :] + _dot(vt_ref[vrows, :], p.astype(BF16))
            m_sc[h:h + 1, :] = m_new

    if mode == "fox":
        @pl.when(active & (ik * tk + tk - 1 > iq * tq))
        def _():
            step(True)

        @pl.when(active & (ik * tk + tk - 1 <= iq * tq))
        def _():
            step(False)
    else:
        @pl.when(active)
        def _():
            step(True)

    @pl.when(j == last_j)
    def _():
        for pair in range(n_heads // 2):
            rows = slice(pair * LANES, (pair + 1) * LANES)
            denom = jnp.concatenate(
                [jnp.broadcast_to(l_sc[2 * pair + i:2 * pair + i + 1, :], (HEAD_DIM, tq)) for i in range(2)], axis=0)
            o_ref[:, rows] = (acc_sc[rows, :] / denom).T


def flash_attention_t(mode, proj_r, proj_t, k_aug, q_aug_t, batch, seq, sel_t=None, expand=None, tq=512, tk=512):
    nq, nk = seq // tq, seq // tk
    if mode == "win":
        assert tq == tk
        n_steps = WINDOW_B // tk + 1

        def kblock(iq, j):
            return jnp.maximum(iq - (n_steps - 1) + j, 0)
    else:
        n_steps = nk

        def kblock(iq, j):
            return jnp.minimum(j, (iq * tq + tq - 1) // tk)

    if mode == "fox":
        in_specs = [pl.BlockSpec((tk, C_WIDTH), lambda b, iq, j: (b * nk + kblock(iq, j), OFF_KC // C_WIDTH)),
                    pl.BlockSpec((tk, C_HEADS * LANES), lambda b, iq, j: (b * nk + kblock(iq, j), 0)),
                    pl.BlockSpec((C_WIDTH, tq), lambda b, iq, j: (OFF_QCT // C_WIDTH, b * nq + iq)),
                    pl.BlockSpec((C_WIDTH, tq), lambda b, iq, j: (0, b * nq + iq)),
                    pl.BlockSpec((C_WIDTH, tk), lambda b, iq, j: (OFF_VCT // C_WIDTH, b * nk + kblock(iq, j)))]
    else:
        kcol = (OFF_KBS if mode == "slc" else OFF_KBW) // KV_WIDTH
        vrow = (OFF_VBST if mode == "slc" else OFF_VBWT) // KV_WIDTH
        in_specs = [pl.BlockSpec((tk, KV_WIDTH), lambda b, iq, j: (b * nk + kblock(iq, j), kcol)),
                    pl.BlockSpec((tk, LANES), lambda b, iq, j: (kblock(iq, j), 0)),
                    pl.BlockSpec((B_WIDTH, tq), lambda b, iq, j: (OFF_QBT // B_WIDTH, b * nq + iq)),
                    pl.BlockSpec((B_WIDTH, tq), lambda b, iq, j: (0, iq)),
                    pl.BlockSpec((KV_WIDTH, tk), lambda b, iq, j: (vrow, b * nk + kblock(iq, j)))]
    args = [proj_r, k_aug, proj_t, q_aug_t, proj_t]
    if mode == "slc":
        n_slc = seq // SEL_BLOCK
        in_specs += [pl.BlockSpec((1, B_KV_HEADS, n_slc, tq), lambda b, iq, j: (b, 0, 0, iq)),
                     pl.BlockSpec((tk, n_slc), lambda b, iq, j: (kblock(iq, j), 0))]
        args += [sel_t, expand]
    width = C_WIDTH if mode == "fox" else B_WIDTH
    n_heads = width // HEAD_DIM
    return pl.pallas_call(
        functools.partial(_flash_t_kernel, mode=mode, tq=tq, tk=tk, n_steps=n_steps),
        grid=(batch, nq, n_steps),
        in_specs=in_specs,
        out_specs=pl.BlockSpec((tq, width), lambda b, iq, j: (b * nq + iq, 0)),
        out_shape=jax.ShapeDtypeStruct((batch * seq, width), F32),
        scratch_shapes=[pltpu.VMEM((16, tq), F32), pltpu.VMEM((16, tq), F32),
                        pltpu.VMEM((n_heads * HEAD_DIM, tq), F32)],
        compiler_params=_cparams(("parallel", "parallel", "arbitrary")),
        name=mode + "_attention",
    )(*args)


def _compress_kernel(t_ref, pe_ref, w1_ref, w2_ref, w2t_ref, o_ref, ot_ref):
    w1 = w1_ref[0]
    hid = _dot(t_ref[0, 0, 0], w1) + _dot(pe_ref[0], w1)[0:1, :]
    act = (hid * _sigmoid(hid)).astype(BF16)
    o_ref[0, 0, 0] = _dot(act, w2_ref[0]).astype(o_ref.dtype)
    ot_ref[0, 0, 0] = _dot_nt(w2t_ref[0], act).astype(ot_ref.dtype)


def nsa_compress(flat, pe_rows, w1, w2):
    _, batch, n_kv, n_chunks, width = flat.shape
    hidden = w1.shape[-1]
    return pl.pallas_call(
        _compress_kernel,
        grid=(2, batch, n_kv),
        in_specs=[pl.BlockSpec((1, 1, 1, n_chunks, width), lambda s, b, h: (s, b, h, 0, 0)),
                  pl.BlockSpec((1, 8, width), lambda s, b, h: (s, 0, 0)),
                  pl.BlockSpec((1, width, hidden), lambda s, b, h: (s, 0, 0)),
                  pl.BlockSpec((1, hidden, HEAD_DIM), lambda s, b, h: (s, 0, 0)),
                  pl.BlockSpec((1, HEAD_DIM, hidden), lambda s, b, h: (s, 0, 0))],
        out_specs=[pl.BlockSpec((1, 1, 1, n_chunks, HEAD_DIM), lambda s, b, h: (s, b, h, 0, 0)),
                   pl.BlockSpec((1, 1, 1, HEAD_DIM, n_chunks), lambda s, b, h: (s, b, h, 0, 0))],
        out_shape=[jax.ShapeDtypeStruct((2, batch, n_kv, n_chunks, HEAD_DIM), BF16),
                   jax.ShapeDtypeStruct((2, batch, n_kv, HEAD_DIM, n_chunks), BF16)],
        compiler_params=_cparams(("parallel", "parallel", "parallel")),
        name="nsa_compress",
    )(flat, pe_rows, w1, w2, jnp.swapaxes(w2, 1, 2))


def _cmp_attn_t_kernel(qt_ref, kc_ref, vct_ref, inter_ref, o_ref, selt_ref, ot_sc, *, tq):
    iq = pl.program_id(1)
    n_chunks = kc_ref.shape[3]
    n_slc = inter_ref.shape[0]
    t = iq * tq + lax.broadcasted_iota(jnp.int32, (n_chunks, tq), 1)
    n = lax.broadcasted_iota(jnp.int32, (n_chunks, tq), 0)
    rel = t - (n * CMP_STRIDE + CMP_LEN - 1)
    mask = rel >= 0
    relf = rel.astype(F32)
    t_s = iq * tq + lax.broadcasted_iota(jnp.int32, (n_slc, tq), 1)
    jj = lax.broadcasted_iota(jnp.int32, (n_slc, tq), 0)
    cur = t_s >> (SEL_BLOCK.bit_length() - 1)
    valid = jj * SEL_BLOCK <= t_s
    forced = (jj == 0) | (jj == cur) | (jj == cur - 1)
    jf = jj.astype(F32)
    inter = inter_ref[...]
    for h in range(B_KV_HEADS):
        kc = kc_ref[0, 0, h]
        vct = vct_ref[0, 0, h]
        p_sum = jnp.zeros((n_chunks, tq), F32)
        for g in range(B_GROUP):
            hd = h * B_GROUP + g
            rows = slice(hd * HEAD_DIM, (hd + 1) * HEAD_DIM)
            s = jnp.where(mask, _dot(kc, qt_ref[rows, :]) - SLOPES_B[hd] * relf, NEG_INF)
            m = jnp.max(s, axis=0, keepdims=True)
            e = jnp.where(mask, jnp.exp(s - m), 0.0)
            l = jnp.sum(e, axis=0, keepdims=True)
            p = e / jnp.where(l > 0.0, l, 1.0)
            p_sum = p_sum + p
            ot_sc[rows, :] = _dot(vct, p.astype(BF16))
        hi, mid, lo = _split3(p_sum)
        importance = _dot(inter, hi) + _dot(inter, mid) + _dot(inter, lo)
        score = jnp.where(valid, importance + jnp.where(forced, FORCE_BONUS, 0.0), NEG_INF)
        chosen = jnp.zeros((n_slc, tq), F32)
        for _ in range(min(N_SELECT, n_slc)):
            best = jnp.max(score, axis=0, keepdims=True)
            first = jnp.min(jnp.where(score == best, jf, float(n_slc)), axis=0, keepdims=True)
            pick = jf == first
            chosen = jnp.where(pick, 1.0, chosen)
            score = jnp.where(pick, -jnp.inf, score)
        selt_ref[0, h] = jnp.where(valid, chosen, 0.0).astype(selt_ref.dtype)
    for pair in range(B_HEADS // 2):
        rows = slice(pair * LANES, (pair + 1) * LANES)
        o_ref[:, rows] = ot_sc[rows, :].T


def nsa_compressed_attention(proj_t, kvc, kvct, inter_t, batch, seq, tq=512):
    nq = seq // tq
    n_chunks = kvc.shape[3]
    n_slc = seq // SEL_BLOCK
    return pl.pallas_call(
        functools.partial(_cmp_attn_t_kernel, tq=tq),
        grid=(batch, nq),
        in_specs=[pl.BlockSpec((B_WIDTH, tq), lambda b, i: (OFF_QBT // B_WIDTH, b * nq + i)),
                  pl.BlockSpec((1, 1, B_KV_HEADS, n_chunks, HEAD_DIM), lambda b, i: (0, b, 0, 0, 0)),
                  pl.BlockSpec((1, 1, B_KV_HEADS, HEAD_DIM, n_chunks), lambda b, i: (1, b, 0, 0, 0)),
                  pl.BlockSpec((n_slc, n_chunks), lambda b, i: (0, 0))],
        out_specs=[pl.BlockSpec((tq, B_WIDTH), lambda b, i: (b * nq + i, 0)),
                   pl.BlockSpec((1, B_KV_HEADS, n_slc, tq), lambda b, i: (b, 0, 0, i))],
        out_shape=[jax.ShapeDtypeStruct((batch * seq, B_WIDTH), F32),
                   jax.ShapeDtypeStruct((batch, B_KV_HEADS, n_slc, seq), BF16)],
        scratch_shapes=[pltpu.VMEM((B_WIDTH, tq), F32)],
        compiler_params=_cparams(("parallel", "parallel")),
        name="nsa_cmp_attention",
    )(proj_t, kvc, kvct, inter_t)


def _mix_out_kernel(oa_ref, ocmp_ref, oslc_ref, owin_ref, oc_ref, tail_ref, g_ref, w_ref, x_ref, o_ref, mixed_sc):
    @pl.when(pl.program_id(1) == 0)
    def _():
        gates = _sigmoid(tail_ref[...])
        parts = []
        for hd in range(B_HEADS):
            sl = slice(hd * HEAD_DIM, (hd + 1) * HEAD_DIM)
            parts.append(gates[:, 3 * hd:3 * hd + 1] * ocmp_ref[:, sl]
                         + gates[:, 3 * hd + 1:3 * hd + 2] * oslc_ref[:, sl]
                         + gates[:, 3 * hd + 2:3 * hd + 3] * owin_ref[:, sl])
        o_b = jnp.concatenate(parts, axis=-1)
        b0, c0 = A_WIDTH, A_WIDTH + B_WIDTH
        mixed_sc[:, 0:b0] = _rms(oa_ref[...], g_ref[:, 0:b0]).astype(BF16)
        mixed_sc[:, b0:c0] = _rms(o_b, g_ref[:, b0:c0]).astype(BF16)
        mixed_sc[:, c0:] = _rms(oc_ref[...], g_ref[:, c0:]).astype(BF16)

    o_ref[...] = x_ref[...] + _dot(mixed_sc[...], w_ref[...])


def mix_out(o_a, o_cmp, o_slc, o_win, o_c, tail, gain, w_out, x, tm=256, tn=512):
    n, d = x.shape
    width = w_out.shape[0]

    def rows(w):
        return pl.BlockSpec((tm, w), lambda i, j: (i, 0))

    return pl.pallas_call(
        _mix_out_kernel,
        grid=(n // tm, d // tn),
        in_specs=[rows(A_WIDTH), rows(B_WIDTH), rows(B_WIDTH), rows(B_WIDTH), rows(C_WIDTH), rows(LANES),
                  pl.BlockSpec((1, width), lambda i, j: (0, 0)),
                  pl.BlockSpec((width, tn), lambda i, j: (0, j)),
                  pl.BlockSpec((tm, tn), lambda i, j: (i, j))],
        out_specs=pl.BlockSpec((tm, tn), lambda i, j: (i, j)),
        out_shape=jax.ShapeDtypeStruct((n, d), F32),
        scratch_shapes=[pltpu.VMEM((tm, width), BF16)],
        compiler_params=_cparams(("parallel", "arbitrary")),
        name="mix_out",
    )(o_a, o_cmp, o_slc, o_win, o_c, tail, gain.reshape(1, width), w_out, x)


def _fresh_weights(be_ref):
    i = pl.program_id(1)
    return (i == 0) | (be_ref[i] != be_ref[jnp.maximum(i - 1, 0)])


def _ffn_up_kernel(be_ref, nv_ref, x_ref, wg_ref, wu_ref, o_ref, wg_sc, wu_sc):
    i = pl.program_id(1)

    @pl.when(_fresh_weights(be_ref))
    def _():
        wg_sc[...] = wg_ref[0].astype(BF16)
        wu_sc[...] = wu_ref[0].astype(BF16)

    @pl.when(i < nv_ref[0])
    def _():
        x = x_ref[...].astype(BF16)
        gate = _dot(x, wg_sc[...])
        up = _dot(x, wu_sc[...])
        o_ref[...] = (gate * _sigmoid(gate) * up).astype(o_ref.dtype)

    @pl.when(i >= nv_ref[0])
    def _():
        o_ref[...] = jnp.zeros_like(o_ref)


def _ffn_down_kernel(be_ref, nv_ref, h_ref, wd_ref, *rest):
    o_ref, wd_sc = rest[-2], rest[-1]
    i = pl.program_id(1)

    @pl.when(_fresh_weights(be_ref))
    def _():
        wd_sc[...] = wd_ref[0].astype(BF16)

    @pl.when(i < nv_ref[0])
    def _():
        y = _dot(h_ref[...], wd_sc[...])
        if len(rest) == 3:
            y = y + rest[0][...]
        o_ref[...] = y

    @pl.when(i >= nv_ref[0])
    def _():
        o_ref[...] = jnp.zeros_like(o_ref)


def grouped_swiglu(xs, block_expert, n_valid, w_gate, w_up, w_down, residual=None, tm=256, tf=512, tn=512):
    rows, d = xs.shape
    f = w_gate.shape[-1]
    nblk = rows // tm

    def xrow(i, nv):
        return jnp.minimum(i, nv[0] - 1)

    hidden = pl.pallas_call(
        _ffn_up_kernel,
        grid_spec=pltpu.PrefetchScalarGridSpec(
            num_scalar_prefetch=2,
            grid=(f // tf, nblk),
            in_specs=[pl.BlockSpec((tm, d), lambda j, i, be, nv: (xrow(i, nv), 0)),
                      pl.BlockSpec((1, d, tf), lambda j, i, be, nv: (be[i], 0, j)),
                      pl.BlockSpec((1, d, tf), lambda j, i, be, nv: (be[i], 0, j))],
            out_specs=pl.BlockSpec((tm, tf), lambda j, i, be, nv: (i, j)),
            scratch_shapes=[pltpu.VMEM((d, tf), BF16), pltpu.VMEM((d, tf), BF16)]),
        out_shape=jax.ShapeDtypeStruct((rows, f), BF16),
        compiler_params=_cparams(("parallel", "arbitrary")),
        name="ffn_up",
    )(block_expert, n_valid, xs, w_gate, w_up)

    in_specs = [pl.BlockSpec((tm, f), lambda j, i, be, nv: (xrow(i, nv), 0)),
                pl.BlockSpec((1, f, tn), lambda j, i, be, nv: (be[i], 0, j))]
    args = [hidden, w_down]
    if residual is not None:
        in_specs.append(pl.BlockSpec((tm, tn), lambda j, i, be, nv: (i, j)))
        args.append(residual)
    return pl.pallas_call(
        _ffn_down_kernel,
        grid_spec=pltpu.PrefetchScalarGridSpec(
            num_scalar_prefetch=2,
            grid=(d // tn, nblk),
            in_specs=in_specs,
            out_specs=pl.BlockSpec((tm, tn), lambda j, i, be, nv: (i, j)),
            scratch_shapes=[pltpu.VMEM((f, tn), BF16)]),
        out_shape=jax.ShapeDtypeStruct((rows, d), F32),
        compiler_params=_cparams(("parallel", "arbitrary")),
        name="ffn_down",
    )(block_expert, n_valid, *args)


def _router_kernel(x_ref, g_ref, wr_ref, h_ref, ri_ref, rf_ref, cnt_ref, carry_sc):
    @pl.when(pl.program_id(0) == 0)
    def _():
        carry_sc[...] = jnp.zeros_like(carry_sc)

    tm = x_ref.shape[0]
    h = _rms(x_ref[...], g_ref[...])
    h_ref[...] = h
    h1, h2, h3 = _split3(h)
    w1, w2, w3 = wr_ref[0], wr_ref[1], wr_ref[2]
    logits = (_dot(h1, w1) + _dot(h1, w2) + _dot(h2, w1)) + (_dot(h1, w3) + _dot(h2, w2) + _dot(h3, w1))
    lane_i = lax.broadcasted_iota(jnp.int32, (tm, LANES), 1)
    lane = lane_i.astype(F32)
    logits = jnp.where(lane_i < N_EXPERTS, logits, -jnp.inf)
    v1 = jnp.max(logits, axis=-1, keepdims=True)
    e1 = jnp.min(jnp.where(logits == v1, lane, float(LANES)), axis=-1, keepdims=True)
    rest = jnp.where(lane == e1, -jnp.inf, logits)
    v2 = jnp.max(rest, axis=-1, keepdims=True)
    e2 = jnp.min(jnp.where(rest == v2, lane, float(LANES)), axis=-1, keepdims=True)
    z = jnp.exp(v2 - v1)
    g1 = 1.0 / (1.0 + z)
    g2 = z / (1.0 + z)
    chosen = (lane == e1) | (lane == e2)
    onehot = jnp.where(chosen, 1.0, 0.0)
    r = lax.broadcasted_iota(jnp.int32, (tm, tm), 0)
    c = lax.broadcasted_iota(jnp.int32, (tm, tm), 1)
    before = _dot((c < r).astype(BF16), onehot.astype(BF16)) + carry_sc[0:1, :]
    r1 = jnp.sum(jnp.where(lane == e1, before, 0.0), axis=-1, keepdims=True)
    r2 = jnp.sum(jnp.where(lane == e2, before, 0.0), axis=-1, keepdims=True)
    carry = carry_sc[0:1, :] + jnp.sum(onehot, axis=0, keepdims=True)
    carry_sc[0:1, :] = carry
    cnt_ref[...] = jnp.broadcast_to(carry, cnt_ref.shape)
    packed = jnp.where(lane_i == 0, e1, jnp.where(lane_i == 1, e2, jnp.where(
        lane_i == 2, r1, jnp.where(lane_i == 3, r2, 0.0))))
    ri_ref[...] = packed.astype(jnp.int32)
    rf_ref[...] = jnp.where(lane_i == 0, g1, jnp.where(lane_i == 1, g2, 0.0))


def moe_route(x, gain, w_router3, tm=512):
    n, d = x.shape
    return pl.pallas_call(
        _router_kernel,
        grid=(n // tm,),
        in_specs=[pl.BlockSpec((tm, d), lambda i: (i, 0)),
                  pl.BlockSpec((1, d), lambda i: (0, 0)),
                  pl.BlockSpec((3, d, LANES), lambda i: (0, 0, 0))],
        out_specs=[pl.BlockSpec((tm, d), lambda i: (i, 0)),
                   pl.BlockSpec((tm, LANES), lambda i: (i, 0)),
                   pl.BlockSpec((tm, LANES), lambda i: (i, 0)),
                   pl.BlockSpec((8, LANES), lambda i: (0, 0))],
        out_shape=[jax.ShapeDtypeStruct((n, d), F32),
                   jax.ShapeDtypeStruct((n, LANES), jnp.int32),
                   jax.ShapeDtypeStruct((n, LANES), F32),
                   jax.ShapeDtypeStruct((8, LANES), F32)],
        scratch_shapes=[pltpu.VMEM((8, LANES), F32)],
        compiler_params=_cparams(("arbitrary",)),
        name="moe_route",
    )(x, gain.reshape(1, d), w_router3)


def _row_copy(src_ref, src_row, dst_ref, dst_row, sem):
    return pltpu.make_async_copy(src_ref.at[pl.ds(src_row, 1)], dst_ref.at[pl.ds(dst_row, 1)], sem)


def _dispatch_kernel(dest_ref, h_ref, zeros_ref, xs_ref, sem):
    del zeros_ref
    td = h_ref.shape[0]
    base = pl.program_id(0) * td

    def start(r, carry):
        for k in range(2):
            _row_copy(h_ref, r, xs_ref, dest_ref[2 * (base + r) + k], sem).start()
        return carry

    def wait(r, carry):
        for k in range(2):
            _row_copy(h_ref, r, xs_ref, dest_ref[2 * (base + r) + k], sem).wait()
        return carry

    lax.fori_loop(0, td, start, 0)
    lax.fori_loop(0, td, wait, 0)


def moe_dispatch(h, dest, cap, td=128):
    n, d = h.shape
    return pl.pallas_call(
        _dispatch_kernel,
        grid_spec=pltpu.PrefetchScalarGridSpec(
            num_scalar_prefetch=1,
            grid=(n // td,),
            in_specs=[pl.BlockSpec((td, d), lambda i, dest: (i, 0)),
                      pl.BlockSpec(memory_space=pl.ANY)],
            out_specs=pl.BlockSpec(memory_space=pl.ANY),
            scratch_shapes=[pltpu.SemaphoreType.DMA(())]),
        out_shape=jax.ShapeDtypeStruct((cap, d), h.dtype),
        input_output_aliases={2: 0},
        compiler_params=_cparams(("arbitrary",)),
        name="moe_dispatch",
    )(dest, h, jnp.zeros((cap, d), h.dtype))


def _combine_kernel(dest_ref, x_ref, gate_ref, y_ref, o_ref, ya_sc, yb_sc, sem):
    tc = x_ref.shape[0]
    base = pl.program_id(0) * tc

    def start(r, carry):
        _row_copy(y_ref, dest_ref[2 * (base + r)], ya_sc, r, sem).start()
        _row_copy(y_ref, dest_ref[2 * (base + r) + 1], yb_sc, r, sem).start()
        return carry

    def wait(r, carry):
        _row_copy(y_ref, dest_ref[2 * (base + r)], ya_sc, r, sem).wait()
        _row_copy(y_ref, dest_ref[2 * (base + r) + 1], yb_sc, r, sem).wait()
        return carry

    lax.fori_loop(0, tc, start, 0)
    lax.fori_loop(0, tc, wait, 0)
    gates = gate_ref[...]
    o_ref[...] = x_ref[...] + (gates[:, 0:1] * ya_sc[...] + gates[:, 1:2] * yb_sc[...])


def moe_combine(x, gates, y, dest, tc=128):
    n, d = x.shape
    return pl.pallas_call(
        _combine_kernel,
        grid_spec=pltpu.PrefetchScalarGridSpec(
            num_scalar_prefetch=1,
            grid=(n // tc,),
            in_specs=[pl.BlockSpec((tc, d), lambda i, dest: (i, 0)),
                      pl.BlockSpec((tc, LANES), lambda i, dest: (i, 0)),
                      pl.BlockSpec(memory_space=pl.ANY)],
            out_specs=pl.BlockSpec((tc, d), lambda i, dest: (i, 0)),
            scratch_shapes=[pltpu.VMEM((tc, d), F32), pltpu.VMEM((tc, d), F32),
                            pltpu.SemaphoreType.DMA(())]),
        out_shape=jax.ShapeDtypeStruct((n, d), F32),
        compiler_params=_cparams(("arbitrary",)),
        name="moe_combine",
    )(dest, x, gates, y)


def moe_layer(x, gain, w_router, w_gate, w_up, w_down, first_expert, tm=256):
    n, d = x.shape
    wr = jnp.pad(w_router.astype(F32), ((0, 0), (0, LANES - N_EXPERTS)))
    h, info, gates, counts = moe_route(x, gain, jnp.stack(_split3(wr)))
    counts = counts[0, :N_EXPERTS].astype(jnp.int32)
    padded = (counts + tm - 1) // tm * tm
    pad_ends = jnp.cumsum(padded)
    pad_starts = pad_ends - padded
    experts, ranks = info[:, 0:2], info[:, 2:4]
    dest = (pad_starts[experts] + ranks).reshape(-1).astype(jnp.int32)
    nblk = (2 * n) // tm + N_EXPERTS
    n_valid = (pad_ends[-1] // tm).astype(jnp.int32).reshape(1)
    blk = jnp.minimum(jnp.arange(nblk, dtype=jnp.int32), n_valid[0] - 1) * tm
    block_expert = jnp.minimum(jnp.sum(blk[:, None] >= pad_ends[None, :], axis=1), N_EXPERTS - 1).astype(jnp.int32)
    xs = moe_dispatch(h, dest, nblk * tm)
    y = grouped_swiglu(xs, block_expert + first_expert, n_valid, w_gate, w_up, w_down, tm=tm)
    return moe_combine(x, gates, y, dest)


def _project_weights(w):
    def cols(a, n):
        return w[:, a:a + n]

    qa, ka, va, qb = cols(0, 512), cols(512, 128), cols(640, 128), cols(768, 768)
    kbc, vbc, kbs, vbs, kbw, vbw = (cols(1536 + 128 * i, 128) for i in range(6))
    gb, qc, kc, vc, fc = cols(2304, 36), cols(2340, 768), cols(3108, 768), cols(3876, 768), cols(4644, 12)
    w_rows = jnp.concatenate([kc, ka, va, qa, kbc, vbc, kbs, kbw], axis=1).astype(BF16)
    w_t = jnp.concatenate([qb * Q_SCALE, qc * Q_SCALE, vc, vbs, vbw], axis=1).T.astype(BF16)
    tail = jnp.concatenate([gb, fc], axis=1)
    tail = jnp.pad(tail, ((0, 0), (0, LANES - tail.shape[1]))).astype(BF16)
    return w_rows, w_t, tail


def _chunk_blocks(cols, batch, seq):
    n_chunks = seq // CMP_STRIDE
    t = cols.reshape(batch, seq, B_KV_HEADS, HEAD_DIM).transpose(0, 2, 1, 3)
    t = t.reshape(batch, B_KV_HEADS, n_chunks, CMP_STRIDE * HEAD_DIM)
    nxt = jnp.concatenate([t[:, :, 1:], jnp.zeros_like(t[:, :, :1])], axis=2)
    return jnp.concatenate([t, nxt], axis=-1)


def _overlap_matrices(seq):
    n_chunks, n_slc = seq // CMP_STRIDE, seq // SEL_BLOCK
    c_start = np.arange(n_chunks) * CMP_STRIDE
    s_start = np.arange(n_slc) * SEL_BLOCK
    inter = np.maximum(np.minimum(c_start[:, None] + CMP_LEN, s_start[None, :] + SEL_BLOCK)
                       - np.maximum(c_start[:, None], s_start[None, :]), 0) / CMP_LEN
    inter[(seq - CMP_LEN) // CMP_STRIDE + 1:] = 0.0
    expand = (np.arange(seq)[:, None] // SEL_BLOCK == np.arange(n_slc)[None, :])
    return jnp.asarray(inter.T, BF16), jnp.asarray(expand, BF16)


def mixer_layer(x, batch, seq, w_in, w_out, norm_mix, mix_out_norm, sinks, cmp_pe, cmp_w1, cmp_w2, f_bias):
    w_rows, w_t, w_tail = _project_weights(w_in)
    proj_r = norm_matmul(x, norm_mix, w_rows, BF16)
    proj_t = norm_matmul(x, norm_mix, w_t, BF16, transposed=True)
    tail = norm_matmul(x, norm_mix, w_tail, F32)
    o_a = swa_attention(proj_r, sinks.astype(F32), batch, seq)

    bias_row = jnp.zeros((1, LANES), F32).at[0, TAIL_FC:TAIL_FC + C_HEADS].set(f_bias.astype(F32))
    fox_ka, fox_qat = fox_augmentation(tail, bias_row, batch, seq)
    o_c = flash_attention_t("fox", proj_r, proj_t, fox_ka, fox_qat, batch, seq)

    flat = jnp.stack([_chunk_blocks(proj_r[:, OFF_KBC:OFF_KBC + KV_WIDTH], batch, seq),
                      _chunk_blocks(proj_r[:, OFF_VBC:OFF_VBC + KV_WIDTH], batch, seq)])
    pe_rows = jnp.broadcast_to(cmp_pe.reshape(2, 1, CMP_LEN * HEAD_DIM), (2, 8, CMP_LEN * HEAD_DIM)).astype(BF16)
    kvc, kvct = nsa_compress(flat, pe_rows, cmp_w1.astype(BF16), cmp_w2.astype(BF16))
    inter_t, expand = _overlap_matrices(seq)
    o_cmp, sel_t = nsa_compressed_attention(proj_t, kvc, kvct, inter_t, batch, seq)
    pos_ka, pos_qat = _alibi_augmentation(seq)
    o_slc = flash_attention_t("slc", proj_r, proj_t, pos_ka, pos_qat, batch, seq, sel_t=sel_t, expand=expand)
    o_win = flash_attention_t("win", proj_r, proj_t, pos_ka, pos_qat, batch, seq)
    return mix_out(o_a, o_cmp, o_slc, o_win, o_c, tail, mix_out_norm, w_out.astype(BF16), x)


def dense_layer(x, gain, w_gate, w_up, w_down, index, tm=256):
    n = x.shape[0]
    h = rmsnorm_rows(x, gain, BF16)
    nblk = n // tm
    return grouped_swiglu(h, jnp.full((nblk,), index, jnp.int32), jnp.full((1,), nblk, jnp.int32),
                          w_gate, w_up, w_down, residual=x, tm=tm)


def kernel(x, w_in, w_out, norm_mix, mix_out_norm, attn_sinks, nsa_cmp_pe, nsa_cmp_w1, nsa_cmp_w2, fox_f_bias,
           norm_ffn, ffn_w_gate, ffn_w_up, ffn_w_down, moe_router, moe_w_gate, moe_w_up, moe_w_down, norm_final):
    batch, seq, d = x.shape
    depth = w_in.shape[0]
    f = ffn_w_gate.shape[-1]
    moe_gate, moe_up = moe_w_gate.reshape(-1, d, f), moe_w_up.reshape(-1, d, f)
    moe_down = moe_w_down.reshape(-1, f, d)
    xf = x.reshape(batch * seq, d).astype(F32)
    for layer in range(depth):
        xf = mixer_layer(xf, batch, seq, w_in[layer], w_out[layer], norm_mix[layer], mix_out_norm[layer],
                         attn_sinks[layer], nsa_cmp_pe[layer], nsa_cmp_w1[layer], nsa_cmp_w2[layer],
                         fox_f_bias[layer])
        i = layer // 2
        if layer % 2 == 0:
            xf = dense_layer(xf, norm_ffn[layer], ffn_w_gate, ffn_w_up, ffn_w_down, i)
        else:
            xf = moe_layer(xf, norm_ffn[layer], moe_router[i], moe_gate, moe_up, moe_down, i * N_EXPERTS)
    return rmsnorm_rows(xf, norm_final, x.dtype).reshape(batch, seq, d)
```

```python
import functools

import jax
import jax.numpy as jnp
import numpy as np
from jax import lax
from jax.experimental import pallas as pl
from jax.experimental.pallas import tpu as pltpu

F32 = jnp.float32
BF16 = jnp.bfloat16

HEAD_DIM = 64
A_HEADS, A_KV_HEADS = 8, 2
B_HEADS, B_KV_HEADS = 12, 2
C_HEADS = 12
B_GROUP = B_HEADS // B_KV_HEADS
A_WIDTH, B_WIDTH, C_WIDTH = A_HEADS * HEAD_DIM, B_HEADS * HEAD_DIM, C_HEADS * HEAD_DIM
KV_WIDTH = 2 * HEAD_DIM
N_GATE_COLS = B_HEADS * 3
WINDOW_A = 128
WINDOW_B = 512
CMP_LEN, CMP_STRIDE = 32, 16
SEL_BLOCK, N_SELECT = 64, 16
N_EXPERTS = 8
RMS_EPS = 1e-6
NEG_INF = -1e30
M_INIT = -1e20
FORCE_BONUS = 1e4
LANES = 128
VMEM_LIMIT = 56 * 1024 * 1024
Q_SCALE = HEAD_DIM ** -0.5

OFF_KC, OFF_KA, OFF_VA, OFF_QA = 0, 768, 896, 1024
OFF_KBC, OFF_VBC, OFF_KBS, OFF_KBW = 1536, 1664, 1792, 1920
ROW_WIDTH = 2048
OFF_QBT, OFF_QCT, OFF_VCT, OFF_VBST, OFF_VBWT = 0, 768, 1536, 2304, 2432
T_WIDTH = 2560
TAIL_FC = N_GATE_COLS
ONE_LANE = LANES - 1
N_AUG = HEAD_DIM


def _alibi(n):
    return [float(2.0 ** (-8.0 * i / n)) for i in range(1, n + 1)]


SLOPES_A = _alibi(A_HEADS)
SLOPES_B = _alibi(B_HEADS)


def _cparams(sem):
    return pltpu.CompilerParams(dimension_semantics=sem, vmem_limit_bytes=VMEM_LIMIT)


def _dot(a, b):
    return jnp.dot(a, b, preferred_element_type=F32)


def _dot_nt(a, b):
    return lax.dot_general(a, b, (((1,), (1,)), ((), ())), preferred_element_type=F32)


def _split3(x):
    hi = x.astype(BF16)
    r1 = x - hi.astype(F32)
    mid = r1.astype(BF16)
    lo = (r1 - mid.astype(F32)).astype(BF16)
    return hi, mid, lo


def _rms(x, gain):
    return x * lax.rsqrt(jnp.mean(x * x, axis=-1, keepdims=True) + RMS_EPS) * gain


def _sigmoid(x):
    return 1.0 / (1.0 + jnp.exp(-x))


def _norm_kernel(x_ref, g_ref, o_ref):
    o_ref[...] = _rms(x_ref[...], g_ref[...]).astype(o_ref.dtype)


def rmsnorm_rows(x, gain, out_dtype, tm=512):
    n, d = x.shape
    return pl.pallas_call(
        _norm_kernel,
        grid=(n // tm,),
        in_specs=[pl.BlockSpec((tm, d), lambda i: (i, 0)), pl.BlockSpec((1, d), lambda i: (0, 0))],
        out_specs=pl.BlockSpec((tm, d), lambda i: (i, 0)),
        out_shape=jax.ShapeDtypeStruct((n, d), out_dtype),
        compiler_params=_cparams(("parallel",)),
        name="rmsnorm",
    )(x, gain.reshape(1, d))


def _norm_matmul_kernel(x_ref, g_ref, w_ref, o_ref, h_sc, *, transposed):
    @pl.when(pl.program_id(1) == 0)
    def _():
        h_sc[...] = _rms(x_ref[...], g_ref[...]).astype(BF16)

    if transposed:
        o_ref[...] = _dot_nt(w_ref[...], h_sc[...]).astype(o_ref.dtype)
    else:
        o_ref[...] = _dot(h_sc[...], w_ref[...]).astype(o_ref.dtype)


def norm_matmul(x, gain, w, out_dtype, transposed=False, tm=1024, tn=512):
    n, d = x.shape
    nn = w.shape[0] if transposed else w.shape[1]
    tn = min(tn, nn)
    if transposed:
        w_spec = pl.BlockSpec((tn, d), lambda i, j: (j, 0))
        o_spec = pl.BlockSpec((tn, tm), lambda i, j: (j, i))
        o_shape = (nn, n)
    else:
        w_spec = pl.BlockSpec((d, tn), lambda i, j: (0, j))
        o_spec = pl.BlockSpec((tm, tn), lambda i, j: (i, j))
        o_shape = (n, nn)
    return pl.pallas_call(
        functools.partial(_norm_matmul_kernel, transposed=transposed),
        grid=(n // tm, nn // tn),
        in_specs=[pl.BlockSpec((tm, d), lambda i, j: (i, 0)),
                  pl.BlockSpec((1, d), lambda i, j: (0, 0)),
                  w_spec],
        out_specs=o_spec,
        out_shape=jax.ShapeDtypeStruct(o_shape, out_dtype),
        scratch_shapes=[pltpu.VMEM((tm, d), BF16)],
        compiler_params=_cparams(("parallel", "arbitrary")),
        name="norm_matmul_t" if transposed else "norm_matmul",
    )(x, gain.reshape(1, d), w)


def _swa_kernel(sink_ref, q_ref, kp_ref, kc_ref, vp_ref, vc_ref, o_ref):
    n = pl.program_id(1)
    tq = q_ref.shape[0]
    q = q_ref[...]
    k = jnp.concatenate([kp_ref[...], kc_ref[...]], axis=0)
    v = jnp.concatenate([vp_ref[...], vc_ref[...]], axis=0)
    i = lax.broadcasted_iota(jnp.int32, (tq, 2 * tq), 0)
    j = lax.broadcasted_iota(jnp.int32, (tq, 2 * tq), 1)
    rel = tq + i - j
    mask = (rel >= 0) & (rel < WINDOW_A) & ((j >= tq) | (n > 0))
    relf = rel.astype(F32)
    group = A_HEADS // A_KV_HEADS
    for h in range(A_HEADS):
        kv = h // group
        qh = q[:, h * HEAD_DIM:(h + 1) * HEAD_DIM]
        kh = k[:, kv * HEAD_DIM:(kv + 1) * HEAD_DIM]
        vh = v[:, kv * HEAD_DIM:(kv + 1) * HEAD_DIM]
        s = _dot_nt(qh, kh) * Q_SCALE - SLOPES_A[h] * relf
        s = jnp.where(mask, s, NEG_INF)
        sink = sink_ref[h]
        m = jnp.maximum(jnp.max(s, axis=-1, keepdims=True), sink)
        p = jnp.exp(s - m)
        l = jnp.sum(p, axis=-1, keepdims=True) + jnp.exp(sink - m)
        o_ref[:, h * HEAD_DIM:(h + 1) * HEAD_DIM] = _dot(p.astype(BF16), vh) / l


def swa_attention(proj, sinks, batch, seq):
    tq = WINDOW_A
    nb = seq // tq
    kcol, vcol = OFF_KA // KV_WIDTH, OFF_VA // KV_WIDTH

    def cur(col):
        return pl.BlockSpec((tq, KV_WIDTH), lambda b, n: (b * nb + n, col))

    def prev(col):
        return pl.BlockSpec((tq, KV_WIDTH), lambda b, n: (b * nb + jnp.maximum(n - 1, 0), col))

    return pl.pallas_call(
        _swa_kernel,
        grid=(batch, nb),
        in_specs=[pl.BlockSpec(memory_space=pltpu.SMEM),
                  pl.BlockSpec((tq, A_WIDTH), lambda b, n: (b * nb + n, OFF_QA // A_WIDTH)),
                  prev(kcol), cur(kcol), prev(vcol), cur(vcol)],
        out_specs=pl.BlockSpec((tq, A_WIDTH), lambda b, n: (b * nb + n, 0)),
        out_shape=jax.ShapeDtypeStruct((batch * seq, A_WIDTH), F32),
        compiler_params=_cparams(("parallel", "parallel")),
        name="swa_attention",
    )(sinks, proj, proj, proj, proj, proj)


def _fox_aug_kernel(z_ref, b_ref, pk_ref, pq_ref, ka_ref, qa_ref, carry_sc):
    @pl.when(pl.program_id(1) == 0)
    def _():
        carry_sc[...] = jnp.zeros_like(carry_sc)

    z = z_ref[...] + b_ref[...]
    log_f = -(jnp.maximum(-z, 0.0) + jnp.log1p(jnp.exp(-jnp.abs(z))))
    ts = z.shape[0]
    r = lax.broadcasted_iota(jnp.int32, (ts, ts), 0)
    c = lax.broadcasted_iota(jnp.int32, (ts, ts), 1)
    tri = (c <= r).astype(BF16)
    hi, mid, lo = _split3(log_f)
    cum = _dot(tri, hi) + _dot(tri, mid) + _dot(tri, lo) + carry_sc[0:1, :]
    carry_sc[0:1, :] = cum[ts - 1:ts, :]
    hi, mid, lo = _split3(cum)
    lane = lax.broadcasted_iota(jnp.int32, (ts, LANES), 1)
    hi = jnp.where(lane == ONE_LANE, 1.0, hi).astype(BF16)
    ka_ref[...] = (_dot(hi, pk_ref[0]) + _dot(mid, pk_ref[1]) + _dot(lo, pk_ref[2])).astype(BF16)
    qa_ref[...] = (_dot_nt(pq_ref[0], hi) + _dot_nt(pq_ref[1], mid) + _dot_nt(pq_ref[2], lo)).astype(BF16)


def _aug_lane(h):
    return HEAD_DIM if h % 2 == 0 else 0


def _fox_placement():
    pk = np.zeros((3, LANES, C_HEADS * LANES), np.float32)
    pq = np.zeros((3, C_HEADS * N_AUG, LANES), np.float32)
    for h in range(C_HEADS):
        src, kbase, qbase = TAIL_FC + h, h * LANES + _aug_lane(h), h * N_AUG
        for piece in range(3):
            pk[piece, src, kbase + piece] = -1.0
            pq[piece, qbase + 3 + piece, src] = 1.0
            pk[0, ONE_LANE, kbase + 3 + piece] = 1.0
            pq[0, qbase + piece, ONE_LANE] = 1.0
    return jnp.asarray(pk, BF16), jnp.asarray(pq, BF16)


def fox_augmentation(tail, bias_row, batch, seq, ts=512):
    nt = seq // ts
    pk, pq = _fox_placement()
    kw, qw = C_HEADS * LANES, C_HEADS * N_AUG
    return pl.pallas_call(
        _fox_aug_kernel,
        grid=(batch, nt),
        in_specs=[pl.BlockSpec((ts, LANES), lambda b, i: (b * nt + i, 0)),
                  pl.BlockSpec((1, LANES), lambda b, i: (0, 0)),
                  pl.BlockSpec((3, LANES, kw), lambda b, i: (0, 0, 0)),
                  pl.BlockSpec((3, qw, LANES), lambda b, i: (0, 0, 0))],
        out_specs=[pl.BlockSpec((ts, kw), lambda b, i: (b * nt + i, 0)),
                   pl.BlockSpec((qw, ts), lambda b, i: (0, b * nt + i))],
        out_shape=[jax.ShapeDtypeStruct((batch * seq, kw), BF16),
                   jax.ShapeDtypeStruct((qw, batch * seq), BF16)],
        scratch_shapes=[pltpu.VMEM((8, LANES), F32)],
        compiler_params=_cparams(("parallel", "arbitrary")),
        name="fox_augmentation",
    )(tail, bias_row, pk, pq)


def _alibi_augmentation(seq):
    pos = np.arange(seq)
    ka = np.zeros((seq, LANES), np.float32)
    for base in (0, HEAD_DIM):
        ka[:, base:base + 3] = 1.0
        ka[:, base + 3:base + 6] = ((pos >> 8) << 8)[:, None]
        ka[:, base + 6:base + 9] = (pos & 255)[:, None]
    slopes = jnp.asarray(SLOPES_B, F32)
    st = _split3(-(slopes[:, None] * jnp.asarray(pos, F32)[None, :]))
    sl = _split3(slopes)
    qa = jnp.zeros((B_HEADS, N_AUG, seq), BF16)
    for piece in range(3):
        qa = qa.at[:, piece, :].set(st[piece])
        qa = qa.at[:, 3 + piece, :].set(sl[piece][:, None])
        qa = qa.at[:, 6 + piece, :].set(sl[piece][:, None])
    return jnp.asarray(ka, BF16), qa.reshape(B_HEADS * N_AUG, seq)


def _flash_t_kernel(*refs, mode, tq, tk, n_steps):
    if mode == "slc":
        k_ref, ka_ref, qt_ref, qat_ref, vt_ref, gate_ref, selt_ref, exp_ref, o_ref, m_sc, l_sc, acc_sc = refs
    elif mode == "win":
        k_ref, ka_ref, qt_ref, qat_ref, vt_ref, gate_ref, o_ref, m_sc, l_sc, acc_sc = refs
    else:
        k_ref, ka_ref, qt_ref, qat_ref, vt_ref, o_ref, m_sc, l_sc, acc_sc = refs
    n_heads = qt_ref.shape[0] // HEAD_DIM
    group = 1 if mode == "fox" else B_GROUP
    iq, j = pl.program_id(1), pl.program_id(2)
    if mode == "win":
        ik = iq - (n_steps - 1) + j
        last_j = n_steps - 1
        active = ik >= 0
    else:
        ik = j
        last_j = (iq * tq + tq - 1) // tk
        active = j <= last_j

    @pl.when(j == 0)
    def _():
        m_sc[...] = jnp.full_like(m_sc, M_INIT)
        l_sc[...] = jnp.zeros_like(l_sc)
        acc_sc[...] = jnp.zeros_like(acc_sc)

    def step(masked):
        ok = None
        if masked:
            rel = ((iq * tq + lax.broadcasted_iota(jnp.int32, (tk, tq), 1))
                   - (ik * tk + lax.broadcasted_iota(jnp.int32, (tk, tq), 0)))
            ok = rel >= 0
            if mode == "win":
                ok = ok & (rel < WINDOW_B)
        lane = lax.broadcasted_iota(jnp.int32, (tk, LANES), 1)
        ka = None
        for h in range(n_heads):
            kv = h // group
            rows = slice(h * HEAD_DIM, (h + 1) * HEAD_DIM)
            if h % group == 0:
                pair = slice((kv // 2) * LANES, (kv // 2 + 1) * LANES)
                aug = ka_ref[:, kv * LANES:(kv + 1) * LANES] if mode == "fox" else ka_ref[...]
                own = (lane < HEAD_DIM) if kv % 2 == 0 else (lane >= HEAD_DIM)
                ka = jnp.where(own, k_ref[:, pair], aug)
                ok_h = ok
                if mode == "slc":
                    ok_h = ok & (_dot(exp_ref[...], selt_ref[0, kv]) > 0.5)
            halves = [qt_ref[rows, :], qat_ref[rows, :]]
            qa = jnp.concatenate(halves if kv % 2 == 0 else halves[::-1], axis=0)
            s = _dot(ka, qa)
            if ok_h is not None:
                s = jnp.where(ok_h, s, NEG_INF)
            m_prev = m_sc[h:h + 1, :]
            m_new = jnp.maximum(m_prev, jnp.max(s, axis=0, keepdims=True))
            alpha = jnp.exp(m_prev - m_new)
            p = jnp.exp(s - m_new)
            l_sc[h:h + 1, :] = alpha * l_sc[h:h + 1, :] + jnp.sum(p, axis=0, keepdims=True)
            vrows = slice(kv * HEAD_DIM, (kv + 1) * HEAD_DIM)
            acc_sc[rows, :] = alpha * acc_sc[rows, :] + _dot(vt_ref[vrows, :], p.astype(BF16))
            m_sc[h:h + 1, :] = m_new

    if mode == "fox":
        @pl.when(active & (ik * tk + tk - 1 > iq * tq))
        def _():
            step(True)

        @pl.when(active & (ik * tk + tk - 1 <= iq * tq))
        def _():
            step(False)
    else:
        @pl.when(active)
        def _():
            step(True)

    @pl.when(j == last_j)
    def _():
        branch = {"fox": None, "slc": 1, "win": 2}[mode]
        for pair in range(n_heads // 2):
            rows = slice(pair * LANES, (pair + 1) * LANES)
            heads = (2 * pair, 2 * pair + 1)
            denom = jnp.concatenate(
                [jnp.broadcast_to(l_sc[h:h + 1, :], (HEAD_DIM, tq)) for h in heads], axis=0)
            out = acc_sc[rows, :] / denom
            if branch is not None:
                out = out * jnp.concatenate(
                    [jnp.broadcast_to(_sigmoid(gate_ref[3 * h + branch:3 * h + branch + 1, :]), (HEAD_DIM, tq))
                     for h in heads], axis=0)
            o_ref[:, rows] = out.T


def flash_attention_t(mode, proj_r, proj_t, k_aug, q_aug_t, batch, seq, gates_t=None, sel_t=None, expand=None,
                      tq=512, tk=512):
    nq, nk = seq // tq, seq // tk
    if mode == "win":
        assert tq == tk
        n_steps = WINDOW_B // tk + 1

        def kblock(iq, j):
            return jnp.maximum(iq - (n_steps - 1) + j, 0)
    else:
        n_steps = nk

        def kblock(iq, j):
            return jnp.minimum(j, (iq * tq + tq - 1) // tk)

    if mode == "fox":
        in_specs = [pl.BlockSpec((tk, C_WIDTH), lambda b, iq, j: (b * nk + kblock(iq, j), OFF_KC // C_WIDTH)),
                    pl.BlockSpec((tk, C_HEADS * LANES), lambda b, iq, j: (b * nk + kblock(iq, j), 0)),
                    pl.BlockSpec((C_WIDTH, tq), lambda b, iq, j: (OFF_QCT // C_WIDTH, b * nq + iq)),
                    pl.BlockSpec((C_WIDTH, tq), lambda b, iq, j: (0, b * nq + iq)),
                    pl.BlockSpec((C_WIDTH, tk), lambda b, iq, j: (OFF_VCT // C_WIDTH, b * nk + kblock(iq, j)))]
    else:
        kcol = (OFF_KBS if mode == "slc" else OFF_KBW) // KV_WIDTH
        vrow = (OFF_VBST if mode == "slc" else OFF_VBWT) // KV_WIDTH
        in_specs = [pl.BlockSpec((tk, KV_WIDTH), lambda b, iq, j: (b * nk + kblock(iq, j), kcol)),
                    pl.BlockSpec((tk, LANES), lambda b, iq, j: (kblock(iq, j), 0)),
                    pl.BlockSpec((B_WIDTH, tq), lambda b, iq, j: (OFF_QBT // B_WIDTH, b * nq + iq)),
                    pl.BlockSpec((B_WIDTH, tq), lambda b, iq, j: (0, iq)),
                    pl.BlockSpec((KV_WIDTH, tk), lambda b, iq, j: (vrow, b * nk + kblock(iq, j)))]
    args = [proj_r, k_aug, proj_t, q_aug_t, proj_t]
    if mode != "fox":
        in_specs.append(pl.BlockSpec((LANES, tq), lambda b, iq, j: (0, b * nq + iq)))
        args.append(gates_t)
    if mode == "slc":
        n_slc = seq // SEL_BLOCK
        in_specs += [pl.BlockSpec((1, B_KV_HEADS, n_slc, tq), lambda b, iq, j: (b, 0, 0, iq)),
                     pl.BlockSpec((tk, n_slc), lambda b, iq, j: (kblock(iq, j), 0))]
        args += [sel_t, expand]
    width = C_WIDTH if mode == "fox" else B_WIDTH
    n_heads = width // HEAD_DIM
    return pl.pallas_call(
        functools.partial(_flash_t_kernel, mode=mode, tq=tq, tk=tk, n_steps=n_steps),
        grid=(batch, nq, n_steps),
        in_specs=in_specs,
        out_specs=pl.BlockSpec((tq, width), lambda b, iq, j: (b * nq + iq, 0)),
        out_shape=jax.ShapeDtypeStruct((batch * seq, width), F32),
        scratch_shapes=[pltpu.VMEM((16, tq), F32), pltpu.VMEM((16, tq), F32),
                        pltpu.VMEM((n_heads * HEAD_DIM, tq), F32)],
        compiler_params=_cparams(("parallel", "parallel", "arbitrary")),
        name=mode + "_attention",
    )(*args)


def _compress_kernel(t_ref, pe_ref, w1_ref, w2_ref, w2t_ref, o_ref, ot_ref):
    w1 = w1_ref[0]
    hid = _dot(t_ref[0, 0, 0], w1) + _dot(pe_ref[0], w1)[0:1, :]
    act = (hid * _sigmoid(hid)).astype(BF16)
    o_ref[0, 0, 0] = _dot(act, w2_ref[0]).astype(o_ref.dtype)
    ot_ref[0, 0, 0] = _dot_nt(w2t_ref[0], act).astype(ot_ref.dtype)


def nsa_compress(flat, pe_rows, w1, w2):
    _, batch, n_kv, n_chunks, width = flat.shape
    hidden = w1.shape[-1]
    return pl.pallas_call(
        _compress_kernel,
        grid=(2, batch, n_kv),
        in_specs=[pl.BlockSpec((1, 1, 1, n_chunks, width), lambda s, b, h: (s, b, h, 0, 0)),
                  pl.BlockSpec((1, 8, width), lambda s, b, h: (s, 0, 0)),
                  pl.BlockSpec((1, width, hidden), lambda s, b, h: (s, 0, 0)),
                  pl.BlockSpec((1, hidden, HEAD_DIM), lambda s, b, h: (s, 0, 0)),
                  pl.BlockSpec((1, HEAD_DIM, hidden), lambda s, b, h: (s, 0, 0))],
        out_specs=[pl.BlockSpec((1, 1, 1, n_chunks, HEAD_DIM), lambda s, b, h: (s, b, h, 0, 0)),
                   pl.BlockSpec((1, 1, 1, HEAD_DIM, n_chunks), lambda s, b, h: (s, b, h, 0, 0))],
        out_shape=[jax.ShapeDtypeStruct((2, batch, n_kv, n_chunks, HEAD_DIM), BF16),
                   jax.ShapeDtypeStruct((2, batch, n_kv, HEAD_DIM, n_chunks), BF16)],
        compiler_params=_cparams(("parallel", "parallel", "parallel")),
        name="nsa_compress",
    )(flat, pe_rows, w1, w2, jnp.swapaxes(w2, 1, 2))


def _cmp_attn_t_kernel(qt_ref, kc_ref, vct_ref, inter_ref, gate_ref, o_ref, selt_ref, ot_sc, *, tq):
    iq = pl.program_id(1)
    n_chunks = kc_ref.shape[3]
    n_slc = inter_ref.shape[0]
    t = iq * tq + lax.broadcasted_iota(jnp.int32, (n_chunks, tq), 1)
    n = lax.broadcasted_iota(jnp.int32, (n_chunks, tq), 0)
    rel = t - (n * CMP_STRIDE + CMP_LEN - 1)
    mask = rel >= 0
    relf = rel.astype(F32)
    t_s = iq * tq + lax.broadcasted_iota(jnp.int32, (n_slc, tq), 1)
    jj = lax.broadcasted_iota(jnp.int32, (n_slc, tq), 0)
    cur = t_s >> (SEL_BLOCK.bit_length() - 1)
    valid = jj * SEL_BLOCK <= t_s
    forced = (jj == 0) | (jj == cur) | (jj == cur - 1)
    jf = jj.astype(F32)
    inter = inter_ref[...]
    for h in range(B_KV_HEADS):
        kc = kc_ref[0, 0, h]
        vct = vct_ref[0, 0, h]
        p_sum = jnp.zeros((n_chunks, tq), F32)
        for g in range(B_GROUP):
            hd = h * B_GROUP + g
            rows = slice(hd * HEAD_DIM, (hd + 1) * HEAD_DIM)
            s = jnp.where(mask, _dot(kc, qt_ref[rows, :]) - SLOPES_B[hd] * relf, NEG_INF)
            m = jnp.max(s, axis=0, keepdims=True)
            e = jnp.where(mask, jnp.exp(s - m), 0.0)
            l = jnp.sum(e, axis=0, keepdims=True)
            p = e / jnp.where(l > 0.0, l, 1.0)
            p_sum = p_sum + p
            ot_sc[rows, :] = _dot(vct, p.astype(BF16)) * _sigmoid(gate_ref[3 * hd:3 * hd + 1, :])
        hi, mid, lo = _split3(p_sum)
        importance = _dot(inter, hi) + _dot(inter, mid) + _dot(inter, lo)
        score = jnp.where(valid, importance + jnp.where(forced, FORCE_BONUS, 0.0), NEG_INF)
        chosen = jnp.zeros((n_slc, tq), F32)
        for _ in range(min(N_SELECT, n_slc)):
            best = jnp.max(score, axis=0, keepdims=True)
            first = jnp.min(jnp.where(score == best, jf, float(n_slc)), axis=0, keepdims=True)
            pick = jf == first
            chosen = jnp.where(pick, 1.0, chosen)
            score = jnp.where(pick, -jnp.inf, score)
        selt_ref[0, h] = jnp.where(valid, chosen, 0.0).astype(selt_ref.dtype)
    for pair in range(B_HEADS // 2):
        rows = slice(pair * LANES, (pair + 1) * LANES)
        o_ref[:, rows] = ot_sc[rows, :].T


def nsa_compressed_attention(proj_t, kvc, kvct, inter_t, gates_t, batch, seq, tq=512):
    nq = seq // tq
    n_chunks = kvc.shape[3]
    n_slc = seq // SEL_BLOCK
    return pl.pallas_call(
        functools.partial(_cmp_attn_t_kernel, tq=tq),
        grid=(batch, nq),
        in_specs=[pl.BlockSpec((B_WIDTH, tq), lambda b, i: (OFF_QBT // B_WIDTH, b * nq + i)),
                  pl.BlockSpec((1, 1, B_KV_HEADS, n_chunks, HEAD_DIM), lambda b, i: (0, b, 0, 0, 0)),
                  pl.BlockSpec((1, 1, B_KV_HEADS, HEAD_DIM, n_chunks), lambda b, i: (1, b, 0, 0, 0)),
                  pl.BlockSpec((n_slc, n_chunks), lambda b, i: (0, 0)),
                  pl.BlockSpec((LANES, tq), lambda b, i: (0, b * nq + i))],
        out_specs=[pl.BlockSpec((tq, B_WIDTH), lambda b, i: (b * nq + i, 0)),
                   pl.BlockSpec((1, B_KV_HEADS, n_slc, tq), lambda b, i: (b, 0, 0, i))],
        out_shape=[jax.ShapeDtypeStruct((batch * seq, B_WIDTH), F32),
                   jax.ShapeDtypeStruct((batch, B_KV_HEADS, n_slc, seq), BF16)],
        scratch_shapes=[pltpu.VMEM((B_WIDTH, tq), F32)],
        compiler_params=_cparams(("parallel", "parallel")),
        name="nsa_cmp_attention",
    )(proj_t, kvc, kvct, inter_t, gates_t)


def _mix_out_kernel(oa_ref, ocmp_ref, oslc_ref, owin_ref, oc_ref, g_ref, w_ref, x_ref, o_ref, mixed_sc):
    @pl.when(pl.program_id(1) == 0)
    def _():
        o_b = ocmp_ref[...] + oslc_ref[...] + owin_ref[...]
        b0, c0 = A_WIDTH, A_WIDTH + B_WIDTH
        mixed_sc[:, 0:b0] = _rms(oa_ref[...], g_ref[:, 0:b0]).astype(BF16)
        mixed_sc[:, b0:c0] = _rms(o_b, g_ref[:, b0:c0]).astype(BF16)
        mixed_sc[:, c0:] = _rms(oc_ref[...], g_ref[:, c0:]).astype(BF16)

    o_ref[...] = x_ref[...] + _dot(mixed_sc[...], w_ref[...])


def mix_out(o_a, o_cmp, o_slc, o_win, o_c, gain, w_out, x, tm=512, tn=512):
    n, d = x.shape
    width = w_out.shape[0]

    def rows(w):
        return pl.BlockSpec((tm, w), lambda i, j: (i, 0))

    return pl.pallas_call(
        _mix_out_kernel,
        grid=(n // tm, d // tn),
        in_specs=[rows(A_WIDTH), rows(B_WIDTH), rows(B_WIDTH), rows(B_WIDTH), rows(C_WIDTH),
                  pl.BlockSpec((1, width), lambda i, j: (0, 0)),
                  pl.BlockSpec((width, tn), lambda i, j: (0, j)),
                  pl.BlockSpec((tm, tn), lambda i, j: (i, j))],
        out_specs=pl.BlockSpec((tm, tn), lambda i, j: (i, j)),
        out_shape=jax.ShapeDtypeStruct((n, d), F32),
        scratch_shapes=[pltpu.VMEM((tm, width), BF16)],
        compiler_params=_cparams(("parallel", "arbitrary")),
        name="mix_out",
    )(o_a, o_cmp, o_slc, o_win, o_c, gain.reshape(1, width), w_out, x)


def _fresh_weights(be_ref):
    i = pl.program_id(1)
    return (i == 0) | (be_ref[i] != be_ref[jnp.maximum(i - 1, 0)])


def _ffn_up_kernel(be_ref, nv_ref, x_ref, wg_ref, wu_ref, o_ref, wg_sc, wu_sc):
    i = pl.program_id(1)

    @pl.when(_fresh_weights(be_ref))
    def _():
        wg_sc[...] = wg_ref[0].astype(BF16)
        wu_sc[...] = wu_ref[0].astype(BF16)

    @pl.when(i < nv_ref[0])
    def _():
        x = x_ref[...].astype(BF16)
        gate = _dot(x, wg_sc[...])
        up = _dot(x, wu_sc[...])
        o_ref[...] = (gate * _sigmoid(gate) * up).astype(o_ref.dtype)

    @pl.when(i >= nv_ref[0])
    def _():
        o_ref[...] = jnp.zeros_like(o_ref)


def _ffn_down_kernel(be_ref, nv_ref, h_ref, wd_ref, *rest):
    o_ref, wd_sc = rest[-2], rest[-1]
    i = pl.program_id(1)

    @pl.when(_fresh_weights(be_ref))
    def _():
        wd_sc[...] = wd_ref[0].astype(BF16)

    @pl.when(i < nv_ref[0])
    def _():
        y = _dot(h_ref[...], wd_sc[...])
        if len(rest) == 3:
            y = y + rest[0][...]
        o_ref[...] = y

    @pl.when(i >= nv_ref[0])
    def _():
        o_ref[...] = jnp.zeros_like(o_ref)


def grouped_swiglu(xs, block_expert, n_valid, w_gate, w_up, w_down, residual=None, tm=512, tf=512, tn=256):
    rows, d = xs.shape
    f = w_gate.shape[-1]
    nblk = rows // tm

    def xrow(i, nv):
        return jnp.minimum(i, nv[0] - 1)

    hidden = pl.pallas_call(
        _ffn_up_kernel,
        grid_spec=pltpu.PrefetchScalarGridSpec(
            num_scalar_prefetch=2,
            grid=(f // tf, nblk),
            in_specs=[pl.BlockSpec((tm, d), lambda j, i, be, nv: (xrow(i, nv), 0)),
                      pl.BlockSpec((1, d, tf), lambda j, i, be, nv: (be[i], 0, j)),
                      pl.BlockSpec((1, d, tf), lambda j, i, be, nv: (be[i], 0, j))],
            out_specs=pl.BlockSpec((tm, tf), lambda j, i, be, nv: (i, j)),
            scratch_shapes=[pltpu.VMEM((d, tf), BF16), pltpu.VMEM((d, tf), BF16)]),
        out_shape=jax.ShapeDtypeStruct((rows, f), BF16),
        compiler_params=_cparams(("parallel", "arbitrary")),
        name="ffn_up",
    )(block_expert, n_valid, xs, w_gate, w_up)

    in_specs = [pl.BlockSpec((tm, f), lambda j, i, be, nv: (xrow(i, nv), 0)),
                pl.BlockSpec((1, f, tn), lambda j, i, be, nv: (be[i], 0, j))]
    args = [hidden, w_down]
    if residual is not None:
        in_specs.append(pl.BlockSpec((tm, tn), lambda j, i, be, nv: (i, j)))
        args.append(residual)
    return pl.pallas_call(
        _ffn_down_kernel,
        grid_spec=pltpu.PrefetchScalarGridSpec(
            num_scalar_prefetch=2,
            grid=(d // tn, nblk),
            in_specs=in_specs,
            out_specs=pl.BlockSpec((tm, tn), lambda j, i, be, nv: (i, j)),
            scratch_shapes=[pltpu.VMEM((f, tn), BF16)]),
        out_shape=jax.ShapeDtypeStruct((rows, d), F32),
        compiler_params=_cparams(("parallel", "arbitrary")),
        name="ffn_down",
    )(block_expert, n_valid, *args)


def _router_kernel(x_ref, g_ref, wr_ref, h_ref, ri_ref, rf_ref, cnt_ref, carry_sc):
    @pl.when(pl.program_id(0) == 0)
    def _():
        carry_sc[...] = jnp.zeros_like(carry_sc)

    tm = x_ref.shape[0]
    h = _rms(x_ref[...], g_ref[...])
    h_ref[...] = h
    h1, h2, h3 = _split3(h)
    w1, w2, w3 = wr_ref[0], wr_ref[1], wr_ref[2]
    logits = (_dot(h1, w1) + _dot(h1, w2) + _dot(h2, w1)) + (_dot(h1, w3) + _dot(h2, w2) + _dot(h3, w1))
    lane_i = lax.broadcasted_iota(jnp.int32, (tm, LANES), 1)
    lane = lane_i.astype(F32)
    logits = jnp.where(lane_i < N_EXPERTS, logits, -jnp.inf)
    v1 = jnp.max(logits, axis=-1, keepdims=True)
    e1 = jnp.min(jnp.where(logits == v1, lane, float(LANES)), axis=-1, keepdims=True)
    rest = jnp.where(lane == e1, -jnp.inf, logits)
    v2 = jnp.max(rest, axis=-1, keepdims=True)
    e2 = jnp.min(jnp.where(rest == v2, lane, float(LANES)), axis=-1, keepdims=True)
    z = jnp.exp(v2 - v1)
    g1 = 1.0 / (1.0 + z)
    g2 = z / (1.0 + z)
    chosen = (lane == e1) | (lane == e2)
    onehot = jnp.where(chosen, 1.0, 0.0)
    r = lax.broadcasted_iota(jnp.int32, (tm, tm), 0)
    c = lax.broadcasted_iota(jnp.int32, (tm, tm), 1)
    before = _dot((c < r).astype(BF16), onehot.astype(BF16)) + carry_sc[0:1, :]
    r1 = jnp.sum(jnp.where(lane == e1, before, 0.0), axis=-1, keepdims=True)
    r2 = jnp.sum(jnp.where(lane == e2, before, 0.0), axis=-1, keepdims=True)
    carry = carry_sc[0:1, :] + jnp.sum(onehot, axis=0, keepdims=True)
    carry_sc[0:1, :] = carry
    cnt_ref[...] = jnp.broadcast_to(carry, cnt_ref.shape)
    packed = jnp.where(lane_i == 0, e1, jnp.where(lane_i == 1, e2, jnp.where(
        lane_i == 2, r1, jnp.where(lane_i == 3, r2, 0.0))))
    ri_ref[...] = packed.astype(jnp.int32)
    rf_ref[...] = jnp.where(lane_i == 0, g1, jnp.where(lane_i == 1, g2, 0.0))


def moe_route(x, gain, w_router3, tm=512):
    n, d = x.shape
    return pl.pallas_call(
        _router_kernel,
        grid=(n // tm,),
        in_specs=[pl.BlockSpec((tm, d), lambda i: (i, 0)),
                  pl.BlockSpec((1, d), lambda i: (0, 0)),
                  pl.BlockSpec((3, d, LANES), lambda i: (0, 0, 0))],
        out_specs=[pl.BlockSpec((tm, d), lambda i: (i, 0)),
                   pl.BlockSpec((tm, LANES), lambda i: (i, 0)),
                   pl.BlockSpec((tm, LANES), lambda i: (i, 0)),
                   pl.BlockSpec((8, LANES), lambda i: (0, 0))],
        out_shape=[jax.ShapeDtypeStruct((n, d), F32),
                   jax.ShapeDtypeStruct((n, LANES), jnp.int32),
                   jax.ShapeDtypeStruct((n, LANES), F32),
                   jax.ShapeDtypeStruct((8, LANES), F32)],
        scratch_shapes=[pltpu.VMEM((8, LANES), F32)],
        compiler_params=_cparams(("arbitrary",)),
        name="moe_route",
    )(x, gain.reshape(1, d), w_router3)


def _row_copy(src_ref, src_row, dst_ref, dst_row, sem):
    return pltpu.make_async_copy(src_ref.at[pl.ds(src_row, 1)], dst_ref.at[pl.ds(dst_row, 1)], sem)


def _dispatch_kernel(dest_ref, h_ref, zeros_ref, xs_ref, sem):
    del zeros_ref
    td = h_ref.shape[0]
    base = pl.program_id(0) * td

    def start(r, carry):
        for k in range(2):
            _row_copy(h_ref, r, xs_ref, dest_ref[2 * (base + r) + k], sem).start()
        return carry

    def wait(r, carry):
        for k in range(2):
            _row_copy(h_ref, r, xs_ref, dest_ref[2 * (base + r) + k], sem).wait()
        return carry

    lax.fori_loop(0, td, start, 0)
    lax.fori_loop(0, td, wait, 0)


def moe_dispatch(h, dest, cap, td=256):
    n, d = h.shape
    return pl.pallas_call(
        _dispatch_kernel,
        grid_spec=pltpu.PrefetchScalarGridSpec(
            num_scalar_prefetch=1,
            grid=(n // td,),
            in_specs=[pl.BlockSpec((td, d), lambda i, dest: (i, 0)),
                      pl.BlockSpec(memory_space=pl.ANY)],
            out_specs=pl.BlockSpec(memory_space=pl.ANY),
            scratch_shapes=[pltpu.SemaphoreType.DMA(())]),
        out_shape=jax.ShapeDtypeStruct((cap, d), h.dtype),
        input_output_aliases={2: 0},
        compiler_params=_cparams(("arbitrary",)),
        name="moe_dispatch",
    )(dest, h, jnp.zeros((cap, d), h.dtype))


def _combine_kernel(dest_ref, x_ref, gate_ref, y_ref, o_ref, ya_sc, yb_sc, sem):
    tc = x_ref.shape[0]
    base = pl.program_id(0) * tc

    def start(r, carry):
        _row_copy(y_ref, dest_ref[2 * (base + r)], ya_sc, r, sem).start()
        _row_copy(y_ref, dest_ref[2 * (base + r) + 1], yb_sc, r, sem).start()
        return carry

    def wait(r, carry):
        _row_copy(y_ref, dest_ref[2 * (base + r)], ya_sc, r, sem).wait()
        _row_copy(y_ref, dest_ref[2 * (base + r) + 1], yb_sc, r, sem).wait()
        return carry

    lax.fori_loop(0, tc, start, 0)
    lax.fori_loop(0, tc, wait, 0)
    gates = gate_ref[...]
    o_ref[...] = x_ref[...] + (gates[:, 0:1] * ya_sc[...] + gates[:, 1:2] * yb_sc[...])


def moe_combine(x, gates, y, dest, tc=256):
    n, d = x.shape
    return pl.pallas_call(
        _combine_kernel,
        grid_spec=pltpu.PrefetchScalarGridSpec(
            num_scalar_prefetch=1,
            grid=(n // tc,),
            in_specs=[pl.BlockSpec((tc, d), lambda i, dest: (i, 0)),
                      pl.BlockSpec((tc, LANES), lambda i, dest: (i, 0)),
                      pl.BlockSpec(memory_space=pl.ANY)],
            out_specs=pl.BlockSpec((tc, d), lambda i, dest: (i, 0)),
            scratch_shapes=[pltpu.VMEM((tc, d), F32), pltpu.VMEM((tc, d), F32),
                            pltpu.SemaphoreType.DMA(())]),
        out_shape=jax.ShapeDtypeStruct((n, d), F32),
        compiler_params=_cparams(("arbitrary",)),
        name="moe_combine",
    )(dest, x, gates, y)


def moe_layer(x, gain, w_router, w_gate, w_up, w_down, first_expert, tm=512):
    n, d = x.shape
    wr = jnp.pad(w_router.astype(F32), ((0, 0), (0, LANES - N_EXPERTS)))
    h, info, gates, counts = moe_route(x, gain, jnp.stack(_split3(wr)))
    counts = counts[0, :N_EXPERTS].astype(jnp.int32)
    padded = (counts + tm - 1) // tm * tm
    pad_ends = jnp.cumsum(padded)
    pad_starts = pad_ends - padded
    experts, ranks = info[:, 0:2], info[:, 2:4]
    dest = (pad_starts[experts] + ranks).reshape(-1).astype(jnp.int32)
    nblk = (2 * n) // tm + N_EXPERTS
    n_valid = (pad_ends[-1] // tm).astype(jnp.int32).reshape(1)
    blk = jnp.minimum(jnp.arange(nblk, dtype=jnp.int32), n_valid[0] - 1) * tm
    block_expert = jnp.minimum(jnp.sum(blk[:, None] >= pad_ends[None, :], axis=1), N_EXPERTS - 1).astype(jnp.int32)
    xs = moe_dispatch(h, dest, nblk * tm)
    y = grouped_swiglu(xs, block_expert + first_expert, n_valid, w_gate, w_up, w_down, tm=tm)
    return moe_combine(x, gates, y, dest)


def _project_weights(w):
    def cols(a, n):
        return w[:, a:a + n]

    qa, ka, va, qb = cols(0, 512), cols(512, 128), cols(640, 128), cols(768, 768)
    kbc, vbc, kbs, vbs, kbw, vbw = (cols(1536 + 128 * i, 128) for i in range(6))
    gb, qc, kc, vc, fc = cols(2304, 36), cols(2340, 768), cols(3108, 768), cols(3876, 768), cols(4644, 12)
    w_rows = jnp.concatenate([kc, ka, va, qa, kbc, vbc, kbs, kbw], axis=1).astype(BF16)
    w_t = jnp.concatenate([qb * Q_SCALE, qc * Q_SCALE, vc, vbs, vbw], axis=1).T.astype(BF16)
    tail = jnp.concatenate([gb, fc], axis=1)
    tail = jnp.pad(tail, ((0, 0), (0, LANES - tail.shape[1]))).astype(BF16)
    return w_rows, w_t, tail


def _chunk_blocks(cols, batch, seq):
    n_chunks = seq // CMP_STRIDE
    t = cols.reshape(batch, seq, B_KV_HEADS, HEAD_DIM).transpose(0, 2, 1, 3)
    t = t.reshape(batch, B_KV_HEADS, n_chunks, CMP_STRIDE * HEAD_DIM)
    nxt = jnp.concatenate([t[:, :, 1:], jnp.zeros_like(t[:, :, :1])], axis=2)
    return jnp.concatenate([t, nxt], axis=-1)


def _overlap_matrices(seq):
    n_chunks, n_slc = seq // CMP_STRIDE, seq // SEL_BLOCK
    c_start = np.arange(n_chunks) * CMP_STRIDE
    s_start = np.arange(n_slc) * SEL_BLOCK
    inter = np.maximum(np.minimum(c_start[:, None] + CMP_LEN, s_start[None, :] + SEL_BLOCK)
                       - np.maximum(c_start[:, None], s_start[None, :]), 0) / CMP_LEN
    inter[(seq - CMP_LEN) // CMP_STRIDE + 1:] = 0.0
    expand = (np.arange(seq)[:, None] // SEL_BLOCK == np.arange(n_slc)[None, :])
    return jnp.asarray(inter.T, BF16), jnp.asarray(expand, BF16)


def mixer_layer(x, batch, seq, w_in, w_out, norm_mix, mix_out_norm, sinks, cmp_pe, cmp_w1, cmp_w2, f_bias):
    w_rows, w_t, w_tail = _project_weights(w_in)
    proj_r = norm_matmul(x, norm_mix, w_rows, BF16)
    proj_t = norm_matmul(x, norm_mix, w_t, BF16, transposed=True)
    tail = norm_matmul(x, norm_mix, w_tail, F32)
    tail_t = norm_matmul(x, norm_mix, w_tail.T, F32, transposed=True)
    o_a = swa_attention(proj_r, sinks.astype(F32), batch, seq)

    bias_row = jnp.zeros((1, LANES), F32).at[0, TAIL_FC:TAIL_FC + C_HEADS].set(f_bias.astype(F32))
    fox_ka, fox_qat = fox_augmentation(tail, bias_row, batch, seq)
    o_c = flash_attention_t("fox", proj_r, proj_t, fox_ka, fox_qat, batch, seq)

    flat = jnp.stack([_chunk_blocks(proj_r[:, OFF_KBC:OFF_KBC + KV_WIDTH], batch, seq),
                      _chunk_blocks(proj_r[:, OFF_VBC:OFF_VBC + KV_WIDTH], batch, seq)])
    pe_rows = jnp.broadcast_to(cmp_pe.reshape(2, 1, CMP_LEN * HEAD_DIM), (2, 8, CMP_LEN * HEAD_DIM)).astype(BF16)
    kvc, kvct = nsa_compress(flat, pe_rows, cmp_w1.astype(BF16), cmp_w2.astype(BF16))
    inter_t, expand = _overlap_matrices(seq)
    o_cmp, sel_t = nsa_compressed_attention(proj_t, kvc, kvct, inter_t, tail_t, batch, seq)
    pos_ka, pos_qat = _alibi_augmentation(seq)
    o_slc = flash_attention_t("slc", proj_r, proj_t, pos_ka, pos_qat, batch, seq, gates_t=tail_t, sel_t=sel_t,
                              expand=expand)
    o_win = flash_attention_t("win", proj_r, proj_t, pos_ka, pos_qat, batch, seq, gates_t=tail_t)
    return mix_out(o_a, o_cmp, o_slc, o_win, o_c, mix_out_norm, w_out.astype(BF16), x)


def dense_layer(x, gain, w_gate, w_up, w_down, index, tm=1024):
    n = x.shape[0]
    h = rmsnorm_rows(x, gain, BF16)
    nblk = n // tm
    return grouped_swiglu(h, jnp.full((nblk,), index, jnp.int32), jnp.full((1,), nblk, jnp.int32),
                          w_gate, w_up, w_down, residual=x, tm=tm)


def kernel(x, w_in, w_out, norm_mix, mix_out_norm, attn_sinks, nsa_cmp_pe, nsa_cmp_w1, nsa_cmp_w2, fox_f_bias,
           norm_ffn, ffn_w_gate, ffn_w_up, ffn_w_down, moe_router, moe_w_gate, moe_w_up, moe_w_down, norm_final):
    batch, seq, d = x.shape
    depth = w_in.shape[0]
    f = ffn_w_gate.shape[-1]
    moe_gate, moe_up = moe_w_gate.reshape(-1, d, f), moe_w_up.reshape(-1, d, f)
    moe_down = moe_w_down.reshape(-1, f, d)
    xf = x.reshape(batch * seq, d).astype(F32)
    for layer in range(depth):
        xf = mixer_layer(xf, batch, seq, w_in[layer], w_out[layer], norm_mix[layer], mix_out_norm[layer],
                         attn_sinks[layer], nsa_cmp_pe[layer], nsa_cmp_w1[layer], nsa_cmp_w2[layer],
                         fox_f_bias[layer])
        i = layer // 2
        if layer % 2 == 0:
            xf = dense_layer(xf, norm_ffn[layer], ffn_w_gate, ffn_w_up, ffn_w_down, i)
        else:
            xf = moe_layer(xf, norm_ffn[layer], moe_router[i], moe_gate, moe_up, moe_down, i * N_EXPERTS)
    return rmsnorm_rows(xf, norm_final, x.dtype).reshape(batch, seq, d)
```

```python
import functools

import jax
import jax.numpy as jnp
import numpy as np
from jax import lax
from jax.experimental import pallas as pl
from jax.experimental.pallas import tpu as pltpu

F32 = jnp.float32
BF16 = jnp.bfloat16

HEAD_DIM = 64
A_HEADS, A_KV_HEADS = 8, 2
B_HEADS, B_KV_HEADS = 12, 2
C_HEADS = 12
B_GROUP = B_HEADS // B_KV_HEADS
A_WIDTH, B_WIDTH, C_WIDTH = A_HEADS * HEAD_DIM, B_HEADS * HEAD_DIM, C_HEADS * HEAD_DIM
KV_WIDTH = 2 * HEAD_DIM
N_GATE_COLS = B_HEADS * 3
WINDOW_A = 128
WINDOW_B = 512
CMP_LEN, CMP_STRIDE = 32, 16
SEL_BLOCK, N_SELECT = 64, 16
N_EXPERTS = 8
RMS_EPS = 1e-6
NEG_INF = -1e30
M_INIT = -1e20
FORCE_BONUS = 1e4
LANES = 128
VMEM_LIMIT = 56 * 1024 * 1024
Q_SCALE = HEAD_DIM ** -0.5

OFF_KC, OFF_KA, OFF_VA, OFF_QA = 0, 768, 896, 1024
OFF_KBC, OFF_VBC, OFF_KBS, OFF_KBW = 1536, 1664, 1792, 1920
ROW_WIDTH = 2048
OFF_QBT, OFF_QCT, OFF_VCT, OFF_VBST, OFF_VBWT = 0, 768, 1536, 2304, 2432
T_WIDTH = 2560
TAIL_FC = N_GATE_COLS
ONE_LANE = LANES - 1
N_AUG = HEAD_DIM
Q_AUG_ROWS = 16
LOG2E = 1.4426950408889634
MASK_PEN = -1e30
VMEM_LIMIT_BIG = 60 * 1024 * 1024


def _alibi(n):
    return [float(2.0 ** (-8.0 * i / n)) for i in range(1, n + 1)]


SLOPES_A = _alibi(A_HEADS)
SLOPES_B = _alibi(B_HEADS)


def _cparams(sem, vmem=VMEM_LIMIT):
    return pltpu.CompilerParams(dimension_semantics=sem, vmem_limit_bytes=vmem)


def _dot(a, b):
    return jnp.dot(a, b, preferred_element_type=F32)


def _dot_nt(a, b):
    return lax.dot_general(a, b, (((1,), (1,)), ((), ())), preferred_element_type=F32)


def _split3(x):
    hi = x.astype(BF16)
    r1 = x - hi.astype(F32)
    mid = r1.astype(BF16)
    lo = (r1 - mid.astype(F32)).astype(BF16)
    return hi, mid, lo


def _rms(x, gain):
    return x * lax.rsqrt(jnp.mean(x * x, axis=-1, keepdims=True) + RMS_EPS) * gain


def _sigmoid(x):
    return 1.0 / (1.0 + jnp.exp(-x))


def _norm_kernel(x_ref, g_ref, o_ref):
    o_ref[...] = _rms(x_ref[...], g_ref[...]).astype(o_ref.dtype)


def rmsnorm_rows(x, gain, out_dtype, tm=512):
    n, d = x.shape
    return pl.pallas_call(
        _norm_kernel,
        grid=(n // tm,),
        in_specs=[pl.BlockSpec((tm, d), lambda i: (i, 0)), pl.BlockSpec((1, d), lambda i: (0, 0))],
        out_specs=pl.BlockSpec((tm, d), lambda i: (i, 0)),
        out_shape=jax.ShapeDtypeStruct((n, d), out_dtype),
        compiler_params=_cparams(("parallel",)),
        name="rmsnorm",
    )(x, gain.reshape(1, d))


def _cast_kernel(x_ref, o_ref):
    o_ref[...] = x_ref[...].astype(o_ref.dtype)


def cast_rows(x, out_dtype, tm=512):
    n, d = x.shape
    return pl.pallas_call(
        _cast_kernel,
        grid=(n // tm,),
        in_specs=[pl.BlockSpec((tm, d), lambda i: (i, 0))],
        out_specs=pl.BlockSpec((tm, d), lambda i: (i, 0)),
        out_shape=jax.ShapeDtypeStruct((n, d), out_dtype),
        compiler_params=_cparams(("parallel",)),
        name="cast_rows",
    )(x)


def _norm_matmul_kernel(x_ref, g_ref, w_ref, o_ref, h_sc, *, transposed):
    @pl.when(pl.program_id(1) == 0)
    def _():
        h_sc[...] = _rms(x_ref[...], g_ref[...]).astype(BF16)

    if transposed:
        o_ref[...] = _dot_nt(w_ref[...], h_sc[...]).astype(o_ref.dtype)
    else:
        o_ref[...] = _dot(h_sc[...], w_ref[...]).astype(o_ref.dtype)


def norm_matmul(x, gain, w, out_dtype, transposed=False, tm=1024, tn=512):
    n, d = x.shape
    nn = w.shape[0] if transposed else w.shape[1]
    tn = min(tn, nn)
    if transposed:
        w_spec = pl.BlockSpec((tn, d), lambda i, j: (j, 0))
        o_spec = pl.BlockSpec((tn, tm), lambda i, j: (j, i))
        o_shape = (nn, n)
    else:
        w_spec = pl.BlockSpec((d, tn), lambda i, j: (0, j))
        o_spec = pl.BlockSpec((tm, tn), lambda i, j: (i, j))
        o_shape = (n, nn)
    return pl.pallas_call(
        functools.partial(_norm_matmul_kernel, transposed=transposed),
        grid=(n // tm, nn // tn),
        in_specs=[pl.BlockSpec((tm, d), lambda i, j: (i, 0)),
                  pl.BlockSpec((1, d), lambda i, j: (0, 0)),
                  w_spec],
        out_specs=o_spec,
        out_shape=jax.ShapeDtypeStruct(o_shape, out_dtype),
        scratch_shapes=[pltpu.VMEM((tm, d), BF16)],
        compiler_params=_cparams(("parallel", "arbitrary")),
        name="norm_matmul_t" if transposed else "norm_matmul",
    )(x, gain.reshape(1, d), w)


def _swa_kernel(sink_ref, q_ref, kp_ref, kc_ref, vp_ref, vc_ref, o_ref):
    n = pl.program_id(1)
    tq = q_ref.shape[0]
    q = q_ref[...]
    k = jnp.concatenate([kp_ref[...], kc_ref[...]], axis=0)
    v = jnp.concatenate([vp_ref[...], vc_ref[...]], axis=0)
    i = lax.broadcasted_iota(jnp.int32, (tq, 2 * tq), 0)
    j = lax.broadcasted_iota(jnp.int32, (tq, 2 * tq), 1)
    rel = tq + i - j
    mask = (rel >= 0) & (rel < WINDOW_A) & ((j >= tq) | (n > 0))
    relf = rel.astype(F32)
    group = A_HEADS // A_KV_HEADS
    for h in range(A_HEADS):
        kv = h // group
        qh = q[:, h * HEAD_DIM:(h + 1) * HEAD_DIM]
        kh = k[:, kv * HEAD_DIM:(kv + 1) * HEAD_DIM]
        vh = v[:, kv * HEAD_DIM:(kv + 1) * HEAD_DIM]
        s = _dot_nt(qh, kh) * Q_SCALE - SLOPES_A[h] * relf
        s = jnp.where(mask, s, NEG_INF)
        sink = sink_ref[h]
        m = jnp.maximum(jnp.max(s, axis=-1, keepdims=True), sink)
        p = jnp.exp(s - m)
        l = jnp.sum(p, axis=-1, keepdims=True) + jnp.exp(sink - m)
        o_ref[:, h * HEAD_DIM:(h + 1) * HEAD_DIM] = _dot(p.astype(BF16), vh) / l


def swa_attention(proj, sinks, batch, seq):
    tq = WINDOW_A
    nb = seq // tq
    kcol, vcol = OFF_KA // KV_WIDTH, OFF_VA // KV_WIDTH

    def cur(col):
        return pl.BlockSpec((tq, KV_WIDTH), lambda b, n: (b * nb + n, col))

    def prev(col):
        return pl.BlockSpec((tq, KV_WIDTH), lambda b, n: (b * nb + jnp.maximum(n - 1, 0), col))

    return pl.pallas_call(
        _swa_kernel,
        grid=(batch, nb),
        in_specs=[pl.BlockSpec(memory_space=pltpu.SMEM),
                  pl.BlockSpec((tq, A_WIDTH), lambda b, n: (b * nb + n, OFF_QA // A_WIDTH)),
                  prev(kcol), cur(kcol), prev(vcol), cur(vcol)],
        out_specs=pl.BlockSpec((tq, A_WIDTH), lambda b, n: (b * nb + n, 0)),
        out_shape=jax.ShapeDtypeStruct((batch * seq, A_WIDTH), F32),
        compiler_params=_cparams(("parallel", "parallel")),
        name="swa_attention",
    )(sinks, proj, proj, proj, proj, proj)


def _fox_aug_kernel(z_ref, b_ref, pk_ref, pq_ref, ka_ref, qa_ref, carry_sc):
    @pl.when(pl.program_id(1) == 0)
    def _():
        carry_sc[...] = jnp.zeros_like(carry_sc)

    z = z_ref[...] + b_ref[...]
    log_f = -(jnp.maximum(-z, 0.0) + jnp.log1p(jnp.exp(-jnp.abs(z))))
    ts = z.shape[0]
    r = lax.broadcasted_iota(jnp.int32, (ts, ts), 0)
    c = lax.broadcasted_iota(jnp.int32, (ts, ts), 1)
    tri = (c <= r).astype(BF16)
    hi, mid, lo = _split3(log_f)
    cum = _dot(tri, hi) + _dot(tri, mid) + _dot(tri, lo) + carry_sc[0:1, :]
    carry_sc[0:1, :] = cum[ts - 1:ts, :]
    hi, mid, lo = _split3(cum * LOG2E)
    lane = lax.broadcasted_iota(jnp.int32, (ts, LANES), 1)
    hi = jnp.where(lane == ONE_LANE, 1.0, hi).astype(BF16)
    ka_ref[...] = (_dot(hi, pk_ref[0]) + _dot(mid, pk_ref[1]) + _dot(lo, pk_ref[2])).astype(BF16)
    qa_ref[...] = (_dot_nt(pq_ref[0], hi) + _dot_nt(pq_ref[1], mid) + _dot_nt(pq_ref[2], lo)).astype(BF16)


def _aug_lane(h):
    return HEAD_DIM if h % 2 == 0 else 0


def _fox_placement():
    pk = np.zeros((3, LANES, C_HEADS * LANES), np.float32)
    pq = np.zeros((3, C_HEADS * Q_AUG_ROWS, LANES), np.float32)
    for h in range(C_HEADS):
        src, kbase, qbase = TAIL_FC + h, h * LANES + _aug_lane(h), h * Q_AUG_ROWS
        for piece in range(3):
            pk[piece, src, kbase + piece] = -1.0
            pq[piece, qbase + 3 + piece, src] = 1.0
            pk[0, ONE_LANE, kbase + 3 + piece] = 1.0
            pq[0, qbase + piece, ONE_LANE] = 1.0
    return jnp.asarray(pk, BF16), jnp.asarray(pq, BF16)


def fox_augmentation(tail, bias_row, batch, seq, ts=512):
    nt = seq // ts
    pk, pq = _fox_placement()
    kw, qw = C_HEADS * LANES, C_HEADS * Q_AUG_ROWS
    return pl.pallas_call(
        _fox_aug_kernel,
        grid=(batch, nt),
        in_specs=[pl.BlockSpec((ts, LANES), lambda b, i: (b * nt + i, 0)),
                  pl.BlockSpec((1, LANES), lambda b, i: (0, 0)),
                  pl.BlockSpec((3, LANES, kw), lambda b, i: (0, 0, 0)),
                  pl.BlockSpec((3, qw, LANES), lambda b, i: (0, 0, 0))],
        out_specs=[pl.BlockSpec((ts, kw), lambda b, i: (b * nt + i, 0)),
                   pl.BlockSpec((qw, ts), lambda b, i: (0, b * nt + i))],
        out_shape=[jax.ShapeDtypeStruct((batch * seq, kw), BF16),
                   jax.ShapeDtypeStruct((qw, batch * seq), BF16)],
        scratch_shapes=[pltpu.VMEM((8, LANES), F32)],
        compiler_params=_cparams(("parallel", "arbitrary")),
        name="fox_augmentation",
    )(tail, bias_row, pk, pq)


def _alibi_augmentation(seq, tk):
    pos = np.arange(seq)
    ka = np.zeros((seq, LANES), np.float32)
    for base in (0, HEAD_DIM):
        ka[:, base:base + 3] = 1.0
        ka[:, base + 3:base + 6] = ((pos >> 8) << 8)[:, None]
        ka[:, base + 6:base + 9] = (pos & 255)[:, None]
        ka[pos, base + Q_AUG_ROWS + (pos % tk) // SEL_BLOCK] = 1.0
    slopes = jnp.asarray(SLOPES_B, F32) * LOG2E
    st = _split3(-(slopes[:, None] * jnp.asarray(pos, F32)[None, :]))
    sl = _split3(slopes)
    qa = jnp.zeros((B_HEADS, Q_AUG_ROWS, seq), BF16)
    for piece in range(3):
        qa = qa.at[:, piece, :].set(st[piece])
        qa = qa.at[:, 3 + piece, :].set(sl[piece][:, None])
        qa = qa.at[:, 6 + piece, :].set(sl[piece][:, None])
    return jnp.asarray(ka, BF16), qa.reshape(B_HEADS * Q_AUG_ROWS, seq)


def _flash_t_kernel(*refs, mode, tq, tk, n_steps):
    pen_sc = None
    if mode == "slc":
        k_ref, ka_ref, qt_ref, qat_ref, vt_ref, gate_ref, selt_ref, o_ref, m_sc, l_sc, acc_sc, pen_sc = refs
    elif mode == "win":
        k_ref, ka_ref, qt_ref, qat_ref, vt_ref, gate_ref, o_ref, m_sc, l_sc, acc_sc = refs
    else:
        k_ref, ka_ref, qt_ref, qat_ref, vt_ref, o_ref, m_sc, l_sc, acc_sc = refs
    n_heads = qt_ref.shape[0] // HEAD_DIM
    group = 1 if mode == "fox" else B_GROUP
    blocks_per_tile = tk // SEL_BLOCK
    iq, j = pl.program_id(1), pl.program_id(2)
    if mode == "win":
        ik = iq - (n_steps - 1) + j
        last_j = n_steps - 1
        active = ik >= 0
    else:
        ik = j
        last_j = (iq * tq + tq - 1) // tk
        active = j <= last_j

    @pl.when(j == 0)
    def _():
        m_sc[...] = jnp.full_like(m_sc, M_INIT)
        l_sc[...] = jnp.zeros_like(l_sc)
        acc_sc[...] = jnp.zeros_like(acc_sc)
        if mode == "slc":
            pen_sc[...] = (1.0 - selt_ref[0].astype(F32)) * MASK_PEN

    def step(masked):
        ok = None
        if masked:
            rel = ((iq * tq + lax.broadcasted_iota(jnp.int32, (tk, tq), 1))
                   - (ik * tk + lax.broadcasted_iota(jnp.int32, (tk, tq), 0)))
            ok = rel >= 0
            if mode == "win":
                ok = ok & (rel < WINDOW_B)
        lane = lax.broadcasted_iota(jnp.int32, (tk, LANES), 1)
        ones_rows = jnp.ones((Q_AUG_ROWS, tk), BF16)
        per_kv = {}

        def kv_operands(kv):
            if kv not in per_kv:
                pair = slice((kv // 2) * LANES, (kv // 2 + 1) * LANES)
                aug = ka_ref[:, kv * LANES:(kv + 1) * LANES] if mode == "fox" else ka_ref[...]
                own = (lane < HEAD_DIM) if kv % 2 == 0 else (lane >= HEAD_DIM)
                ka = jnp.where(own, k_ref[:, pair], aug)
                v_aug = jnp.concatenate([vt_ref[kv * HEAD_DIM:(kv + 1) * HEAD_DIM, :], ones_rows], axis=0)
                pen = None
                if mode == "slc":
                    pen8 = pen_sc[kv, pl.ds(pl.multiple_of(ik * blocks_per_tile, blocks_per_tile), blocks_per_tile), :]
                    pen = jnp.concatenate(
                        [pen8, jnp.zeros((Q_AUG_ROWS - blocks_per_tile, tq), F32)], axis=0).astype(BF16)
                per_kv[kv] = (ka, v_aug, pen)
            return per_kv[kv]

        def scores(h):
            kv = h // group
            ka, _, pen = kv_operands(kv)
            aug_rows = [qat_ref[h * Q_AUG_ROWS:(h + 1) * Q_AUG_ROWS, :]]
            if mode == "slc":
                aug_rows.append(pen)
            aug_rows.append(jnp.zeros((N_AUG - Q_AUG_ROWS * len(aug_rows), tq), BF16))
            halves = [[qt_ref[h * HEAD_DIM:(h + 1) * HEAD_DIM, :]], aug_rows]
            qa = jnp.concatenate(sum(halves if kv % 2 == 0 else halves[::-1], []), axis=0)
            s = _dot(ka, qa)
            return s if ok is None else jnp.where(ok, s, NEG_INF)

        def probabilities(h, s):
            m_prev = m_sc[h:h + 1, :]
            m_new = jnp.maximum(m_prev, jnp.max(s, axis=0, keepdims=True))
            m_sc[h:h + 1, :] = m_new
            return jnp.exp2(s - m_new).astype(BF16), jnp.exp2(m_prev - m_new)

        def accumulate(h, p, alpha):
            rows = slice(h * HEAD_DIM, (h + 1) * HEAD_DIM)
            pv = _dot(kv_operands(h // group)[1], p)
            l_sc[h:h + 1, :] = alpha * l_sc[h:h + 1, :] + pv[HEAD_DIM:HEAD_DIM + 1, :]
            acc_sc[rows, :] = alpha * acc_sc[rows, :] + pv[0:HEAD_DIM, :]

        s_cur, pending = scores(0), None
        for h in range(n_heads):
            s_next = scores(h + 1) if h + 1 < n_heads else None
            p_alpha = probabilities(h, s_cur)
            if pending is not None:
                accumulate(*pending)
            pending = (h,) + p_alpha
            s_cur = s_next
        accumulate(*pending)

    if mode == "win":
        @pl.when(active)
        def _():
            step(True)
    else:
        @pl.when(active & (ik * tk + tk - 1 > iq * tq))
        def _():
            step(True)

        @pl.when(active & (ik * tk + tk - 1 <= iq * tq))
        def _():
            step(False)

    @pl.when(j == last_j)
    def _():
        branch = {"fox": None, "slc": 1, "win": 2}[mode]
        for pair in range(n_heads // 2):
            rows = slice(pair * LANES, (pair + 1) * LANES)
            heads = (2 * pair, 2 * pair + 1)
            denom = jnp.concatenate(
                [jnp.broadcast_to(l_sc[h:h + 1, :], (HEAD_DIM, tq)) for h in heads], axis=0)
            out = acc_sc[rows, :] / denom
            if branch is not None:
                out = out * jnp.concatenate(
                    [jnp.broadcast_to(_sigmoid(gate_ref[3 * h + branch:3 * h + branch + 1, :]), (HEAD_DIM, tq))
                     for h in heads], axis=0)
            o_ref[:, rows] = out.T


def flash_attention_t(mode, proj_r, proj_t, k_aug, q_aug_t, batch, seq, gates_t=None, sel_t=None, tq=512, tk=512):
    nq, nk = seq // tq, seq // tk
    if mode == "win":
        assert tq == tk
        n_steps = WINDOW_B // tk + 1

        def kblock(iq, j):
            return jnp.maximum(iq - (n_steps - 1) + j, 0)
    else:
        n_steps = nk

        def kblock(iq, j):
            return jnp.minimum(j, (iq * tq + tq - 1) // tk)

    if mode == "fox":
        in_specs = [pl.BlockSpec((tk, C_WIDTH), lambda b, iq, j: (b * nk + kblock(iq, j), OFF_KC // C_WIDTH)),
                    pl.BlockSpec((tk, C_HEADS * LANES), lambda b, iq, j: (b * nk + kblock(iq, j), 0)),
                    pl.BlockSpec((C_WIDTH, tq), lambda b, iq, j: (OFF_QCT // C_WIDTH, b * nq + iq)),
                    pl.BlockSpec((C_HEADS * Q_AUG_ROWS, tq), lambda b, iq, j: (0, b * nq + iq)),
                    pl.BlockSpec((C_WIDTH, tk), lambda b, iq, j: (OFF_VCT // C_WIDTH, b * nk + kblock(iq, j)))]
    else:
        kcol = (OFF_KBS if mode == "slc" else OFF_KBW) // KV_WIDTH
        vrow = (OFF_VBST if mode == "slc" else OFF_VBWT) // KV_WIDTH
        in_specs = [pl.BlockSpec((tk, KV_WIDTH), lambda b, iq, j: (b * nk + kblock(iq, j), kcol)),
                    pl.BlockSpec((tk, LANES), lambda b, iq, j: (kblock(iq, j), 0)),
                    pl.BlockSpec((B_WIDTH, tq), lambda b, iq, j: (OFF_QBT // B_WIDTH, b * nq + iq)),
                    pl.BlockSpec((B_HEADS * Q_AUG_ROWS, tq), lambda b, iq, j: (0, iq)),
                    pl.BlockSpec((KV_WIDTH, tk), lambda b, iq, j: (vrow, b * nk + kblock(iq, j)))]
    args = [proj_r, k_aug, proj_t, q_aug_t, proj_t]
    width = C_WIDTH if mode == "fox" else B_WIDTH
    n_heads = width // HEAD_DIM
    scratch = [pltpu.VMEM((16, tq), F32), pltpu.VMEM((16, tq), F32), pltpu.VMEM((n_heads * HEAD_DIM, tq), F32)]
    if mode != "fox":
        in_specs.append(pl.BlockSpec((LANES, tq), lambda b, iq, j: (0, b * nq + iq)))
        args.append(gates_t)
    if mode == "slc":
        n_slc = seq // SEL_BLOCK
        assert tk // SEL_BLOCK <= Q_AUG_ROWS and tk % SEL_BLOCK == 0
        in_specs.append(pl.BlockSpec((1, B_KV_HEADS, n_slc, tq), lambda b, iq, j: (b, 0, 0, iq)))
        args.append(sel_t)
        scratch.append(pltpu.VMEM((B_KV_HEADS, n_slc, tq), F32))
    return pl.pallas_call(
        functools.partial(_flash_t_kernel, mode=mode, tq=tq, tk=tk, n_steps=n_steps),
        grid=(batch, nq, n_steps),
        in_specs=in_specs,
        out_specs=pl.BlockSpec((tq, width), lambda b, iq, j: (b * nq + iq, 0)),
        out_shape=jax.ShapeDtypeStruct((batch * seq, width), F32),
        scratch_shapes=scratch,
        compiler_params=_cparams(("parallel", "parallel", "arbitrary")),
        name=mode + "_attention",
    )(*args)


def _compress_kernel(t_ref, pe_ref, w1_ref, w2_ref, w2t_ref, o_ref, ot_ref):
    w1 = w1_ref[0]
    hid = _dot(t_ref[0, 0, 0], w1) + _dot(pe_ref[0], w1)[0:1, :]
    act = (hid * _sigmoid(hid)).astype(BF16)
    o_ref[0, 0, 0] = _dot(act, w2_ref[0]).astype(o_ref.dtype)
    ot_ref[0, 0, 0] = _dot_nt(w2t_ref[0], act).astype(ot_ref.dtype)


def nsa_compress(flat, pe_rows, w1, w2):
    _, batch, n_kv, n_chunks, width = flat.shape
    hidden = w1.shape[-1]
    return pl.pallas_call(
        _compress_kernel,
        grid=(2, batch, n_kv),
        in_specs=[pl.BlockSpec((1, 1, 1, n_chunks, width), lambda s, b, h: (s, b, h, 0, 0)),
                  pl.BlockSpec((1, 8, width), lambda s, b, h: (s, 0, 0)),
                  pl.BlockSpec((1, width, hidden), lambda s, b, h: (s, 0, 0)),
                  pl.BlockSpec((1, hidden, HEAD_DIM), lambda s, b, h: (s, 0, 0)),
                  pl.BlockSpec((1, HEAD_DIM, hidden), lambda s, b, h: (s, 0, 0))],
        out_specs=[pl.BlockSpec((1, 1, 1, n_chunks, HEAD_DIM), lambda s, b, h: (s, b, h, 0, 0)),
                   pl.BlockSpec((1, 1, 1, HEAD_DIM, n_chunks), lambda s, b, h: (s, b, h, 0, 0))],
        out_shape=[jax.ShapeDtypeStruct((2, batch, n_kv, n_chunks, HEAD_DIM), BF16),
                   jax.ShapeDtypeStruct((2, batch, n_kv, HEAD_DIM, n_chunks), BF16)],
        compiler_params=_cparams(("parallel", "parallel", "parallel")),
        name="nsa_compress",
    )(flat, pe_rows, w1, w2, jnp.swapaxes(w2, 1, 2))


def _cmp_attn_t_kernel(qt_ref, kc_ref, vct_ref, inter_ref, gate_ref, o_ref, selt_ref, ot_sc, *, tq):
    iq = pl.program_id(1)
    n_chunks = kc_ref.shape[3]
    n_slc = inter_ref.shape[0]
    t = iq * tq + lax.broadcasted_iota(jnp.int32, (n_chunks, tq), 1)
    n = lax.broadcasted_iota(jnp.int32, (n_chunks, tq), 0)
    rel = t - (n * CMP_STRIDE + CMP_LEN - 1)
    mask = rel >= 0
    relf = rel.astype(F32)
    t_s = iq * tq + lax.broadcasted_iota(jnp.int32, (n_slc, tq), 1)
    jj = lax.broadcasted_iota(jnp.int32, (n_slc, tq), 0)
    cur = t_s >> (SEL_BLOCK.bit_length() - 1)
    valid = jj * SEL_BLOCK <= t_s
    forced = (jj == 0) | (jj == cur) | (jj == cur - 1)
    jf = jj.astype(F32)
    inter = inter_ref[...]
    for h in range(B_KV_HEADS):
        kc = kc_ref[0, 0, h]
        vct = vct_ref[0, 0, h]
        p_sum = jnp.zeros((n_chunks, tq), F32)
        for g in range(B_GROUP):
            hd = h * B_GROUP + g
            rows = slice(hd * HEAD_DIM, (hd + 1) * HEAD_DIM)
            s = jnp.where(mask, _dot(kc, qt_ref[rows, :]) - (SLOPES_B[hd] * LOG2E) * relf, NEG_INF)
            m = jnp.max(s, axis=0, keepdims=True)
            e = jnp.where(mask, jnp.exp2(s - m), 0.0)
            l = jnp.sum(e, axis=0, keepdims=True)
            p = e / jnp.where(l > 0.0, l, 1.0)
            p_sum = p_sum + p
            ot_sc[rows, :] = _dot(vct, p.astype(BF16)) * _sigmoid(gate_ref[3 * hd:3 * hd + 1, :])
        hi, mid, lo = _split3(p_sum)
        importance = _dot(inter, hi) + _dot(inter, mid) + _dot(inter, lo)
        score = jnp.where(valid, importance + jnp.where(forced, FORCE_BONUS, 0.0), NEG_INF)
        chosen = jnp.zeros((n_slc, tq), F32)
        for _ in range(min(N_SELECT, n_slc)):
            best = jnp.max(score, axis=0, keepdims=True)
            first = jnp.min(jnp.where(score == best, jf, float(n_slc)), axis=0, keepdims=True)
            pick = jf == first
            chosen = jnp.where(pick, 1.0, chosen)
            score = jnp.where(pick, -jnp.inf, score)
        selt_ref[0, h] = jnp.where(valid, chosen, 0.0).astype(selt_ref.dtype)
    for pair in range(B_HEADS // 2):
        rows = slice(pair * LANES, (pair + 1) * LANES)
        o_ref[:, rows] = ot_sc[rows, :].T


def nsa_compressed_attention(proj_t, kvc, kvct, inter_t, gates_t, batch, seq, tq=512):
    nq = seq // tq
    n_chunks = kvc.shape[3]
    n_slc = seq // SEL_BLOCK
    return pl.pallas_call(
        functools.partial(_cmp_attn_t_kernel, tq=tq),
        grid=(batch, nq),
        in_specs=[pl.BlockSpec((B_WIDTH, tq), lambda b, i: (OFF_QBT // B_WIDTH, b * nq + i)),
                  pl.BlockSpec((1, 1, B_KV_HEADS, n_chunks, HEAD_DIM), lambda b, i: (0, b, 0, 0, 0)),
                  pl.BlockSpec((1, 1, B_KV_HEADS, HEAD_DIM, n_chunks), lambda b, i: (1, b, 0, 0, 0)),
                  pl.BlockSpec((n_slc, n_chunks), lambda b, i: (0, 0)),
                  pl.BlockSpec((LANES, tq), lambda b, i: (0, b * nq + i))],
        out_specs=[pl.BlockSpec((tq, B_WIDTH), lambda b, i: (b * nq + i, 0)),
                   pl.BlockSpec((1, B_KV_HEADS, n_slc, tq), lambda b, i: (b, 0, 0, i))],
        out_shape=[jax.ShapeDtypeStruct((batch * seq, B_WIDTH), F32),
                   jax.ShapeDtypeStruct((batch, B_KV_HEADS, n_slc, seq), BF16)],
        scratch_shapes=[pltpu.VMEM((B_WIDTH, tq), F32)],
        compiler_params=_cparams(("parallel", "parallel")),
        name="nsa_cmp_attention",
    )(proj_t, kvc, kvct, inter_t, gates_t)


def _mix_out_kernel(oa_ref, ocmp_ref, oslc_ref, owin_ref, oc_ref, g_ref, w_ref, x_ref, o_ref, mixed_sc):
    @pl.when(pl.program_id(1) == 0)
    def _():
        o_b = ocmp_ref[...] + oslc_ref[...] + owin_ref[...]
        b0, c0 = A_WIDTH, A_WIDTH + B_WIDTH
        mixed_sc[:, 0:b0] = _rms(oa_ref[...], g_ref[:, 0:b0]).astype(BF16)
        mixed_sc[:, b0:c0] = _rms(o_b, g_ref[:, b0:c0]).astype(BF16)
        mixed_sc[:, c0:] = _rms(oc_ref[...], g_ref[:, c0:]).astype(BF16)

    o_ref[...] = x_ref[...] + _dot(mixed_sc[...], w_ref[...])


def mix_out(o_a, o_cmp, o_slc, o_win, o_c, gain, w_out, x, tm=512, tn=512):
    n, d = x.shape
    width = w_out.shape[0]

    def rows(w):
        return pl.BlockSpec((tm, w), lambda i, j: (i, 0))

    return pl.pallas_call(
        _mix_out_kernel,
        grid=(n // tm, d // tn),
        in_specs=[rows(A_WIDTH), rows(B_WIDTH), rows(B_WIDTH), rows(B_WIDTH), rows(C_WIDTH),
                  pl.BlockSpec((1, width), lambda i, j: (0, 0)),
                  pl.BlockSpec((width, tn), lambda i, j: (0, j)),
                  pl.BlockSpec((tm, tn), lambda i, j: (i, j))],
        out_specs=pl.BlockSpec((tm, tn), lambda i, j: (i, j)),
        out_shape=jax.ShapeDtypeStruct((n, d), F32),
        scratch_shapes=[pltpu.VMEM((tm, width), BF16)],
        compiler_params=_cparams(("parallel", "arbitrary")),
        name="mix_out",
    )(o_a, o_cmp, o_slc, o_win, o_c, gain.reshape(1, width), w_out, x)


def _fresh_weights(be_ref):
    i = pl.program_id(1)
    return (i == 0) | (be_ref[i] != be_ref[jnp.maximum(i - 1, 0)])


def _ffn_up_kernel(be_ref, nv_ref, x_ref, wg_ref, wu_ref, o_ref, wg_sc, wu_sc):
    i = pl.program_id(1)

    @pl.when(_fresh_weights(be_ref))
    def _():
        wg_sc[...] = wg_ref[0].astype(BF16)
        wu_sc[...] = wu_ref[0].astype(BF16)

    @pl.when(i < nv_ref[0])
    def _():
        x = x_ref[...].astype(BF16)
        gate = _dot(x, wg_sc[...])
        up = _dot(x, wu_sc[...])
        o_ref[...] = (gate * _sigmoid(gate) * up).astype(o_ref.dtype)

    @pl.when(i >= nv_ref[0])
    def _():
        o_ref[...] = jnp.zeros_like(o_ref)


def _ffn_down_kernel(be_ref, nv_ref, h_ref, wd_ref, *rest):
    o_ref, wd_sc = rest[-2], rest[-1]
    i = pl.program_id(1)

    @pl.when(_fresh_weights(be_ref))
    def _():
        wd_sc[...] = wd_ref[0].astype(BF16)

    @pl.when(i < nv_ref[0])
    def _():
        y = _dot(h_ref[...], wd_sc[...])
        if len(rest) == 3:
            y = y + rest[0][...]
        o_ref[...] = y

    @pl.when(i >= nv_ref[0])
    def _():
        o_ref[...] = jnp.zeros_like(o_ref)


def grouped_swiglu(xs, block_expert, n_valid, w_gate, w_up, w_down, residual=None, tm=512, tf=512, tm_down=512,
                   tn=512):
    rows, d = xs.shape
    f = w_gate.shape[-1]
    nblk = rows // tm

    def xrow(i, nv):
        return jnp.minimum(i, nv[0] - 1)

    hidden = pl.pallas_call(
        _ffn_up_kernel,
        grid_spec=pltpu.PrefetchScalarGridSpec(
            num_scalar_prefetch=2,
            grid=(f // tf, nblk),
            in_specs=[pl.BlockSpec((tm, d), lambda j, i, be, nv: (xrow(i, nv), 0)),
                      pl.BlockSpec((1, d, tf), lambda j, i, be, nv: (be[i], 0, j)),
                      pl.BlockSpec((1, d, tf), lambda j, i, be, nv: (be[i], 0, j))],
            out_specs=pl.BlockSpec((tm, tf), lambda j, i, be, nv: (i, j)),
            scratch_shapes=[pltpu.VMEM((d, tf), BF16), pltpu.VMEM((d, tf), BF16)]),
        out_shape=jax.ShapeDtypeStruct((rows, f), BF16),
        compiler_params=_cparams(("parallel", "arbitrary"), VMEM_LIMIT_BIG),
        name="ffn_up",
    )(block_expert, n_valid, xs, w_gate, w_up)

    split = tm // tm_down
    in_specs = [pl.BlockSpec((tm_down, f), lambda j, i, be, nv: (xrow(i, nv), 0)),
                pl.BlockSpec((1, f, tn), lambda j, i, be, nv: (be[i], 0, j))]
    args = [hidden, w_down]
    if residual is not None:
        in_specs.append(pl.BlockSpec((tm_down, tn), lambda j, i, be, nv: (i, j)))
        args.append(residual)
    return pl.pallas_call(
        _ffn_down_kernel,
        grid_spec=pltpu.PrefetchScalarGridSpec(
            num_scalar_prefetch=2,
            grid=(d // tn, nblk * split),
            in_specs=in_specs,
            out_specs=pl.BlockSpec((tm_down, tn), lambda j, i, be, nv: (i, j)),
            scratch_shapes=[pltpu.VMEM((f, tn), BF16)]),
        out_shape=jax.ShapeDtypeStruct((rows, d), F32),
        compiler_params=_cparams(("parallel", "arbitrary"), VMEM_LIMIT_BIG),
        name="ffn_down",
    )(jnp.repeat(block_expert, split), n_valid * split, *args)


def _router_kernel(x_ref, g_ref, wr_ref, h_ref, ri_ref, rf_ref, cnt_ref, carry_sc):
    @pl.when(pl.program_id(0) == 0)
    def _():
        carry_sc[...] = jnp.zeros_like(carry_sc)

    tm = x_ref.shape[0]
    h = _rms(x_ref[...], g_ref[...])
    h_ref[...] = h
    h1, h2, h3 = _split3(h)
    w1, w2, w3 = wr_ref[0], wr_ref[1], wr_ref[2]
    logits = (_dot(h1, w1) + _dot(h1, w2) + _dot(h2, w1)) + (_dot(h1, w3) + _dot(h2, w2) + _dot(h3, w1))
    lane_i = lax.broadcasted_iota(jnp.int32, (tm, LANES), 1)
    lane = lane_i.astype(F32)
    logits = jnp.where(lane_i < N_EXPERTS, logits, -jnp.inf)
    v1 = jnp.max(logits, axis=-1, keepdims=True)
    e1 = jnp.min(jnp.where(logits == v1, lane, float(LANES)), axis=-1, keepdims=True)
    rest = jnp.where(lane == e1, -jnp.inf, logits)
    v2 = jnp.max(rest, axis=-1, keepdims=True)
    e2 = jnp.min(jnp.where(rest == v2, lane, float(LANES)), axis=-1, keepdims=True)
    z = jnp.exp(v2 - v1)
    g1 = 1.0 / (1.0 + z)
    g2 = z / (1.0 + z)
    chosen = (lane == e1) | (lane == e2)
    onehot = jnp.where(chosen, 1.0, 0.0)
    r = lax.broadcasted_iota(jnp.int32, (tm, tm), 0)
    c = lax.broadcasted_iota(jnp.int32, (tm, tm), 1)
    before = _dot((c < r).astype(BF16), onehot.astype(BF16)) + carry_sc[0:1, :]
    r1 = jnp.sum(jnp.where(lane == e1, before, 0.0), axis=-1, keepdims=True)
    r2 = jnp.sum(jnp.where(lane == e2, before, 0.0), axis=-1, keepdims=True)
    carry = carry_sc[0:1, :] + jnp.sum(onehot, axis=0, keepdims=True)
    carry_sc[0:1, :] = carry
    cnt_ref[...] = jnp.broadcast_to(carry, cnt_ref.shape)
    packed = jnp.where(lane_i == 0, e1, jnp.where(lane_i == 1, e2, jnp.where(
        lane_i == 2, r1, jnp.where(lane_i == 3, r2, 0.0))))
    ri_ref[...] = packed.astype(jnp.int32)
    rf_ref[...] = jnp.where(lane_i == 0, g1, jnp.where(lane_i == 1, g2, 0.0))


def moe_route(x, gain, w_router3, tm=512):
    n, d = x.shape
    return pl.pallas_call(
        _router_kernel,
        grid=(n // tm,),
        in_specs=[pl.BlockSpec((tm, d), lambda i: (i, 0)),
                  pl.BlockSpec((1, d), lambda i: (0, 0)),
                  pl.BlockSpec((3, d, LANES), lambda i: (0, 0, 0))],
        out_specs=[pl.BlockSpec((tm, d), lambda i: (i, 0)),
                   pl.BlockSpec((tm, LANES), lambda i: (i, 0)),
                   pl.BlockSpec((tm, LANES), lambda i: (i, 0)),
                   pl.BlockSpec((8, LANES), lambda i: (0, 0))],
        out_shape=[jax.ShapeDtypeStruct((n, d), F32),
                   jax.ShapeDtypeStruct((n, LANES), jnp.int32),
                   jax.ShapeDtypeStruct((n, LANES), F32),
                   jax.ShapeDtypeStruct((8, LANES), F32)],
        scratch_shapes=[pltpu.VMEM((8, LANES), F32)],
        compiler_params=_cparams(("arbitrary",)),
        name="moe_route",
    )(x, gain.reshape(1, d), w_router3)


def _row_copy(src_ref, src_row, dst_ref, dst_row, sem):
    return pltpu.make_async_copy(src_ref.at[pl.ds(src_row, 1)], dst_ref.at[pl.ds(dst_row, 1)], sem)


def _dispatch_kernel(dest_ref, h_ref, zeros_ref, xs_ref, sem):
    del zeros_ref
    td = h_ref.shape[0]
    base = pl.program_id(0) * td

    def start(r, carry):
        for k in range(2):
            _row_copy(h_ref, r, xs_ref, dest_ref[2 * (base + r) + k], sem).start()
        return carry

    def wait(r, carry):
        for k in range(2):
            _row_copy(h_ref, r, xs_ref, dest_ref[2 * (base + r) + k], sem).wait()
        return carry

    lax.fori_loop(0, td, start, 0)
    lax.fori_loop(0, td, wait, 0)


def moe_dispatch(h, dest, cap, td=256):
    n, d = h.shape
    return pl.pallas_call(
        _dispatch_kernel,
        grid_spec=pltpu.PrefetchScalarGridSpec(
            num_scalar_prefetch=1,
            grid=(n // td,),
            in_specs=[pl.BlockSpec((td, d), lambda i, dest: (i, 0)),
                      pl.BlockSpec(memory_space=pl.ANY)],
            out_specs=pl.BlockSpec(memory_space=pl.ANY),
            scratch_shapes=[pltpu.SemaphoreType.DMA(())]),
        out_shape=jax.ShapeDtypeStruct((cap, d), h.dtype),
        input_output_aliases={2: 0},
        compiler_params=_cparams(("arbitrary",)),
        name="moe_dispatch",
    )(dest, h, jnp.zeros((cap, d), h.dtype))


def _combine_kernel(dest_ref, x_ref, gate_ref, y_ref, o_ref, ya_sc, yb_sc, sem):
    tc = x_ref.shape[0]
    base = pl.program_id(0) * tc

    def start(r, carry):
        _row_copy(y_ref, dest_ref[2 * (base + r)], ya_sc, r, sem).start()
        _row_copy(y_ref, dest_ref[2 * (base + r) + 1], yb_sc, r, sem).start()
        return carry

    def wait(r, carry):
        _row_copy(y_ref, dest_ref[2 * (base + r)], ya_sc, r, sem).wait()
        _row_copy(y_ref, dest_ref[2 * (base + r) + 1], yb_sc, r, sem).wait()
        return carry

    lax.fori_loop(0, tc, start, 0)
    lax.fori_loop(0, tc, wait, 0)
    gates = gate_ref[...]
    o_ref[...] = x_ref[...] + (gates[:, 0:1] * ya_sc[...] + gates[:, 1:2] * yb_sc[...])


def moe_combine(x, gates, y, dest, tc=256):
    n, d = x.shape
    return pl.pallas_call(
        _combine_kernel,
        grid_spec=pltpu.PrefetchScalarGridSpec(
            num_scalar_prefetch=1,
            grid=(n // tc,),
            in_specs=[pl.BlockSpec((tc, d), lambda i, dest: (i, 0)),
                      pl.BlockSpec((tc, LANES), lambda i, dest: (i, 0)),
                      pl.BlockSpec(memory_space=pl.ANY)],
            out_specs=pl.BlockSpec((tc, d), lambda i, dest: (i, 0)),
            scratch_shapes=[pltpu.VMEM((tc, d), F32), pltpu.VMEM((tc, d), F32),
                            pltpu.SemaphoreType.DMA(())]),
        out_shape=jax.ShapeDtypeStruct((n, d), F32),
        compiler_params=_cparams(("arbitrary",)),
        name="moe_combine",
    )(dest, x, gates, y)


def moe_layer(x, gain, w_router, w_gate, w_up, w_down, first_expert, tm=512):
    n, d = x.shape
    wr = jnp.pad(w_router.astype(F32), ((0, 0), (0, LANES - N_EXPERTS)))
    h, info, gates, counts = moe_route(x, gain, jnp.stack(_split3(wr)))
    counts = counts[0, :N_EXPERTS].astype(jnp.int32)
    padded = (counts + tm - 1) // tm * tm
    pad_ends = jnp.cumsum(padded)
    pad_starts = pad_ends - padded
    experts, ranks = info[:, 0:2], info[:, 2:4]
    dest = (pad_starts[experts] + ranks).reshape(-1).astype(jnp.int32)
    nblk = (2 * n) // tm + N_EXPERTS
    n_valid = (pad_ends[-1] // tm).astype(jnp.int32).reshape(1)
    blk = jnp.minimum(jnp.arange(nblk, dtype=jnp.int32), n_valid[0] - 1) * tm
    block_expert = jnp.minimum(jnp.sum(blk[:, None] >= pad_ends[None, :], axis=1), N_EXPERTS - 1).astype(jnp.int32)
    xs = cast_rows(moe_dispatch(h, dest, nblk * tm), BF16)
    y = grouped_swiglu(xs, block_expert + first_expert, n_valid, w_gate, w_up, w_down, tm=tm, tf=1024, tm_down=tm)
    return moe_combine(x, gates, y, dest)


def _project_weights(w):
    def cols(a, n):
        return w[:, a:a + n]

    qa, ka, va, qb = cols(0, 512), cols(512, 128), cols(640, 128), cols(768, 768)
    kbc, vbc, kbs, vbs, kbw, vbw = (cols(1536 + 128 * i, 128) for i in range(6))
    gb, qc, kc, vc, fc = cols(2304, 36), cols(2340, 768), cols(3108, 768), cols(3876, 768), cols(4644, 12)
    w_rows = jnp.concatenate([kc, ka, va, qa, kbc, vbc, kbs, kbw], axis=1).astype(BF16)
    q_fold = Q_SCALE * LOG2E
    w_t = jnp.concatenate([qb * q_fold, qc * q_fold, vc, vbs, vbw], axis=1).T.astype(BF16)
    tail = jnp.concatenate([gb, fc], axis=1)
    tail = jnp.pad(tail, ((0, 0), (0, LANES - tail.shape[1]))).astype(BF16)
    return w_rows, w_t, tail


def _chunk_blocks(cols, batch, seq):
    n_chunks = seq // CMP_STRIDE
    t = cols.reshape(batch, seq, B_KV_HEADS, HEAD_DIM).transpose(0, 2, 1, 3)
    t = t.reshape(batch, B_KV_HEADS, n_chunks, CMP_STRIDE * HEAD_DIM)
    nxt = jnp.concatenate([t[:, :, 1:], jnp.zeros_like(t[:, :, :1])], axis=2)
    return jnp.concatenate([t, nxt], axis=-1)


def _overlap_matrix_t(seq):
    n_chunks, n_slc = seq // CMP_STRIDE, seq // SEL_BLOCK
    c_start = np.arange(n_chunks) * CMP_STRIDE
    s_start = np.arange(n_slc) * SEL_BLOCK
    inter = np.maximum(np.minimum(c_start[:, None] + CMP_LEN, s_start[None, :] + SEL_BLOCK)
                       - np.maximum(c_start[:, None], s_start[None, :]), 0) / CMP_LEN
    inter[(seq - CMP_LEN) // CMP_STRIDE + 1:] = 0.0
    return jnp.asarray(inter.T, BF16)


def mixer_layer(x, batch, seq, w_in, w_out, norm_mix, mix_out_norm, sinks, cmp_pe, cmp_w1, cmp_w2, f_bias):
    w_rows, w_t, w_tail = _project_weights(w_in)
    proj_r = norm_matmul(x, norm_mix, w_rows, BF16)
    proj_t = norm_matmul(x, norm_mix, w_t, BF16, transposed=True)
    tail = norm_matmul(x, norm_mix, w_tail, F32)
    tail_t = norm_matmul(x, norm_mix, w_tail.T, F32, transposed=True)
    o_a = swa_attention(proj_r, sinks.astype(F32), batch, seq)

    bias_row = jnp.zeros((1, LANES), F32).at[0, TAIL_FC:TAIL_FC + C_HEADS].set(f_bias.astype(F32))
    fox_ka, fox_qat = fox_augmentation(tail, bias_row, batch, seq)
    o_c = flash_attention_t("fox", proj_r, proj_t, fox_ka, fox_qat, batch, seq)

    flat = jnp.stack([_chunk_blocks(proj_r[:, OFF_KBC:OFF_KBC + KV_WIDTH], batch, seq),
                      _chunk_blocks(proj_r[:, OFF_VBC:OFF_VBC + KV_WIDTH], batch, seq)])
    pe_rows = jnp.broadcast_to(cmp_pe.reshape(2, 1, CMP_LEN * HEAD_DIM), (2, 8, CMP_LEN * HEAD_DIM)).astype(BF16)
    kvc, kvct = nsa_compress(flat, pe_rows, cmp_w1.astype(BF16), cmp_w2.astype(BF16))
    o_cmp, sel_t = nsa_compressed_attention(proj_t, kvc, kvct, _overlap_matrix_t(seq), tail_t, batch, seq)
    tk = 512
    pos_ka, pos_qat = _alibi_augmentation(seq, tk)
    o_slc = flash_attention_t("slc", proj_r, proj_t, pos_ka, pos_qat, batch, seq, gates_t=tail_t, sel_t=sel_t, tk=tk)
    o_win = flash_attention_t("win", proj_r, proj_t, pos_ka, pos_qat, batch, seq, gates_t=tail_t, tk=tk)
    return mix_out(o_a, o_cmp, o_slc, o_win, o_c, mix_out_norm, w_out.astype(BF16), x)


def dense_layer(x, gain, w_gate, w_up, w_down, index, tm=1024):
    n = x.shape[0]
    h = rmsnorm_rows(x, gain, BF16)
    nblk = n // tm
    return grouped_swiglu(h, jnp.full((nblk,), index, jnp.int32), jnp.full((1,), nblk, jnp.int32),
                          w_gate, w_up, w_down, residual=x, tm=tm)


def kernel(x, w_in, w_out, norm_mix, mix_out_norm, attn_sinks, nsa_cmp_pe, nsa_cmp_w1, nsa_cmp_w2, fox_f_bias,
           norm_ffn, ffn_w_gate, ffn_w_up, ffn_w_down, moe_router, moe_w_gate, moe_w_up, moe_w_down, norm_final):
    batch, seq, d = x.shape
    depth = w_in.shape[0]
    f = ffn_w_gate.shape[-1]
    moe_gate, moe_up = moe_w_gate.reshape(-1, d, f), moe_w_up.reshape(-1, d, f)
    moe_down = moe_w_down.reshape(-1, f, d)
    xf = x.reshape(batch * seq, d).astype(F32)
    for layer in range(depth):
        xf = mixer_layer(xf, batch, seq, w_in[layer], w_out[layer], norm_mix[layer], mix_out_norm[layer],
                         attn_sinks[layer], nsa_cmp_pe[layer], nsa_cmp_w1[layer], nsa_cmp_w2[layer],
                         fox_f_bias[layer])
        i = layer // 2
        if layer % 2 == 0:
            xf = dense_layer(xf, norm_ffn[layer], ffn_w_gate, ffn_w_up, ffn_w_down, i)
        else:
            xf = moe_layer(xf, norm_ffn[layer], moe_router[i], moe_gate, moe_up, moe_down, i * N_EXPERTS)
    return rmsnorm_rows(xf, norm_final, x.dtype).reshape(batch, seq, d)
```

```python
import functools

import jax
import jax.numpy as jnp
import numpy as np
from jax import lax
from jax.experimental import pallas as pl
from jax.experimental.pallas import tpu as pltpu

F32 = jnp.float32
BF16 = jnp.bfloat16

HEAD_DIM = 64
A_HEADS, A_KV_HEADS = 8, 2
B_HEADS, B_KV_HEADS = 12, 2
C_HEADS = 12
B_GROUP = B_HEADS // B_KV_HEADS
A_WIDTH, B_WIDTH, C_WIDTH = A_HEADS * HEAD_DIM, B_HEADS * HEAD_DIM, C_HEADS * HEAD_DIM
KV_WIDTH = 2 * HEAD_DIM
N_GATE_COLS = B_HEADS * 3
WINDOW_A = 128
WINDOW_B = 512
CMP_LEN, CMP_STRIDE = 32, 16
SEL_BLOCK, N_SELECT = 64, 16
N_EXPERTS = 8
RMS_EPS = 1e-6
NEG_INF = -1e30
M_INIT = -1e20
FORCE_BONUS = 1e4
LANES = 128
VMEM_LIMIT = 56 * 1024 * 1024
Q_SCALE = HEAD_DIM ** -0.5

OFF_KC, OFF_KA, OFF_VA, OFF_QA = 0, 768, 896, 1024
OFF_KBC, OFF_VBC, OFF_KBS, OFF_KBW = 1536, 1664, 1792, 1920
ROW_WIDTH = 2048
OFF_QBT, OFF_QCT, OFF_VCT, OFF_VBST, OFF_VBWT = 0, 768, 1536, 2304, 2432
T_WIDTH = 2560
TAIL_FC = N_GATE_COLS
ONE_LANE = LANES - 1
N_AUG = HEAD_DIM
Q_AUG_ROWS = 16
LOG2E = 1.4426950408889634
MASK_PEN = -1e30
VMEM_LIMIT_BIG = 60 * 1024 * 1024
DMA_LOOP_UNROLL = 8


def _alibi(n):
    return [float(2.0 ** (-8.0 * i / n)) for i in range(1, n + 1)]


SLOPES_A = _alibi(A_HEADS)
SLOPES_B = _alibi(B_HEADS)


def _cparams(sem, vmem=VMEM_LIMIT):
    return pltpu.CompilerParams(dimension_semantics=sem, vmem_limit_bytes=vmem)


def _dot(a, b):
    return jnp.dot(a, b, preferred_element_type=F32)


def _dot_nt(a, b):
    return lax.dot_general(a, b, (((1,), (1,)), ((), ())), preferred_element_type=F32)


def _split3(x):
    hi = x.astype(BF16)
    r1 = x - hi.astype(F32)
    mid = r1.astype(BF16)
    lo = (r1 - mid.astype(F32)).astype(BF16)
    return hi, mid, lo


def _rms(x, gain):
    return x * lax.rsqrt(jnp.mean(x * x, axis=-1, keepdims=True) + RMS_EPS) * gain


def _sigmoid(x):
    return 1.0 / (1.0 + jnp.exp(-x))


def _norm_kernel(x_ref, g_ref, o_ref):
    o_ref[...] = _rms(x_ref[...], g_ref[...]).astype(o_ref.dtype)


def rmsnorm_rows(x, gain, out_dtype, tm=512):
    n, d = x.shape
    return pl.pallas_call(
        _norm_kernel,
        grid=(n // tm,),
        in_specs=[pl.BlockSpec((tm, d), lambda i: (i, 0)), pl.BlockSpec((1, d), lambda i: (0, 0))],
        out_specs=pl.BlockSpec((tm, d), lambda i: (i, 0)),
        out_shape=jax.ShapeDtypeStruct((n, d), out_dtype),
        compiler_params=_cparams(("parallel",)),
        name="rmsnorm",
    )(x, gain.reshape(1, d))


def _cast_kernel(x_ref, o_ref):
    o_ref[...] = x_ref[...].astype(o_ref.dtype)


def cast_rows(x, out_dtype, tm=512):
    n, d = x.shape
    return pl.pallas_call(
        _cast_kernel,
        grid=(n // tm,),
        in_specs=[pl.BlockSpec((tm, d), lambda i: (i, 0))],
        out_specs=pl.BlockSpec((tm, d), lambda i: (i, 0)),
        out_shape=jax.ShapeDtypeStruct((n, d), out_dtype),
        compiler_params=_cparams(("parallel",)),
        name="cast_rows",
    )(x)


def _projection_kernel(x_ref, g_ref, wr_ref, wt_ref, wtail_ref, wtailt_ref, pr_ref, pt_ref, tail_ref, tailt_ref, h_sc,
                       *, n_row_tiles, n_t_tiles):
    j = pl.program_id(1)

    @pl.when(j == 0)
    def _():
        h_sc[...] = _rms(x_ref[...], g_ref[...]).astype(BF16)

    @pl.when(j < n_row_tiles)
    def _():
        pr_ref[...] = _dot(h_sc[...], wr_ref[...]).astype(pr_ref.dtype)

    @pl.when((j >= n_row_tiles) & (j < n_row_tiles + n_t_tiles))
    def _():
        pt_ref[...] = _dot_nt(wt_ref[...], h_sc[...]).astype(pt_ref.dtype)

    @pl.when(j == n_row_tiles + n_t_tiles)
    def _():
        tail_ref[...] = _dot(h_sc[...], wtail_ref[...])
        tailt_ref[...] = _dot_nt(wtailt_ref[...], h_sc[...])


def input_projection(x, gain, w_rows, w_t, w_tail, tm=1024, tn=512):
    n, d = x.shape
    n_row_tiles, n_t_tiles = w_rows.shape[1] // tn, w_t.shape[0] // tn

    def row_tile(j):
        return jnp.minimum(j, n_row_tiles - 1)

    def t_tile(j):
        return jnp.clip(j - n_row_tiles, 0, n_t_tiles - 1)

    return pl.pallas_call(
        functools.partial(_projection_kernel, n_row_tiles=n_row_tiles, n_t_tiles=n_t_tiles),
        grid=(n // tm, n_row_tiles + n_t_tiles + 1),
        in_specs=[pl.BlockSpec((tm, d), lambda i, j: (i, 0)),
                  pl.BlockSpec((1, d), lambda i, j: (0, 0)),
                  pl.BlockSpec((d, tn), lambda i, j: (0, row_tile(j))),
                  pl.BlockSpec((tn, d), lambda i, j: (t_tile(j), 0)),
                  pl.BlockSpec((d, LANES), lambda i, j: (0, 0)),
                  pl.BlockSpec((LANES, d), lambda i, j: (0, 0))],
        out_specs=[pl.BlockSpec((tm, tn), lambda i, j: (i, row_tile(j))),
                   pl.BlockSpec((tn, tm), lambda i, j: (t_tile(j), i)),
                   pl.BlockSpec((tm, LANES), lambda i, j: (i, 0)),
                   pl.BlockSpec((LANES, tm), lambda i, j: (0, i))],
        out_shape=[jax.ShapeDtypeStruct((n, w_rows.shape[1]), BF16),
                   jax.ShapeDtypeStruct((w_t.shape[0], n), BF16),
                   jax.ShapeDtypeStruct((n, LANES), F32),
                   jax.ShapeDtypeStruct((LANES, n), F32)],
        scratch_shapes=[pltpu.VMEM((tm, d), BF16)],
        compiler_params=_cparams(("parallel", "arbitrary")),
        name="input_projection",
    )(x, gain.reshape(1, d), w_rows, w_t, w_tail, w_tail.T)


def _swa_kernel(sink_ref, q_ref, kp_ref, kc_ref, vp_ref, vc_ref, o_ref):
    n = pl.program_id(1)
    tq = q_ref.shape[0]
    q = q_ref[...]
    k = jnp.concatenate([kp_ref[...], kc_ref[...]], axis=0)
    v = jnp.concatenate([vp_ref[...], vc_ref[...]], axis=0)
    i = lax.broadcasted_iota(jnp.int32, (tq, 2 * tq), 0)
    j = lax.broadcasted_iota(jnp.int32, (tq, 2 * tq), 1)
    rel = tq + i - j
    mask = (rel >= 0) & (rel < WINDOW_A) & ((j >= tq) | (n > 0))
    relf = rel.astype(F32)
    group = A_HEADS // A_KV_HEADS
    for h in range(A_HEADS):
        kv = h // group
        qh = q[:, h * HEAD_DIM:(h + 1) * HEAD_DIM]
        kh = k[:, kv * HEAD_DIM:(kv + 1) * HEAD_DIM]
        vh = v[:, kv * HEAD_DIM:(kv + 1) * HEAD_DIM]
        s = _dot_nt(qh, kh) * Q_SCALE - SLOPES_A[h] * relf
        s = jnp.where(mask, s, NEG_INF)
        sink = sink_ref[h]
        m = jnp.maximum(jnp.max(s, axis=-1, keepdims=True), sink)
        p = jnp.exp(s - m)
        l = jnp.sum(p, axis=-1, keepdims=True) + jnp.exp(sink - m)
        o_ref[:, h * HEAD_DIM:(h + 1) * HEAD_DIM] = _dot(p.astype(BF16), vh) / l


def swa_attention(proj, sinks, batch, seq):
    tq = WINDOW_A
    nb = seq // tq
    kcol, vcol = OFF_KA // KV_WIDTH, OFF_VA // KV_WIDTH

    def cur(col):
        return pl.BlockSpec((tq, KV_WIDTH), lambda b, n: (b * nb + n, col))

    def prev(col):
        return pl.BlockSpec((tq, KV_WIDTH), lambda b, n: (b * nb + jnp.maximum(n - 1, 0), col))

    return pl.pallas_call(
        _swa_kernel,
        grid=(batch, nb),
        in_specs=[pl.BlockSpec(memory_space=pltpu.SMEM),
                  pl.BlockSpec((tq, A_WIDTH), lambda b, n: (b * nb + n, OFF_QA // A_WIDTH)),
                  prev(kcol), cur(kcol), prev(vcol), cur(vcol)],
        out_specs=pl.BlockSpec((tq, A_WIDTH), lambda b, n: (b * nb + n, 0)),
        out_shape=jax.ShapeDtypeStruct((batch * seq, A_WIDTH), F32),
        compiler_params=_cparams(("parallel", "parallel")),
        name="swa_attention",
    )(sinks, proj, proj, proj, proj, proj)


def _fox_aug_kernel(z_ref, b_ref, pk_ref, pq_ref, ka_ref, qa_ref, carry_sc):
    @pl.when(pl.program_id(1) == 0)
    def _():
        carry_sc[...] = jnp.zeros_like(carry_sc)

    z = z_ref[...] + b_ref[...]
    log_f = -(jnp.maximum(-z, 0.0) + jnp.log1p(jnp.exp(-jnp.abs(z))))
    ts = z.shape[0]
    r = lax.broadcasted_iota(jnp.int32, (ts, ts), 0)
    c = lax.broadcasted_iota(jnp.int32, (ts, ts), 1)
    tri = (c <= r).astype(BF16)
    hi, mid, lo = _split3(log_f)
    cum = _dot(tri, hi) + _dot(tri, mid) + _dot(tri, lo) + carry_sc[0:1, :]
    carry_sc[0:1, :] = cum[ts - 1:ts, :]
    hi, mid, lo = _split3(cum * LOG2E)
    lane = lax.broadcasted_iota(jnp.int32, (ts, LANES), 1)
    hi = jnp.where(lane == ONE_LANE, 1.0, hi).astype(BF16)
    ka_ref[...] = (_dot(hi, pk_ref[0]) + _dot(mid, pk_ref[1]) + _dot(lo, pk_ref[2])).astype(BF16)
    qa_ref[...] = (_dot_nt(pq_ref[0], hi) + _dot_nt(pq_ref[1], mid) + _dot_nt(pq_ref[2], lo)).astype(BF16)


def _aug_lane(h):
    return HEAD_DIM if h % 2 == 0 else 0


def _fox_placement():
    pk = np.zeros((3, LANES, C_HEADS * LANES), np.float32)
    pq = np.zeros((3, C_HEADS * Q_AUG_ROWS, LANES), np.float32)
    for h in range(C_HEADS):
        src, kbase, qbase = TAIL_FC + h, h * LANES + _aug_lane(h), h * Q_AUG_ROWS
        for piece in range(3):
            pk[piece, src, kbase + piece] = -1.0
            pq[piece, qbase + 3 + piece, src] = 1.0
            pk[0, ONE_LANE, kbase + 3 + piece] = 1.0
            pq[0, qbase + piece, ONE_LANE] = 1.0
    return jnp.asarray(pk, BF16), jnp.asarray(pq, BF16)


def fox_augmentation(tail, bias_row, batch, seq, ts=512):
    nt = seq // ts
    pk, pq = _fox_placement()
    kw, qw = C_HEADS * LANES, C_HEADS * Q_AUG_ROWS
    return pl.pallas_call(
        _fox_aug_kernel,
        grid=(batch, nt),
        in_specs=[pl.BlockSpec((ts, LANES), lambda b, i: (b * nt + i, 0)),
                  pl.BlockSpec((1, LANES), lambda b, i: (0, 0)),
                  pl.BlockSpec((3, LANES, kw), lambda b, i: (0, 0, 0)),
                  pl.BlockSpec((3, qw, LANES), lambda b, i: (0, 0, 0))],
        out_specs=[pl.BlockSpec((ts, kw), lambda b, i: (b * nt + i, 0)),
                   pl.BlockSpec((qw, ts), lambda b, i: (0, b * nt + i))],
        out_shape=[jax.ShapeDtypeStruct((batch * seq, kw), BF16),
                   jax.ShapeDtypeStruct((qw, batch * seq), BF16)],
        scratch_shapes=[pltpu.VMEM((8, LANES), F32)],
        compiler_params=_cparams(("parallel", "arbitrary")),
        name="fox_augmentation",
    )(tail, bias_row, pk, pq)


def _alibi_augmentation(seq, tk):
    pos = np.arange(seq)
    ka = np.zeros((seq, LANES), np.float32)
    for base in (0, HEAD_DIM):
        ka[:, base:base + 3] = 1.0
        ka[:, base + 3:base + 6] = ((pos >> 8) << 8)[:, None]
        ka[:, base + 6:base + 9] = (pos & 255)[:, None]
        ka[pos, base + Q_AUG_ROWS + (pos % tk) // SEL_BLOCK] = 1.0
    def split3_np(x):
        pieces, rest = [], x.astype(np.float32)
        for _ in range(3):
            piece = rest.astype(jnp.bfloat16).astype(np.float32)
            pieces.append(piece)
            rest = rest - piece
        return pieces

    slopes = (np.asarray(SLOPES_B, np.float32) * np.float32(LOG2E)).astype(np.float32)
    st = split3_np(-(slopes[:, None] * pos.astype(np.float32)[None, :]))
    sl = split3_np(slopes)
    qa = np.zeros((B_HEADS, Q_AUG_ROWS, seq), np.float32)
    for piece in range(3):
        qa[:, piece, :] = st[piece]
        qa[:, 3 + piece, :] = sl[piece][:, None]
        qa[:, 6 + piece, :] = sl[piece][:, None]
    return jnp.asarray(ka, BF16), jnp.asarray(qa.reshape(B_HEADS * Q_AUG_ROWS, seq), BF16)


def _flash_t_kernel(*refs, mode, tq, tk, n_steps):
    pen_sc = None
    if mode == "slc":
        (live_ref, k_ref, ka_ref, qt_ref, qat_ref, vt_ref, gate_ref, selt_ref, o_ref,
         m_sc, l_sc, acc_sc, pen_sc) = refs
    elif mode == "win":
        k_ref, ka_ref, qt_ref, qat_ref, vt_ref, gate_ref, o_ref, m_sc, l_sc, acc_sc = refs
    else:
        k_ref, ka_ref, qt_ref, qat_ref, vt_ref, o_ref, m_sc, l_sc, acc_sc = refs
    n_heads = qt_ref.shape[0] // HEAD_DIM
    group = 1 if mode == "fox" else B_GROUP
    blocks_per_tile = tk // SEL_BLOCK
    iq, j = pl.program_id(1), pl.program_id(2)
    if mode == "win":
        ik = iq - (n_steps - 1) + j
        last_j = n_steps - 1
        active = ik >= 0
    else:
        ik = j
        last_j = (iq * tq + tq - 1) // tk
        active = j <= last_j

    @pl.when(j == 0)
    def _():
        m_sc[...] = jnp.full_like(m_sc, M_INIT)
        l_sc[...] = jnp.zeros_like(l_sc)
        acc_sc[...] = jnp.zeros_like(acc_sc)
        if mode == "slc":
            pen_sc[...] = (1.0 - selt_ref[0].astype(F32)) * MASK_PEN

    def step(masked, heads):
        ok = None
        if masked:
            rel = ((iq * tq + lax.broadcasted_iota(jnp.int32, (tk, tq), 1))
                   - (ik * tk + lax.broadcasted_iota(jnp.int32, (tk, tq), 0)))
            ok = rel >= 0
            if mode == "win":
                ok = ok & (rel < WINDOW_B)
        lane = lax.broadcasted_iota(jnp.int32, (tk, LANES), 1)
        ones_rows = jnp.ones((Q_AUG_ROWS, tk), BF16)
        per_kv = {}

        def kv_operands(kv):
            if kv not in per_kv:
                pair = slice((kv // 2) * LANES, (kv // 2 + 1) * LANES)
                aug = ka_ref[:, kv * LANES:(kv + 1) * LANES] if mode == "fox" else ka_ref[...]
                own = (lane < HEAD_DIM) if kv % 2 == 0 else (lane >= HEAD_DIM)
                ka = jnp.where(own, k_ref[:, pair], aug)
                v_aug = jnp.concatenate([vt_ref[kv * HEAD_DIM:(kv + 1) * HEAD_DIM, :], ones_rows], axis=0)
                pen = None
                if mode == "slc":
                    pen8 = pen_sc[kv, pl.ds(pl.multiple_of(ik * blocks_per_tile, blocks_per_tile), blocks_per_tile), :]
                    pen = jnp.concatenate(
                        [pen8, jnp.zeros((Q_AUG_ROWS - blocks_per_tile, tq), F32)], axis=0).astype(BF16)
                per_kv[kv] = (ka, v_aug, pen)
            return per_kv[kv]

        def scores(h):
            kv = h // group
            ka, _, pen = kv_operands(kv)
            aug_rows = [qat_ref[h * Q_AUG_ROWS:(h + 1) * Q_AUG_ROWS, :]]
            if mode == "slc":
                aug_rows.append(pen)
            aug_rows.append(jnp.zeros((N_AUG - Q_AUG_ROWS * len(aug_rows), tq), BF16))
            halves = [[qt_ref[h * HEAD_DIM:(h + 1) * HEAD_DIM, :]], aug_rows]
            qa = jnp.concatenate(sum(halves if kv % 2 == 0 else halves[::-1], []), axis=0)
            s = _dot(ka, qa)
            return s if ok is None else jnp.where(ok, s, NEG_INF)

        def probabilities(h, s):
            m_prev = m_sc[h:h + 1, :]
            m_new = jnp.maximum(m_prev, jnp.max(s, axis=0, keepdims=True))
            m_sc[h:h + 1, :] = m_new
            return jnp.exp2(s - m_new).astype(BF16), jnp.exp2(m_prev - m_new)

        def accumulate(h, p, alpha):
            rows = slice(h * HEAD_DIM, (h + 1) * HEAD_DIM)
            pv = _dot(kv_operands(h // group)[1], p)
            l_sc[h:h + 1, :] = alpha * l_sc[h:h + 1, :] + pv[HEAD_DIM:HEAD_DIM + 1, :]
            acc_sc[rows, :] = alpha * acc_sc[rows, :] + pv[0:HEAD_DIM, :]

        s_cur, pending = scores(heads[0]), None
        for i, h in enumerate(heads):
            s_next = scores(heads[i + 1]) if i + 1 < len(heads) else None
            p_alpha = probabilities(h, s_cur)
            if pending is not None:
                accumulate(*pending)
            pending = (h,) + p_alpha
            s_cur = s_next
        accumulate(*pending)

    all_heads = list(range(n_heads))
    on_diagonal = ik * tk + tk - 1 > iq * tq
    if mode == "win":
        pl.when(active)(lambda: step(True, all_heads))
    elif mode == "fox":
        pl.when(active & on_diagonal)(lambda: step(True, all_heads))
        pl.when(active & jnp.logical_not(on_diagonal))(lambda: step(False, all_heads))
    else:
        tile = (pl.program_id(0) * pl.num_programs(1) + iq) * n_steps + ik
        for kv in range(B_KV_HEADS):
            heads = all_heads[kv * group:(kv + 1) * group]
            live = active & (live_ref[tile * B_KV_HEADS + kv] > 0)
            pl.when(live & on_diagonal)(functools.partial(step, True, heads))
            pl.when(live & jnp.logical_not(on_diagonal))(functools.partial(step, False, heads))

    @pl.when(j == last_j)
    def _():
        branch = {"fox": None, "slc": 1, "win": 2}[mode]
        for pair in range(n_heads // 2):
            rows = slice(pair * LANES, (pair + 1) * LANES)
            heads = (2 * pair, 2 * pair + 1)
            denom = jnp.concatenate(
                [jnp.broadcast_to(l_sc[h:h + 1, :], (HEAD_DIM, tq)) for h in heads], axis=0)
            out = acc_sc[rows, :] / denom
            if branch is not None:
                out = out * jnp.concatenate(
                    [jnp.broadcast_to(_sigmoid(gate_ref[3 * h + branch:3 * h + branch + 1, :]), (HEAD_DIM, tq))
                     for h in heads], axis=0)
            o_ref[:, rows] = out.T


def flash_attention_t(mode, proj_r, proj_t, k_aug, q_aug_t, batch, seq, gates_t=None, sel_t=None, tq=512, tk=512):
    nq, nk = seq // tq, seq // tk
    if mode == "win":
        assert tq == tk
        n_steps = WINDOW_B // tk + 1

        def kblock(iq, j):
            return jnp.maximum(iq - (n_steps - 1) + j, 0)
    else:
        n_steps = nk

        def kblock(iq, j):
            return jnp.minimum(j, (iq * tq + tq - 1) // tk)

    if mode == "fox":
        in_specs = [pl.BlockSpec((tk, C_WIDTH), lambda b, iq, j, *_: (b * nk + kblock(iq, j), OFF_KC // C_WIDTH)),
                    pl.BlockSpec((tk, C_HEADS * LANES), lambda b, iq, j, *_: (b * nk + kblock(iq, j), 0)),
                    pl.BlockSpec((C_WIDTH, tq), lambda b, iq, j, *_: (OFF_QCT // C_WIDTH, b * nq + iq)),
                    pl.BlockSpec((C_HEADS * Q_AUG_ROWS, tq), lambda b, iq, j, *_: (0, b * nq + iq)),
                    pl.BlockSpec((C_WIDTH, tk), lambda b, iq, j, *_: (OFF_VCT // C_WIDTH, b * nk + kblock(iq, j)))]
    else:
        kcol = (OFF_KBS if mode == "slc" else OFF_KBW) // KV_WIDTH
        vrow = (OFF_VBST if mode == "slc" else OFF_VBWT) // KV_WIDTH
        in_specs = [pl.BlockSpec((tk, KV_WIDTH), lambda b, iq, j, *_: (b * nk + kblock(iq, j), kcol)),
                    pl.BlockSpec((tk, LANES), lambda b, iq, j, *_: (kblock(iq, j), 0)),
                    pl.BlockSpec((B_WIDTH, tq), lambda b, iq, j, *_: (OFF_QBT // B_WIDTH, b * nq + iq)),
                    pl.BlockSpec((B_HEADS * Q_AUG_ROWS, tq), lambda b, iq, j, *_: (0, iq)),
                    pl.BlockSpec((KV_WIDTH, tk), lambda b, iq, j, *_: (vrow, b * nk + kblock(iq, j)))]
    args = [proj_r, k_aug, proj_t, q_aug_t, proj_t]
    prefetch = []
    width = C_WIDTH if mode == "fox" else B_WIDTH
    n_heads = width // HEAD_DIM
    scratch = [pltpu.VMEM((16, tq), F32), pltpu.VMEM((16, tq), F32), pltpu.VMEM((n_heads * HEAD_DIM, tq), F32)]
    if mode != "fox":
        in_specs.append(pl.BlockSpec((LANES, tq), lambda b, iq, j, *_: (0, b * nq + iq)))
        args.append(gates_t)
    if mode == "slc":
        n_slc = seq // SEL_BLOCK
        per_tile = tk // SEL_BLOCK
        assert per_tile <= Q_AUG_ROWS and tk % SEL_BLOCK == 0
        in_specs.append(pl.BlockSpec((1, B_KV_HEADS, n_slc, tq), lambda b, iq, j, *_: (b, 0, 0, iq)))
        args.append(sel_t)
        scratch.append(pltpu.VMEM((B_KV_HEADS, n_slc, tq), F32))
        live = sel_t.reshape(batch, B_KV_HEADS, nk, per_tile, nq, tq).max(axis=(3, 5)) > 0
        prefetch.append(live.transpose(0, 3, 2, 1).reshape(-1).astype(jnp.int32))
    return pl.pallas_call(
        functools.partial(_flash_t_kernel, mode=mode, tq=tq, tk=tk, n_steps=n_steps),
        grid_spec=pltpu.PrefetchScalarGridSpec(
            num_scalar_prefetch=len(prefetch),
            grid=(batch, nq, n_steps),
            in_specs=in_specs,
            out_specs=pl.BlockSpec((tq, width), lambda b, iq, j, *_: (b * nq + iq, 0)),
            scratch_shapes=scratch),
        out_shape=jax.ShapeDtypeStruct((batch * seq, width), F32),
        compiler_params=_cparams(("parallel", "parallel", "arbitrary")),
        name=mode + "_attention",
    )(*prefetch, *args)


def _compress_kernel(t_ref, pe_ref, w1_ref, w2_ref, w2t_ref, o_ref, ot_ref):
    w1 = w1_ref[0]
    hid = _dot(t_ref[0, 0, 0], w1) + _dot(pe_ref[0], w1)[0:1, :]
    act = (hid * _sigmoid(hid)).astype(BF16)
    o_ref[0, 0, 0] = _dot(act, w2_ref[0]).astype(o_ref.dtype)
    ot_ref[0, 0, 0] = _dot_nt(w2t_ref[0], act).astype(ot_ref.dtype)


def nsa_compress(flat, pe_rows, w1, w2):
    _, batch, n_kv, n_chunks, width = flat.shape
    hidden = w1.shape[-1]
    return pl.pallas_call(
        _compress_kernel,
        grid=(2, batch, n_kv),
        in_specs=[pl.BlockSpec((1, 1, 1, n_chunks, width), lambda s, b, h: (s, b, h, 0, 0)),
                  pl.BlockSpec((1, 8, width), lambda s, b, h: (s, 0, 0)),
                  pl.BlockSpec((1, width, hidden), lambda s, b, h: (s, 0, 0)),
                  pl.BlockSpec((1, hidden, HEAD_DIM), lambda s, b, h: (s, 0, 0)),
                  pl.BlockSpec((1, HEAD_DIM, hidden), lambda s, b, h: (s, 0, 0))],
        out_specs=[pl.BlockSpec((1, 1, 1, n_chunks, HEAD_DIM), lambda s, b, h: (s, b, h, 0, 0)),
                   pl.BlockSpec((1, 1, 1, HEAD_DIM, n_chunks), lambda s, b, h: (s, b, h, 0, 0))],
        out_shape=[jax.ShapeDtypeStruct((2, batch, n_kv, n_chunks, HEAD_DIM), BF16),
                   jax.ShapeDtypeStruct((2, batch, n_kv, HEAD_DIM, n_chunks), BF16)],
        compiler_params=_cparams(("parallel", "parallel", "parallel")),
        name="nsa_compress",
    )(flat, pe_rows, w1, w2, jnp.swapaxes(w2, 1, 2))


def _cmp_attn_t_kernel(qt_ref, kc_ref, vct_ref, inter_ref, gate_ref, o_ref, selt_ref, ot_sc, *, tq):
    iq = pl.program_id(1)
    n_chunks = kc_ref.shape[3]
    n_slc = inter_ref.shape[0]
    t = iq * tq + lax.broadcasted_iota(jnp.int32, (n_chunks, tq), 1)
    n = lax.broadcasted_iota(jnp.int32, (n_chunks, tq), 0)
    rel = t - (n * CMP_STRIDE + CMP_LEN - 1)
    mask = rel >= 0
    relf = rel.astype(F32)
    t_s = iq * tq + lax.broadcasted_iota(jnp.int32, (n_slc, tq), 1)
    jj = lax.broadcasted_iota(jnp.int32, (n_slc, tq), 0)
    cur = t_s >> (SEL_BLOCK.bit_length() - 1)
    valid = jj * SEL_BLOCK <= t_s
    forced = (jj == 0) | (jj == cur) | (jj == cur - 1)
    jf = jj.astype(F32)
    inter = inter_ref[...]
    def raw_scores(hd):
        return _dot(kc_ref[0, 0, hd // B_GROUP], qt_ref[hd * HEAD_DIM:(hd + 1) * HEAD_DIM, :])

    raw_next = raw_scores(0)
    for h in range(B_KV_HEADS):
        vct = vct_ref[0, 0, h]
        p_sum = jnp.zeros((n_chunks, tq), F32)
        for g in range(B_GROUP):
            hd = h * B_GROUP + g
            rows = slice(hd * HEAD_DIM, (hd + 1) * HEAD_DIM)
            raw, raw_next = raw_next, (raw_scores(hd + 1) if hd + 1 < B_HEADS else None)
            s = jnp.where(mask, raw - (SLOPES_B[hd] * LOG2E) * relf, NEG_INF)
            m = jnp.max(s, axis=0, keepdims=True)
            e = jnp.where(mask, jnp.exp2(s - m), 0.0)
            l = jnp.sum(e, axis=0, keepdims=True)
            p = e / jnp.where(l > 0.0, l, 1.0)
            p_sum = p_sum + p
            ot_sc[rows, :] = _dot(vct, p.astype(BF16)) * _sigmoid(gate_ref[3 * hd:3 * hd + 1, :])
        hi, mid, lo = _split3(p_sum)
        importance = _dot(inter, hi) + _dot(inter, mid) + _dot(inter, lo)
        score = jnp.where(valid, importance + jnp.where(forced, FORCE_BONUS, 0.0), NEG_INF)
        chosen = jnp.zeros((n_slc, tq), F32)
        for _ in range(min(N_SELECT, n_slc)):
            best = jnp.max(score, axis=0, keepdims=True)
            first = jnp.min(jnp.where(score == best, jf, float(n_slc)), axis=0, keepdims=True)
            pick = jf == first
            chosen = jnp.where(pick, 1.0, chosen)
            score = jnp.where(pick, -jnp.inf, score)
        selt_ref[0, h] = jnp.where(valid, chosen, 0.0).astype(selt_ref.dtype)
    for pair in range(B_HEADS // 2):
        rows = slice(pair * LANES, (pair + 1) * LANES)
        o_ref[:, rows] = ot_sc[rows, :].T


def nsa_compressed_attention(proj_t, kvc, kvct, inter_t, gates_t, batch, seq, tq=512):
    nq = seq // tq
    n_chunks = kvc.shape[3]
    n_slc = seq // SEL_BLOCK
    return pl.pallas_call(
        functools.partial(_cmp_attn_t_kernel, tq=tq),
        grid=(batch, nq),
        in_specs=[pl.BlockSpec((B_WIDTH, tq), lambda b, i: (OFF_QBT // B_WIDTH, b * nq + i)),
                  pl.BlockSpec((1, 1, B_KV_HEADS, n_chunks, HEAD_DIM), lambda b, i: (0, b, 0, 0, 0)),
                  pl.BlockSpec((1, 1, B_KV_HEADS, HEAD_DIM, n_chunks), lambda b, i: (1, b, 0, 0, 0)),
                  pl.BlockSpec((n_slc, n_chunks), lambda b, i: (0, 0)),
                  pl.BlockSpec((LANES, tq), lambda b, i: (0, b * nq + i))],
        out_specs=[pl.BlockSpec((tq, B_WIDTH), lambda b, i: (b * nq + i, 0)),
                   pl.BlockSpec((1, B_KV_HEADS, n_slc, tq), lambda b, i: (b, 0, 0, i))],
        out_shape=[jax.ShapeDtypeStruct((batch * seq, B_WIDTH), F32),
                   jax.ShapeDtypeStruct((batch, B_KV_HEADS, n_slc, seq), BF16)],
        scratch_shapes=[pltpu.VMEM((B_WIDTH, tq), F32)],
        compiler_params=_cparams(("parallel", "parallel")),
        name="nsa_cmp_attention",
    )(proj_t, kvc, kvct, inter_t, gates_t)


def _mix_out_kernel(oa_ref, ocmp_ref, oslc_ref, owin_ref, oc_ref, g_ref, w_ref, x_ref, o_ref, mixed_sc):
    @pl.when(pl.program_id(1) == 0)
    def _():
        o_b = ocmp_ref[...] + oslc_ref[...] + owin_ref[...]
        b0, c0 = A_WIDTH, A_WIDTH + B_WIDTH
        mixed_sc[:, 0:b0] = _rms(oa_ref[...], g_ref[:, 0:b0]).astype(BF16)
        mixed_sc[:, b0:c0] = _rms(o_b, g_ref[:, b0:c0]).astype(BF16)
        mixed_sc[:, c0:] = _rms(oc_ref[...], g_ref[:, c0:]).astype(BF16)

    o_ref[...] = x_ref[...] + _dot(mixed_sc[...], w_ref[...])


def mix_out(o_a, o_cmp, o_slc, o_win, o_c, gain, w_out, x, tm=512, tn=512):
    n, d = x.shape
    width = w_out.shape[0]

    def rows(w):
        return pl.BlockSpec((tm, w), lambda i, j: (i, 0))

    return pl.pallas_call(
        _mix_out_kernel,
        grid=(n // tm, d // tn),
        in_specs=[rows(A_WIDTH), rows(B_WIDTH), rows(B_WIDTH), rows(B_WIDTH), rows(C_WIDTH),
                  pl.BlockSpec((1, width), lambda i, j: (0, 0)),
                  pl.BlockSpec((width, tn), lambda i, j: (0, j)),
                  pl.BlockSpec((tm, tn), lambda i, j: (i, j))],
        out_specs=pl.BlockSpec((tm, tn), lambda i, j: (i, j)),
        out_shape=jax.ShapeDtypeStruct((n, d), F32),
        scratch_shapes=[pltpu.VMEM((tm, width), BF16)],
        compiler_params=_cparams(("parallel", "arbitrary")),
        name="mix_out",
    )(o_a, o_cmp, o_slc, o_win, o_c, gain.reshape(1, width), w_out, x)


def _fresh_weights(be_ref):
    i = pl.program_id(1)
    return (i == 0) | (be_ref[i] != be_ref[jnp.maximum(i - 1, 0)])


def _ffn_up_kernel(be_ref, nv_ref, x_ref, wg_ref, wu_ref, o_ref, wg_sc, wu_sc):
    i = pl.program_id(1)

    @pl.when(_fresh_weights(be_ref))
    def _():
        wg_sc[...] = wg_ref[0].astype(BF16)
        wu_sc[...] = wu_ref[0].astype(BF16)

    @pl.when(i < nv_ref[0])
    def _():
        x = x_ref[...].astype(BF16)
        gate = _dot(x, wg_sc[...])
        up = _dot(x, wu_sc[...])
        o_ref[...] = (gate * _sigmoid(gate) * up).astype(o_ref.dtype)

    @pl.when(i >= nv_ref[0])
    def _():
        o_ref[...] = jnp.zeros_like(o_ref)


def _ffn_down_kernel(be_ref, nv_ref, h_ref, wd_ref, *rest):
    o_ref, wd_sc = rest[-2], rest[-1]
    i = pl.program_id(1)

    @pl.when(_fresh_weights(be_ref))
    def _():
        wd_sc[...] = wd_ref[0].astype(BF16)

    @pl.when(i < nv_ref[0])
    def _():
        y = _dot(h_ref[...], wd_sc[...])
        if len(rest) == 3:
            y = y + rest[0][...]
        o_ref[...] = y

    @pl.when(i >= nv_ref[0])
    def _():
        o_ref[...] = jnp.zeros_like(o_ref)


def grouped_swiglu(xs, block_expert, n_valid, w_gate, w_up, w_down, residual=None, tm=512, tf=512, tm_down=512,
                   tn=512):
    rows, d = xs.shape
    f = w_gate.shape[-1]
    nblk = rows // tm

    def xrow(i, nv):
        return jnp.minimum(i, nv[0] - 1)

    hidden = pl.pallas_call(
        _ffn_up_kernel,
        grid_spec=pltpu.PrefetchScalarGridSpec(
            num_scalar_prefetch=2,
            grid=(f // tf, nblk),
            in_specs=[pl.BlockSpec((tm, d), lambda j, i, be, nv: (xrow(i, nv), 0)),
                      pl.BlockSpec((1, d, tf), lambda j, i, be, nv: (be[i], 0, j)),
                      pl.BlockSpec((1, d, tf), lambda j, i, be, nv: (be[i], 0, j))],
            out_specs=pl.BlockSpec((tm, tf), lambda j, i, be, nv: (i, j)),
            scratch_shapes=[pltpu.VMEM((d, tf), BF16), pltpu.VMEM((d, tf), BF16)]),
        out_shape=jax.ShapeDtypeStruct((rows, f), BF16),
        compiler_params=_cparams(("parallel", "arbitrary"), VMEM_LIMIT_BIG),
        name="ffn_up",
    )(block_expert, n_valid, xs, w_gate, w_up)

    split = tm // tm_down
    in_specs = [pl.BlockSpec((tm_down, f), lambda j, i, be, nv: (xrow(i, nv), 0)),
                pl.BlockSpec((1, f, tn), lambda j, i, be, nv: (be[i], 0, j))]
    args = [hidden, w_down]
    if residual is not None:
        in_specs.append(pl.BlockSpec((tm_down, tn), lambda j, i, be, nv: (i, j)))
        args.append(residual)
    return pl.pallas_call(
        _ffn_down_kernel,
        grid_spec=pltpu.PrefetchScalarGridSpec(
            num_scalar_prefetch=2,
            grid=(d // tn, nblk * split),
            in_specs=in_specs,
            out_specs=pl.BlockSpec((tm_down, tn), lambda j, i, be, nv: (i, j)),
            scratch_shapes=[pltpu.VMEM((f, tn), BF16)]),
        out_shape=jax.ShapeDtypeStruct((rows, d), F32),
        compiler_params=_cparams(("parallel", "arbitrary"), VMEM_LIMIT_BIG),
        name="ffn_down",
    )(jnp.repeat(block_expert, split), n_valid * split, *args)


def _router_kernel(x_ref, g_ref, wr_ref, h_ref, ri_ref, rf_ref, cnt_ref, carry_sc):
    @pl.when(pl.program_id(0) == 0)
    def _():
        carry_sc[...] = jnp.zeros_like(carry_sc)

    tm = x_ref.shape[0]
    h = _rms(x_ref[...], g_ref[...])
    h_ref[...] = h
    h1, h2, h3 = _split3(h)
    w1, w2, w3 = wr_ref[0], wr_ref[1], wr_ref[2]
    logits = (_dot(h1, w1) + _dot(h1, w2) + _dot(h2, w1)) + (_dot(h1, w3) + _dot(h2, w2) + _dot(h3, w1))
    lane_i = lax.broadcasted_iota(jnp.int32, (tm, LANES), 1)
    lane = lane_i.astype(F32)
    logits = jnp.where(lane_i < N_EXPERTS, logits, -jnp.inf)
    v1 = jnp.max(logits, axis=-1, keepdims=True)
    e1 = jnp.min(jnp.where(logits == v1, lane, float(LANES)), axis=-1, keepdims=True)
    rest = jnp.where(lane == e1, -jnp.inf, logits)
    v2 = jnp.max(rest, axis=-1, keepdims=True)
    e2 = jnp.min(jnp.where(rest == v2, lane, float(LANES)), axis=-1, keepdims=True)
    z = jnp.exp(v2 - v1)
    g1 = 1.0 / (1.0 + z)
    g2 = z / (1.0 + z)
    chosen = (lane == e1) | (lane == e2)
    onehot = jnp.where(chosen, 1.0, 0.0)
    r = lax.broadcasted_iota(jnp.int32, (tm, tm), 0)
    c = lax.broadcasted_iota(jnp.int32, (tm, tm), 1)
    before = _dot((c < r).astype(BF16), onehot.astype(BF16)) + carry_sc[0:1, :]
    r1 = jnp.sum(jnp.where(lane == e1, before, 0.0), axis=-1, keepdims=True)
    r2 = jnp.sum(jnp.where(lane == e2, before, 0.0), axis=-1, keepdims=True)
    carry = carry_sc[0:1, :] + jnp.sum(onehot, axis=0, keepdims=True)
    carry_sc[0:1, :] = carry
    cnt_ref[...] = jnp.broadcast_to(carry, cnt_ref.shape)
    packed = jnp.where(lane_i == 0, e1, jnp.where(lane_i == 1, e2, jnp.where(
        lane_i == 2, r1, jnp.where(lane_i == 3, r2, 0.0))))
    ri_ref[...] = packed.astype(jnp.int32)
    rf_ref[...] = jnp.where(lane_i == 0, g1, jnp.where(lane_i == 1, g2, 0.0))


def moe_route(x, gain, w_router3, tm=512):
    n, d = x.shape
    return pl.pallas_call(
        _router_kernel,
        grid=(n // tm,),
        in_specs=[pl.BlockSpec((tm, d), lambda i: (i, 0)),
                  pl.BlockSpec((1, d), lambda i: (0, 0)),
                  pl.BlockSpec((3, d, LANES), lambda i: (0, 0, 0))],
        out_specs=[pl.BlockSpec((tm, d), lambda i: (i, 0)),
                   pl.BlockSpec((tm, LANES), lambda i: (i, 0)),
                   pl.BlockSpec((tm, LANES), lambda i: (i, 0)),
                   pl.BlockSpec((8, LANES), lambda i: (0, 0))],
        out_shape=[jax.ShapeDtypeStruct((n, d), F32),
                   jax.ShapeDtypeStruct((n, LANES), jnp.int32),
                   jax.ShapeDtypeStruct((n, LANES), F32),
                   jax.ShapeDtypeStruct((8, LANES), F32)],
        scratch_shapes=[pltpu.VMEM((8, LANES), F32)],
        compiler_params=_cparams(("arbitrary",)),
        name="moe_route",
    )(x, gain.reshape(1, d), w_router3)


def _row_copy(src_ref, src_row, dst_ref, dst_row, sem):
    return pltpu.make_async_copy(src_ref.at[pl.ds(src_row, 1)], dst_ref.at[pl.ds(dst_row, 1)], sem)


def _dispatch_kernel(dest_ref, h_ref, zeros_ref, xs_ref, sem):
    del zeros_ref
    td = h_ref.shape[0]
    base = pl.program_id(0) * td

    def start(r, carry):
        for k in range(2):
            _row_copy(h_ref, r, xs_ref, dest_ref[2 * (base + r) + k], sem).start()
        return carry

    def wait(r, carry):
        for k in range(2):
            _row_copy(h_ref, r, xs_ref, dest_ref[2 * (base + r) + k], sem).wait()
        return carry

    lax.fori_loop(0, td, start, 0, unroll=DMA_LOOP_UNROLL)
    lax.fori_loop(0, td, wait, 0, unroll=DMA_LOOP_UNROLL)


def moe_dispatch(h, dest, cap, td=256):
    n, d = h.shape
    return pl.pallas_call(
        _dispatch_kernel,
        grid_spec=pltpu.PrefetchScalarGridSpec(
            num_scalar_prefetch=1,
            grid=(n // td,),
            in_specs=[pl.BlockSpec((td, d), lambda i, dest: (i, 0)),
                      pl.BlockSpec(memory_space=pl.ANY)],
            out_specs=pl.BlockSpec(memory_space=pl.ANY),
            scratch_shapes=[pltpu.SemaphoreType.DMA(())]),
        out_shape=jax.ShapeDtypeStruct((cap, d), h.dtype),
        input_output_aliases={2: 0},
        compiler_params=_cparams(("arbitrary",)),
        name="moe_dispatch",
    )(dest, h, jnp.zeros((cap, d), h.dtype))


def _combine_kernel(dest_ref, x_ref, gate_ref, y_ref, o_ref, ya_sc, yb_sc, sem):
    tc = x_ref.shape[0]
    base = pl.program_id(0) * tc

    def start(r, carry):
        _row_copy(y_ref, dest_ref[2 * (base + r)], ya_sc, r, sem).start()
        _row_copy(y_ref, dest_ref[2 * (base + r) + 1], yb_sc, r, sem).start()
        return carry

    def wait(r, carry):
        _row_copy(y_ref, dest_ref[2 * (base + r)], ya_sc, r, sem).wait()
        _row_copy(y_ref, dest_ref[2 * (base + r) + 1], yb_sc, r, sem).wait()
        return carry

    lax.fori_loop(0, tc, start, 0, unroll=DMA_LOOP_UNROLL)
    lax.fori_loop(0, tc, wait, 0, unroll=DMA_LOOP_UNROLL)
    gates = gate_ref[...]
    o_ref[...] = x_ref[...] + (gates[:, 0:1] * ya_sc[...] + gates[:, 1:2] * yb_sc[...])


def moe_combine(x, gates, y, dest, tc=256):
    n, d = x.shape
    return pl.pallas_call(
        _combine_kernel,
        grid_spec=pltpu.PrefetchScalarGridSpec(
            num_scalar_prefetch=1,
            grid=(n // tc,),
            in_specs=[pl.BlockSpec((tc, d), lambda i, dest: (i, 0)),
                      pl.BlockSpec((tc, LANES), lambda i, dest: (i, 0)),
                      pl.BlockSpec(memory_space=pl.ANY)],
            out_specs=pl.BlockSpec((tc, d), lambda i, dest: (i, 0)),
            scratch_shapes=[pltpu.VMEM((tc, d), F32), pltpu.VMEM((tc, d), F32),
                            pltpu.SemaphoreType.DMA(())]),
        out_shape=jax.ShapeDtypeStruct((n, d), F32),
        compiler_params=_cparams(("arbitrary",)),
        name="moe_combine",
    )(dest, x, gates, y)


def moe_layer(x, gain, w_router, w_gate, w_up, w_down, first_expert, tm=512):
    n, d = x.shape
    wr = jnp.pad(w_router.astype(F32), ((0, 0), (0, LANES - N_EXPERTS)))
    h, info, gates, counts = moe_route(x, gain, jnp.stack(_split3(wr)))
    counts = counts[0, :N_EXPERTS].astype(jnp.int32)
    padded = (counts + tm - 1) // tm * tm
    pad_ends = jnp.cumsum(padded)
    pad_starts = pad_ends - padded
    experts, ranks = info[:, 0:2], info[:, 2:4]
    dest = (pad_starts[experts] + ranks).reshape(-1).astype(jnp.int32)
    nblk = (2 * n) // tm + N_EXPERTS
    n_valid = (pad_ends[-1] // tm).astype(jnp.int32).reshape(1)
    blk = jnp.minimum(jnp.arange(nblk, dtype=jnp.int32), n_valid[0] - 1) * tm
    block_expert = jnp.minimum(jnp.sum(blk[:, None] >= pad_ends[None, :], axis=1), N_EXPERTS - 1).astype(jnp.int32)
    xs = cast_rows(moe_dispatch(h, dest, nblk * tm), BF16)
    y = grouped_swiglu(xs, block_expert + first_expert, n_valid, w_gate, w_up, w_down, tm=tm, tf=1024, tm_down=tm)
    return moe_combine(x, gates, y, dest)


def _project_weights(w):
    def cols(a, n):
        return w[:, a:a + n]

    qa, ka, va, qb = cols(0, 512), cols(512, 128), cols(640, 128), cols(768, 768)
    kbc, vbc, kbs, vbs, kbw, vbw = (cols(1536 + 128 * i, 128) for i in range(6))
    gb, qc, kc, vc, fc = cols(2304, 36), cols(2340, 768), cols(3108, 768), cols(3876, 768), cols(4644, 12)
    w_rows = jnp.concatenate([kc, ka, va, qa, kbc, vbc, kbs, kbw], axis=1).astype(BF16)
    q_fold = Q_SCALE * LOG2E
    w_t = jnp.concatenate([qb * q_fold, qc * q_fold, vc, vbs, vbw], axis=1).T.astype(BF16)
    tail = jnp.concatenate([gb, fc], axis=1)
    tail = jnp.pad(tail, ((0, 0), (0, LANES - tail.shape[1]))).astype(BF16)
    return w_rows, w_t, tail


def _chunk_blocks(cols, batch, seq):
    n_chunks = seq // CMP_STRIDE
    t = cols.reshape(batch, seq, B_KV_HEADS, HEAD_DIM).transpose(0, 2, 1, 3)
    t = t.reshape(batch, B_KV_HEADS, n_chunks, CMP_STRIDE * HEAD_DIM)
    nxt = jnp.concatenate([t[:, :, 1:], jnp.zeros_like(t[:, :, :1])], axis=2)
    return jnp.concatenate([t, nxt], axis=-1)


def _overlap_matrix_t(seq):
    n_chunks, n_slc = seq // CMP_STRIDE, seq // SEL_BLOCK
    c_start = np.arange(n_chunks) * CMP_STRIDE
    s_start = np.arange(n_slc) * SEL_BLOCK
    inter = np.maximum(np.minimum(c_start[:, None] + CMP_LEN, s_start[None, :] + SEL_BLOCK)
                       - np.maximum(c_start[:, None], s_start[None, :]), 0) / CMP_LEN
    inter[(seq - CMP_LEN) // CMP_STRIDE + 1:] = 0.0
    return jnp.asarray(inter.T, BF16)


def mixer_layer(x, batch, seq, w_in, w_out, norm_mix, mix_out_norm, sinks, cmp_pe, cmp_w1, cmp_w2, f_bias):
    w_rows, w_t, w_tail = _project_weights(w_in)
    proj_r, proj_t, tail, tail_t = input_projection(x, norm_mix, w_rows, w_t, w_tail)
    o_a = swa_attention(proj_r, sinks.astype(F32), batch, seq)

    bias_row = jnp.zeros((1, LANES), F32).at[0, TAIL_FC:TAIL_FC + C_HEADS].set(f_bias.astype(F32))
    fox_ka, fox_qat = fox_augmentation(tail, bias_row, batch, seq)
    o_c = flash_attention_t("fox", proj_r, proj_t, fox_ka, fox_qat, batch, seq)

    flat = jnp.stack([_chunk_blocks(proj_r[:, OFF_KBC:OFF_KBC + KV_WIDTH], batch, seq),
                      _chunk_blocks(proj_r[:, OFF_VBC:OFF_VBC + KV_WIDTH], batch, seq)])
    pe_rows = jnp.broadcast_to(cmp_pe.reshape(2, 1, CMP_LEN * HEAD_DIM), (2, 8, CMP_LEN * HEAD_DIM)).astype(BF16)
    kvc, kvct = nsa_compress(flat, pe_rows, cmp_w1.astype(BF16), cmp_w2.astype(BF16))
    o_cmp, sel_t = nsa_compressed_attention(proj_t, kvc, kvct, _overlap_matrix_t(seq), tail_t, batch, seq)
    tk = 512
    pos_ka, pos_qat = _alibi_augmentation(seq, tk)
    o_slc = flash_attention_t("slc", proj_r, proj_t, pos_ka, pos_qat, batch, seq, gates_t=tail_t, sel_t=sel_t, tk=tk)
    o_win = flash_attention_t("win", proj_r, proj_t, pos_ka, pos_qat, batch, seq, gates_t=tail_t, tk=tk)
    return mix_out(o_a, o_cmp, o_slc, o_win, o_c, mix_out_norm, w_out.astype(BF16), x)


def dense_layer(x, gain, w_gate, w_up, w_down, index, tm=1024):
    n = x.shape[0]
    h = rmsnorm_rows(x, gain, BF16)
    nblk = n // tm
    return grouped_swiglu(h, jnp.full((nblk,), index, jnp.int32), jnp.full((1,), nblk, jnp.int32),
                          w_gate, w_up, w_down, residual=x, tm=tm)


def kernel(x, w_in, w_out, norm_mix, mix_out_norm, attn_sinks, nsa_cmp_pe, nsa_cmp_w1, nsa_cmp_w2, fox_f_bias,
           norm_ffn, ffn_w_gate, ffn_w_up, ffn_w_down, moe_router, moe_w_gate, moe_w_up, moe_w_down, norm_final):
    batch, seq, d = x.shape
    depth = w_in.shape[0]
    f = ffn_w_gate.shape[-1]
    moe_gate, moe_up = moe_w_gate.reshape(-1, d, f), moe_w_up.reshape(-1, d, f)
    moe_down = moe_w_down.reshape(-1, f, d)
    xf = x.reshape(batch * seq, d).astype(F32)
    for layer in range(depth):
        xf = mixer_layer(xf, batch, seq, w_in[layer], w_out[layer], norm_mix[layer], mix_out_norm[layer],
                         attn_sinks[layer], nsa_cmp_pe[layer], nsa_cmp_w1[layer], nsa_cmp_w2[layer],
                         fox_f_bias[layer])
        i = layer // 2
        if layer % 2 == 0:
            xf = dense_layer(xf, norm_ffn[layer], ffn_w_gate, ffn_w_up, ffn_w_down, i)
        else:
            xf = moe_layer(xf, norm_ffn[layer], moe_router[i], moe_gate, moe_up, moe_down, i * N_EXPERTS)
    return rmsnorm_rows(xf, norm_final, x.dtype).reshape(batch, seq, d)
```

```python
import functools

import jax
import jax.numpy as jnp
import numpy as np
from jax import lax
from jax.experimental import pallas as pl
from jax.experimental.pallas import tpu as pltpu

F32 = jnp.float32
BF16 = jnp.bfloat16

HEAD_DIM = 64
A_HEADS, A_KV_HEADS = 8, 2
B_HEADS, B_KV_HEADS = 12, 2
C_HEADS = 12
B_GROUP = B_HEADS // B_KV_HEADS
A_WIDTH, B_WIDTH, C_WIDTH = A_HEADS * HEAD_DIM, B_HEADS * HEAD_DIM, C_HEADS * HEAD_DIM
KV_WIDTH = 2 * HEAD_DIM
N_GATE_COLS = B_HEADS * 3
WINDOW_A = 128
WINDOW_B = 512
CMP_LEN, CMP_STRIDE = 32, 16
SEL_BLOCK, N_SELECT = 64, 16
N_EXPERTS = 8
RMS_EPS = 1e-6
NEG_INF = -1e30
M_INIT = -1e20
FORCE_BONUS = 1e4
LANES = 128
VMEM_LIMIT = 56 * 1024 * 1024
Q_SCALE = HEAD_DIM ** -0.5

OFF_KC, OFF_KA, OFF_VA, OFF_QA = 0, 768, 896, 1024
OFF_KBC, OFF_VBC, OFF_KBS, OFF_KBW = 1536, 1664, 1792, 1920
ROW_WIDTH = 2048
OFF_QBT, OFF_QCT, OFF_VCT, OFF_VBST, OFF_VBWT = 0, 768, 1536, 2304, 2432
T_WIDTH = 2560
TAIL_FC = N_GATE_COLS
ONE_LANE = LANES - 1
N_AUG = HEAD_DIM
Q_AUG_ROWS = 16
LOG2E = 1.4426950408889634
MASK_PEN = -1e30
VMEM_LIMIT_BIG = 60 * 1024 * 1024
DMA_LOOP_UNROLL = 8


def _alibi(n):
    return [float(2.0 ** (-8.0 * i / n)) for i in range(1, n + 1)]


SLOPES_A = _alibi(A_HEADS)
SLOPES_B = _alibi(B_HEADS)


def _cparams(sem, vmem=VMEM_LIMIT):
    return pltpu.CompilerParams(dimension_semantics=sem, vmem_limit_bytes=vmem)


def _dot(a, b):
    return jnp.dot(a, b, preferred_element_type=F32)


def _dot_nt(a, b):
    return lax.dot_general(a, b, (((1,), (1,)), ((), ())), preferred_element_type=F32)


def _split3(x):
    hi = x.astype(BF16)
    r1 = x - hi.astype(F32)
    mid = r1.astype(BF16)
    lo = (r1 - mid.astype(F32)).astype(BF16)
    return hi, mid, lo


def _rms(x, gain):
    return x * lax.rsqrt(jnp.mean(x * x, axis=-1, keepdims=True) + RMS_EPS) * gain


def _sigmoid(x):
    return 1.0 / (1.0 + jnp.exp(-x))


def _norm_kernel(x_ref, g_ref, o_ref):
    o_ref[...] = _rms(x_ref[...], g_ref[...]).astype(o_ref.dtype)


def rmsnorm_rows(x, gain, out_dtype, tm=512):
    n, d = x.shape
    return pl.pallas_call(
        _norm_kernel,
        grid=(n // tm,),
        in_specs=[pl.BlockSpec((tm, d), lambda i: (i, 0)), pl.BlockSpec((1, d), lambda i: (0, 0))],
        out_specs=pl.BlockSpec((tm, d), lambda i: (i, 0)),
        out_shape=jax.ShapeDtypeStruct((n, d), out_dtype),
        compiler_params=_cparams(("parallel",)),
        name="rmsnorm",
    )(x, gain.reshape(1, d))


def _cast_kernel(x_ref, o_ref):
    o_ref[...] = x_ref[...].astype(o_ref.dtype)


def cast_rows(x, out_dtype, tm=512):
    n, d = x.shape
    return pl.pallas_call(
        _cast_kernel,
        grid=(n // tm,),
        in_specs=[pl.BlockSpec((tm, d), lambda i: (i, 0))],
        out_specs=pl.BlockSpec((tm, d), lambda i: (i, 0)),
        out_shape=jax.ShapeDtypeStruct((n, d), out_dtype),
        compiler_params=_cparams(("parallel",)),
        name="cast_rows",
    )(x)


def _projection_kernel(x_ref, g_ref, wr_ref, wt_ref, wtail_ref, wtailt_ref, pr_ref, pt_ref, tail_ref, tailt_ref, h_sc,
                       *, n_row_tiles, n_t_tiles):
    j = pl.program_id(1)

    @pl.when(j == 0)
    def _():
        h_sc[...] = _rms(x_ref[...], g_ref[...]).astype(BF16)
        tail_ref[...] = _dot(h_sc[...], wtail_ref[...])
        tailt_ref[...] = _dot_nt(wtailt_ref[...], h_sc[...])

    @pl.when((j >= 1) & (j <= n_row_tiles))
    def _():
        pr_ref[...] = _dot(h_sc[...], wr_ref[...]).astype(pr_ref.dtype)

    @pl.when(j > n_row_tiles)
    def _():
        pt_ref[...] = _dot_nt(wt_ref[...], h_sc[...]).astype(pt_ref.dtype)


def input_projection(x, gain, w_rows, w_t, w_tail, tm=1024, tn=512):
    n, d = x.shape
    n_row_tiles, n_t_tiles = w_rows.shape[1] // tn, w_t.shape[0] // tn

    def row_tile(j):
        return jnp.clip(j - 1, 0, n_row_tiles - 1)

    def t_tile(j):
        return jnp.clip(j - 1 - n_row_tiles, 0, n_t_tiles - 1)

    return pl.pallas_call(
        functools.partial(_projection_kernel, n_row_tiles=n_row_tiles, n_t_tiles=n_t_tiles),
        grid=(n // tm, n_row_tiles + n_t_tiles + 1),
        in_specs=[pl.BlockSpec((tm, d), lambda i, j: (i, 0)),
                  pl.BlockSpec((1, d), lambda i, j: (0, 0)),
                  pl.BlockSpec((d, tn), lambda i, j: (0, row_tile(j))),
                  pl.BlockSpec((tn, d), lambda i, j: (t_tile(j), 0)),
                  pl.BlockSpec((d, LANES), lambda i, j: (0, 0)),
                  pl.BlockSpec((LANES, d), lambda i, j: (0, 0))],
        out_specs=[pl.BlockSpec((tm, tn), lambda i, j: (i, row_tile(j))),
                   pl.BlockSpec((tn, tm), lambda i, j: (t_tile(j), i)),
                   pl.BlockSpec((tm, LANES), lambda i, j: (i, 0)),
                   pl.BlockSpec((LANES, tm), lambda i, j: (0, i))],
        out_shape=[jax.ShapeDtypeStruct((n, w_rows.shape[1]), BF16),
                   jax.ShapeDtypeStruct((w_t.shape[0], n), BF16),
                   jax.ShapeDtypeStruct((n, LANES), F32),
                   jax.ShapeDtypeStruct((LANES, n), F32)],
        scratch_shapes=[pltpu.VMEM((tm, d), BF16)],
        compiler_params=_cparams(("parallel", "arbitrary")),
        name="input_projection",
    )(x, gain.reshape(1, d), w_rows, w_t, w_tail, w_tail.T)


def _swa_kernel(sink_ref, q_ref, kp_ref, kc_ref, vp_ref, vc_ref, o_ref):
    n = pl.program_id(1)
    tq = q_ref.shape[0]
    q = q_ref[...]
    k = jnp.concatenate([kp_ref[...], kc_ref[...]], axis=0)
    v = jnp.concatenate([vp_ref[...], vc_ref[...]], axis=0)
    i = lax.broadcasted_iota(jnp.int32, (tq, 2 * tq), 0)
    j = lax.broadcasted_iota(jnp.int32, (tq, 2 * tq), 1)
    rel = tq + i - j
    mask = (rel >= 0) & (rel < WINDOW_A) & ((j >= tq) | (n > 0))
    relf = rel.astype(F32)
    group = A_HEADS // A_KV_HEADS
    for h in range(A_HEADS):
        kv = h // group
        qh = q[:, h * HEAD_DIM:(h + 1) * HEAD_DIM]
        kh = k[:, kv * HEAD_DIM:(kv + 1) * HEAD_DIM]
        vh = v[:, kv * HEAD_DIM:(kv + 1) * HEAD_DIM]
        s = _dot_nt(qh, kh) * Q_SCALE - SLOPES_A[h] * relf
        s = jnp.where(mask, s, NEG_INF)
        sink = sink_ref[h]
        m = jnp.maximum(jnp.max(s, axis=-1, keepdims=True), sink)
        p = jnp.exp(s - m)
        l = jnp.sum(p, axis=-1, keepdims=True) + jnp.exp(sink - m)
        o_ref[:, h * HEAD_DIM:(h + 1) * HEAD_DIM] = _dot(p.astype(BF16), vh) / l


def swa_attention(proj, sinks, batch, seq):
    tq = WINDOW_A
    nb = seq // tq
    kcol, vcol = OFF_KA // KV_WIDTH, OFF_VA // KV_WIDTH

    def cur(col):
        return pl.BlockSpec((tq, KV_WIDTH), lambda b, n: (b * nb + n, col))

    def prev(col):
        return pl.BlockSpec((tq, KV_WIDTH), lambda b, n: (b * nb + jnp.maximum(n - 1, 0), col))

    return pl.pallas_call(
        _swa_kernel,
        grid=(batch, nb),
        in_specs=[pl.BlockSpec(memory_space=pltpu.SMEM),
                  pl.BlockSpec((tq, A_WIDTH), lambda b, n: (b * nb + n, OFF_QA // A_WIDTH)),
                  prev(kcol), cur(kcol), prev(vcol), cur(vcol)],
        out_specs=pl.BlockSpec((tq, A_WIDTH), lambda b, n: (b * nb + n, 0)),
        out_shape=jax.ShapeDtypeStruct((batch * seq, A_WIDTH), F32),
        compiler_params=_cparams(("parallel", "parallel")),
        name="swa_attention",
    )(sinks, proj, proj, proj, proj, proj)


def _fox_aug_kernel(z_ref, b_ref, pk_ref, pq_ref, ka_ref, qa_ref, carry_sc):
    @pl.when(pl.program_id(1) == 0)
    def _():
        carry_sc[...] = jnp.zeros_like(carry_sc)

    z = z_ref[...] + b_ref[...]
    log_f = -(jnp.maximum(-z, 0.0) + jnp.log1p(jnp.exp(-jnp.abs(z))))
    ts = z.shape[0]
    r = lax.broadcasted_iota(jnp.int32, (ts, ts), 0)
    c = lax.broadcasted_iota(jnp.int32, (ts, ts), 1)
    tri = (c <= r).astype(BF16)
    hi, mid, lo = _split3(log_f)
    cum = _dot(tri, hi) + _dot(tri, mid) + _dot(tri, lo) + carry_sc[0:1, :]
    carry_sc[0:1, :] = cum[ts - 1:ts, :]
    hi, mid, lo = _split3(cum * LOG2E)
    lane = lax.broadcasted_iota(jnp.int32, (ts, LANES), 1)
    hi = jnp.where(lane == ONE_LANE, 1.0, hi).astype(BF16)
    ka_ref[...] = (_dot(hi, pk_ref[0]) + _dot(mid, pk_ref[1]) + _dot(lo, pk_ref[2])).astype(BF16)
    qa_ref[...] = (_dot_nt(pq_ref[0], hi) + _dot_nt(pq_ref[1], mid) + _dot_nt(pq_ref[2], lo)).astype(BF16)


def _aug_lane(h):
    return HEAD_DIM if h % 2 == 0 else 0


def _fox_placement():
    pk = np.zeros((3, LANES, C_HEADS * LANES), np.float32)
    pq = np.zeros((3, C_HEADS * Q_AUG_ROWS, LANES), np.float32)
    for h in range(C_HEADS):
        src, kbase, qbase = TAIL_FC + h, h * LANES + _aug_lane(h), h * Q_AUG_ROWS
        for piece in range(3):
            pk[piece, src, kbase + piece] = -1.0
            pq[piece, qbase + 3 + piece, src] = 1.0
            pk[0, ONE_LANE, kbase + 3 + piece] = 1.0
            pq[0, qbase + piece, ONE_LANE] = 1.0
    return jnp.asarray(pk, BF16), jnp.asarray(pq, BF16)


def fox_augmentation(tail, bias_row, batch, seq, ts=512):
    nt = seq // ts
    pk, pq = _fox_placement()
    kw, qw = C_HEADS * LANES, C_HEADS * Q_AUG_ROWS
    return pl.pallas_call(
        _fox_aug_kernel,
        grid=(batch, nt),
        in_specs=[pl.BlockSpec((ts, LANES), lambda b, i: (b * nt + i, 0)),
                  pl.BlockSpec((1, LANES), lambda b, i: (0, 0)),
                  pl.BlockSpec((3, LANES, kw), lambda b, i: (0, 0, 0)),
                  pl.BlockSpec((3, qw, LANES), lambda b, i: (0, 0, 0))],
        out_specs=[pl.BlockSpec((ts, kw), lambda b, i: (b * nt + i, 0)),
                   pl.BlockSpec((qw, ts), lambda b, i: (0, b * nt + i))],
        out_shape=[jax.ShapeDtypeStruct((batch * seq, kw), BF16),
                   jax.ShapeDtypeStruct((qw, batch * seq), BF16)],
        scratch_shapes=[pltpu.VMEM((8, LANES), F32)],
        compiler_params=_cparams(("parallel", "arbitrary")),
        name="fox_augmentation",
    )(tail, bias_row, pk, pq)


def _alibi_augmentation(seq, tk):
    pos = np.arange(seq)
    ka = np.zeros((seq, LANES), np.float32)
    for base in (0, HEAD_DIM):
        ka[:, base:base + 3] = 1.0
        ka[:, base + 3:base + 6] = ((pos >> 8) << 8)[:, None]
        ka[:, base + 6:base + 9] = (pos & 255)[:, None]
        ka[pos, base + Q_AUG_ROWS + (pos % tk) // SEL_BLOCK] = 1.0
    def split3_np(x):
        pieces, rest = [], x.astype(np.float32)
        for _ in range(3):
            piece = rest.astype(jnp.bfloat16).astype(np.float32)
            pieces.append(piece)
            rest = rest - piece
        return pieces

    slopes = (np.asarray(SLOPES_B, np.float32) * np.float32(LOG2E)).astype(np.float32)
    st = split3_np(-(slopes[:, None] * pos.astype(np.float32)[None, :]))
    sl = split3_np(slopes)
    qa = np.zeros((B_HEADS, Q_AUG_ROWS, seq), np.float32)
    for piece in range(3):
        qa[:, piece, :] = st[piece]
        qa[:, 3 + piece, :] = sl[piece][:, None]
        qa[:, 6 + piece, :] = sl[piece][:, None]
    return jnp.asarray(ka, BF16), jnp.asarray(qa.reshape(B_HEADS * Q_AUG_ROWS, seq), BF16)


def _flash_t_kernel(*refs, mode, tq, tk, n_steps):
    pen_sc = None
    if mode == "slc":
        (live_ref, k_ref, ka_ref, qt_ref, qat_ref, vt_ref, gate_ref, selt_ref, o_ref,
         m_sc, l_sc, acc_sc, pen_sc) = refs
    elif mode == "win":
        k_ref, ka_ref, qt_ref, qat_ref, vt_ref, gate_ref, o_ref, m_sc, l_sc, acc_sc = refs
    else:
        k_ref, ka_ref, qt_ref, qat_ref, vt_ref, o_ref, m_sc, l_sc, acc_sc = refs
    n_heads = qt_ref.shape[0] // HEAD_DIM
    group = 1 if mode == "fox" else B_GROUP
    blocks_per_tile = tk // SEL_BLOCK
    iq, j = pl.program_id(1), pl.program_id(2)
    if mode == "win":
        ik = iq - (n_steps - 1) + j
        last_j = n_steps - 1
        active = ik >= 0
    else:
        ik = j
        last_j = (iq * tq + tq - 1) // tk
        active = j <= last_j

    @pl.when(j == 0)
    def _():
        m_sc[...] = jnp.full_like(m_sc, M_INIT)
        l_sc[...] = jnp.zeros_like(l_sc)
        acc_sc[...] = jnp.zeros_like(acc_sc)
        if mode == "slc":
            pen_sc[...] = (1.0 - selt_ref[0].astype(F32)) * MASK_PEN

    def step(masked, heads):
        ok = None
        if masked:
            rel = ((iq * tq + lax.broadcasted_iota(jnp.int32, (tk, tq), 1))
                   - (ik * tk + lax.broadcasted_iota(jnp.int32, (tk, tq), 0)))
            ok = rel >= 0
            if mode == "win":
                ok = ok & (rel < WINDOW_B)
        lane = lax.broadcasted_iota(jnp.int32, (tk, LANES), 1)
        ones_rows = jnp.ones((Q_AUG_ROWS, tk), BF16)
        per_kv = {}

        def kv_operands(kv):
            if kv not in per_kv:
                pair = slice((kv // 2) * LANES, (kv // 2 + 1) * LANES)
                aug = ka_ref[:, kv * LANES:(kv + 1) * LANES] if mode == "fox" else ka_ref[...]
                own = (lane < HEAD_DIM) if kv % 2 == 0 else (lane >= HEAD_DIM)
                ka = jnp.where(own, k_ref[:, pair], aug)
                v_aug = jnp.concatenate([vt_ref[kv * HEAD_DIM:(kv + 1) * HEAD_DIM, :], ones_rows], axis=0)
                pen = None
                if mode == "slc":
                    pen8 = pen_sc[kv, pl.ds(pl.multiple_of(ik * blocks_per_tile, blocks_per_tile), blocks_per_tile), :]
                    pen = jnp.concatenate(
                        [pen8, jnp.zeros((Q_AUG_ROWS - blocks_per_tile, tq), F32)], axis=0).astype(BF16)
                per_kv[kv] = (ka, v_aug, pen)
            return per_kv[kv]

        def scores(h):
            kv = h // group
            ka, _, pen = kv_operands(kv)
            aug_rows = [qat_ref[h * Q_AUG_ROWS:(h + 1) * Q_AUG_ROWS, :]]
            if mode == "slc":
                aug_rows.append(pen)
            aug_rows.append(jnp.zeros((N_AUG - Q_AUG_ROWS * len(aug_rows), tq), BF16))
            halves = [[qt_ref[h * HEAD_DIM:(h + 1) * HEAD_DIM, :]], aug_rows]
            qa = jnp.concatenate(sum(halves if kv % 2 == 0 else halves[::-1], []), axis=0)
            s = _dot(ka, qa)
            return s if ok is None else jnp.where(ok, s, NEG_INF)

        def probabilities(h, s):
            m_prev = m_sc[h:h + 1, :]
            m_new = jnp.maximum(m_prev, jnp.max(s, axis=0, keepdims=True))
            m_sc[h:h + 1, :] = m_new
            return jnp.exp2(s - m_new).astype(BF16), jnp.exp2(m_prev - m_new)

        def accumulate(h, p, alpha):
            rows = slice(h * HEAD_DIM, (h + 1) * HEAD_DIM)
            pv = _dot(kv_operands(h // group)[1], p)
            l_sc[h:h + 1, :] = alpha * l_sc[h:h + 1, :] + pv[HEAD_DIM:HEAD_DIM + 1, :]
            acc_sc[rows, :] = alpha * acc_sc[rows, :] + pv[0:HEAD_DIM, :]

        s_cur, pending = scores(heads[0]), None
        for i, h in enumerate(heads):
            s_next = scores(heads[i + 1]) if i + 1 < len(heads) else None
            p_alpha = probabilities(h, s_cur)
            if pending is not None:
                accumulate(*pending)
            pending = (h,) + p_alpha
            s_cur = s_next
        accumulate(*pending)

    all_heads = list(range(n_heads))
    on_diagonal = ik * tk + tk - 1 > iq * tq
    if mode == "win":
        pl.when(active)(lambda: step(True, all_heads))
    elif mode == "fox":
        pl.when(active & on_diagonal)(lambda: step(True, all_heads))
        pl.when(active & jnp.logical_not(on_diagonal))(lambda: step(False, all_heads))
    else:
        tile = (pl.program_id(0) * pl.num_programs(1) + iq) * n_steps + ik
        for kv in range(B_KV_HEADS):
            heads = all_heads[kv * group:(kv + 1) * group]
            live = active & (live_ref[tile * B_KV_HEADS + kv] > 0)
            pl.when(live & on_diagonal)(functools.partial(step, True, heads))
            pl.when(live & jnp.logical_not(on_diagonal))(functools.partial(step, False, heads))

    @pl.when(j == last_j)
    def _():
        branch = {"fox": None, "slc": 1, "win": 2}[mode]
        for pair in range(n_heads // 2):
            rows = slice(pair * LANES, (pair + 1) * LANES)
            heads = (2 * pair, 2 * pair + 1)
            denom = jnp.concatenate(
                [jnp.broadcast_to(l_sc[h:h + 1, :], (HEAD_DIM, tq)) for h in heads], axis=0)
            out = acc_sc[rows, :] / denom
            if branch is not None:
                out = out * jnp.concatenate(
                    [jnp.broadcast_to(_sigmoid(gate_ref[3 * h + branch:3 * h + branch + 1, :]), (HEAD_DIM, tq))
                     for h in heads], axis=0)
            o_ref[:, rows] = out.T


def flash_attention_t(mode, proj_r, proj_t, k_aug, q_aug_t, batch, seq, gates_t=None, sel_t=None, tq=512, tk=512):
    nq, nk = seq // tq, seq // tk
    if mode == "win":
        assert tq == tk
        n_steps = WINDOW_B // tk + 1

        def kblock(iq, j):
            return jnp.maximum(iq - (n_steps - 1) + j, 0)
    else:
        n_steps = nk

        def kblock(iq, j):
            return jnp.minimum(j, (iq * tq + tq - 1) // tk)

    if mode == "fox":
        in_specs = [pl.BlockSpec((tk, C_WIDTH), lambda b, iq, j, *_: (b * nk + kblock(iq, j), OFF_KC // C_WIDTH)),
                    pl.BlockSpec((tk, C_HEADS * LANES), lambda b, iq, j, *_: (b * nk + kblock(iq, j), 0)),
                    pl.BlockSpec((C_WIDTH, tq), lambda b, iq, j, *_: (OFF_QCT // C_WIDTH, b * nq + iq)),
                    pl.BlockSpec((C_HEADS * Q_AUG_ROWS, tq), lambda b, iq, j, *_: (0, b * nq + iq)),
                    pl.BlockSpec((C_WIDTH, tk), lambda b, iq, j, *_: (OFF_VCT // C_WIDTH, b * nk + kblock(iq, j)))]
    else:
        kcol = (OFF_KBS if mode == "slc" else OFF_KBW) // KV_WIDTH
        vrow = (OFF_VBST if mode == "slc" else OFF_VBWT) // KV_WIDTH
        in_specs = [pl.BlockSpec((tk, KV_WIDTH), lambda b, iq, j, *_: (b * nk + kblock(iq, j), kcol)),
                    pl.BlockSpec((tk, LANES), lambda b, iq, j, *_: (kblock(iq, j), 0)),
                    pl.BlockSpec((B_WIDTH, tq), lambda b, iq, j, *_: (OFF_QBT // B_WIDTH, b * nq + iq)),
                    pl.BlockSpec((B_HEADS * Q_AUG_ROWS, tq), lambda b, iq, j, *_: (0, iq)),
                    pl.BlockSpec((KV_WIDTH, tk), lambda b, iq, j, *_: (vrow, b * nk + kblock(iq, j)))]
    args = [proj_r, k_aug, proj_t, q_aug_t, proj_t]
    prefetch = []
    width = C_WIDTH if mode == "fox" else B_WIDTH
    n_heads = width // HEAD_DIM
    scratch = [pltpu.VMEM((16, tq), F32), pltpu.VMEM((16, tq), F32), pltpu.VMEM((n_heads * HEAD_DIM, tq), F32)]
    if mode != "fox":
        in_specs.append(pl.BlockSpec((LANES, tq), lambda b, iq, j, *_: (0, b * nq + iq)))
        args.append(gates_t)
    if mode == "slc":
        n_slc = seq // SEL_BLOCK
        per_tile = tk // SEL_BLOCK
        assert per_tile <= Q_AUG_ROWS and tk % SEL_BLOCK == 0
        in_specs.append(pl.BlockSpec((1, B_KV_HEADS, n_slc, tq), lambda b, iq, j, *_: (b, 0, 0, iq)))
        args.append(sel_t)
        scratch.append(pltpu.VMEM((B_KV_HEADS, n_slc, tq), F32))
        live = sel_t.reshape(batch, B_KV_HEADS, nk, per_tile, nq, tq).max(axis=(3, 5)) > 0
        prefetch.append(live.transpose(0, 3, 2, 1).reshape(-1).astype(jnp.int32))
    return pl.pallas_call(
        functools.partial(_flash_t_kernel, mode=mode, tq=tq, tk=tk, n_steps=n_steps),
        grid_spec=pltpu.PrefetchScalarGridSpec(
            num_scalar_prefetch=len(prefetch),
            grid=(batch, nq, n_steps),
            in_specs=in_specs,
            out_specs=pl.BlockSpec((tq, width), lambda b, iq, j, *_: (b * nq + iq, 0)),
            scratch_shapes=scratch),
        out_shape=jax.ShapeDtypeStruct((batch * seq, width), F32),
        compiler_params=_cparams(("parallel", "parallel", "arbitrary")),
        name=mode + "_attention",
    )(*prefetch, *args)


def _compress_kernel(t_ref, pe_ref, w1_ref, w2_ref, w2t_ref, o_ref, ot_ref):
    w1 = w1_ref[0]
    hid = _dot(t_ref[0, 0, 0], w1) + _dot(pe_ref[0], w1)[0:1, :]
    act = (hid * _sigmoid(hid)).astype(BF16)
    o_ref[0, 0, 0] = _dot(act, w2_ref[0]).astype(o_ref.dtype)
    ot_ref[0, 0, 0] = _dot_nt(w2t_ref[0], act).astype(ot_ref.dtype)


def nsa_compress(flat, pe_rows, w1, w2):
    _, batch, n_kv, n_chunks, width = flat.shape
    hidden = w1.shape[-1]
    return pl.pallas_call(
        _compress_kernel,
        grid=(2, batch, n_kv),
        in_specs=[pl.BlockSpec((1, 1, 1, n_chunks, width), lambda s, b, h: (s, b, h, 0, 0)),
                  pl.BlockSpec((1, 8, width), lambda s, b, h: (s, 0, 0)),
                  pl.BlockSpec((1, width, hidden), lambda s, b, h: (s, 0, 0)),
                  pl.BlockSpec((1, hidden, HEAD_DIM), lambda s, b, h: (s, 0, 0)),
                  pl.BlockSpec((1, HEAD_DIM, hidden), lambda s, b, h: (s, 0, 0))],
        out_specs=[pl.BlockSpec((1, 1, 1, n_chunks, HEAD_DIM), lambda s, b, h: (s, b, h, 0, 0)),
                   pl.BlockSpec((1, 1, 1, HEAD_DIM, n_chunks), lambda s, b, h: (s, b, h, 0, 0))],
        out_shape=[jax.ShapeDtypeStruct((2, batch, n_kv, n_chunks, HEAD_DIM), BF16),
                   jax.ShapeDtypeStruct((2, batch, n_kv, HEAD_DIM, n_chunks), BF16)],
        compiler_params=_cparams(("parallel", "parallel", "parallel")),
        name="nsa_compress",
    )(flat, pe_rows, w1, w2, jnp.swapaxes(w2, 1, 2))


def _cmp_attn_t_kernel(qt_ref, kc_ref, vct_ref, inter_ref, gate_ref, o_ref, selt_ref, ot_sc, *, tq):
    iq = pl.program_id(1)
    n_chunks = kc_ref.shape[3]
    n_slc = inter_ref.shape[0]
    t = iq * tq + lax.broadcasted_iota(jnp.int32, (n_chunks, tq), 1)
    n = lax.broadcasted_iota(jnp.int32, (n_chunks, tq), 0)
    rel = t - (n * CMP_STRIDE + CMP_LEN - 1)
    mask = rel >= 0
    relf = rel.astype(F32)
    t_s = iq * tq + lax.broadcasted_iota(jnp.int32, (n_slc, tq), 1)
    jj = lax.broadcasted_iota(jnp.int32, (n_slc, tq), 0)
    cur = t_s >> (SEL_BLOCK.bit_length() - 1)
    valid = jj * SEL_BLOCK <= t_s
    forced = (jj == 0) | (jj == cur) | (jj == cur - 1)
    jf = jj.astype(F32)
    inter = inter_ref[...]
    def raw_scores(hd):
        return _dot(kc_ref[0, 0, hd // B_GROUP], qt_ref[hd * HEAD_DIM:(hd + 1) * HEAD_DIM, :])

    raw_next = raw_scores(0)
    for h in range(B_KV_HEADS):
        vct = vct_ref[0, 0, h]
        p_sum = jnp.zeros((n_chunks, tq), F32)
        for g in range(B_GROUP):
            hd = h * B_GROUP + g
            rows = slice(hd * HEAD_DIM, (hd + 1) * HEAD_DIM)
            raw, raw_next = raw_next, (raw_scores(hd + 1) if hd + 1 < B_HEADS else None)
            s = jnp.where(mask, raw - (SLOPES_B[hd] * LOG2E) * relf, NEG_INF)
            m = jnp.max(s, axis=0, keepdims=True)
            e = jnp.where(mask, jnp.exp2(s - m), 0.0)
            l = jnp.sum(e, axis=0, keepdims=True)
            p = e / jnp.where(l > 0.0, l, 1.0)
            p_sum = p_sum + p
            ot_sc[rows, :] = _dot(vct, p.astype(BF16)) * _sigmoid(gate_ref[3 * hd:3 * hd + 1, :])
        hi, mid, lo = _split3(p_sum)
        importance = _dot(inter, hi) + _dot(inter, mid) + _dot(inter, lo)
        score = jnp.where(valid, importance + jnp.where(forced, FORCE_BONUS, 0.0), NEG_INF)
        chosen = jnp.zeros((n_slc, tq), F32)
        for _ in range(min(N_SELECT, n_slc)):
            best = jnp.max(score, axis=0, keepdims=True)
            first = jnp.min(jnp.where(score == best, jf, float(n_slc)), axis=0, keepdims=True)
            pick = jf == first
            chosen = jnp.where(pick, 1.0, chosen)
            score = jnp.where(pick, -jnp.inf, score)
        selt_ref[0, h] = jnp.where(valid, chosen, 0.0).astype(selt_ref.dtype)
    for pair in range(B_HEADS // 2):
        rows = slice(pair * LANES, (pair + 1) * LANES)
        o_ref[:, rows] = ot_sc[rows, :].T


def nsa_compressed_attention(proj_t, kvc, kvct, inter_t, gates_t, batch, seq, tq=512):
    nq = seq // tq
    n_chunks = kvc.shape[3]
    n_slc = seq // SEL_BLOCK
    return pl.pallas_call(
        functools.partial(_cmp_attn_t_kernel, tq=tq),
        grid=(batch, nq),
        in_specs=[pl.BlockSpec((B_WIDTH, tq), lambda b, i: (OFF_QBT // B_WIDTH, b * nq + i)),
                  pl.BlockSpec((1, 1, B_KV_HEADS, n_chunks, HEAD_DIM), lambda b, i: (0, b, 0, 0, 0)),
                  pl.BlockSpec((1, 1, B_KV_HEADS, HEAD_DIM, n_chunks), lambda b, i: (1, b, 0, 0, 0)),
                  pl.BlockSpec((n_slc, n_chunks), lambda b, i: (0, 0)),
                  pl.BlockSpec((LANES, tq), lambda b, i: (0, b * nq + i))],
        out_specs=[pl.BlockSpec((tq, B_WIDTH), lambda b, i: (b * nq + i, 0)),
                   pl.BlockSpec((1, B_KV_HEADS, n_slc, tq), lambda b, i: (b, 0, 0, i))],
        out_shape=[jax.ShapeDtypeStruct((batch * seq, B_WIDTH), F32),
                   jax.ShapeDtypeStruct((batch, B_KV_HEADS, n_slc, seq), BF16)],
        scratch_shapes=[pltpu.VMEM((B_WIDTH, tq), F32)],
        compiler_params=_cparams(("parallel", "parallel")),
        name="nsa_cmp_attention",
    )(proj_t, kvc, kvct, inter_t, gates_t)


def _mix_out_kernel(oa_ref, ocmp_ref, oslc_ref, owin_ref, oc_ref, g_ref, w_ref, x_ref, gn_ref, o_ref, h_ref):
    o_b = ocmp_ref[...] + oslc_ref[...] + owin_ref[...]
    b0, c0 = A_WIDTH, A_WIDTH + B_WIDTH
    mixed = jnp.concatenate([_rms(oa_ref[...], g_ref[:, 0:b0]).astype(BF16),
                             _rms(o_b, g_ref[:, b0:c0]).astype(BF16),
                             _rms(oc_ref[...], g_ref[:, c0:]).astype(BF16)], axis=1)
    x_new = x_ref[...] + _dot(mixed, w_ref[...])
    o_ref[...] = x_new
    h_ref[...] = _rms(x_new, gn_ref[...]).astype(h_ref.dtype)


def mix_out(o_a, o_cmp, o_slc, o_win, o_c, gain, w_out, x, next_gain, tm=256):
    n, d = x.shape
    width = w_out.shape[0]

    def rows(w):
        return pl.BlockSpec((tm, w), lambda i: (i, 0))

    return pl.pallas_call(
        _mix_out_kernel,
        grid=(n // tm,),
        in_specs=[rows(A_WIDTH), rows(B_WIDTH), rows(B_WIDTH), rows(B_WIDTH), rows(C_WIDTH),
                  pl.BlockSpec((1, width), lambda i: (0, 0)),
                  pl.BlockSpec((width, d), lambda i: (0, 0)),
                  rows(d),
                  pl.BlockSpec((1, d), lambda i: (0, 0))],
        out_specs=[rows(d), rows(d)],
        out_shape=[jax.ShapeDtypeStruct((n, d), F32), jax.ShapeDtypeStruct((n, d), BF16)],
        compiler_params=_cparams(("parallel",)),
        name="mix_out",
    )(o_a, o_cmp, o_slc, o_win, o_c, gain.reshape(1, width), w_out, x, next_gain.reshape(1, d))


PART_FULL, PART_FIRST, PART_SECOND, PART_EMPTY, PART_IDLE = range(5)


def _fresh_weights(exp_ref):
    t = pl.program_id(1)
    return (t == 0) | (exp_ref[t] != exp_ref[jnp.maximum(t - 1, 0)])


def _for_each_part(part, o_ref, compute):
    half = o_ref.shape[0] // 2

    @pl.when(part == PART_FULL)
    def _():
        o_ref[...] = compute(slice(None))

    @pl.when(part == PART_FIRST)
    def _():
        o_ref[0:half, :] = compute(slice(0, half))
        o_ref[half:, :] = jnp.zeros((half, o_ref.shape[1]), o_ref.dtype)

    @pl.when(part == PART_SECOND)
    def _():
        o_ref[half:, :] = compute(slice(half, 2 * half))

    @pl.when(part == PART_EMPTY)
    def _():
        o_ref[...] = jnp.zeros_like(o_ref)


def _ffn_up_kernel(blk_ref, exp_ref, part_ref, x_ref, wg_ref, wu_ref, o_ref, wg_sc, wu_sc):
    @pl.when(_fresh_weights(exp_ref))
    def _():
        wg_sc[...] = wg_ref[0].astype(BF16)
        wu_sc[...] = wu_ref[0].astype(BF16)

    def compute(rows):
        x = x_ref[rows, :].astype(BF16)
        gate = _dot(x, wg_sc[...])
        up = _dot(x, wu_sc[...])
        return (gate * _sigmoid(gate) * up).astype(o_ref.dtype)

    _for_each_part(part_ref[pl.program_id(1)], o_ref, compute)


def _ffn_down_kernel(blk_ref, exp_ref, part_ref, h_ref, wd_ref, *rest):
    o_ref, wd_sc = rest[-2], rest[-1]

    @pl.when(_fresh_weights(exp_ref))
    def _():
        wd_sc[...] = wd_ref[0].astype(BF16)

    def compute(rows):
        y = _dot(h_ref[rows, :], wd_sc[...])
        return y + rest[0][rows, :] if len(rest) == 3 else y

    _for_each_part(part_ref[pl.program_id(1)], o_ref, compute)


def _whole_block_items(n_blocks, expert):
    return (jnp.arange(n_blocks, dtype=jnp.int32), jnp.full((n_blocks,), expert, jnp.int32),
            jnp.full((n_blocks,), PART_FULL, jnp.int32))


def grouped_swiglu(xs, items_up, items_down, w_gate, w_up, w_down, residual=None, tm=512, tf=512, tm_down=512,
                   tn=512):
    rows, d = xs.shape
    f = w_gate.shape[-1]

    hidden = pl.pallas_call(
        _ffn_up_kernel,
        grid_spec=pltpu.PrefetchScalarGridSpec(
            num_scalar_prefetch=3,
            grid=(f // tf, items_up[0].shape[0]),
            in_specs=[pl.BlockSpec((tm, d), lambda j, t, blk, ex, part: (blk[t], 0)),
                      pl.BlockSpec((1, d, tf), lambda j, t, blk, ex, part: (ex[t], 0, j)),
                      pl.BlockSpec((1, d, tf), lambda j, t, blk, ex, part: (ex[t], 0, j))],
            out_specs=pl.BlockSpec((tm, tf), lambda j, t, blk, ex, part: (blk[t], j)),
            scratch_shapes=[pltpu.VMEM((d, tf), BF16), pltpu.VMEM((d, tf), BF16)]),
        out_shape=jax.ShapeDtypeStruct((rows, f), BF16),
        compiler_params=_cparams(("parallel", "arbitrary"), VMEM_LIMIT_BIG),
        name="ffn_up",
    )(*items_up, xs, w_gate, w_up)

    in_specs = [pl.BlockSpec((tm_down, f), lambda j, t, blk, ex, part: (blk[t], 0)),
                pl.BlockSpec((1, f, tn), lambda j, t, blk, ex, part: (ex[t], 0, j))]
    args = [hidden, w_down]
    if residual is not None:
        in_specs.append(pl.BlockSpec((tm_down, tn), lambda j, t, blk, ex, part: (blk[t], j)))
        args.append(residual)
    return pl.pallas_call(
        _ffn_down_kernel,
        grid_spec=pltpu.PrefetchScalarGridSpec(
            num_scalar_prefetch=3,
            grid=(d // tn, items_down[0].shape[0]),
            in_specs=in_specs,
            out_specs=pl.BlockSpec((tm_down, tn), lambda j, t, blk, ex, part: (blk[t], j)),
            scratch_shapes=[pltpu.VMEM((f, tn), BF16)]),
        out_shape=jax.ShapeDtypeStruct((rows, d), F32),
        compiler_params=_cparams(("parallel", "arbitrary"), VMEM_LIMIT_BIG),
        name="ffn_down",
    )(*items_down, *args)


def _router_kernel(x_ref, g_ref, wr_ref, h_ref, ri_ref, rf_ref, cnt_ref, carry_sc):
    @pl.when(pl.program_id(0) == 0)
    def _():
        carry_sc[...] = jnp.zeros_like(carry_sc)

    tm = x_ref.shape[0]
    h = _rms(x_ref[...], g_ref[...])
    h_ref[...] = h
    h1, h2, h3 = _split3(h)
    w1, w2, w3 = wr_ref[0], wr_ref[1], wr_ref[2]
    logits = (_dot(h1, w1) + _dot(h1, w2) + _dot(h2, w1)) + (_dot(h1, w3) + _dot(h2, w2) + _dot(h3, w1))
    lane_i = lax.broadcasted_iota(jnp.int32, (tm, LANES), 1)
    lane = lane_i.astype(F32)
    logits = jnp.where(lane_i < N_EXPERTS, logits, -jnp.inf)
    v1 = jnp.max(logits, axis=-1, keepdims=True)
    e1 = jnp.min(jnp.where(logits == v1, lane, float(LANES)), axis=-1, keepdims=True)
    rest = jnp.where(lane == e1, -jnp.inf, logits)
    v2 = jnp.max(rest, axis=-1, keepdims=True)
    e2 = jnp.min(jnp.where(rest == v2, lane, float(LANES)), axis=-1, keepdims=True)
    z = jnp.exp(v2 - v1)
    g1 = 1.0 / (1.0 + z)
    g2 = z / (1.0 + z)
    chosen = (lane == e1) | (lane == e2)
    onehot = jnp.where(chosen, 1.0, 0.0)
    r = lax.broadcasted_iota(jnp.int32, (tm, tm), 0)
    c = lax.broadcasted_iota(jnp.int32, (tm, tm), 1)
    before = _dot((c < r).astype(BF16), onehot.astype(BF16)) + carry_sc[0:1, :]
    r1 = jnp.sum(jnp.where(lane == e1, before, 0.0), axis=-1, keepdims=True)
    r2 = jnp.sum(jnp.where(lane == e2, before, 0.0), axis=-1, keepdims=True)
    carry = carry_sc[0:1, :] + jnp.sum(onehot, axis=0, keepdims=True)
    carry_sc[0:1, :] = carry
    cnt_ref[...] = jnp.broadcast_to(carry, cnt_ref.shape)
    packed = jnp.where(lane_i == 0, e1, jnp.where(lane_i == 1, e2, jnp.where(
        lane_i == 2, r1, jnp.where(lane_i == 3, r2, 0.0))))
    ri_ref[...] = packed.astype(jnp.int32)
    rf_ref[...] = jnp.where(lane_i == 0, g1, jnp.where(lane_i == 1, g2, 0.0))


def moe_route(x, gain, w_router3, tm=512):
    n, d = x.shape
    return pl.pallas_call(
        _router_kernel,
        grid=(n // tm,),
        in_specs=[pl.BlockSpec((tm, d), lambda i: (i, 0)),
                  pl.BlockSpec((1, d), lambda i: (0, 0)),
                  pl.BlockSpec((3, d, LANES), lambda i: (0, 0, 0))],
        out_specs=[pl.BlockSpec((tm, d), lambda i: (i, 0)),
                   pl.BlockSpec((tm, LANES), lambda i: (i, 0)),
                   pl.BlockSpec((tm, LANES), lambda i: (i, 0)),
                   pl.BlockSpec((8, LANES), lambda i: (0, 0))],
        out_shape=[jax.ShapeDtypeStruct((n, d), F32),
                   jax.ShapeDtypeStruct((n, LANES), jnp.int32),
                   jax.ShapeDtypeStruct((n, LANES), F32),
                   jax.ShapeDtypeStruct((8, LANES), F32)],
        scratch_shapes=[pltpu.VMEM((8, LANES), F32)],
        compiler_params=_cparams(("arbitrary",)),
        name="moe_route",
    )(x, gain.reshape(1, d), w_router3)


def _row_copy(src_ref, src_row, dst_ref, dst_row, sem):
    return pltpu.make_async_copy(src_ref.at[pl.ds(src_row, 1)], dst_ref.at[pl.ds(dst_row, 1)], sem)


def _dispatch_kernel(dest_ref, h_ref, zeros_ref, xs_ref, sem):
    del zeros_ref
    td = h_ref.shape[0]
    base = pl.program_id(0) * td

    def start(r, carry):
        for k in range(2):
            _row_copy(h_ref, r, xs_ref, dest_ref[2 * (base + r) + k], sem).start()
        return carry

    def wait(r, carry):
        for k in range(2):
            _row_copy(h_ref, r, xs_ref, dest_ref[2 * (base + r) + k], sem).wait()
        return carry

    lax.fori_loop(0, td, start, 0, unroll=DMA_LOOP_UNROLL)
    lax.fori_loop(0, td, wait, 0, unroll=DMA_LOOP_UNROLL)


def moe_dispatch(h, dest, cap, td=256):
    n, d = h.shape
    return pl.pallas_call(
        _dispatch_kernel,
        grid_spec=pltpu.PrefetchScalarGridSpec(
            num_scalar_prefetch=1,
            grid=(n // td,),
            in_specs=[pl.BlockSpec((td, d), lambda i, dest: (i, 0)),
                      pl.BlockSpec(memory_space=pl.ANY)],
            out_specs=pl.BlockSpec(memory_space=pl.ANY),
            scratch_shapes=[pltpu.SemaphoreType.DMA(())]),
        out_shape=jax.ShapeDtypeStruct((cap, d), h.dtype),
        input_output_aliases={2: 0},
        compiler_params=_cparams(("arbitrary",)),
        name="moe_dispatch",
    )(dest, h, jnp.zeros((cap, d), h.dtype))


def _combine_kernel(dest_ref, x_ref, gate_ref, y_ref, o_ref, ya_sc, yb_sc, sem):
    tc = x_ref.shape[0]
    base = pl.program_id(0) * tc

    def start(r, carry):
        _row_copy(y_ref, dest_ref[2 * (base + r)], ya_sc, r, sem).start()
        _row_copy(y_ref, dest_ref[2 * (base + r) + 1], yb_sc, r, sem).start()
        return carry

    def wait(r, carry):
        _row_copy(y_ref, dest_ref[2 * (base + r)], ya_sc, r, sem).wait()
        _row_copy(y_ref, dest_ref[2 * (base + r) + 1], yb_sc, r, sem).wait()
        return carry

    lax.fori_loop(0, tc, start, 0, unroll=DMA_LOOP_UNROLL)
    lax.fori_loop(0, tc, wait, 0, unroll=DMA_LOOP_UNROLL)
    gates = gate_ref[...]
    o_ref[...] = x_ref[...] + (gates[:, 0:1] * ya_sc[...] + gates[:, 1:2] * yb_sc[...])


def moe_combine(x, gates, y, dest, tc=256):
    n, d = x.shape
    return pl.pallas_call(
        _combine_kernel,
        grid_spec=pltpu.PrefetchScalarGridSpec(
            num_scalar_prefetch=1,
            grid=(n // tc,),
            in_specs=[pl.BlockSpec((tc, d), lambda i, dest: (i, 0)),
                      pl.BlockSpec((tc, LANES), lambda i, dest: (i, 0)),
                      pl.BlockSpec(memory_space=pl.ANY)],
            out_specs=pl.BlockSpec((tc, d), lambda i, dest: (i, 0)),
            scratch_shapes=[pltpu.VMEM((tc, d), F32), pltpu.VMEM((tc, d), F32),
                            pltpu.SemaphoreType.DMA(())]),
        out_shape=jax.ShapeDtypeStruct((n, d), F32),
        compiler_params=_cparams(("arbitrary",)),
        name="moe_combine",
    )(dest, x, gates, y)


def moe_layer(x, gain, w_router, w_gate, w_up, w_down, first_expert, tm=512):
    n, d = x.shape
    wr = jnp.pad(w_router.astype(F32), ((0, 0), (0, LANES - N_EXPERTS)))
    h, info, gates, counts = moe_route(x, gain, jnp.stack(_split3(wr)))
    half = tm // 2
    counts = counts[0, :N_EXPERTS].astype(jnp.int32)
    padded = (counts + half - 1) // half * half
    pad_ends = jnp.cumsum(padded)
    pad_starts = pad_ends - padded
    experts, ranks = info[:, 0:2], info[:, 2:4]
    dest = (pad_starts[experts] + ranks).reshape(-1).astype(jnp.int32)
    n_halves = (2 * n) // half + N_EXPERTS
    assert n_halves % 2 == 0
    items = _expert_items(pad_ends, n_halves, half, first_expert)
    xs = cast_rows(moe_dispatch(h, dest, n_halves * half), BF16)
    y = grouped_swiglu(xs, items, items, w_gate, w_up, w_down, tm=tm, tf=1024, tm_down=tm)
    return moe_combine(x, gates, y, dest)


def _expert_items(pad_ends, n_halves, half, first_expert):
    n_blocks = n_halves // 2
    start = jnp.arange(n_halves, dtype=jnp.int32) * half
    used = start < pad_ends[-1]
    owner = jnp.minimum(jnp.sum(start[:, None] >= pad_ends[None, :], axis=1), N_EXPERTS - 1).astype(jnp.int32)
    e0, e1, u0, u1 = owner[0::2], owner[1::2], used[0::2], used[1::2]
    straddle = u0 & u1 & (e0 != e1)
    n_items = 1 + straddle.astype(jnp.int32)
    ends = jnp.cumsum(n_items)
    t = jnp.arange(n_blocks + N_EXPERTS, dtype=jnp.int32)
    blk = jnp.minimum(jnp.sum(ends[None, :] <= t[:, None], axis=1), n_blocks - 1).astype(jnp.int32)
    second = straddle[blk] & (t - (ends - n_items)[blk] == 1)
    whole = jnp.where(u0, jnp.where(u1, PART_FULL, PART_FIRST), PART_EMPTY)
    part = jnp.where(straddle[blk], jnp.where(second, PART_SECOND, PART_FIRST), whole[blk])
    part = jnp.where(t < ends[-1], part, PART_IDLE).astype(jnp.int32)
    expert = jnp.where(second, e1[blk], e0[blk]) + first_expert
    return blk, expert.astype(jnp.int32), part


def _project_weights(w):
    def cols(a, n):
        return w[:, a:a + n]

    qa, ka, va, qb = cols(0, 512), cols(512, 128), cols(640, 128), cols(768, 768)
    kbc, vbc, kbs, vbs, kbw, vbw = (cols(1536 + 128 * i, 128) for i in range(6))
    gb, qc, kc, vc, fc = cols(2304, 36), cols(2340, 768), cols(3108, 768), cols(3876, 768), cols(4644, 12)
    w_rows = jnp.concatenate([kc, ka, va, qa, kbc, vbc, kbs, kbw], axis=1).astype(BF16)
    q_fold = Q_SCALE * LOG2E
    w_t = jnp.concatenate([qb * q_fold, qc * q_fold, vc, vbs, vbw], axis=1).T.astype(BF16)
    tail = jnp.concatenate([gb, fc], axis=1)
    tail = jnp.pad(tail, ((0, 0), (0, LANES - tail.shape[1]))).astype(BF16)
    return w_rows, w_t, tail


def _chunk_blocks(cols, batch, seq):
    n_chunks = seq // CMP_STRIDE
    t = cols.reshape(batch, seq, B_KV_HEADS, HEAD_DIM).transpose(0, 2, 1, 3)
    t = t.reshape(batch, B_KV_HEADS, n_chunks, CMP_STRIDE * HEAD_DIM)
    nxt = jnp.concatenate([t[:, :, 1:], jnp.zeros_like(t[:, :, :1])], axis=2)
    return jnp.concatenate([t, nxt], axis=-1)


def _overlap_matrix_t(seq):
    n_chunks, n_slc = seq // CMP_STRIDE, seq // SEL_BLOCK
    c_start = np.arange(n_chunks) * CMP_STRIDE
    s_start = np.arange(n_slc) * SEL_BLOCK
    inter = np.maximum(np.minimum(c_start[:, None] + CMP_LEN, s_start[None, :] + SEL_BLOCK)
                       - np.maximum(c_start[:, None], s_start[None, :]), 0) / CMP_LEN
    inter[(seq - CMP_LEN) // CMP_STRIDE + 1:] = 0.0
    return jnp.asarray(inter.T, BF16)


def mixer_layer(x, batch, seq, w_in, w_out, norm_mix, mix_out_norm, sinks, cmp_pe, cmp_w1, cmp_w2, f_bias, norm_ffn):
    w_rows, w_t, w_tail = _project_weights(w_in)
    proj_r, proj_t, tail, tail_t = input_projection(x, norm_mix, w_rows, w_t, w_tail)
    o_a = swa_attention(proj_r, sinks.astype(F32), batch, seq)

    bias_row = jnp.zeros((1, LANES), F32).at[0, TAIL_FC:TAIL_FC + C_HEADS].set(f_bias.astype(F32))
    fox_ka, fox_qat = fox_augmentation(tail, bias_row, batch, seq)
    o_c = flash_attention_t("fox", proj_r, proj_t, fox_ka, fox_qat, batch, seq)

    flat = jnp.stack([_chunk_blocks(proj_r[:, OFF_KBC:OFF_KBC + KV_WIDTH], batch, seq),
                      _chunk_blocks(proj_r[:, OFF_VBC:OFF_VBC + KV_WIDTH], batch, seq)])
    pe_rows = jnp.broadcast_to(cmp_pe.reshape(2, 1, CMP_LEN * HEAD_DIM), (2, 8, CMP_LEN * HEAD_DIM)).astype(BF16)
    kvc, kvct = nsa_compress(flat, pe_rows, cmp_w1.astype(BF16), cmp_w2.astype(BF16))
    o_cmp, sel_t = nsa_compressed_attention(proj_t, kvc, kvct, _overlap_matrix_t(seq), tail_t, batch, seq)
    tk = 512
    pos_ka, pos_qat = _alibi_augmentation(seq, tk)
    o_slc = flash_attention_t("slc", proj_r, proj_t, pos_ka, pos_qat, batch, seq, gates_t=tail_t, sel_t=sel_t, tk=tk)
    o_win = flash_attention_t("win", proj_r, proj_t, pos_ka, pos_qat, batch, seq, gates_t=tail_t, tk=tk)
    return mix_out(o_a, o_cmp, o_slc, o_win, o_c, mix_out_norm, w_out.astype(BF16), x, norm_ffn)


def dense_layer(x, h, w_gate, w_up, w_down, index, tm=1024):
    n = x.shape[0]
    tm_down = 512
    return grouped_swiglu(h, _whole_block_items(n // tm, index), _whole_block_items(n // tm_down, index),
                          w_gate, w_up, w_down, residual=x, tm=tm, tm_down=tm_down)


def kernel(x, w_in, w_out, norm_mix, mix_out_norm, attn_sinks, nsa_cmp_pe, nsa_cmp_w1, nsa_cmp_w2, fox_f_bias,
           norm_ffn, ffn_w_gate, ffn_w_up, ffn_w_down, moe_router, moe_w_gate, moe_w_up, moe_w_down, norm_final):
    batch, seq, d = x.shape
    depth = w_in.shape[0]
    f = ffn_w_gate.shape[-1]
    moe_gate, moe_up = moe_w_gate.reshape(-1, d, f), moe_w_up.reshape(-1, d, f)
    moe_down = moe_w_down.reshape(-1, f, d)
    xf = x.reshape(batch * seq, d).astype(F32)
    for layer in range(depth):
        xf, hf = mixer_layer(xf, batch, seq, w_in[layer], w_out[layer], norm_mix[layer], mix_out_norm[layer],
                             attn_sinks[layer], nsa_cmp_pe[layer], nsa_cmp_w1[layer], nsa_cmp_w2[layer],
                             fox_f_bias[layer], norm_ffn[layer])
        i = layer // 2
        if layer % 2 == 0:
            xf = dense_layer(xf, hf, ffn_w_gate, ffn_w_up, ffn_w_down, i)
        else:
            xf = moe_layer(xf, norm_ffn[layer], moe_router[i], moe_gate, moe_up, moe_down, i * N_EXPERTS)
    return rmsnorm_rows(xf, norm_final, x.dtype).reshape(batch, seq, d)
```

```python
import functools

import jax
import jax.numpy as jnp
import numpy as np
from jax import lax
from jax.experimental import pallas as pl
from jax.experimental.pallas import tpu as pltpu

F32 = jnp.float32
BF16 = jnp.bfloat16

HEAD_DIM = 64
A_HEADS, A_KV_HEADS = 8, 2
B_HEADS, B_KV_HEADS = 12, 2
C_HEADS = 12
B_GROUP = B_HEADS // B_KV_HEADS
A_WIDTH, B_WIDTH, C_WIDTH = A_HEADS * HEAD_DIM, B_HEADS * HEAD_DIM, C_HEADS * HEAD_DIM
KV_WIDTH = 2 * HEAD_DIM
N_GATE_COLS = B_HEADS * 3
WINDOW_A = 128
WINDOW_B = 512
CMP_LEN, CMP_STRIDE = 32, 16
SEL_BLOCK, N_SELECT = 64, 16
N_EXPERTS = 8
RMS_EPS = 1e-6
NEG_INF = -1e30
M_INIT = -1e20
FORCE_BONUS = 1e4
LANES = 128
VMEM_LIMIT = 56 * 1024 * 1024
Q_SCALE = HEAD_DIM ** -0.5

OFF_KC, OFF_KA, OFF_VA, OFF_QA = 0, 768, 896, 1024
OFF_KBC, OFF_VBC, OFF_KBS, OFF_KBW = 1536, 1664, 1792, 1920
ROW_WIDTH = 2048
OFF_QBT, OFF_QCT, OFF_VCT, OFF_VBST, OFF_VBWT = 0, 768, 1536, 2304, 2432
T_WIDTH = 2560
TAIL_FC = N_GATE_COLS
ONE_LANE = LANES - 1
N_AUG = HEAD_DIM
Q_AUG_ROWS = 16
LOG2E = 1.4426950408889634
MASK_PEN = -1e30
VMEM_LIMIT_BIG = 60 * 1024 * 1024
DMA_LOOP_UNROLL = 8


def _alibi(n):
    return [float(2.0 ** (-8.0 * i / n)) for i in range(1, n + 1)]


SLOPES_A = _alibi(A_HEADS)
SLOPES_B = _alibi(B_HEADS)


def _cparams(sem, vmem=VMEM_LIMIT):
    return pltpu.CompilerParams(dimension_semantics=sem, vmem_limit_bytes=vmem)


def _dot(a, b):
    return jnp.dot(a, b, preferred_element_type=F32)


def _dot_nt(a, b):
    return lax.dot_general(a, b, (((1,), (1,)), ((), ())), preferred_element_type=F32)


def _split3(x):
    hi = x.astype(BF16)
    r1 = x - hi.astype(F32)
    mid = r1.astype(BF16)
    lo = (r1 - mid.astype(F32)).astype(BF16)
    return hi, mid, lo


def _rms(x, gain):
    return x * lax.rsqrt(jnp.mean(x * x, axis=-1, keepdims=True) + RMS_EPS) * gain


def _sigmoid(x):
    return 1.0 / (1.0 + jnp.exp(-x))


def _norm_kernel(x_ref, g_ref, o_ref):
    o_ref[...] = _rms(x_ref[...], g_ref[...]).astype(o_ref.dtype)


def rmsnorm_rows(x, gain, out_dtype, tm=512):
    n, d = x.shape
    return pl.pallas_call(
        _norm_kernel,
        grid=(n // tm,),
        in_specs=[pl.BlockSpec((tm, d), lambda i: (i, 0)), pl.BlockSpec((1, d), lambda i: (0, 0))],
        out_specs=pl.BlockSpec((tm, d), lambda i: (i, 0)),
        out_shape=jax.ShapeDtypeStruct((n, d), out_dtype),
        compiler_params=_cparams(("parallel",)),
        name="rmsnorm",
    )(x, gain.reshape(1, d))


def _projection_kernel(x_ref, g_ref, wr_ref, wt_ref, wtail_ref, wtailt_ref, pr_ref, pt_ref, tail_ref, tailt_ref, h_sc,
                       *, n_row_tiles, n_t_tiles):
    j = pl.program_id(1)

    @pl.when(j == 0)
    def _():
        h_sc[...] = _rms(x_ref[...], g_ref[...]).astype(BF16)
        tail_ref[...] = _dot(h_sc[...], wtail_ref[...])
        tailt_ref[...] = _dot_nt(wtailt_ref[...], h_sc[...])

    @pl.when((j >= 1) & (j <= n_row_tiles))
    def _():
        pr_ref[...] = _dot(h_sc[...], wr_ref[...]).astype(pr_ref.dtype)

    @pl.when(j > n_row_tiles)
    def _():
        pt_ref[...] = _dot_nt(wt_ref[...], h_sc[...]).astype(pt_ref.dtype)


def input_projection(x, gain, w_rows, w_t, w_tail, tm=1024, tn=512):
    n, d = x.shape
    n_row_tiles, n_t_tiles = w_rows.shape[1] // tn, w_t.shape[0] // tn

    def row_tile(j):
        return jnp.clip(j - 1, 0, n_row_tiles - 1)

    def t_tile(j):
        return jnp.clip(j - 1 - n_row_tiles, 0, n_t_tiles - 1)

    return pl.pallas_call(
        functools.partial(_projection_kernel, n_row_tiles=n_row_tiles, n_t_tiles=n_t_tiles),
        grid=(n // tm, n_row_tiles + n_t_tiles + 1),
        in_specs=[pl.BlockSpec((tm, d), lambda i, j: (i, 0)),
                  pl.BlockSpec((1, d), lambda i, j: (0, 0)),
                  pl.BlockSpec((d, tn), lambda i, j: (0, row_tile(j))),
                  pl.BlockSpec((tn, d), lambda i, j: (t_tile(j), 0)),
                  pl.BlockSpec((d, LANES), lambda i, j: (0, 0)),
                  pl.BlockSpec((LANES, d), lambda i, j: (0, 0))],
        out_specs=[pl.BlockSpec((tm, tn), lambda i, j: (i, row_tile(j))),
                   pl.BlockSpec((tn, tm), lambda i, j: (t_tile(j), i)),
                   pl.BlockSpec((tm, LANES), lambda i, j: (i, 0)),
                   pl.BlockSpec((LANES, tm), lambda i, j: (0, i))],
        out_shape=[jax.ShapeDtypeStruct((n, w_rows.shape[1]), BF16),
                   jax.ShapeDtypeStruct((w_t.shape[0], n), BF16),
                   jax.ShapeDtypeStruct((n, LANES), F32),
                   jax.ShapeDtypeStruct((LANES, n), F32)],
        scratch_shapes=[pltpu.VMEM((tm, d), BF16)],
        compiler_params=_cparams(("parallel", "arbitrary")),
        name="input_projection",
    )(x, gain.reshape(1, d), w_rows, w_t, w_tail, w_tail.T)


def _swa_kernel(sink_ref, q_ref, kp_ref, kc_ref, vp_ref, vc_ref, o_ref):
    n = pl.program_id(1)
    tq = q_ref.shape[0]
    q = q_ref[...]
    k = jnp.concatenate([kp_ref[...], kc_ref[...]], axis=0)
    v = jnp.concatenate([vp_ref[...], vc_ref[...]], axis=0)
    i = lax.broadcasted_iota(jnp.int32, (tq, 2 * tq), 0)
    j = lax.broadcasted_iota(jnp.int32, (tq, 2 * tq), 1)
    rel = tq + i - j
    mask = (rel >= 0) & (rel < WINDOW_A) & ((j >= tq) | (n > 0))
    relf = rel.astype(F32)
    group = A_HEADS // A_KV_HEADS
    for h in range(A_HEADS):
        kv = h // group
        qh = q[:, h * HEAD_DIM:(h + 1) * HEAD_DIM]
        kh = k[:, kv * HEAD_DIM:(kv + 1) * HEAD_DIM]
        vh = v[:, kv * HEAD_DIM:(kv + 1) * HEAD_DIM]
        s = _dot_nt(qh, kh) * Q_SCALE - SLOPES_A[h] * relf
        s = jnp.where(mask, s, NEG_INF)
        sink = sink_ref[h]
        m = jnp.maximum(jnp.max(s, axis=-1, keepdims=True), sink)
        p = jnp.exp(s - m)
        l = jnp.sum(p, axis=-1, keepdims=True) + jnp.exp(sink - m)
        o_ref[:, h * HEAD_DIM:(h + 1) * HEAD_DIM] = _dot(p.astype(BF16), vh) / l


def swa_attention(proj, sinks, batch, seq):
    tq = WINDOW_A
    nb = seq // tq
    kcol, vcol = OFF_KA // KV_WIDTH, OFF_VA // KV_WIDTH

    def cur(col):
        return pl.BlockSpec((tq, KV_WIDTH), lambda b, n: (b * nb + n, col))

    def prev(col):
        return pl.BlockSpec((tq, KV_WIDTH), lambda b, n: (b * nb + jnp.maximum(n - 1, 0), col))

    return pl.pallas_call(
        _swa_kernel,
        grid=(batch, nb),
        in_specs=[pl.BlockSpec(memory_space=pltpu.SMEM),
                  pl.BlockSpec((tq, A_WIDTH), lambda b, n: (b * nb + n, OFF_QA // A_WIDTH)),
                  prev(kcol), cur(kcol), prev(vcol), cur(vcol)],
        out_specs=pl.BlockSpec((tq, A_WIDTH), lambda b, n: (b * nb + n, 0)),
        out_shape=jax.ShapeDtypeStruct((batch * seq, A_WIDTH), F32),
        compiler_params=_cparams(("parallel", "parallel")),
        name="swa_attention",
    )(sinks, proj, proj, proj, proj, proj)


def _fox_aug_kernel(z_ref, b_ref, pk_ref, pq_ref, ka_ref, qa_ref, carry_sc):
    @pl.when(pl.program_id(1) == 0)
    def _():
        carry_sc[...] = jnp.zeros_like(carry_sc)

    z = z_ref[...] + b_ref[...]
    log_f = -(jnp.maximum(-z, 0.0) + jnp.log1p(jnp.exp(-jnp.abs(z))))
    ts = z.shape[0]
    r = lax.broadcasted_iota(jnp.int32, (ts, ts), 0)
    c = lax.broadcasted_iota(jnp.int32, (ts, ts), 1)
    tri = (c <= r).astype(BF16)
    hi, mid, lo = _split3(log_f)
    cum = _dot(tri, hi) + _dot(tri, mid) + _dot(tri, lo) + carry_sc[0:1, :]
    carry_sc[0:1, :] = cum[ts - 1:ts, :]
    hi, mid, lo = _split3(cum * LOG2E)
    lane = lax.broadcasted_iota(jnp.int32, (ts, LANES), 1)
    hi = jnp.where(lane == ONE_LANE, 1.0, hi).astype(BF16)
    ka_ref[...] = (_dot(hi, pk_ref[0]) + _dot(mid, pk_ref[1]) + _dot(lo, pk_ref[2])).astype(BF16)
    qa_ref[...] = (_dot_nt(pq_ref[0], hi) + _dot_nt(pq_ref[1], mid) + _dot_nt(pq_ref[2], lo)).astype(BF16)


def _aug_lane(h):
    return HEAD_DIM if h % 2 == 0 else 0


def _fox_placement():
    pk = np.zeros((3, LANES, C_HEADS * LANES), np.float32)
    pq = np.zeros((3, C_HEADS * Q_AUG_ROWS, LANES), np.float32)
    for h in range(C_HEADS):
        src, kbase, qbase = TAIL_FC + h, h * LANES + _aug_lane(h), h * Q_AUG_ROWS
        for piece in range(3):
            pk[piece, src, kbase + piece] = -1.0
            pq[piece, qbase + 3 + piece, src] = 1.0
            pk[0, ONE_LANE, kbase + 3 + piece] = 1.0
            pq[0, qbase + piece, ONE_LANE] = 1.0
    return jnp.asarray(pk, BF16), jnp.asarray(pq, BF16)


def fox_augmentation(tail, bias_row, batch, seq, ts=1024):
    nt = seq // ts
    pk, pq = _fox_placement()
    kw, qw = C_HEADS * LANES, C_HEADS * Q_AUG_ROWS
    return pl.pallas_call(
        _fox_aug_kernel,
        grid=(batch, nt),
        in_specs=[pl.BlockSpec((ts, LANES), lambda b, i: (b * nt + i, 0)),
                  pl.BlockSpec((1, LANES), lambda b, i: (0, 0)),
                  pl.BlockSpec((3, LANES, kw), lambda b, i: (0, 0, 0)),
                  pl.BlockSpec((3, qw, LANES), lambda b, i: (0, 0, 0))],
        out_specs=[pl.BlockSpec((ts, kw), lambda b, i: (b * nt + i, 0)),
                   pl.BlockSpec((qw, ts), lambda b, i: (0, b * nt + i))],
        out_shape=[jax.ShapeDtypeStruct((batch * seq, kw), BF16),
                   jax.ShapeDtypeStruct((qw, batch * seq), BF16)],
        scratch_shapes=[pltpu.VMEM((8, LANES), F32)],
        compiler_params=_cparams(("parallel", "arbitrary")),
        name="fox_augmentation",
    )(tail, bias_row, pk, pq)


def _alibi_augmentation(seq, tk):
    pos = np.arange(seq)
    ka = np.zeros((seq, LANES), np.float32)
    for base in (0, HEAD_DIM):
        ka[:, base:base + 3] = 1.0
        ka[:, base + 3:base + 6] = ((pos >> 8) << 8)[:, None]
        ka[:, base + 6:base + 9] = (pos & 255)[:, None]
        ka[pos, base + Q_AUG_ROWS + (pos % tk) // SEL_BLOCK] = 1.0
    def split3_np(x):
        pieces, rest = [], x.astype(np.float32)
        for _ in range(3):
            piece = rest.astype(jnp.bfloat16).astype(np.float32)
            pieces.append(piece)
            rest = rest - piece
        return pieces

    slopes = (np.asarray(SLOPES_B, np.float32) * np.float32(LOG2E)).astype(np.float32)
    st = split3_np(-(slopes[:, None] * pos.astype(np.float32)[None, :]))
    sl = split3_np(slopes)
    qa = np.zeros((B_HEADS, Q_AUG_ROWS, seq), np.float32)
    for piece in range(3):
        qa[:, piece, :] = st[piece]
        qa[:, 3 + piece, :] = sl[piece][:, None]
        qa[:, 6 + piece, :] = sl[piece][:, None]
    return jnp.asarray(ka, BF16), jnp.asarray(qa.reshape(B_HEADS * Q_AUG_ROWS, seq), BF16)


def _flash_t_kernel(*refs, mode, tq, tk, n_steps):
    pen_sc = None
    if mode == "slc":
        (live_ref, k_ref, ka_ref, qt_ref, qat_ref, vt_ref, gate_ref, selt_ref, o_ref,
         m_sc, l_sc, acc_sc, pen_sc) = refs
    elif mode == "win":
        k_ref, ka_ref, qt_ref, qat_ref, vt_ref, gate_ref, o_ref, m_sc, l_sc, acc_sc = refs
    else:
        k_ref, ka_ref, qt_ref, qat_ref, vt_ref, o_ref, m_sc, l_sc, acc_sc = refs
    n_heads = qt_ref.shape[0] // HEAD_DIM
    group = 1 if mode == "fox" else B_GROUP
    blocks_per_tile = tk // SEL_BLOCK
    iq, j = pl.program_id(1), pl.program_id(2)
    if mode == "win":
        ik = iq - (n_steps - 1) + j
        last_j = n_steps - 1
        active = ik >= 0
    else:
        ik = j
        last_j = (iq * tq + tq - 1) // tk
        active = j <= last_j

    @pl.when(j == 0)
    def _():
        m_sc[...] = jnp.full_like(m_sc, M_INIT)
        l_sc[...] = jnp.zeros_like(l_sc)
        acc_sc[...] = jnp.zeros_like(acc_sc)
        if mode == "slc":
            pen_sc[...] = (1.0 - selt_ref[0].astype(F32)) * MASK_PEN

    def step(masked, heads):
        ok = None
        if masked:
            rel = ((iq * tq + lax.broadcasted_iota(jnp.int32, (tk, tq), 1))
                   - (ik * tk + lax.broadcasted_iota(jnp.int32, (tk, tq), 0)))
            ok = rel >= 0
            if mode == "win":
                ok = ok & (rel < WINDOW_B)
        lane = lax.broadcasted_iota(jnp.int32, (tk, LANES), 1)
        ones_rows = jnp.ones((Q_AUG_ROWS, tk), BF16)
        per_kv = {}

        def kv_operands(kv):
            if kv not in per_kv:
                pair = slice((kv // 2) * LANES, (kv // 2 + 1) * LANES)
                aug = ka_ref[:, kv * LANES:(kv + 1) * LANES] if mode == "fox" else ka_ref[...]
                own = (lane < HEAD_DIM) if kv % 2 == 0 else (lane >= HEAD_DIM)
                ka = jnp.where(own, k_ref[:, pair], aug)
                v_aug = jnp.concatenate([vt_ref[kv * HEAD_DIM:(kv + 1) * HEAD_DIM, :], ones_rows], axis=0)
                pen = None
                if mode == "slc":
                    pen8 = pen_sc[kv, pl.ds(pl.multiple_of(ik * blocks_per_tile, blocks_per_tile), blocks_per_tile), :]
                    pen = jnp.concatenate(
                        [pen8, jnp.zeros((Q_AUG_ROWS - blocks_per_tile, tq), F32)], axis=0).astype(BF16)
                per_kv[kv] = (ka, v_aug, pen)
            return per_kv[kv]

        def scores(h):
            kv = h // group
            ka, _, pen = kv_operands(kv)
            aug_rows = [qat_ref[h * Q_AUG_ROWS:(h + 1) * Q_AUG_ROWS, :]]
            if mode == "slc":
                aug_rows.append(pen)
            aug_rows.append(jnp.zeros((N_AUG - Q_AUG_ROWS * len(aug_rows), tq), BF16))
            halves = [[qt_ref[h * HEAD_DIM:(h + 1) * HEAD_DIM, :]], aug_rows]
            qa = jnp.concatenate(sum(halves if kv % 2 == 0 else halves[::-1], []), axis=0)
            s = _dot(ka, qa)
            return s if ok is None else jnp.where(ok, s, NEG_INF)

        def probabilities(h, s):
            m_prev = m_sc[h:h + 1, :]
            m_new = jnp.maximum(m_prev, jnp.max(s, axis=0, keepdims=True))
            m_sc[h:h + 1, :] = m_new
            return jnp.exp2(s - m_new).astype(BF16), jnp.exp2(m_prev - m_new)

        def accumulate(h, p, alpha):
            rows = slice(h * HEAD_DIM, (h + 1) * HEAD_DIM)
            pv = _dot(kv_operands(h // group)[1], p)
            l_sc[h:h + 1, :] = alpha * l_sc[h:h + 1, :] + pv[HEAD_DIM:HEAD_DIM + 1, :]
            acc_sc[rows, :] = alpha * acc_sc[rows, :] + pv[0:HEAD_DIM, :]

        s_cur, pending = scores(heads[0]), None
        for i, h in enumerate(heads):
            s_next = scores(heads[i + 1]) if i + 1 < len(heads) else None
            p_alpha = probabilities(h, s_cur)
            if pending is not None:
                accumulate(*pending)
            pending = (h,) + p_alpha
            s_cur = s_next
        accumulate(*pending)

    all_heads = list(range(n_heads))
    on_diagonal = ik * tk + tk - 1 > iq * tq
    if mode == "win":
        pl.when(active)(lambda: step(True, all_heads))
    elif mode == "fox":
        pl.when(active & on_diagonal)(lambda: step(True, all_heads))
        pl.when(active & jnp.logical_not(on_diagonal))(lambda: step(False, all_heads))
    else:
        tile = (pl.program_id(0) * pl.num_programs(1) + iq) * n_steps + ik
        for kv in range(B_KV_HEADS):
            heads = all_heads[kv * group:(kv + 1) * group]
            live = active & (live_ref[tile * B_KV_HEADS + kv] > 0)
            pl.when(live & on_diagonal)(functools.partial(step, True, heads))
            pl.when(live & jnp.logical_not(on_diagonal))(functools.partial(step, False, heads))

    @pl.when(j == last_j)
    def _():
        branch = {"fox": None, "slc": 1, "win": 2}[mode]
        for pair in range(n_heads // 2):
            rows = slice(pair * LANES, (pair + 1) * LANES)
            heads = (2 * pair, 2 * pair + 1)
            denom = jnp.concatenate(
                [jnp.broadcast_to(l_sc[h:h + 1, :], (HEAD_DIM, tq)) for h in heads], axis=0)
            out = acc_sc[rows, :] / denom
            if branch is not None:
                out = out * jnp.concatenate(
                    [jnp.broadcast_to(_sigmoid(gate_ref[3 * h + branch:3 * h + branch + 1, :]), (HEAD_DIM, tq))
                     for h in heads], axis=0)
            o_ref[:, rows] = out.T


def flash_attention_t(mode, proj_r, proj_t, k_aug, q_aug_t, batch, seq, gates_t=None, sel_t=None, tq=512, tk=512):
    nq, nk = seq // tq, seq // tk
    if mode == "win":
        assert tq == tk
        n_steps = WINDOW_B // tk + 1

        def kblock(iq, j):
            return jnp.maximum(iq - (n_steps - 1) + j, 0)
    else:
        n_steps = nk

        def kblock(iq, j):
            return jnp.minimum(j, (iq * tq + tq - 1) // tk)

    if mode == "fox":
        in_specs = [pl.BlockSpec((tk, C_WIDTH), lambda b, iq, j, *_: (b * nk + kblock(iq, j), OFF_KC // C_WIDTH)),
                    pl.BlockSpec((tk, C_HEADS * LANES), lambda b, iq, j, *_: (b * nk + kblock(iq, j), 0)),
                    pl.BlockSpec((C_WIDTH, tq), lambda b, iq, j, *_: (OFF_QCT // C_WIDTH, b * nq + iq)),
                    pl.BlockSpec((C_HEADS * Q_AUG_ROWS, tq), lambda b, iq, j, *_: (0, b * nq + iq)),
                    pl.BlockSpec((C_WIDTH, tk), lambda b, iq, j, *_: (OFF_VCT // C_WIDTH, b * nk + kblock(iq, j)))]
    else:
        kcol = (OFF_KBS if mode == "slc" else OFF_KBW) // KV_WIDTH
        vrow = (OFF_VBST if mode == "slc" else OFF_VBWT) // KV_WIDTH
        in_specs = [pl.BlockSpec((tk, KV_WIDTH), lambda b, iq, j, *_: (b * nk + kblock(iq, j), kcol)),
                    pl.BlockSpec((tk, LANES), lambda b, iq, j, *_: (kblock(iq, j), 0)),
                    pl.BlockSpec((B_WIDTH, tq), lambda b, iq, j, *_: (OFF_QBT // B_WIDTH, b * nq + iq)),
                    pl.BlockSpec((B_HEADS * Q_AUG_ROWS, tq), lambda b, iq, j, *_: (0, iq)),
                    pl.BlockSpec((KV_WIDTH, tk), lambda b, iq, j, *_: (vrow, b * nk + kblock(iq, j)))]
    args = [proj_r, k_aug, proj_t, q_aug_t, proj_t]
    prefetch = []
    width = C_WIDTH if mode == "fox" else B_WIDTH
    n_heads = width // HEAD_DIM
    scratch = [pltpu.VMEM((16, tq), F32), pltpu.VMEM((16, tq), F32), pltpu.VMEM((n_heads * HEAD_DIM, tq), F32)]
    if mode != "fox":
        in_specs.append(pl.BlockSpec((LANES, tq), lambda b, iq, j, *_: (0, b * nq + iq)))
        args.append(gates_t)
    if mode == "slc":
        n_slc = seq // SEL_BLOCK
        per_tile = tk // SEL_BLOCK
        assert per_tile <= Q_AUG_ROWS and tk % SEL_BLOCK == 0
        in_specs.append(pl.BlockSpec((1, B_KV_HEADS, n_slc, tq), lambda b, iq, j, *_: (b, 0, 0, iq)))
        args.append(sel_t)
        scratch.append(pltpu.VMEM((B_KV_HEADS, n_slc, tq), F32))
        live = sel_t.reshape(batch, B_KV_HEADS, nk, per_tile, nq, tq).max(axis=(3, 5)) > 0
        prefetch.append(live.transpose(0, 3, 2, 1).reshape(-1).astype(jnp.int32))
    return pl.pallas_call(
        functools.partial(_flash_t_kernel, mode=mode, tq=tq, tk=tk, n_steps=n_steps),
        grid_spec=pltpu.PrefetchScalarGridSpec(
            num_scalar_prefetch=len(prefetch),
            grid=(batch, nq, n_steps),
            in_specs=in_specs,
            out_specs=pl.BlockSpec((tq, width), lambda b, iq, j, *_: (b * nq + iq, 0)),
            scratch_shapes=scratch),
        out_shape=jax.ShapeDtypeStruct((batch * seq, width), F32),
        compiler_params=_cparams(("parallel", "parallel", "arbitrary")),
        name=mode + "_attention",
    )(*prefetch, *args)


def _compress_kernel(t_ref, pe_ref, w1_ref, w2_ref, w2t_ref, o_ref, ot_ref):
    w1 = w1_ref[0]
    hid = _dot(t_ref[0, 0, 0], w1) + _dot(pe_ref[0], w1)[0:1, :]
    act = (hid * _sigmoid(hid)).astype(BF16)
    o_ref[0, 0, 0] = _dot(act, w2_ref[0]).astype(o_ref.dtype)
    ot_ref[0, 0, 0] = _dot_nt(w2t_ref[0], act).astype(ot_ref.dtype)


def nsa_compress(flat, pe_rows, w1, w2):
    _, batch, n_kv, n_chunks, width = flat.shape
    hidden = w1.shape[-1]
    return pl.pallas_call(
        _compress_kernel,
        grid=(2, batch, n_kv),
        in_specs=[pl.BlockSpec((1, 1, 1, n_chunks, width), lambda s, b, h: (s, b, h, 0, 0)),
                  pl.BlockSpec((1, 8, width), lambda s, b, h: (s, 0, 0)),
                  pl.BlockSpec((1, width, hidden), lambda s, b, h: (s, 0, 0)),
                  pl.BlockSpec((1, hidden, HEAD_DIM), lambda s, b, h: (s, 0, 0)),
                  pl.BlockSpec((1, HEAD_DIM, hidden), lambda s, b, h: (s, 0, 0))],
        out_specs=[pl.BlockSpec((1, 1, 1, n_chunks, HEAD_DIM), lambda s, b, h: (s, b, h, 0, 0)),
                   pl.BlockSpec((1, 1, 1, HEAD_DIM, n_chunks), lambda s, b, h: (s, b, h, 0, 0))],
        out_shape=[jax.ShapeDtypeStruct((2, batch, n_kv, n_chunks, HEAD_DIM), BF16),
                   jax.ShapeDtypeStruct((2, batch, n_kv, HEAD_DIM, n_chunks), BF16)],
        compiler_params=_cparams(("parallel", "parallel", "parallel")),
        name="nsa_compress",
    )(flat, pe_rows, w1, w2, jnp.swapaxes(w2, 1, 2))


def _cmp_attn_t_kernel(qt_ref, kc_ref, vct_ref, inter_ref, gate_ref, o_ref, selt_ref, ot_sc, *, tq):
    iq = pl.program_id(1)
    n_chunks = kc_ref.shape[3]
    n_slc = inter_ref.shape[0]
    t = iq * tq + lax.broadcasted_iota(jnp.int32, (n_chunks, tq), 1)
    n = lax.broadcasted_iota(jnp.int32, (n_chunks, tq), 0)
    rel = t - (n * CMP_STRIDE + CMP_LEN - 1)
    mask = rel >= 0
    relf = rel.astype(F32)
    t_s = iq * tq + lax.broadcasted_iota(jnp.int32, (n_slc, tq), 1)
    jj = lax.broadcasted_iota(jnp.int32, (n_slc, tq), 0)
    cur = t_s >> (SEL_BLOCK.bit_length() - 1)
    valid = jj * SEL_BLOCK <= t_s
    forced = (jj == 0) | (jj == cur) | (jj == cur - 1)
    jf = jj.astype(F32)
    inter = inter_ref[...]
    def raw_scores(hd):
        return _dot(kc_ref[0, 0, hd // B_GROUP], qt_ref[hd * HEAD_DIM:(hd + 1) * HEAD_DIM, :])

    raw_next = raw_scores(0)
    for h in range(B_KV_HEADS):
        vct = vct_ref[0, 0, h]
        p_sum = jnp.zeros((n_chunks, tq), F32)
        for g in range(B_GROUP):
            hd = h * B_GROUP + g
            rows = slice(hd * HEAD_DIM, (hd + 1) * HEAD_DIM)
            raw, raw_next = raw_next, (raw_scores(hd + 1) if hd + 1 < B_HEADS else None)
            s = jnp.where(mask, raw - (SLOPES_B[hd] * LOG2E) * relf, NEG_INF)
            m = jnp.max(s, axis=0, keepdims=True)
            e = jnp.where(mask, jnp.exp2(s - m), 0.0)
            l = jnp.sum(e, axis=0, keepdims=True)
            p = e / jnp.where(l > 0.0, l, 1.0)
            p_sum = p_sum + p
            ot_sc[rows, :] = _dot(vct, p.astype(BF16)) * _sigmoid(gate_ref[3 * hd:3 * hd + 1, :])
        hi, mid, lo = _split3(p_sum)
        importance = _dot(inter, hi) + _dot(inter, mid) + _dot(inter, lo)
        score = jnp.where(valid, importance + jnp.where(forced, FORCE_BONUS, 0.0), NEG_INF)
        chosen = jnp.zeros((n_slc, tq), F32)
        for _ in range(min(N_SELECT, n_slc)):
            best = jnp.max(score, axis=0, keepdims=True)
            first = jnp.min(jnp.where(score == best, jf, float(n_slc)), axis=0, keepdims=True)
            pick = jf == first
            chosen = jnp.where(pick, 1.0, chosen)
            score = jnp.where(pick, -jnp.inf, score)
        selt_ref[0, h] = jnp.where(valid, chosen, 0.0).astype(selt_ref.dtype)
    for pair in range(B_HEADS // 2):
        rows = slice(pair * LANES, (pair + 1) * LANES)
        o_ref[:, rows] = ot_sc[rows, :].T


def nsa_compressed_attention(proj_t, kvc, kvct, inter_t, gates_t, batch, seq, tq=512):
    nq = seq // tq
    n_chunks = kvc.shape[3]
    n_slc = seq // SEL_BLOCK
    return pl.pallas_call(
        functools.partial(_cmp_attn_t_kernel, tq=tq),
        grid=(batch, nq),
        in_specs=[pl.BlockSpec((B_WIDTH, tq), lambda b, i: (OFF_QBT // B_WIDTH, b * nq + i)),
                  pl.BlockSpec((1, 1, B_KV_HEADS, n_chunks, HEAD_DIM), lambda b, i: (0, b, 0, 0, 0)),
                  pl.BlockSpec((1, 1, B_KV_HEADS, HEAD_DIM, n_chunks), lambda b, i: (1, b, 0, 0, 0)),
                  pl.BlockSpec((n_slc, n_chunks), lambda b, i: (0, 0)),
                  pl.BlockSpec((LANES, tq), lambda b, i: (0, b * nq + i))],
        out_specs=[pl.BlockSpec((tq, B_WIDTH), lambda b, i: (b * nq + i, 0)),
                   pl.BlockSpec((1, B_KV_HEADS, n_slc, tq), lambda b, i: (b, 0, 0, i))],
        out_shape=[jax.ShapeDtypeStruct((batch * seq, B_WIDTH), F32),
                   jax.ShapeDtypeStruct((batch, B_KV_HEADS, n_slc, seq), BF16)],
        scratch_shapes=[pltpu.VMEM((B_WIDTH, tq), F32)],
        compiler_params=_cparams(("parallel", "parallel")),
        name="nsa_cmp_attention",
    )(proj_t, kvc, kvct, inter_t, gates_t)


def _mix_out_kernel(oa_ref, ocmp_ref, oslc_ref, owin_ref, oc_ref, g_ref, w_ref, x_ref, gn_ref, o_ref, h_ref):
    o_b = ocmp_ref[...] + oslc_ref[...] + owin_ref[...]
    b0, c0 = A_WIDTH, A_WIDTH + B_WIDTH
    mixed = jnp.concatenate([_rms(oa_ref[...], g_ref[:, 0:b0]).astype(BF16),
                             _rms(o_b, g_ref[:, b0:c0]).astype(BF16),
                             _rms(oc_ref[...], g_ref[:, c0:]).astype(BF16)], axis=1)
    x_new = x_ref[...] + _dot(mixed, w_ref[...])
    o_ref[...] = x_new
    h_ref[...] = _rms(x_new, gn_ref[...]).astype(h_ref.dtype)


def mix_out(o_a, o_cmp, o_slc, o_win, o_c, gain, w_out, x, next_gain, tm=256):
    n, d = x.shape
    width = w_out.shape[0]

    def rows(w):
        return pl.BlockSpec((tm, w), lambda i: (i, 0))

    return pl.pallas_call(
        _mix_out_kernel,
        grid=(n // tm,),
        in_specs=[rows(A_WIDTH), rows(B_WIDTH), rows(B_WIDTH), rows(B_WIDTH), rows(C_WIDTH),
                  pl.BlockSpec((1, width), lambda i: (0, 0)),
                  pl.BlockSpec((width, d), lambda i: (0, 0)),
                  rows(d),
                  pl.BlockSpec((1, d), lambda i: (0, 0))],
        out_specs=[rows(d), rows(d)],
        out_shape=[jax.ShapeDtypeStruct((n, d), F32), jax.ShapeDtypeStruct((n, d), BF16)],
        compiler_params=_cparams(("parallel",)),
        name="mix_out",
    )(o_a, o_cmp, o_slc, o_win, o_c, gain.reshape(1, width), w_out, x, next_gain.reshape(1, d))


PART_FULL, PART_FIRST, PART_SECOND, PART_EMPTY, PART_IDLE = range(5)


def _fresh_weights(exp_ref):
    t = pl.program_id(1)
    return (t == 0) | (exp_ref[t] != exp_ref[jnp.maximum(t - 1, 0)])


def _for_each_part(part, o_ref, compute):
    half = o_ref.shape[0] // 2

    @pl.when(part == PART_FULL)
    def _():
        o_ref[...] = compute(slice(None))

    @pl.when(part == PART_FIRST)
    def _():
        o_ref[0:half, :] = compute(slice(0, half))
        o_ref[half:, :] = jnp.zeros((half, o_ref.shape[1]), o_ref.dtype)

    @pl.when(part == PART_SECOND)
    def _():
        o_ref[half:, :] = compute(slice(half, 2 * half))

    @pl.when(part == PART_EMPTY)
    def _():
        o_ref[...] = jnp.zeros_like(o_ref)


def _ffn_up_kernel(blk_ref, exp_ref, part_ref, x_ref, wg_ref, wu_ref, o_ref, wg_sc, wu_sc):
    @pl.when(_fresh_weights(exp_ref))
    def _():
        wg_sc[...] = wg_ref[0].astype(BF16)
        wu_sc[...] = wu_ref[0].astype(BF16)

    def compute(rows):
        if x_ref.dtype == jnp.int32:
            c = x_ref.shape[1]
            x_lo, x_hi = _unpack_bf16_pairs(x_ref[rows, :])
            gate = _dot(x_lo, wg_sc[0:c, :]) + _dot(x_hi, wg_sc[c:, :])
            up = _dot(x_lo, wu_sc[0:c, :]) + _dot(x_hi, wu_sc[c:, :])
        else:
            x = x_ref[rows, :]
            gate = _dot(x, wg_sc[...])
            up = _dot(x, wu_sc[...])
        return (gate * _sigmoid(gate) * up).astype(o_ref.dtype)

    _for_each_part(part_ref[pl.program_id(1)], o_ref, compute)


def _ffn_down_kernel(blk_ref, exp_ref, part_ref, h_ref, wd_ref, *rest):
    o_ref, wd_sc = rest[-2], rest[-1]

    @pl.when(_fresh_weights(exp_ref))
    def _():
        wd_sc[...] = wd_ref[0].astype(BF16)

    def compute(rows):
        y = _dot(h_ref[rows, :], wd_sc[...])
        return y + rest[0][rows, :] if len(rest) == 3 else y

    _for_each_part(part_ref[pl.program_id(1)], o_ref, compute)


def _whole_block_items(n_blocks, expert):
    return (jnp.arange(n_blocks, dtype=jnp.int32), jnp.full((n_blocks,), expert, jnp.int32),
            jnp.full((n_blocks,), PART_FULL, jnp.int32))


def grouped_swiglu(xs, items_up, items_down, w_gate, w_up, w_down, residual=None, tm=512, tf=512, tm_down=512,
                   tn=512):
    rows, x_width = xs.shape
    _, d, f = w_gate.shape

    hidden = pl.pallas_call(
        _ffn_up_kernel,
        grid_spec=pltpu.PrefetchScalarGridSpec(
            num_scalar_prefetch=3,
            grid=(f // tf, items_up[0].shape[0]),
            in_specs=[pl.BlockSpec((tm, x_width), lambda j, t, blk, ex, part: (blk[t], 0)),
                      pl.BlockSpec((1, d, tf), lambda j, t, blk, ex, part: (ex[t], 0, j)),
                      pl.BlockSpec((1, d, tf), lambda j, t, blk, ex, part: (ex[t], 0, j))],
            out_specs=pl.BlockSpec((tm, tf), lambda j, t, blk, ex, part: (blk[t], j)),
            scratch_shapes=[pltpu.VMEM((d, tf), BF16), pltpu.VMEM((d, tf), BF16)]),
        out_shape=jax.ShapeDtypeStruct((rows, f), BF16),
        compiler_params=_cparams(("parallel", "arbitrary"), VMEM_LIMIT_BIG),
        name="ffn_up",
    )(*items_up, xs, w_gate, w_up)

    in_specs = [pl.BlockSpec((tm_down, f), lambda j, t, blk, ex, part: (blk[t], 0)),
                pl.BlockSpec((1, f, tn), lambda j, t, blk, ex, part: (ex[t], 0, j))]
    args = [hidden, w_down]
    if residual is not None:
        in_specs.append(pl.BlockSpec((tm_down, tn), lambda j, t, blk, ex, part: (blk[t], j)))
        args.append(residual)
    return pl.pallas_call(
        _ffn_down_kernel,
        grid_spec=pltpu.PrefetchScalarGridSpec(
            num_scalar_prefetch=3,
            grid=(d // tn, items_down[0].shape[0]),
            in_specs=in_specs,
            out_specs=pl.BlockSpec((tm_down, tn), lambda j, t, blk, ex, part: (blk[t], j)),
            scratch_shapes=[pltpu.VMEM((f, tn), BF16)]),
        out_shape=jax.ShapeDtypeStruct((rows, d), F32),
        compiler_params=_cparams(("parallel", "arbitrary"), VMEM_LIMIT_BIG),
        name="ffn_down",
    )(*items_down, *args)


def _router_kernel(x_ref, g_ref, wr_ref, h_ref, ri_ref, rf_ref, cnt_ref, carry_sc):
    @pl.when(pl.program_id(0) == 0)
    def _():
        carry_sc[...] = jnp.zeros_like(carry_sc)

    tm = x_ref.shape[0]
    h = _rms(x_ref[...], g_ref[...])
    h_ref[...] = _pack_bf16_pairs(h)
    h1, h2, _ = _split3(h)
    w1, w2 = wr_ref[0], wr_ref[1]
    logits = _dot(h1, w1) + (_dot(h1, w2) + _dot(h2, w1))
    lane_i = lax.broadcasted_iota(jnp.int32, (tm, LANES), 1)
    lane = lane_i.astype(F32)
    logits = jnp.where(lane_i < N_EXPERTS, logits, -jnp.inf)
    v1 = jnp.max(logits, axis=-1, keepdims=True)
    e1 = jnp.min(jnp.where(logits == v1, lane, float(LANES)), axis=-1, keepdims=True)
    rest = jnp.where(lane == e1, -jnp.inf, logits)
    v2 = jnp.max(rest, axis=-1, keepdims=True)
    e2 = jnp.min(jnp.where(rest == v2, lane, float(LANES)), axis=-1, keepdims=True)
    z = jnp.exp(v2 - v1)
    g1 = 1.0 / (1.0 + z)
    g2 = z / (1.0 + z)
    chosen = (lane == e1) | (lane == e2)
    onehot = jnp.where(chosen, 1.0, 0.0)
    r = lax.broadcasted_iota(jnp.int32, (tm, tm), 0)
    c = lax.broadcasted_iota(jnp.int32, (tm, tm), 1)
    before = _dot((c < r).astype(BF16), onehot.astype(BF16)) + carry_sc[0:1, :]
    r1 = jnp.sum(jnp.where(lane == e1, before, 0.0), axis=-1, keepdims=True)
    r2 = jnp.sum(jnp.where(lane == e2, before, 0.0), axis=-1, keepdims=True)
    carry = carry_sc[0:1, :] + jnp.sum(onehot, axis=0, keepdims=True)
    carry_sc[0:1, :] = carry
    cnt_ref[...] = jnp.broadcast_to(carry, cnt_ref.shape)
    packed = jnp.where(lane_i == 0, e1, jnp.where(lane_i == 1, e2, jnp.where(
        lane_i == 2, r1, jnp.where(lane_i == 3, r2, 0.0))))
    ri_ref[...] = packed.astype(jnp.int32)
    rf_ref[...] = jnp.where(lane_i == 0, g1, jnp.where(lane_i == 1, g2, 0.0))


def _pack_bf16_pairs(h):
    bits = lax.bitcast_convert_type(h, jnp.int32)
    rounded = bits + 0x7FFF + (lax.shift_right_logical(bits, 16) & 1)
    c = h.shape[1] // 2
    return lax.shift_right_logical(rounded[:, :c], 16) | (rounded[:, c:] & -65536)


def _unpack_bf16_pairs(words):
    lo = lax.bitcast_convert_type(lax.shift_left(words, 16), F32)
    hi = lax.bitcast_convert_type(words & -65536, F32)
    return lo.astype(BF16), hi.astype(BF16)


def moe_route(x, gain, w_router3, tm=512):
    n, d = x.shape
    return pl.pallas_call(
        _router_kernel,
        grid=(n // tm,),
        in_specs=[pl.BlockSpec((tm, d), lambda i: (i, 0)),
                  pl.BlockSpec((1, d), lambda i: (0, 0)),
                  pl.BlockSpec((2, d, LANES), lambda i: (0, 0, 0))],
        out_specs=[pl.BlockSpec((tm, d // 2), lambda i: (i, 0)),
                   pl.BlockSpec((tm, LANES), lambda i: (i, 0)),
                   pl.BlockSpec((tm, LANES), lambda i: (i, 0)),
                   pl.BlockSpec((8, LANES), lambda i: (0, 0))],
        out_shape=[jax.ShapeDtypeStruct((n, d // 2), jnp.int32),
                   jax.ShapeDtypeStruct((n, LANES), jnp.int32),
                   jax.ShapeDtypeStruct((n, LANES), F32),
                   jax.ShapeDtypeStruct((8, LANES), F32)],
        scratch_shapes=[pltpu.VMEM((8, LANES), F32)],
        compiler_params=_cparams(("arbitrary",)),
        name="moe_route",
    )(x, gain.reshape(1, d), w_router3)


def _row_copy(src_ref, src_row, dst_ref, dst_row, sem):
    return pltpu.make_async_copy(src_ref.at[pl.ds(src_row, 1)], dst_ref.at[pl.ds(dst_row, 1)], sem)


def _dispatch_kernel(dest_ref, h_ref, zeros_ref, xs_ref, sem):
    del zeros_ref
    td = h_ref.shape[0]
    base = pl.program_id(0) * td

    def start(r, carry):
        for k in range(2):
            _row_copy(h_ref, r, xs_ref, dest_ref[2 * (base + r) + k], sem).start()
        return carry

    def wait(r, carry):
        for k in range(2):
            _row_copy(h_ref, r, xs_ref, dest_ref[2 * (base + r) + k], sem).wait()
        return carry

    lax.fori_loop(0, td, start, 0, unroll=DMA_LOOP_UNROLL)
    lax.fori_loop(0, td, wait, 0, unroll=DMA_LOOP_UNROLL)


def moe_dispatch(h, dest, cap, td=256):
    n, d = h.shape
    return pl.pallas_call(
        _dispatch_kernel,
        grid_spec=pltpu.PrefetchScalarGridSpec(
            num_scalar_prefetch=1,
            grid=(n // td,),
            in_specs=[pl.BlockSpec((td, d), lambda i, dest: (i, 0)),
                      pl.BlockSpec(memory_space=pl.ANY)],
            out_specs=pl.BlockSpec(memory_space=pl.ANY),
            scratch_shapes=[pltpu.SemaphoreType.DMA(())]),
        out_shape=jax.ShapeDtypeStruct((cap, d), h.dtype),
        input_output_aliases={2: 0},
        compiler_params=_cparams(("arbitrary",)),
        name="moe_dispatch",
    )(dest, h, jnp.zeros((cap, d), h.dtype))


def _combine_kernel(dest_ref, x_ref, gate_ref, y_ref, o_ref, ya_sc, yb_sc, sem):
    tc = x_ref.shape[0]
    base = pl.program_id(0) * tc

    def start(r, carry):
        _row_copy(y_ref, dest_ref[2 * (base + r)], ya_sc, r, sem).start()
        _row_copy(y_ref, dest_ref[2 * (base + r) + 1], yb_sc, r, sem).start()
        return carry

    def wait(r, carry):
        _row_copy(y_ref, dest_ref[2 * (base + r)], ya_sc, r, sem).wait()
        _row_copy(y_ref, dest_ref[2 * (base + r) + 1], yb_sc, r, sem).wait()
        return carry

    lax.fori_loop(0, tc, start, 0, unroll=DMA_LOOP_UNROLL)
    lax.fori_loop(0, tc, wait, 0, unroll=DMA_LOOP_UNROLL)
    gates = gate_ref[...]
    o_ref[...] = x_ref[...] + (gates[:, 0:1] * ya_sc[...] + gates[:, 1:2] * yb_sc[...])


def moe_combine(x, gates, y, dest, tc=256):
    n, d = x.shape
    return pl.pallas_call(
        _combine_kernel,
        grid_spec=pltpu.PrefetchScalarGridSpec(
            num_scalar_prefetch=1,
            grid=(n // tc,),
            in_specs=[pl.BlockSpec((tc, d), lambda i, dest: (i, 0)),
                      pl.BlockSpec((tc, LANES), lambda i, dest: (i, 0)),
                      pl.BlockSpec(memory_space=pl.ANY)],
            out_specs=pl.BlockSpec((tc, d), lambda i, dest: (i, 0)),
            scratch_shapes=[pltpu.VMEM((tc, d), F32), pltpu.VMEM((tc, d), F32),
                            pltpu.SemaphoreType.DMA(())]),
        out_shape=jax.ShapeDtypeStruct((n, d), F32),
        compiler_params=_cparams(("arbitrary",)),
        name="moe_combine",
    )(dest, x, gates, y)


def moe_layer(x, gain, w_router, w_gate, w_up, w_down, first_expert, tm=512):
    n, d = x.shape
    wr = jnp.pad(w_router.astype(F32), ((0, 0), (0, LANES - N_EXPERTS)))
    h, info, gates, counts = moe_route(x, gain, jnp.stack(_split3(wr)[:2]))
    half = tm // 2
    counts = counts[0, :N_EXPERTS].astype(jnp.int32)
    padded = (counts + half - 1) // half * half
    pad_ends = jnp.cumsum(padded)
    pad_starts = pad_ends - padded
    experts, ranks = info[:, 0:2], info[:, 2:4]
    dest = (pad_starts[experts] + ranks).reshape(-1).astype(jnp.int32)
    n_halves = (2 * n) // half + N_EXPERTS
    assert n_halves % 2 == 0
    items = _expert_items(pad_ends, n_halves, half, first_expert)
    xs = moe_dispatch(h, dest, n_halves * half)
    y = grouped_swiglu(xs, items, items, w_gate, w_up, w_down, tm=tm, tf=1024, tm_down=tm)
    return moe_combine(x, gates, y, dest)


def _expert_items(pad_ends, n_halves, half, first_expert):
    n_blocks = n_halves // 2
    start = jnp.arange(n_halves, dtype=jnp.int32) * half
    used = start < pad_ends[-1]
    owner = jnp.minimum(jnp.sum(start[:, None] >= pad_ends[None, :], axis=1), N_EXPERTS - 1).astype(jnp.int32)
    e0, e1, u0, u1 = owner[0::2], owner[1::2], used[0::2], used[1::2]
    straddle = u0 & u1 & (e0 != e1)
    n_items = 1 + straddle.astype(jnp.int32)
    ends = jnp.cumsum(n_items)
    t = jnp.arange(n_blocks + N_EXPERTS, dtype=jnp.int32)
    blk = jnp.minimum(jnp.sum(ends[None, :] <= t[:, None], axis=1), n_blocks - 1).astype(jnp.int32)
    second = straddle[blk] & (t - (ends - n_items)[blk] == 1)
    whole = jnp.where(u0, jnp.where(u1, PART_FULL, PART_FIRST), PART_EMPTY)
    part = jnp.where(straddle[blk], jnp.where(second, PART_SECOND, PART_FIRST), whole[blk])
    part = jnp.where(t < ends[-1], part, PART_IDLE).astype(jnp.int32)
    expert = jnp.where(second, e1[blk], e0[blk]) + first_expert
    return blk, expert.astype(jnp.int32), part


def _project_weights(w):
    def cols(a, n):
        return w[:, a:a + n]

    qa, ka, va, qb = cols(0, 512), cols(512, 128), cols(640, 128), cols(768, 768)
    kbc, vbc, kbs, vbs, kbw, vbw = (cols(1536 + 128 * i, 128) for i in range(6))
    gb, qc, kc, vc, fc = cols(2304, 36), cols(2340, 768), cols(3108, 768), cols(3876, 768), cols(4644, 12)
    w_rows = jnp.concatenate([kc, ka, va, qa, kbc, vbc, kbs, kbw], axis=1).astype(BF16)
    q_fold = Q_SCALE * LOG2E
    w_t = jnp.concatenate([qb * q_fold, qc * q_fold, vc, vbs, vbw], axis=1).T.astype(BF16)
    tail = jnp.concatenate([gb, fc], axis=1)
    tail = jnp.pad(tail, ((0, 0), (0, LANES - tail.shape[1]))).astype(BF16)
    return w_rows, w_t, tail


def _chunk_blocks(cols, batch, seq):
    n_chunks = seq // CMP_STRIDE
    t = cols.reshape(batch, seq, B_KV_HEADS, HEAD_DIM).transpose(0, 2, 1, 3)
    t = t.reshape(batch, B_KV_HEADS, n_chunks, CMP_STRIDE * HEAD_DIM)
    nxt = jnp.concatenate([t[:, :, 1:], jnp.zeros_like(t[:, :, :1])], axis=2)
    return jnp.concatenate([t, nxt], axis=-1)


def _overlap_matrix_t(seq):
    n_chunks, n_slc = seq // CMP_STRIDE, seq // SEL_BLOCK
    c_start = np.arange(n_chunks) * CMP_STRIDE
    s_start = np.arange(n_slc) * SEL_BLOCK
    inter = np.maximum(np.minimum(c_start[:, None] + CMP_LEN, s_start[None, :] + SEL_BLOCK)
                       - np.maximum(c_start[:, None], s_start[None, :]), 0) / CMP_LEN
    inter[(seq - CMP_LEN) // CMP_STRIDE + 1:] = 0.0
    return jnp.asarray(inter.T, BF16)


def mixer_layer(x, batch, seq, w_in, w_out, norm_mix, mix_out_norm, sinks, cmp_pe, cmp_w1, cmp_w2, f_bias, norm_ffn):
    w_rows, w_t, w_tail = _project_weights(w_in)
    proj_r, proj_t, tail, tail_t = input_projection(x, norm_mix, w_rows, w_t, w_tail)
    o_a = swa_attention(proj_r, sinks.astype(F32), batch, seq)

    bias_row = jnp.zeros((1, LANES), F32).at[0, TAIL_FC:TAIL_FC + C_HEADS].set(f_bias.astype(F32))
    fox_ka, fox_qat = fox_augmentation(tail, bias_row, batch, seq)
    o_c = flash_attention_t("fox", proj_r, proj_t, fox_ka, fox_qat, batch, seq)

    flat = jnp.stack([_chunk_blocks(proj_r[:, OFF_KBC:OFF_KBC + KV_WIDTH], batch, seq),
                      _chunk_blocks(proj_r[:, OFF_VBC:OFF_VBC + KV_WIDTH], batch, seq)])
    pe_rows = jnp.broadcast_to(cmp_pe.reshape(2, 1, CMP_LEN * HEAD_DIM), (2, 8, CMP_LEN * HEAD_DIM)).astype(BF16)
    kvc, kvct = nsa_compress(flat, pe_rows, cmp_w1.astype(BF16), cmp_w2.astype(BF16))
    o_cmp, sel_t = nsa_compressed_attention(proj_t, kvc, kvct, _overlap_matrix_t(seq), tail_t, batch, seq)
    tk = 512
    pos_ka, pos_qat = _alibi_augmentation(seq, tk)
    o_slc = flash_attention_t("slc", proj_r, proj_t, pos_ka, pos_qat, batch, seq, gates_t=tail_t, sel_t=sel_t, tk=tk)
    o_win = flash_attention_t("win", proj_r, proj_t, pos_ka, pos_qat, batch, seq, gates_t=tail_t, tk=tk)
    return mix_out(o_a, o_cmp, o_slc, o_win, o_c, mix_out_norm, w_out.astype(BF16), x, norm_ffn)


def dense_layer(x, h, w_gate, w_up, w_down, index, tm=1024):
    n = x.shape[0]
    tm_down = 512
    return grouped_swiglu(h, _whole_block_items(n // tm, index), _whole_block_items(n // tm_down, index),
                          w_gate, w_up, w_down, residual=x, tm=tm, tm_down=tm_down)


def kernel(x, w_in, w_out, norm_mix, mix_out_norm, attn_sinks, nsa_cmp_pe, nsa_cmp_w1, nsa_cmp_w2, fox_f_bias,
           norm_ffn, ffn_w_gate, ffn_w_up, ffn_w_down, moe_router, moe_w_gate, moe_w_up, moe_w_down, norm_final):
    batch, seq, d = x.shape
    depth = w_in.shape[0]
    f = ffn_w_gate.shape[-1]
    moe_gate, moe_up = moe_w_gate.reshape(-1, d, f), moe_w_up.reshape(-1, d, f)
    moe_down = moe_w_down.reshape(-1, f, d)
    xf = x.reshape(batch * seq, d).astype(F32)
    for layer in range(depth):
        xf, hf = mixer_layer(xf, batch, seq, w_in[layer], w_out[layer], norm_mix[layer], mix_out_norm[layer],
                             attn_sinks[layer], nsa_cmp_pe[layer], nsa_cmp_w1[layer], nsa_cmp_w2[layer],
                             fox_f_bias[layer], norm_ffn[layer])
        i = layer // 2
        if layer % 2 == 0:
            xf = dense_layer(xf, hf, ffn_w_gate, ffn_w_up, ffn_w_down, i)
        else:
            xf = moe_layer(xf, norm_ffn[layer], moe_router[i], moe_gate, moe_up, moe_down, i * N_EXPERTS)
    return rmsnorm_rows(xf, norm_final, x.dtype).reshape(batch, seq, d)
```

```python
import functools

import jax
import jax.numpy as jnp
import numpy as np
from jax import lax
from jax.experimental import pallas as pl
from jax.experimental.pallas import tpu as pltpu

F32 = jnp.float32
BF16 = jnp.bfloat16

HEAD_DIM = 64
A_HEADS, A_KV_HEADS = 8, 2
B_HEADS, B_KV_HEADS = 12, 2
C_HEADS = 12
B_GROUP = B_HEADS // B_KV_HEADS
A_WIDTH, B_WIDTH, C_WIDTH = A_HEADS * HEAD_DIM, B_HEADS * HEAD_DIM, C_HEADS * HEAD_DIM
KV_WIDTH = 2 * HEAD_DIM
N_GATE_COLS = B_HEADS * 3
WINDOW_A = 128
WINDOW_B = 512
CMP_LEN, CMP_STRIDE = 32, 16
SEL_BLOCK, N_SELECT = 64, 16
N_EXPERTS = 8
RMS_EPS = 1e-6
NEG_INF = -1e30
M_INIT = -1e20
FORCE_BONUS = 1e4
LANES = 128
VMEM_LIMIT = 56 * 1024 * 1024
Q_SCALE = HEAD_DIM ** -0.5

OFF_KC, OFF_KA, OFF_VA, OFF_QA = 0, 768, 896, 1024
OFF_KBC, OFF_VBC, OFF_KBS, OFF_KBW = 1536, 1664, 1792, 1920
ROW_WIDTH = 2048
OFF_QBT, OFF_QCT, OFF_VCT, OFF_VBST, OFF_VBWT = 0, 768, 1536, 2304, 2432
T_WIDTH = 2560
TAIL_FC = N_GATE_COLS
ONE_LANE = LANES - 1
N_AUG = HEAD_DIM
Q_AUG_ROWS = 16
LOG2E = 1.4426950408889634
MASK_PEN = -1e30
VMEM_LIMIT_BIG = 60 * 1024 * 1024
DMA_LOOP_UNROLL = 8


def _alibi(n):
    return [float(2.0 ** (-8.0 * i / n)) for i in range(1, n + 1)]


SLOPES_A = _alibi(A_HEADS)
SLOPES_B = _alibi(B_HEADS)


def _cparams(sem, vmem=VMEM_LIMIT):
    return pltpu.CompilerParams(dimension_semantics=sem, vmem_limit_bytes=vmem)


def _dot(a, b):
    return jnp.dot(a, b, preferred_element_type=F32)


def _dot_nt(a, b):
    return lax.dot_general(a, b, (((1,), (1,)), ((), ())), preferred_element_type=F32)


def _split3(x):
    hi = x.astype(BF16)
    r1 = x - hi.astype(F32)
    mid = r1.astype(BF16)
    lo = (r1 - mid.astype(F32)).astype(BF16)
    return hi, mid, lo


def _rms(x, gain):
    return x * lax.rsqrt(jnp.mean(x * x, axis=-1, keepdims=True) + RMS_EPS) * gain


def _sigmoid(x):
    return 1.0 / (1.0 + jnp.exp(-x))


def _norm_kernel(x_ref, g_ref, o_ref):
    o_ref[...] = _rms(x_ref[...], g_ref[...]).astype(o_ref.dtype)


def rmsnorm_rows(x, gain, out_dtype, tm=512):
    n, d = x.shape
    return pl.pallas_call(
        _norm_kernel,
        grid=(n // tm,),
        in_specs=[pl.BlockSpec((tm, d), lambda i: (i, 0)), pl.BlockSpec((1, d), lambda i: (0, 0))],
        out_specs=pl.BlockSpec((tm, d), lambda i: (i, 0)),
        out_shape=jax.ShapeDtypeStruct((n, d), out_dtype),
        compiler_params=_cparams(("parallel",)),
        name="rmsnorm",
    )(x, gain.reshape(1, d))


def _projection_kernel(x_ref, g_ref, wr_ref, wt_ref, wtail_ref, wtailt_ref, pr_ref, pt_ref, tail_ref, tailt_ref, h_sc,
                       *, n_row_tiles, n_t_tiles):
    j = pl.program_id(1)

    @pl.when(j == 0)
    def _():
        h_sc[...] = _rms(x_ref[...], g_ref[...]).astype(BF16)
        tail_ref[...] = _dot(h_sc[...], wtail_ref[...])
        tailt_ref[...] = _dot_nt(wtailt_ref[...], h_sc[...])

    @pl.when((j >= 1) & (j <= n_row_tiles))
    def _():
        pr_ref[...] = _dot(h_sc[...], wr_ref[...]).astype(pr_ref.dtype)

    @pl.when(j > n_row_tiles)
    def _():
        pt_ref[...] = _dot_nt(wt_ref[...], h_sc[...]).astype(pt_ref.dtype)


def input_projection(x, gain, w_rows, w_t, w_tail, tm=1024, tn=512):
    n, d = x.shape
    n_row_tiles, n_t_tiles = w_rows.shape[1] // tn, w_t.shape[0] // tn

    def row_tile(j):
        return jnp.clip(j - 1, 0, n_row_tiles - 1)

    def t_tile(j):
        return jnp.clip(j - 1 - n_row_tiles, 0, n_t_tiles - 1)

    return pl.pallas_call(
        functools.partial(_projection_kernel, n_row_tiles=n_row_tiles, n_t_tiles=n_t_tiles),
        grid=(n // tm, n_row_tiles + n_t_tiles + 1),
        in_specs=[pl.BlockSpec((tm, d), lambda i, j: (i, 0)),
                  pl.BlockSpec((1, d), lambda i, j: (0, 0)),
                  pl.BlockSpec((d, tn), lambda i, j: (0, row_tile(j))),
                  pl.BlockSpec((tn, d), lambda i, j: (t_tile(j), 0)),
                  pl.BlockSpec((d, LANES), lambda i, j: (0, 0)),
                  pl.BlockSpec((LANES, d), lambda i, j: (0, 0))],
        out_specs=[pl.BlockSpec((tm, tn), lambda i, j: (i, row_tile(j))),
                   pl.BlockSpec((tn, tm), lambda i, j: (t_tile(j), i)),
                   pl.BlockSpec((tm, LANES), lambda i, j: (i, 0)),
                   pl.BlockSpec((LANES, tm), lambda i, j: (0, i))],
        out_shape=[jax.ShapeDtypeStruct((n, w_rows.shape[1]), BF16),
                   jax.ShapeDtypeStruct((w_t.shape[0], n), BF16),
                   jax.ShapeDtypeStruct((n, LANES), F32),
                   jax.ShapeDtypeStruct((LANES, n), F32)],
        scratch_shapes=[pltpu.VMEM((tm, d), BF16)],
        compiler_params=_cparams(("parallel", "arbitrary")),
        name="input_projection",
    )(x, gain.reshape(1, d), w_rows, w_t, w_tail, w_tail.T)


def _swa_kernel(sink_ref, q_ref, kp_ref, kc_ref, vp_ref, vc_ref, o_ref):
    n = pl.program_id(1)
    tq = q_ref.shape[0]
    q = q_ref[...]
    k = jnp.concatenate([kp_ref[...], kc_ref[...]], axis=0)
    v = jnp.concatenate([vp_ref[...], vc_ref[...]], axis=0)
    i = lax.broadcasted_iota(jnp.int32, (tq, 2 * tq), 0)
    j = lax.broadcasted_iota(jnp.int32, (tq, 2 * tq), 1)
    rel = tq + i - j
    mask = (rel >= 0) & (rel < WINDOW_A) & ((j >= tq) | (n > 0))
    relf = rel.astype(F32)
    group = A_HEADS // A_KV_HEADS
    for h in range(A_HEADS):
        kv = h // group
        qh = q[:, h * HEAD_DIM:(h + 1) * HEAD_DIM]
        kh = k[:, kv * HEAD_DIM:(kv + 1) * HEAD_DIM]
        vh = v[:, kv * HEAD_DIM:(kv + 1) * HEAD_DIM]
        s = _dot_nt(qh, kh) * Q_SCALE - SLOPES_A[h] * relf
        s = jnp.where(mask, s, NEG_INF)
        sink = sink_ref[h]
        m = jnp.maximum(jnp.max(s, axis=-1, keepdims=True), sink)
        p = jnp.exp(s - m)
        l = jnp.sum(p, axis=-1, keepdims=True) + jnp.exp(sink - m)
        o_ref[:, h * HEAD_DIM:(h + 1) * HEAD_DIM] = _dot(p.astype(BF16), vh) / l


def swa_attention(proj, sinks, batch, seq):
    tq = WINDOW_A
    nb = seq // tq
    kcol, vcol = OFF_KA // KV_WIDTH, OFF_VA // KV_WIDTH

    def cur(col):
        return pl.BlockSpec((tq, KV_WIDTH), lambda b, n: (b * nb + n, col))

    def prev(col):
        return pl.BlockSpec((tq, KV_WIDTH), lambda b, n: (b * nb + jnp.maximum(n - 1, 0), col))

    return pl.pallas_call(
        _swa_kernel,
        grid=(batch, nb),
        in_specs=[pl.BlockSpec(memory_space=pltpu.SMEM),
                  pl.BlockSpec((tq, A_WIDTH), lambda b, n: (b * nb + n, OFF_QA // A_WIDTH)),
                  prev(kcol), cur(kcol), prev(vcol), cur(vcol)],
        out_specs=pl.BlockSpec((tq, A_WIDTH), lambda b, n: (b * nb + n, 0)),
        out_shape=jax.ShapeDtypeStruct((batch * seq, A_WIDTH), F32),
        compiler_params=_cparams(("parallel", "parallel")),
        name="swa_attention",
    )(sinks, proj, proj, proj, proj, proj)


def _fox_aug_kernel(z_ref, b_ref, pk_ref, pq_ref, ka_ref, qa_ref, carry_sc):
    @pl.when(pl.program_id(1) == 0)
    def _():
        carry_sc[...] = jnp.zeros_like(carry_sc)

    z = z_ref[...] + b_ref[...]
    log_f = -(jnp.maximum(-z, 0.0) + jnp.log1p(jnp.exp(-jnp.abs(z))))
    ts = z.shape[0]
    r = lax.broadcasted_iota(jnp.int32, (ts, ts), 0)
    c = lax.broadcasted_iota(jnp.int32, (ts, ts), 1)
    tri = (c <= r).astype(BF16)
    hi, mid, lo = _split3(log_f)
    cum = _dot(tri, hi) + _dot(tri, mid) + _dot(tri, lo) + carry_sc[0:1, :]
    carry_sc[0:1, :] = cum[ts - 1:ts, :]
    hi, mid, lo = _split3(cum * LOG2E)
    lane = lax.broadcasted_iota(jnp.int32, (ts, LANES), 1)
    hi = jnp.where(lane == ONE_LANE, 1.0, hi).astype(BF16)
    ka_ref[...] = (_dot(hi, pk_ref[0]) + _dot(mid, pk_ref[1]) + _dot(lo, pk_ref[2])).astype(BF16)
    qa_ref[...] = (_dot_nt(pq_ref[0], hi) + _dot_nt(pq_ref[1], mid) + _dot_nt(pq_ref[2], lo)).astype(BF16)


def _aug_lane(h):
    return HEAD_DIM if h % 2 == 0 else 0


def _fox_placement():
    pk = np.zeros((3, LANES, C_HEADS * LANES), np.float32)
    pq = np.zeros((3, C_HEADS * Q_AUG_ROWS, LANES), np.float32)
    for h in range(C_HEADS):
        src, kbase, qbase = TAIL_FC + h, h * LANES + _aug_lane(h), h * Q_AUG_ROWS
        for piece in range(3):
            pk[piece, src, kbase + piece] = -1.0
            pq[piece, qbase + 3 + piece, src] = 1.0
            pk[0, ONE_LANE, kbase + 3 + piece] = 1.0
            pq[0, qbase + piece, ONE_LANE] = 1.0
    return jnp.asarray(pk, BF16), jnp.asarray(pq, BF16)


def fox_augmentation(tail, bias_row, batch, seq, ts=512):
    nt = seq // ts
    pk, pq = _fox_placement()
    kw, qw = C_HEADS * LANES, C_HEADS * Q_AUG_ROWS
    return pl.pallas_call(
        _fox_aug_kernel,
        grid=(batch, nt),
        in_specs=[pl.BlockSpec((ts, LANES), lambda b, i: (b * nt + i, 0)),
                  pl.BlockSpec((1, LANES), lambda b, i: (0, 0)),
                  pl.BlockSpec((3, LANES, kw), lambda b, i: (0, 0, 0)),
                  pl.BlockSpec((3, qw, LANES), lambda b, i: (0, 0, 0))],
        out_specs=[pl.BlockSpec((ts, kw), lambda b, i: (b * nt + i, 0)),
                   pl.BlockSpec((qw, ts), lambda b, i: (0, b * nt + i))],
        out_shape=[jax.ShapeDtypeStruct((batch * seq, kw), BF16),
                   jax.ShapeDtypeStruct((qw, batch * seq), BF16)],
        scratch_shapes=[pltpu.VMEM((8, LANES), F32)],
        compiler_params=_cparams(("parallel", "arbitrary")),
        name="fox_augmentation",
    )(tail, bias_row, pk, pq)


def _alibi_augmentation(seq, tk):
    pos = np.arange(seq)
    ka = np.zeros((seq, LANES), np.float32)
    for base in (0, HEAD_DIM):
        ka[:, base:base + 3] = 1.0
        ka[:, base + 3:base + 6] = ((pos >> 8) << 8)[:, None]
        ka[:, base + 6:base + 9] = (pos & 255)[:, None]
        ka[pos, base + Q_AUG_ROWS + (pos % tk) // SEL_BLOCK] = 1.0
    def split3_np(x):
        pieces, rest = [], x.astype(np.float32)
        for _ in range(3):
            piece = rest.astype(jnp.bfloat16).astype(np.float32)
            pieces.append(piece)
            rest = rest - piece
        return pieces

    slopes = (np.asarray(SLOPES_B, np.float32) * np.float32(LOG2E)).astype(np.float32)
    st = split3_np(-(slopes[:, None] * pos.astype(np.float32)[None, :]))
    sl = split3_np(slopes)
    qa = np.zeros((B_HEADS, Q_AUG_ROWS, seq), np.float32)
    for piece in range(3):
        qa[:, piece, :] = st[piece]
        qa[:, 3 + piece, :] = sl[piece][:, None]
        qa[:, 6 + piece, :] = sl[piece][:, None]
    return jnp.asarray(ka, BF16), jnp.asarray(qa.reshape(B_HEADS * Q_AUG_ROWS, seq), BF16)


def _flash_t_kernel(*refs, mode, tq, tk, n_steps, seq):
    pen_sc = None
    if mode == "slc":
        (iq_ref, ik_ref, live_ref, k_ref, ka_ref, qt_ref, qat_ref, vt_ref, gate_ref, selt_ref, o_ref,
         m_sc, l_sc, acc_sc, pen_sc) = refs
    elif mode == "win":
        k_ref, ka_ref, qt_ref, qat_ref, vt_ref, gate_ref, o_ref, m_sc, l_sc, acc_sc = refs
    else:
        iq_ref, ik_ref, k_ref, ka_ref, qt_ref, qat_ref, vt_ref, o_ref, m_sc, l_sc, acc_sc = refs
    n_heads = qt_ref.shape[0] // HEAD_DIM
    group = 1 if mode == "fox" else B_GROUP
    blocks_per_tile = tk // SEL_BLOCK
    if mode == "win":
        iq, j = pl.program_id(1), pl.program_id(2)
        ik = iq - (n_steps - 1) + j
        first, last = j == 0, j == n_steps - 1
        active = ik >= 0
    else:
        iq, ik = iq_ref[pl.program_id(1)], ik_ref[pl.program_id(1)]
        first, last = ik == 0, ik == (iq * tq + tq - 1) // tk
        active = True

    @pl.when(first)
    def _():
        m_sc[...] = jnp.full_like(m_sc, M_INIT)
        l_sc[...] = jnp.zeros_like(l_sc)
        acc_sc[...] = jnp.zeros_like(acc_sc)
        if mode == "slc":
            pen_sc[...] = (1.0 - selt_ref[0].astype(F32)) * MASK_PEN

    def step(masked, heads):
        ok = None
        if masked:
            rel = ((iq * tq + lax.broadcasted_iota(jnp.int32, (tk, tq), 1))
                   - (ik * tk + lax.broadcasted_iota(jnp.int32, (tk, tq), 0)))
            ok = rel >= 0
            if mode == "win":
                ok = ok & (rel < WINDOW_B)
        lane = lax.broadcasted_iota(jnp.int32, (tk, LANES), 1)
        ones_rows = jnp.ones((Q_AUG_ROWS, tk), BF16)
        per_kv = {}

        def kv_operands(kv):
            if kv not in per_kv:
                pair = slice((kv // 2) * LANES, (kv // 2 + 1) * LANES)
                aug = ka_ref[:, kv * LANES:(kv + 1) * LANES] if mode == "fox" else ka_ref[...]
                own = (lane < HEAD_DIM) if kv % 2 == 0 else (lane >= HEAD_DIM)
                ka = jnp.where(own, k_ref[:, pair], aug)
                v_aug = jnp.concatenate([vt_ref[kv * HEAD_DIM:(kv + 1) * HEAD_DIM, :], ones_rows], axis=0)
                pen = None
                if mode == "slc":
                    pen8 = pen_sc[kv, pl.ds(pl.multiple_of(ik * blocks_per_tile, blocks_per_tile), blocks_per_tile), :]
                    pen = jnp.concatenate(
                        [pen8, jnp.zeros((Q_AUG_ROWS - blocks_per_tile, tq), F32)], axis=0).astype(BF16)
                per_kv[kv] = (ka, v_aug, pen)
            return per_kv[kv]

        def scores(h):
            kv = h // group
            ka, _, pen = kv_operands(kv)
            aug_rows = [qat_ref[h * Q_AUG_ROWS:(h + 1) * Q_AUG_ROWS, :]]
            if mode == "slc":
                aug_rows.append(pen)
            aug_rows.append(jnp.zeros((N_AUG - Q_AUG_ROWS * len(aug_rows), tq), BF16))
            halves = [[qt_ref[h * HEAD_DIM:(h + 1) * HEAD_DIM, :]], aug_rows]
            qa = jnp.concatenate(sum(halves if kv % 2 == 0 else halves[::-1], []), axis=0)
            s = _dot(ka, qa)
            return s if ok is None else jnp.where(ok, s, NEG_INF)

        def probabilities(h, s):
            m_prev = m_sc[h:h + 1, :]
            m_new = jnp.maximum(m_prev, jnp.max(s, axis=0, keepdims=True))
            m_sc[h:h + 1, :] = m_new
            return jnp.exp2(s - m_new).astype(BF16), jnp.exp2(m_prev - m_new)

        def accumulate(h, p, alpha):
            rows = slice(h * HEAD_DIM, (h + 1) * HEAD_DIM)
            pv = _dot(kv_operands(h // group)[1], p)
            l_sc[h:h + 1, :] = alpha * l_sc[h:h + 1, :] + pv[HEAD_DIM:HEAD_DIM + 1, :]
            acc_sc[rows, :] = alpha * acc_sc[rows, :] + pv[0:HEAD_DIM, :]

        s_cur, pending = scores(heads[0]), None
        for i, h in enumerate(heads):
            s_next = scores(heads[i + 1]) if i + 1 < len(heads) else None
            p_alpha = probabilities(h, s_cur)
            if pending is not None:
                accumulate(*pending)
            pending = (h,) + p_alpha
            s_cur = s_next
        accumulate(*pending)

    all_heads = list(range(n_heads))
    on_diagonal = ik * tk + tk - 1 > iq * tq
    if mode == "win":
        pl.when(active)(lambda: step(True, all_heads))
    elif mode == "fox":
        pl.when(on_diagonal)(lambda: step(True, all_heads))
        pl.when(jnp.logical_not(on_diagonal))(lambda: step(False, all_heads))
    else:
        tile = (pl.program_id(0) * (seq // tq) + iq) * (seq // tk) + ik
        for kv in range(B_KV_HEADS):
            heads = all_heads[kv * group:(kv + 1) * group]
            live = live_ref[tile * B_KV_HEADS + kv] > 0
            pl.when(live & on_diagonal)(functools.partial(step, True, heads))
            pl.when(live & jnp.logical_not(on_diagonal))(functools.partial(step, False, heads))

    @pl.when(last)
    def _():
        branch = {"fox": None, "slc": 1, "win": 2}[mode]
        for pair in range(n_heads // 2):
            rows = slice(pair * LANES, (pair + 1) * LANES)
            heads = (2 * pair, 2 * pair + 1)
            denom = jnp.concatenate(
                [jnp.broadcast_to(l_sc[h:h + 1, :], (HEAD_DIM, tq)) for h in heads], axis=0)
            out = acc_sc[rows, :] / denom
            if branch is not None:
                out = out * jnp.concatenate(
                    [jnp.broadcast_to(_sigmoid(gate_ref[3 * h + branch:3 * h + branch + 1, :]), (HEAD_DIM, tq))
                     for h in heads], axis=0)
            o_ref[:, rows] = out.T


def flash_attention_t(mode, proj_r, proj_t, k_aug, q_aug_t, batch, seq, gates_t=None, sel_t=None, tq=512, tk=512):
    nq, nk = seq // tq, seq // tk
    if mode == "win":
        assert tq == tk
        n_steps = WINDOW_B // tk + 1
        grid = (batch, nq, n_steps)
        prefetch = []

        def tiles(iq, j):
            return iq, jnp.maximum(iq - (n_steps - 1) + j, 0)
    else:
        n_steps = nk
        pairs = [(iq, ik) for iq in range(nq) for ik in range((iq * tq + tq - 1) // tk + 1)]
        grid = (batch, len(pairs))
        prefetch = [jnp.asarray([p[0] for p in pairs], jnp.int32), jnp.asarray([p[1] for p in pairs], jnp.int32)]

        def tiles(p, iq_tab, ik_tab, *_):
            return iq_tab[p], ik_tab[p]

    def kv_rows(col):
        return lambda b, *g: (b * nk + tiles(*g)[1], col)

    def kv_cols(row):
        return lambda b, *g: (row, b * nk + tiles(*g)[1])

    def q_cols(row, per_batch=True):
        return lambda b, *g: (row, (b * nq if per_batch else 0) + tiles(*g)[0])

    if mode == "fox":
        in_specs = [pl.BlockSpec((tk, C_WIDTH), kv_rows(OFF_KC // C_WIDTH)),
                    pl.BlockSpec((tk, C_HEADS * LANES), kv_rows(0)),
                    pl.BlockSpec((C_WIDTH, tq), q_cols(OFF_QCT // C_WIDTH)),
                    pl.BlockSpec((C_HEADS * Q_AUG_ROWS, tq), q_cols(0)),
                    pl.BlockSpec((C_WIDTH, tk), kv_cols(OFF_VCT // C_WIDTH))]
    else:
        kcol = (OFF_KBS if mode == "slc" else OFF_KBW) // KV_WIDTH
        vrow = (OFF_VBST if mode == "slc" else OFF_VBWT) // KV_WIDTH
        in_specs = [pl.BlockSpec((tk, KV_WIDTH), kv_rows(kcol)),
                    pl.BlockSpec((tk, LANES), lambda b, *g: (tiles(*g)[1], 0)),
                    pl.BlockSpec((B_WIDTH, tq), q_cols(OFF_QBT // B_WIDTH)),
                    pl.BlockSpec((B_HEADS * Q_AUG_ROWS, tq), q_cols(0, per_batch=False)),
                    pl.BlockSpec((KV_WIDTH, tk), kv_cols(vrow))]
    args = [proj_r, k_aug, proj_t, q_aug_t, proj_t]
    width = C_WIDTH if mode == "fox" else B_WIDTH
    n_heads = width // HEAD_DIM
    scratch = [pltpu.VMEM((16, tq), F32), pltpu.VMEM((16, tq), F32), pltpu.VMEM((n_heads * HEAD_DIM, tq), F32)]
    if mode != "fox":
        in_specs.append(pl.BlockSpec((LANES, tq), q_cols(0)))
        args.append(gates_t)
    if mode == "slc":
        n_slc = seq // SEL_BLOCK
        per_tile = tk // SEL_BLOCK
        assert per_tile <= Q_AUG_ROWS and tk % SEL_BLOCK == 0
        in_specs.append(pl.BlockSpec((1, B_KV_HEADS, n_slc, tq), lambda b, *g: (b, 0, 0, tiles(*g)[0])))
        args.append(sel_t)
        scratch.append(pltpu.VMEM((B_KV_HEADS, n_slc, tq), F32))
        live = sel_t.reshape(batch, B_KV_HEADS, nk, per_tile, nq, tq).max(axis=(3, 5)) > 0
        prefetch.append(live.transpose(0, 3, 2, 1).reshape(-1).astype(jnp.int32))
    return pl.pallas_call(
        functools.partial(_flash_t_kernel, mode=mode, tq=tq, tk=tk, n_steps=n_steps, seq=seq),
        grid_spec=pltpu.PrefetchScalarGridSpec(
            num_scalar_prefetch=len(prefetch),
            grid=grid,
            in_specs=in_specs,
            out_specs=pl.BlockSpec((tq, width), lambda b, *g: (b * nq + tiles(*g)[0], 0)),
            scratch_shapes=scratch),
        out_shape=jax.ShapeDtypeStruct((batch * seq, width), F32),
        compiler_params=_cparams(("parallel",) * (len(grid) - 1) + ("arbitrary",)),
        name=mode + "_attention",
    )(*prefetch, *args)


def _compress_kernel(t_ref, pe_ref, w1_ref, w2_ref, w2t_ref, o_ref, ot_ref):
    w1 = w1_ref[0]
    hid = _dot(t_ref[0, 0, 0], w1) + _dot(pe_ref[0], w1)[0:1, :]
    act = (hid * _sigmoid(hid)).astype(BF16)
    o_ref[0, 0, 0] = _dot(act, w2_ref[0]).astype(o_ref.dtype)
    ot_ref[0, 0, 0] = _dot_nt(w2t_ref[0], act).astype(ot_ref.dtype)


def nsa_compress(flat, pe_rows, w1, w2):
    _, batch, n_kv, n_chunks, width = flat.shape
    hidden = w1.shape[-1]
    return pl.pallas_call(
        _compress_kernel,
        grid=(2, batch, n_kv),
        in_specs=[pl.BlockSpec((1, 1, 1, n_chunks, width), lambda s, b, h: (s, b, h, 0, 0)),
                  pl.BlockSpec((1, 8, width), lambda s, b, h: (s, 0, 0)),
                  pl.BlockSpec((1, width, hidden), lambda s, b, h: (s, 0, 0)),
                  pl.BlockSpec((1, hidden, HEAD_DIM), lambda s, b, h: (s, 0, 0)),
                  pl.BlockSpec((1, HEAD_DIM, hidden), lambda s, b, h: (s, 0, 0))],
        out_specs=[pl.BlockSpec((1, 1, 1, n_chunks, HEAD_DIM), lambda s, b, h: (s, b, h, 0, 0)),
                   pl.BlockSpec((1, 1, 1, HEAD_DIM, n_chunks), lambda s, b, h: (s, b, h, 0, 0))],
        out_shape=[jax.ShapeDtypeStruct((2, batch, n_kv, n_chunks, HEAD_DIM), BF16),
                   jax.ShapeDtypeStruct((2, batch, n_kv, HEAD_DIM, n_chunks), BF16)],
        compiler_params=_cparams(("parallel", "parallel", "parallel")),
        name="nsa_compress",
    )(flat, pe_rows, w1, w2, jnp.swapaxes(w2, 1, 2))


def _cmp_attn_t_kernel(qt_ref, kc_ref, vct_ref, inter_ref, gate_ref, o_ref, selt_ref, ot_sc, *, tq):
    iq = pl.program_id(1)
    n_chunks = kc_ref.shape[3]
    n_slc = inter_ref.shape[0]
    t = iq * tq + lax.broadcasted_iota(jnp.int32, (n_chunks, tq), 1)
    n = lax.broadcasted_iota(jnp.int32, (n_chunks, tq), 0)
    rel = t - (n * CMP_STRIDE + CMP_LEN - 1)
    mask = rel >= 0
    relf = rel.astype(F32)
    t_s = iq * tq + lax.broadcasted_iota(jnp.int32, (n_slc, tq), 1)
    jj = lax.broadcasted_iota(jnp.int32, (n_slc, tq), 0)
    cur = t_s >> (SEL_BLOCK.bit_length() - 1)
    valid = jj * SEL_BLOCK <= t_s
    forced = (jj == 0) | (jj == cur) | (jj == cur - 1)
    jf = jj.astype(F32)
    inter = inter_ref[...]
    def raw_scores(hd):
        return _dot(kc_ref[0, 0, hd // B_GROUP], qt_ref[hd * HEAD_DIM:(hd + 1) * HEAD_DIM, :])

    raw_next = raw_scores(0)
    for h in range(B_KV_HEADS):
        vct = vct_ref[0, 0, h]
        p_sum = jnp.zeros((n_chunks, tq), F32)
        for g in range(B_GROUP):
            hd = h * B_GROUP + g
            rows = slice(hd * HEAD_DIM, (hd + 1) * HEAD_DIM)
            raw, raw_next = raw_next, (raw_scores(hd + 1) if hd + 1 < B_HEADS else None)
            s = jnp.where(mask, raw - (SLOPES_B[hd] * LOG2E) * relf, NEG_INF)
            m = jnp.max(s, axis=0, keepdims=True)
            e = jnp.where(mask, jnp.exp2(s - m), 0.0)
            l = jnp.sum(e, axis=0, keepdims=True)
            p = e / jnp.where(l > 0.0, l, 1.0)
            p_sum = p_sum + p
            ot_sc[rows, :] = _dot(vct, p.astype(BF16)) * _sigmoid(gate_ref[3 * hd:3 * hd + 1, :])
        hi, mid, lo = _split3(p_sum)
        importance = _dot(inter, hi) + _dot(inter, mid) + _dot(inter, lo)
        score = jnp.where(valid, importance + jnp.where(forced, FORCE_BONUS, 0.0), NEG_INF)
        chosen = jnp.zeros((n_slc, tq), F32)
        for _ in range(min(N_SELECT, n_slc)):
            best = jnp.max(score, axis=0, keepdims=True)
            first = jnp.min(jnp.where(score == best, jf, float(n_slc)), axis=0, keepdims=True)
            pick = jf == first
            chosen = jnp.where(pick, 1.0, chosen)
            score = jnp.where(pick, -jnp.inf, score)
        selt_ref[0, h] = jnp.where(valid, chosen, 0.0).astype(selt_ref.dtype)
    for pair in range(B_HEADS // 2):
        rows = slice(pair * LANES, (pair + 1) * LANES)
        o_ref[:, rows] = ot_sc[rows, :].T


def nsa_compressed_attention(proj_t, kvc, kvct, inter_t, gates_t, batch, seq, tq=512):
    nq = seq // tq
    n_chunks = kvc.shape[3]
    n_slc = seq // SEL_BLOCK
    return pl.pallas_call(
        functools.partial(_cmp_attn_t_kernel, tq=tq),
        grid=(batch, nq),
        in_specs=[pl.BlockSpec((B_WIDTH, tq), lambda b, i: (OFF_QBT // B_WIDTH, b * nq + i)),
                  pl.BlockSpec((1, 1, B_KV_HEADS, n_chunks, HEAD_DIM), lambda b, i: (0, b, 0, 0, 0)),
                  pl.BlockSpec((1, 1, B_KV_HEADS, HEAD_DIM, n_chunks), lambda b, i: (1, b, 0, 0, 0)),
                  pl.BlockSpec((n_slc, n_chunks), lambda b, i: (0, 0)),
                  pl.BlockSpec((LANES, tq), lambda b, i: (0, b * nq + i))],
        out_specs=[pl.BlockSpec((tq, B_WIDTH), lambda b, i: (b * nq + i, 0)),
                   pl.BlockSpec((1, B_KV_HEADS, n_slc, tq), lambda b, i: (b, 0, 0, i))],
        out_shape=[jax.ShapeDtypeStruct((batch * seq, B_WIDTH), F32),
                   jax.ShapeDtypeStruct((batch, B_KV_HEADS, n_slc, seq), BF16)],
        scratch_shapes=[pltpu.VMEM((B_WIDTH, tq), F32)],
        compiler_params=_cparams(("parallel", "parallel")),
        name="nsa_cmp_attention",
    )(proj_t, kvc, kvct, inter_t, gates_t)


def _mix_out_kernel(oa_ref, ocmp_ref, oslc_ref, owin_ref, oc_ref, g_ref, w_ref, x_ref, gn_ref, o_ref, h_ref):
    o_b = ocmp_ref[...] + oslc_ref[...] + owin_ref[...]
    b0, c0 = A_WIDTH, A_WIDTH + B_WIDTH
    mixed = jnp.concatenate([_rms(oa_ref[...], g_ref[:, 0:b0]).astype(BF16),
                             _rms(o_b, g_ref[:, b0:c0]).astype(BF16),
                             _rms(oc_ref[...], g_ref[:, c0:]).astype(BF16)], axis=1)
    x_new = x_ref[...] + _dot(mixed, w_ref[...])
    o_ref[...] = x_new
    h_ref[...] = _rms(x_new, gn_ref[...]).astype(h_ref.dtype)


def mix_out(o_a, o_cmp, o_slc, o_win, o_c, gain, w_out, x, next_gain, tm=256):
    n, d = x.shape
    width = w_out.shape[0]

    def rows(w):
        return pl.BlockSpec((tm, w), lambda i: (i, 0))

    return pl.pallas_call(
        _mix_out_kernel,
        grid=(n // tm,),
        in_specs=[rows(A_WIDTH), rows(B_WIDTH), rows(B_WIDTH), rows(B_WIDTH), rows(C_WIDTH),
                  pl.BlockSpec((1, width), lambda i: (0, 0)),
                  pl.BlockSpec((width, d), lambda i: (0, 0)),
                  rows(d),
                  pl.BlockSpec((1, d), lambda i: (0, 0))],
        out_specs=[rows(d), rows(d)],
        out_shape=[jax.ShapeDtypeStruct((n, d), F32), jax.ShapeDtypeStruct((n, d), BF16)],
        compiler_params=_cparams(("parallel",)),
        name="mix_out",
    )(o_a, o_cmp, o_slc, o_win, o_c, gain.reshape(1, width), w_out, x, next_gain.reshape(1, d))


PART_FULL, PART_FIRST, PART_SECOND, PART_EMPTY, PART_IDLE = range(5)


def _fresh_weights(exp_ref):
    t = pl.program_id(1)
    return (t == 0) | (exp_ref[t] != exp_ref[jnp.maximum(t - 1, 0)])


def _for_each_part(part, o_ref, compute):
    half = o_ref.shape[0] // 2

    @pl.when(part == PART_FULL)
    def _():
        o_ref[...] = compute(slice(None))

    @pl.when(part == PART_FIRST)
    def _():
        o_ref[0:half, :] = compute(slice(0, half))
        o_ref[half:, :] = jnp.zeros((half, o_ref.shape[1]), o_ref.dtype)

    @pl.when(part == PART_SECOND)
    def _():
        o_ref[half:, :] = compute(slice(half, 2 * half))

    @pl.when(part == PART_EMPTY)
    def _():
        o_ref[...] = jnp.zeros_like(o_ref)


def _ffn_up_kernel(blk_ref, exp_ref, part_ref, x_ref, wg_ref, wu_ref, o_ref, wg_sc, wu_sc):
    @pl.when(_fresh_weights(exp_ref))
    def _():
        wg_sc[...] = wg_ref[0].astype(BF16)
        wu_sc[...] = wu_ref[0].astype(BF16)

    def compute(rows):
        if x_ref.dtype == jnp.int32:
            c = x_ref.shape[1]
            x_lo, x_hi = _unpack_bf16_pairs(x_ref[rows, :])
            gate = _dot(x_lo, wg_sc[0:c, :]) + _dot(x_hi, wg_sc[c:, :])
            up = _dot(x_lo, wu_sc[0:c, :]) + _dot(x_hi, wu_sc[c:, :])
        else:
            x = x_ref[rows, :]
            gate = _dot(x, wg_sc[...])
            up = _dot(x, wu_sc[...])
        return (gate * _sigmoid(gate) * up).astype(o_ref.dtype)

    _for_each_part(part_ref[pl.program_id(1)], o_ref, compute)


def _ffn_down_kernel(blk_ref, exp_ref, part_ref, h_ref, wd_ref, *rest):
    o_ref, wd_sc = rest[-2], rest[-1]

    @pl.when(_fresh_weights(exp_ref))
    def _():
        wd_sc[...] = wd_ref[0].astype(BF16)

    def compute(rows):
        y = _dot(h_ref[rows, :], wd_sc[...])
        return y + rest[0][rows, :] if len(rest) == 3 else y

    _for_each_part(part_ref[pl.program_id(1)], o_ref, compute)


def _whole_block_items(n_blocks, expert):
    return (jnp.arange(n_blocks, dtype=jnp.int32), jnp.full((n_blocks,), expert, jnp.int32),
            jnp.full((n_blocks,), PART_FULL, jnp.int32))


def grouped_swiglu(xs, items_up, items_down, w_gate, w_up, w_down, residual=None, tm=512, tf=512, tm_down=512,
                   tn=512):
    rows, x_width = xs.shape
    _, d, f = w_gate.shape

    hidden = pl.pallas_call(
        _ffn_up_kernel,
        grid_spec=pltpu.PrefetchScalarGridSpec(
            num_scalar_prefetch=3,
            grid=(f // tf, items_up[0].shape[0]),
            in_specs=[pl.BlockSpec((tm, x_width), lambda j, t, blk, ex, part: (blk[t], 0)),
                      pl.BlockSpec((1, d, tf), lambda j, t, blk, ex, part: (ex[t], 0, j)),
                      pl.BlockSpec((1, d, tf), lambda j, t, blk, ex, part: (ex[t], 0, j))],
            out_specs=pl.BlockSpec((tm, tf), lambda j, t, blk, ex, part: (blk[t], j)),
            scratch_shapes=[pltpu.VMEM((d, tf), BF16), pltpu.VMEM((d, tf), BF16)]),
        out_shape=jax.ShapeDtypeStruct((rows, f), BF16),
        compiler_params=_cparams(("parallel", "arbitrary"), VMEM_LIMIT_BIG),
        name="ffn_up",
    )(*items_up, xs, w_gate, w_up)

    in_specs = [pl.BlockSpec((tm_down, f), lambda j, t, blk, ex, part: (blk[t], 0)),
                pl.BlockSpec((1, f, tn), lambda j, t, blk, ex, part: (ex[t], 0, j))]
    args = [hidden, w_down]
    if residual is not None:
        in_specs.append(pl.BlockSpec((tm_down, tn), lambda j, t, blk, ex, part: (blk[t], j)))
        args.append(residual)
    return pl.pallas_call(
        _ffn_down_kernel,
        grid_spec=pltpu.PrefetchScalarGridSpec(
            num_scalar_prefetch=3,
            grid=(d // tn, items_down[0].shape[0]),
            in_specs=in_specs,
            out_specs=pl.BlockSpec((tm_down, tn), lambda j, t, blk, ex, part: (blk[t], j)),
            scratch_shapes=[pltpu.VMEM((f, tn), BF16)]),
        out_shape=jax.ShapeDtypeStruct((rows, d), F32),
        compiler_params=_cparams(("parallel", "arbitrary"), VMEM_LIMIT_BIG),
        name="ffn_down",
    )(*items_down, *args)


def _router_kernel(x_ref, g_ref, wr_ref, h_ref, ri_ref, rf_ref, cnt_ref, carry_sc):
    @pl.when(pl.program_id(0) == 0)
    def _():
        carry_sc[...] = jnp.zeros_like(carry_sc)

    tm = x_ref.shape[0]
    h = _rms(x_ref[...], g_ref[...])
    h_ref[...] = _pack_bf16_pairs(h)
    h1, h2, _ = _split3(h)
    w1, w2 = wr_ref[0], wr_ref[1]
    logits = _dot(h1, w1) + (_dot(h1, w2) + _dot(h2, w1))
    lane_i = lax.broadcasted_iota(jnp.int32, (tm, LANES), 1)
    lane = lane_i.astype(F32)
    logits = jnp.where(lane_i < N_EXPERTS, logits, -jnp.inf)
    v1 = jnp.max(logits, axis=-1, keepdims=True)
    e1 = jnp.min(jnp.where(logits == v1, lane, float(LANES)), axis=-1, keepdims=True)
    rest = jnp.where(lane == e1, -jnp.inf, logits)
    v2 = jnp.max(rest, axis=-1, keepdims=True)
    e2 = jnp.min(jnp.where(rest == v2, lane, float(LANES)), axis=-1, keepdims=True)
    z = jnp.exp(v2 - v1)
    g1 = 1.0 / (1.0 + z)
    g2 = z / (1.0 + z)
    chosen = (lane == e1) | (lane == e2)
    onehot = jnp.where(chosen, 1.0, 0.0)
    r = lax.broadcasted_iota(jnp.int32, (tm, tm), 0)
    c = lax.broadcasted_iota(jnp.int32, (tm, tm), 1)
    before = _dot((c < r).astype(BF16), onehot.astype(BF16)) + carry_sc[0:1, :]
    r1 = jnp.sum(jnp.where(lane == e1, before, 0.0), axis=-1, keepdims=True)
    r2 = jnp.sum(jnp.where(lane == e2, before, 0.0), axis=-1, keepdims=True)
    carry = carry_sc[0:1, :] + jnp.sum(onehot, axis=0, keepdims=True)
    carry_sc[0:1, :] = carry
    cnt_ref[...] = jnp.broadcast_to(carry, cnt_ref.shape)
    packed = jnp.where(lane_i == 0, e1, jnp.where(lane_i == 1, e2, jnp.where(
        lane_i == 2, r1, jnp.where(lane_i == 3, r2, 0.0))))
    ri_ref[...] = packed.astype(jnp.int32)
    rf_ref[...] = jnp.where(lane_i == 0, g1, jnp.where(lane_i == 1, g2, 0.0))


def _pack_bf16_pairs(h):
    bits = lax.bitcast_convert_type(h, jnp.int32)
    rounded = bits + 0x7FFF + (lax.shift_right_logical(bits, 16) & 1)
    c = h.shape[1] // 2
    return lax.shift_right_logical(rounded[:, :c], 16) | (rounded[:, c:] & -65536)


def _unpack_bf16_pairs(words):
    lo = lax.bitcast_convert_type(lax.shift_left(words, 16), F32)
    hi = lax.bitcast_convert_type(words & -65536, F32)
    return lo.astype(BF16), hi.astype(BF16)


def moe_route(x, gain, w_router3, tm=512):
    n, d = x.shape
    return pl.pallas_call(
        _router_kernel,
        grid=(n // tm,),
        in_specs=[pl.BlockSpec((tm, d), lambda i: (i, 0)),
                  pl.BlockSpec((1, d), lambda i: (0, 0)),
                  pl.BlockSpec((2, d, LANES), lambda i: (0, 0, 0))],
        out_specs=[pl.BlockSpec((tm, d // 2), lambda i: (i, 0)),
                   pl.BlockSpec((tm, LANES), lambda i: (i, 0)),
                   pl.BlockSpec((tm, LANES), lambda i: (i, 0)),
                   pl.BlockSpec((8, LANES), lambda i: (0, 0))],
        out_shape=[jax.ShapeDtypeStruct((n, d // 2), jnp.int32),
                   jax.ShapeDtypeStruct((n, LANES), jnp.int32),
                   jax.ShapeDtypeStruct((n, LANES), F32),
                   jax.ShapeDtypeStruct((8, LANES), F32)],
        scratch_shapes=[pltpu.VMEM((8, LANES), F32)],
        compiler_params=_cparams(("arbitrary",)),
        name="moe_route",
    )(x, gain.reshape(1, d), w_router3)


def _row_copy(src_ref, src_row, dst_ref, dst_row, sem):
    return pltpu.make_async_copy(src_ref.at[pl.ds(src_row, 1)], dst_ref.at[pl.ds(dst_row, 1)], sem)


def _dispatch_kernel(dest_ref, h_ref, zeros_ref, xs_ref, sem):
    del zeros_ref
    td = h_ref.shape[0]
    base = pl.program_id(0) * td

    def start(r, carry):
        for k in range(2):
            _row_copy(h_ref, r, xs_ref, dest_ref[2 * (base + r) + k], sem).start(priority=k)
        return carry

    def wait(r, carry):
        for k in range(2):
            _row_copy(h_ref, r, xs_ref, dest_ref[2 * (base + r) + k], sem).wait()
        return carry

    lax.fori_loop(0, td, start, 0, unroll=DMA_LOOP_UNROLL)
    lax.fori_loop(0, td, wait, 0, unroll=DMA_LOOP_UNROLL)


def moe_dispatch(h, dest, cap, td=256):
    n, d = h.shape
    return pl.pallas_call(
        _dispatch_kernel,
        grid_spec=pltpu.PrefetchScalarGridSpec(
            num_scalar_prefetch=1,
            grid=(n // td,),
            in_specs=[pl.BlockSpec((td, d), lambda i, dest: (i, 0)),
                      pl.BlockSpec(memory_space=pl.ANY)],
            out_specs=pl.BlockSpec(memory_space=pl.ANY),
            scratch_shapes=[pltpu.SemaphoreType.DMA(())]),
        out_shape=jax.ShapeDtypeStruct((cap, d), h.dtype),
        input_output_aliases={2: 0},
        compiler_params=_cparams(("arbitrary",)),
        name="moe_dispatch",
    )(dest, h, jnp.zeros((cap, d), h.dtype))


def _combine_kernel(dest_ref, x_ref, gate_ref, y_ref, o_ref, ya_sc, yb_sc, sem):
    tc = x_ref.shape[0]
    base = pl.program_id(0) * tc

    def start(r, carry):
        _row_copy(y_ref, dest_ref[2 * (base + r)], ya_sc, r, sem).start(priority=0)
        _row_copy(y_ref, dest_ref[2 * (base + r) + 1], yb_sc, r, sem).start(priority=1)
        return carry

    def wait(r, carry):
        _row_copy(y_ref, dest_ref[2 * (base + r)], ya_sc, r, sem).wait()
        _row_copy(y_ref, dest_ref[2 * (base + r) + 1], yb_sc, r, sem).wait()
        return carry

    lax.fori_loop(0, tc, start, 0, unroll=DMA_LOOP_UNROLL)
    lax.fori_loop(0, tc, wait, 0, unroll=DMA_LOOP_UNROLL)
    gates = gate_ref[...]
    o_ref[...] = x_ref[...] + (gates[:, 0:1] * ya_sc[...] + gates[:, 1:2] * yb_sc[...])


def moe_combine(x, gates, y, dest, tc=256):
    n, d = x.shape
    return pl.pallas_call(
        _combine_kernel,
        grid_spec=pltpu.PrefetchScalarGridSpec(
            num_scalar_prefetch=1,
            grid=(n // tc,),
            in_specs=[pl.BlockSpec((tc, d), lambda i, dest: (i, 0)),
                      pl.BlockSpec((tc, LANES), lambda i, dest: (i, 0)),
                      pl.BlockSpec(memory_space=pl.ANY)],
            out_specs=pl.BlockSpec((tc, d), lambda i, dest: (i, 0)),
            scratch_shapes=[pltpu.VMEM((tc, d), F32), pltpu.VMEM((tc, d), F32),
                            pltpu.SemaphoreType.DMA(())]),
        out_shape=jax.ShapeDtypeStruct((n, d), F32),
        compiler_params=_cparams(("arbitrary",)),
        name="moe_combine",
    )(dest, x, gates, y)


def moe_layer(x, gain, w_router, w_gate, w_up, w_down, first_expert, tm=512):
    n, d = x.shape
    wr = jnp.pad(w_router.astype(F32), ((0, 0), (0, LANES - N_EXPERTS)))
    h, info, gates, counts = moe_route(x, gain, jnp.stack(_split3(wr)[:2]))
    half = tm // 2
    counts = counts[0, :N_EXPERTS].astype(jnp.int32)
    padded = (counts + half - 1) // half * half
    pad_ends = jnp.cumsum(padded)
    pad_starts = pad_ends - padded
    experts, ranks = info[:, 0:2], info[:, 2:4]
    dest = (pad_starts[experts] + ranks).reshape(-1).astype(jnp.int32)
    n_halves = (2 * n) // half + N_EXPERTS
    assert n_halves % 2 == 0
    items = _expert_items(pad_ends, n_halves, half, first_expert)
    xs = moe_dispatch(h, dest, n_halves * half)
    y = grouped_swiglu(xs, items, items, w_gate, w_up, w_down, tm=tm, tf=1024, tm_down=tm)
    return moe_combine(x, gates, y, dest)


def _expert_items(pad_ends, n_halves, half, first_expert):
    n_blocks = n_halves // 2
    start = jnp.arange(n_halves, dtype=jnp.int32) * half
    used = start < pad_ends[-1]
    owner = jnp.minimum(jnp.sum(start[:, None] >= pad_ends[None, :], axis=1), N_EXPERTS - 1).astype(jnp.int32)
    e0, e1, u0, u1 = owner[0::2], owner[1::2], used[0::2], used[1::2]
    straddle = u0 & u1 & (e0 != e1)
    n_items = 1 + straddle.astype(jnp.int32)
    ends = jnp.cumsum(n_items)
    t = jnp.arange(n_blocks + N_EXPERTS, dtype=jnp.int32)
    blk = jnp.minimum(jnp.sum(ends[None, :] <= t[:, None], axis=1), n_blocks - 1).astype(jnp.int32)
    second = straddle[blk] & (t - (ends - n_items)[blk] == 1)
    whole = jnp.where(u0, jnp.where(u1, PART_FULL, PART_FIRST), PART_EMPTY)
    part = jnp.where(straddle[blk], jnp.where(second, PART_SECOND, PART_FIRST), whole[blk])
    part = jnp.where(t < ends[-1], part, PART_IDLE).astype(jnp.int32)
    expert = jnp.where(second, e1[blk], e0[blk]) + first_expert
    return blk, expert.astype(jnp.int32), part


def _project_weights(w):
    def cols(a, n):
        return w[:, a:a + n]

    qa, ka, va, qb = cols(0, 512), cols(512, 128), cols(640, 128), cols(768, 768)
    kbc, vbc, kbs, vbs, kbw, vbw = (cols(1536 + 128 * i, 128) for i in range(6))
    gb, qc, kc, vc, fc = cols(2304, 36), cols(2340, 768), cols(3108, 768), cols(3876, 768), cols(4644, 12)
    w_rows = jnp.concatenate([kc, ka, va, qa, kbc, vbc, kbs, kbw], axis=1).astype(BF16)
    q_fold = Q_SCALE * LOG2E
    w_t = jnp.concatenate([qb * q_fold, qc * q_fold, vc, vbs, vbw], axis=1).T.astype(BF16)
    tail = jnp.concatenate([gb, fc], axis=1)
    tail = jnp.pad(tail, ((0, 0), (0, LANES - tail.shape[1]))).astype(BF16)
    return w_rows, w_t, tail


def _chunk_blocks(cols, batch, seq):
    n_chunks = seq // CMP_STRIDE
    t = cols.reshape(batch, seq, B_KV_HEADS, HEAD_DIM).transpose(0, 2, 1, 3)
    t = t.reshape(batch, B_KV_HEADS, n_chunks, CMP_STRIDE * HEAD_DIM)
    nxt = jnp.concatenate([t[:, :, 1:], jnp.zeros_like(t[:, :, :1])], axis=2)
    return jnp.concatenate([t, nxt], axis=-1)


def _overlap_matrix_t(seq):
    n_chunks, n_slc = seq // CMP_STRIDE, seq // SEL_BLOCK
    c_start = np.arange(n_chunks) * CMP_STRIDE
    s_start = np.arange(n_slc) * SEL_BLOCK
    inter = np.maximum(np.minimum(c_start[:, None] + CMP_LEN, s_start[None, :] + SEL_BLOCK)
                       - np.maximum(c_start[:, None], s_start[None, :]), 0) / CMP_LEN
    inter[(seq - CMP_LEN) // CMP_STRIDE + 1:] = 0.0
    return jnp.asarray(inter.T, BF16)


def mixer_layer(x, batch, seq, w_in, w_out, norm_mix, mix_out_norm, sinks, cmp_pe, cmp_w1, cmp_w2, f_bias, norm_ffn):
    w_rows, w_t, w_tail = _project_weights(w_in)
    proj_r, proj_t, tail, tail_t = input_projection(x, norm_mix, w_rows, w_t, w_tail)
    o_a = swa_attention(proj_r, sinks.astype(F32), batch, seq)

    bias_row = jnp.zeros((1, LANES), F32).at[0, TAIL_FC:TAIL_FC + C_HEADS].set(f_bias.astype(F32))
    fox_ka, fox_qat = fox_augmentation(tail, bias_row, batch, seq)
    o_c = flash_attention_t("fox", proj_r, proj_t, fox_ka, fox_qat, batch, seq)

    flat = jnp.stack([_chunk_blocks(proj_r[:, OFF_KBC:OFF_KBC + KV_WIDTH], batch, seq),
                      _chunk_blocks(proj_r[:, OFF_VBC:OFF_VBC + KV_WIDTH], batch, seq)])
    pe_rows = jnp.broadcast_to(cmp_pe.reshape(2, 1, CMP_LEN * HEAD_DIM), (2, 8, CMP_LEN * HEAD_DIM)).astype(BF16)
    kvc, kvct = nsa_compress(flat, pe_rows, cmp_w1.astype(BF16), cmp_w2.astype(BF16))
    o_cmp, sel_t = nsa_compressed_attention(proj_t, kvc, kvct, _overlap_matrix_t(seq), tail_t, batch, seq)
    tk = 512
    pos_ka, pos_qat = _alibi_augmentation(seq, tk)
    o_slc = flash_attention_t("slc", proj_r, proj_t, pos_ka, pos_qat, batch, seq, gates_t=tail_t, sel_t=sel_t, tk=tk)
    o_win = flash_attention_t("win", proj_r, proj_t, pos_ka, pos_qat, batch, seq, gates_t=tail_t, tk=tk)
    return mix_out(o_a, o_cmp, o_slc, o_win, o_c, mix_out_norm, w_out.astype(BF16), x, norm_ffn)


def dense_layer(x, h, w_gate, w_up, w_down, index, tm=1024):
    n = x.shape[0]
    tm_down = 512
    return grouped_swiglu(h, _whole_block_items(n // tm, index), _whole_block_items(n // tm_down, index),
                          w_gate, w_up, w_down, residual=x, tm=tm, tm_down=tm_down)


def kernel(x, w_in, w_out, norm_mix, mix_out_norm, attn_sinks, nsa_cmp_pe, nsa_cmp_w1, nsa_cmp_w2, fox_f_bias,
           norm_ffn, ffn_w_gate, ffn_w_up, ffn_w_down, moe_router, moe_w_gate, moe_w_up, moe_w_down, norm_final):
    batch, seq, d = x.shape
    depth = w_in.shape[0]
    f = ffn_w_gate.shape[-1]
    moe_gate, moe_up = moe_w_gate.reshape(-1, d, f), moe_w_up.reshape(-1, d, f)
    moe_down = moe_w_down.reshape(-1, f, d)
    xf = x.reshape(batch * seq, d).astype(F32)
    for layer in range(depth):
        xf, hf = mixer_layer(xf, batch, seq, w_in[layer], w_out[layer], norm_mix[layer], mix_out_norm[layer],
                             attn_sinks[layer], nsa_cmp_pe[layer], nsa_cmp_w1[layer], nsa_cmp_w2[layer],
                             fox_f_bias[layer], norm_ffn[layer])
        i = layer // 2
        if layer % 2 == 0:
            xf = dense_layer(xf, hf, ffn_w_gate, ffn_w_up, ffn_w_down, i)
        else:
            xf = moe_layer(xf, norm_ffn[layer], moe_router[i], moe_gate, moe_up, moe_down, i * N_EXPERTS)
    return rmsnorm_rows(xf, norm_final, x.dtype).reshape(batch, seq, d)
```

```python
import functools

import jax
import jax.numpy as jnp
import numpy as np
from jax import lax
from jax.experimental import pallas as pl
from jax.experimental.pallas import tpu as pltpu

F32 = jnp.float32
BF16 = jnp.bfloat16

HEAD_DIM = 64
A_HEADS, A_KV_HEADS = 8, 2
B_HEADS, B_KV_HEADS = 12, 2
C_HEADS = 12
B_GROUP = B_HEADS // B_KV_HEADS
A_WIDTH, B_WIDTH, C_WIDTH = A_HEADS * HEAD_DIM, B_HEADS * HEAD_DIM, C_HEADS * HEAD_DIM
KV_WIDTH = 2 * HEAD_DIM
N_GATE_COLS = B_HEADS * 3
WINDOW_A = 128
WINDOW_B = 512
CMP_LEN, CMP_STRIDE = 32, 16
SEL_BLOCK, N_SELECT = 64, 16
N_EXPERTS = 8
RMS_EPS = 1e-6
NEG_INF = -1e30
M_INIT = -1e20
FORCE_BONUS = 1e4
LANES = 128
VMEM_LIMIT = 56 * 1024 * 1024
Q_SCALE = HEAD_DIM ** -0.5

OFF_KC, OFF_KA, OFF_VA, OFF_QA = 0, 768, 896, 1024
OFF_KBC, OFF_VBC, OFF_KBS, OFF_KBW = 1536, 1664, 1792, 1920
ROW_WIDTH = 2048
OFF_QBT, OFF_QCT, OFF_VCT, OFF_VBST, OFF_VBWT = 0, 768, 1536, 2304, 2432
T_WIDTH = 2560
TAIL_FC = N_GATE_COLS
ONE_LANE = LANES - 1
N_AUG = HEAD_DIM
Q_AUG_ROWS = 16
LOG2E = 1.4426950408889634
MASK_PEN = -1e30
VMEM_LIMIT_BIG = 60 * 1024 * 1024
DMA_LOOP_UNROLL = 8


def _alibi(n):
    return [float(2.0 ** (-8.0 * i / n)) for i in range(1, n + 1)]


SLOPES_A = _alibi(A_HEADS)
SLOPES_B = _alibi(B_HEADS)


def _cparams(sem, vmem=VMEM_LIMIT):
    return pltpu.CompilerParams(dimension_semantics=sem, vmem_limit_bytes=vmem)


def _dot(a, b):
    return jnp.dot(a, b, preferred_element_type=F32)


def _dot_nt(a, b):
    return lax.dot_general(a, b, (((1,), (1,)), ((), ())), preferred_element_type=F32)


def _split3(x):
    hi = x.astype(BF16)
    r1 = x - hi.astype(F32)
    mid = r1.astype(BF16)
    lo = (r1 - mid.astype(F32)).astype(BF16)
    return hi, mid, lo


def _rms(x, gain):
    return x * lax.rsqrt(jnp.mean(x * x, axis=-1, keepdims=True) + RMS_EPS) * gain


def _sigmoid(x):
    return 1.0 / (1.0 + jnp.exp(-x))


def _norm_kernel(x_ref, g_ref, o_ref):
    o_ref[...] = _rms(x_ref[...], g_ref[...]).astype(o_ref.dtype)


def rmsnorm_rows(x, gain, out_dtype, tm=512):
    n, d = x.shape
    return pl.pallas_call(
        _norm_kernel,
        grid=(n // tm,),
        in_specs=[pl.BlockSpec((tm, d), lambda i: (i, 0)), pl.BlockSpec((1, d), lambda i: (0, 0))],
        out_specs=pl.BlockSpec((tm, d), lambda i: (i, 0)),
        out_shape=jax.ShapeDtypeStruct((n, d), out_dtype),
        compiler_params=_cparams(("parallel",)),
        name="rmsnorm",
    )(x, gain.reshape(1, d))


def _projection_kernel(x_ref, g_ref, wr_ref, wt_ref, wtail_ref, wtailt_ref, pr_ref, pt_ref, tail_ref, tailt_ref, h_sc,
                       *, n_row_tiles, n_t_tiles):
    j = pl.program_id(1)

    @pl.when(j == 0)
    def _():
        h_sc[...] = _rms(x_ref[...], g_ref[...]).astype(BF16)
        tail_ref[...] = _dot(h_sc[...], wtail_ref[...])
        tailt_ref[...] = _dot_nt(wtailt_ref[...], h_sc[...])

    @pl.when((j >= 1) & (j <= n_row_tiles))
    def _():
        pr_ref[...] = _dot(h_sc[...], wr_ref[...]).astype(pr_ref.dtype)

    @pl.when(j > n_row_tiles)
    def _():
        pt_ref[...] = _dot_nt(wt_ref[...], h_sc[...]).astype(pt_ref.dtype)


def input_projection(x, gain, w_rows, w_t, w_tail, tm=1024, tn=512):
    n, d = x.shape
    n_row_tiles, n_t_tiles = w_rows.shape[1] // tn, w_t.shape[0] // tn

    def row_tile(j):
        return jnp.clip(j - 1, 0, n_row_tiles - 1)

    def t_tile(j):
        return jnp.clip(j - 1 - n_row_tiles, 0, n_t_tiles - 1)

    return pl.pallas_call(
        functools.partial(_projection_kernel, n_row_tiles=n_row_tiles, n_t_tiles=n_t_tiles),
        grid=(n // tm, n_row_tiles + n_t_tiles + 1),
        in_specs=[pl.BlockSpec((tm, d), lambda i, j: (i, 0)),
                  pl.BlockSpec((1, d), lambda i, j: (0, 0)),
                  pl.BlockSpec((d, tn), lambda i, j: (0, row_tile(j))),
                  pl.BlockSpec((tn, d), lambda i, j: (t_tile(j), 0)),
                  pl.BlockSpec((d, LANES), lambda i, j: (0, 0)),
                  pl.BlockSpec((LANES, d), lambda i, j: (0, 0))],
        out_specs=[pl.BlockSpec((tm, tn), lambda i, j: (i, row_tile(j))),
                   pl.BlockSpec((tn, tm), lambda i, j: (t_tile(j), i)),
                   pl.BlockSpec((tm, LANES), lambda i, j: (i, 0)),
                   pl.BlockSpec((LANES, tm), lambda i, j: (0, i))],
        out_shape=[jax.ShapeDtypeStruct((n, w_rows.shape[1]), BF16),
                   jax.ShapeDtypeStruct((w_t.shape[0], n), BF16),
                   jax.ShapeDtypeStruct((n, LANES), F32),
                   jax.ShapeDtypeStruct((LANES, n), F32)],
        scratch_shapes=[pltpu.VMEM((tm, d), BF16)],
        compiler_params=_cparams(("parallel", "arbitrary")),
        name="input_projection",
    )(x, gain.reshape(1, d), w_rows, w_t, w_tail, w_tail.T)


def _swa_kernel(sink_ref, q_ref, kp_ref, kc_ref, vp_ref, vc_ref, o_ref):
    n = pl.program_id(1)
    tq = q_ref.shape[0]
    q = q_ref[...]
    k = jnp.concatenate([kp_ref[...], kc_ref[...]], axis=0)
    v = jnp.concatenate([vp_ref[...], vc_ref[...]], axis=0)
    i = lax.broadcasted_iota(jnp.int32, (tq, 2 * tq), 0)
    j = lax.broadcasted_iota(jnp.int32, (tq, 2 * tq), 1)
    rel = tq + i - j
    mask = (rel >= 0) & (rel < WINDOW_A) & ((j >= tq) | (n > 0))
    relf = rel.astype(F32)
    group = A_HEADS // A_KV_HEADS
    for h in range(A_HEADS):
        kv = h // group
        qh = q[:, h * HEAD_DIM:(h + 1) * HEAD_DIM]
        kh = k[:, kv * HEAD_DIM:(kv + 1) * HEAD_DIM]
        vh = v[:, kv * HEAD_DIM:(kv + 1) * HEAD_DIM]
        s = _dot_nt(qh, kh) * Q_SCALE - SLOPES_A[h] * relf
        s = jnp.where(mask, s, NEG_INF)
        sink = sink_ref[h]
        m = jnp.maximum(jnp.max(s, axis=-1, keepdims=True), sink)
        p = jnp.exp(s - m)
        l = jnp.sum(p, axis=-1, keepdims=True) + jnp.exp(sink - m)
        o_ref[:, h * HEAD_DIM:(h + 1) * HEAD_DIM] = _dot(p.astype(BF16), vh) / l


def swa_attention(proj, sinks, batch, seq):
    tq = WINDOW_A
    nb = seq // tq
    kcol, vcol = OFF_KA // KV_WIDTH, OFF_VA // KV_WIDTH

    def cur(col):
        return pl.BlockSpec((tq, KV_WIDTH), lambda b, n: (b * nb + n, col))

    def prev(col):
        return pl.BlockSpec((tq, KV_WIDTH), lambda b, n: (b * nb + jnp.maximum(n - 1, 0), col))

    return pl.pallas_call(
        _swa_kernel,
        grid=(batch, nb),
        in_specs=[pl.BlockSpec(memory_space=pltpu.SMEM),
                  pl.BlockSpec((tq, A_WIDTH), lambda b, n: (b * nb + n, OFF_QA // A_WIDTH)),
                  prev(kcol), cur(kcol), prev(vcol), cur(vcol)],
        out_specs=pl.BlockSpec((tq, A_WIDTH), lambda b, n: (b * nb + n, 0)),
        out_shape=jax.ShapeDtypeStruct((batch * seq, A_WIDTH), F32),
        compiler_params=_cparams(("parallel", "parallel")),
        name="swa_attention",
    )(sinks, proj, proj, proj, proj, proj)


def _fox_aug_kernel(z_ref, b_ref, pk_ref, pq_ref, ka_ref, qa_ref, carry_sc):
    @pl.when(pl.program_id(1) == 0)
    def _():
        carry_sc[...] = jnp.zeros_like(carry_sc)

    z = z_ref[...] + b_ref[...]
    log_f = -(jnp.maximum(-z, 0.0) + jnp.log1p(jnp.exp(-jnp.abs(z))))
    ts = z.shape[0]
    r = lax.broadcasted_iota(jnp.int32, (ts, ts), 0)
    c = lax.broadcasted_iota(jnp.int32, (ts, ts), 1)
    tri = (c <= r).astype(BF16)
    hi, mid, lo = _split3(log_f)
    cum = _dot(tri, hi) + _dot(tri, mid) + _dot(tri, lo) + carry_sc[0:1, :]
    carry_sc[0:1, :] = cum[ts - 1:ts, :]
    hi, mid, lo = _split3(cum * LOG2E)
    lane = lax.broadcasted_iota(jnp.int32, (ts, LANES), 1)
    hi = jnp.where(lane == ONE_LANE, 1.0, hi).astype(BF16)
    ka_ref[...] = (_dot(hi, pk_ref[0]) + _dot(mid, pk_ref[1]) + _dot(lo, pk_ref[2])).astype(BF16)
    qa_ref[...] = (_dot_nt(pq_ref[0], hi) + _dot_nt(pq_ref[1], mid) + _dot_nt(pq_ref[2], lo)).astype(BF16)


def _aug_lane(h):
    return HEAD_DIM if h % 2 == 0 else 0


def _fox_placement():
    pk = np.zeros((3, LANES, C_HEADS * LANES), np.float32)
    pq = np.zeros((3, C_HEADS * Q_AUG_ROWS, LANES), np.float32)
    for h in range(C_HEADS):
        src, kbase, qbase = TAIL_FC + h, h * LANES + _aug_lane(h), h * Q_AUG_ROWS
        for piece in range(3):
            pk[piece, src, kbase + piece] = -1.0
            pq[piece, qbase + 3 + piece, src] = 1.0
            pk[0, ONE_LANE, kbase + 3 + piece] = 1.0
            pq[0, qbase + piece, ONE_LANE] = 1.0
    return jnp.asarray(pk, BF16), jnp.asarray(pq, BF16)


def fox_augmentation(tail, bias_row, batch, seq, ts=512):
    nt = seq // ts
    pk, pq = _fox_placement()
    kw, qw = C_HEADS * LANES, C_HEADS * Q_AUG_ROWS
    return pl.pallas_call(
        _fox_aug_kernel,
        grid=(batch, nt),
        in_specs=[pl.BlockSpec((ts, LANES), lambda b, i: (b * nt + i, 0)),
                  pl.BlockSpec((1, LANES), lambda b, i: (0, 0)),
                  pl.BlockSpec((3, LANES, kw), lambda b, i: (0, 0, 0)),
                  pl.BlockSpec((3, qw, LANES), lambda b, i: (0, 0, 0))],
        out_specs=[pl.BlockSpec((ts, kw), lambda b, i: (b * nt + i, 0)),
                   pl.BlockSpec((qw, ts), lambda b, i: (0, b * nt + i))],
        out_shape=[jax.ShapeDtypeStruct((batch * seq, kw), BF16),
                   jax.ShapeDtypeStruct((qw, batch * seq), BF16)],
        scratch_shapes=[pltpu.VMEM((8, LANES), F32)],
        compiler_params=_cparams(("parallel", "arbitrary")),
        name="fox_augmentation",
    )(tail, bias_row, pk, pq)


def _alibi_augmentation(seq, tk):
    pos = np.arange(seq)
    ka = np.zeros((seq, LANES), np.float32)
    for base in (0, HEAD_DIM):
        ka[:, base:base + 3] = 1.0
        ka[:, base + 3:base + 6] = ((pos >> 8) << 8)[:, None]
        ka[:, base + 6:base + 9] = (pos & 255)[:, None]
        ka[pos, base + Q_AUG_ROWS + (pos % tk) // SEL_BLOCK] = 1.0
    def split3_np(x):
        pieces, rest = [], x.astype(np.float32)
        for _ in range(3):
            piece = rest.astype(jnp.bfloat16).astype(np.float32)
            pieces.append(piece)
            rest = rest - piece
        return pieces

    slopes = (np.asarray(SLOPES_B, np.float32) * np.float32(LOG2E)).astype(np.float32)
    st = split3_np(-(slopes[:, None] * pos.astype(np.float32)[None, :]))
    sl = split3_np(slopes)
    qa = np.zeros((B_HEADS, Q_AUG_ROWS, seq), np.float32)
    for piece in range(3):
        qa[:, piece, :] = st[piece]
        qa[:, 3 + piece, :] = sl[piece][:, None]
        qa[:, 6 + piece, :] = sl[piece][:, None]
    return jnp.asarray(ka, BF16), jnp.asarray(qa.reshape(B_HEADS * Q_AUG_ROWS, seq), BF16)


def _flash_t_kernel(*refs, mode, tq, tk, n_steps, seq):
    pen_sc = None
    if mode == "slc":
        (iq_ref, ik_ref, live_ref, k_ref, ka_ref, qt_ref, qat_ref, vt_ref, gate_ref, selt_ref, o_ref,
         m_sc, l_sc, acc_sc, pen_sc) = refs
    elif mode == "win":
        k_ref, ka_ref, qt_ref, qat_ref, vt_ref, gate_ref, o_ref, m_sc, l_sc, acc_sc = refs
    else:
        iq_ref, ik_ref, k_ref, ka_ref, qt_ref, qat_ref, vt_ref, o_ref, m_sc, l_sc, acc_sc = refs
    n_heads = qt_ref.shape[0] // HEAD_DIM
    group = 1 if mode == "fox" else B_GROUP
    blocks_per_tile = tk // SEL_BLOCK
    if mode == "win":
        iq, j = pl.program_id(1), pl.program_id(2)
        ik = iq - (n_steps - 1) + j
        first, last = j == 0, j == n_steps - 1
        active = ik >= 0
    else:
        iq, ik = iq_ref[pl.program_id(1)], ik_ref[pl.program_id(1)]
        first, last = ik == 0, ik == (iq * tq + tq - 1) // tk
        active = True

    @pl.when(first)
    def _():
        m_sc[...] = jnp.full_like(m_sc, M_INIT)
        l_sc[...] = jnp.zeros_like(l_sc)
        acc_sc[...] = jnp.zeros_like(acc_sc)
        if mode == "slc":
            pen_sc[...] = (1.0 - selt_ref[0].astype(F32)) * MASK_PEN

    def step(masked, heads, dead=None):
        hk, hq = tk // 2, tq // 2
        rows_a, rows_b, lanes_b = {None: (slice(0, tk), None, None),
                                   "late": (slice(0, hk), slice(hk, tk), slice(hq, tq)),
                                   "early": (slice(hk, tk), slice(0, hk), slice(0, hq))}[dead]

        def widen(x, fill):
            pad = jnp.full(x.shape, fill, x.dtype)
            return jnp.concatenate([pad, x] if dead == "late" else [x, pad], axis=1)

        ok = None
        if masked:
            rel = ((iq * tq + lax.broadcasted_iota(jnp.int32, (tk, tq), 1))
                   - (ik * tk + lax.broadcasted_iota(jnp.int32, (tk, tq), 0)))
            ok = rel >= 0
            if mode == "win":
                ok = ok & (rel < WINDOW_B)
        lane = lax.broadcasted_iota(jnp.int32, (tk, LANES), 1)
        ones_rows = jnp.ones((Q_AUG_ROWS, tk), BF16)
        per_kv = {}

        def kv_operands(kv):
            if kv not in per_kv:
                pair = slice((kv // 2) * LANES, (kv // 2 + 1) * LANES)
                aug = ka_ref[:, kv * LANES:(kv + 1) * LANES] if mode == "fox" else ka_ref[...]
                own = (lane < HEAD_DIM) if kv % 2 == 0 else (lane >= HEAD_DIM)
                ka = jnp.where(own, k_ref[:, pair], aug)
                v_aug = jnp.concatenate([vt_ref[kv * HEAD_DIM:(kv + 1) * HEAD_DIM, :], ones_rows], axis=0)
                pen = None
                if mode == "slc":
                    pen8 = pen_sc[kv, pl.ds(pl.multiple_of(ik * blocks_per_tile, blocks_per_tile), blocks_per_tile), :]
                    pen = jnp.concatenate(
                        [pen8, jnp.zeros((Q_AUG_ROWS - blocks_per_tile, tq), F32)], axis=0).astype(BF16)
                per_kv[kv] = (ka, v_aug, pen)
            return per_kv[kv]

        def scores(h):
            kv = h // group
            ka, _, pen = kv_operands(kv)
            aug_rows = [qat_ref[h * Q_AUG_ROWS:(h + 1) * Q_AUG_ROWS, :]]
            if mode == "slc":
                aug_rows.append(pen)
            aug_rows.append(jnp.zeros((N_AUG - Q_AUG_ROWS * len(aug_rows), tq), BF16))
            halves = [[qt_ref[h * HEAD_DIM:(h + 1) * HEAD_DIM, :]], aug_rows]
            qa = jnp.concatenate(sum(halves if kv % 2 == 0 else halves[::-1], []), axis=0)
            s_a = _dot(ka[rows_a, :], qa)
            s_b = None if dead is None else _dot(ka[rows_b, :], qa[:, lanes_b])
            if ok is not None:
                s_a = jnp.where(ok[rows_a, :], s_a, NEG_INF)
                s_b = None if dead is None else jnp.where(ok[rows_b, lanes_b], s_b, NEG_INF)
            return s_a, s_b

        def probabilities(h, s):
            s_a, s_b = s
            m_prev = m_sc[h:h + 1, :]
            m_new = jnp.maximum(m_prev, jnp.max(s_a, axis=0, keepdims=True))
            if dead is not None:
                m_new = jnp.maximum(m_new, widen(jnp.max(s_b, axis=0, keepdims=True), NEG_INF))
            m_sc[h:h + 1, :] = m_new
            p_a = jnp.exp2(s_a - m_new).astype(BF16)
            p_b = None if dead is None else jnp.exp2(s_b - m_new[:, lanes_b]).astype(BF16)
            return (p_a, p_b), jnp.exp2(m_prev - m_new)

        def accumulate(h, p, alpha):
            p_a, p_b = p
            rows = slice(h * HEAD_DIM, (h + 1) * HEAD_DIM)
            v_aug = kv_operands(h // group)[1]
            pv = _dot(v_aug[:, rows_a], p_a)
            if dead is not None:
                pv = pv + widen(_dot(v_aug[:, rows_b], p_b), 0.0)
            l_sc[h:h + 1, :] = alpha * l_sc[h:h + 1, :] + pv[HEAD_DIM:HEAD_DIM + 1, :]
            acc_sc[rows, :] = alpha * acc_sc[rows, :] + pv[0:HEAD_DIM, :]

        s_cur, pending = scores(heads[0]), None
        for i, h in enumerate(heads):
            s_next = scores(heads[i + 1]) if i + 1 < len(heads) else None
            p_alpha = probabilities(h, s_cur)
            if pending is not None:
                accumulate(*pending)
            pending = (h,) + p_alpha
            s_cur = s_next
        accumulate(*pending)

    all_heads = list(range(n_heads))
    on_diagonal = ik == iq
    if mode == "win":
        pl.when(active & jnp.logical_not(on_diagonal))(lambda: step(True, all_heads, "early"))
        pl.when(on_diagonal)(lambda: step(True, all_heads, "late"))
    elif mode == "fox":
        pl.when(on_diagonal)(lambda: step(True, all_heads, "late"))
        pl.when(jnp.logical_not(on_diagonal))(lambda: step(False, all_heads))
    else:
        tile = (pl.program_id(0) * (seq // tq) + iq) * (seq // tk) + ik
        for kv in range(B_KV_HEADS):
            heads = all_heads[kv * group:(kv + 1) * group]
            live = live_ref[tile * B_KV_HEADS + kv] > 0
            pl.when(live & on_diagonal)(functools.partial(step, True, heads, "late"))
            pl.when(live & jnp.logical_not(on_diagonal))(functools.partial(step, False, heads))

    @pl.when(last)
    def _():
        branch = {"fox": None, "slc": 1, "win": 2}[mode]
        for pair in range(n_heads // 2):
            rows = slice(pair * LANES, (pair + 1) * LANES)
            heads = (2 * pair, 2 * pair + 1)
            denom = jnp.concatenate(
                [jnp.broadcast_to(l_sc[h:h + 1, :], (HEAD_DIM, tq)) for h in heads], axis=0)
            out = acc_sc[rows, :] / denom
            if branch is not None:
                out = out * jnp.concatenate(
                    [jnp.broadcast_to(_sigmoid(gate_ref[3 * h + branch:3 * h + branch + 1, :]), (HEAD_DIM, tq))
                     for h in heads], axis=0)
            o_ref[:, rows] = out.T


def flash_attention_t(mode, proj_r, proj_t, k_aug, q_aug_t, batch, seq, gates_t=None, sel_t=None, tq=512, tk=512):
    nq, nk = seq // tq, seq // tk
    assert tq == tk
    if mode == "win":
        assert WINDOW_B == tk
        n_steps = WINDOW_B // tk + 1
        grid = (batch, nq, n_steps)
        prefetch = []

        def tiles(iq, j):
            return iq, jnp.maximum(iq - (n_steps - 1) + j, 0)
    else:
        n_steps = nk
        pairs = [(iq, ik) for iq in range(nq) for ik in range((iq * tq + tq - 1) // tk + 1)]
        grid = (batch, len(pairs))
        prefetch = [jnp.asarray([p[0] for p in pairs], jnp.int32), jnp.asarray([p[1] for p in pairs], jnp.int32)]

        def tiles(p, iq_tab, ik_tab, *_):
            return iq_tab[p], ik_tab[p]

    def kv_rows(col):
        return lambda b, *g: (b * nk + tiles(*g)[1], col)

    def kv_cols(row):
        return lambda b, *g: (row, b * nk + tiles(*g)[1])

    def q_cols(row, per_batch=True):
        return lambda b, *g: (row, (b * nq if per_batch else 0) + tiles(*g)[0])

    if mode == "fox":
        in_specs = [pl.BlockSpec((tk, C_WIDTH), kv_rows(OFF_KC // C_WIDTH)),
                    pl.BlockSpec((tk, C_HEADS * LANES), kv_rows(0)),
                    pl.BlockSpec((C_WIDTH, tq), q_cols(OFF_QCT // C_WIDTH)),
                    pl.BlockSpec((C_HEADS * Q_AUG_ROWS, tq), q_cols(0)),
                    pl.BlockSpec((C_WIDTH, tk), kv_cols(OFF_VCT // C_WIDTH))]
    else:
        kcol = (OFF_KBS if mode == "slc" else OFF_KBW) // KV_WIDTH
        vrow = (OFF_VBST if mode == "slc" else OFF_VBWT) // KV_WIDTH
        in_specs = [pl.BlockSpec((tk, KV_WIDTH), kv_rows(kcol)),
                    pl.BlockSpec((tk, LANES), lambda b, *g: (tiles(*g)[1], 0)),
                    pl.BlockSpec((B_WIDTH, tq), q_cols(OFF_QBT // B_WIDTH)),
                    pl.BlockSpec((B_HEADS * Q_AUG_ROWS, tq), q_cols(0, per_batch=False)),
                    pl.BlockSpec((KV_WIDTH, tk), kv_cols(vrow))]
    args = [proj_r, k_aug, proj_t, q_aug_t, proj_t]
    width = C_WIDTH if mode == "fox" else B_WIDTH
    n_heads = width // HEAD_DIM
    scratch = [pltpu.VMEM((16, tq), F32), pltpu.VMEM((16, tq), F32), pltpu.VMEM((n_heads * HEAD_DIM, tq), F32)]
    if mode != "fox":
        in_specs.append(pl.BlockSpec((LANES, tq), q_cols(0)))
        args.append(gates_t)
    if mode == "slc":
        n_slc = seq // SEL_BLOCK
        per_tile = tk // SEL_BLOCK
        assert per_tile <= Q_AUG_ROWS and tk % SEL_BLOCK == 0
        in_specs.append(pl.BlockSpec((1, B_KV_HEADS, n_slc, tq), lambda b, *g: (b, 0, 0, tiles(*g)[0])))
        args.append(sel_t)
        scratch.append(pltpu.VMEM((B_KV_HEADS, n_slc, tq), F32))
        live = sel_t.reshape(batch, B_KV_HEADS, nk, per_tile, nq, tq).max(axis=(3, 5)) > 0
        prefetch.append(live.transpose(0, 3, 2, 1).reshape(-1).astype(jnp.int32))
    return pl.pallas_call(
        functools.partial(_flash_t_kernel, mode=mode, tq=tq, tk=tk, n_steps=n_steps, seq=seq),
        grid_spec=pltpu.PrefetchScalarGridSpec(
            num_scalar_prefetch=len(prefetch),
            grid=grid,
            in_specs=in_specs,
            out_specs=pl.BlockSpec((tq, width), lambda b, *g: (b * nq + tiles(*g)[0], 0)),
            scratch_shapes=scratch),
        out_shape=jax.ShapeDtypeStruct((batch * seq, width), F32),
        compiler_params=_cparams(("parallel",) * (len(grid) - 1) + ("arbitrary",)),
        name=mode + "_attention",
    )(*prefetch, *args)


def _compress_kernel(t_ref, pe_ref, w1_ref, w2_ref, w2t_ref, o_ref, ot_ref):
    w1 = w1_ref[0]
    hid = _dot(t_ref[0, 0, 0], w1) + _dot(pe_ref[0], w1)[0:1, :]
    act = (hid * _sigmoid(hid)).astype(BF16)
    o_ref[0, 0, 0] = _dot(act, w2_ref[0]).astype(o_ref.dtype)
    ot_ref[0, 0, 0] = _dot_nt(w2t_ref[0], act).astype(ot_ref.dtype)


def nsa_compress(flat, pe_rows, w1, w2):
    _, batch, n_kv, n_chunks, width = flat.shape
    hidden = w1.shape[-1]
    return pl.pallas_call(
        _compress_kernel,
        grid=(2, batch, n_kv),
        in_specs=[pl.BlockSpec((1, 1, 1, n_chunks, width), lambda s, b, h: (s, b, h, 0, 0)),
                  pl.BlockSpec((1, 8, width), lambda s, b, h: (s, 0, 0)),
                  pl.BlockSpec((1, width, hidden), lambda s, b, h: (s, 0, 0)),
                  pl.BlockSpec((1, hidden, HEAD_DIM), lambda s, b, h: (s, 0, 0)),
                  pl.BlockSpec((1, HEAD_DIM, hidden), lambda s, b, h: (s, 0, 0))],
        out_specs=[pl.BlockSpec((1, 1, 1, n_chunks, HEAD_DIM), lambda s, b, h: (s, b, h, 0, 0)),
                   pl.BlockSpec((1, 1, 1, HEAD_DIM, n_chunks), lambda s, b, h: (s, b, h, 0, 0))],
        out_shape=[jax.ShapeDtypeStruct((2, batch, n_kv, n_chunks, HEAD_DIM), BF16),
                   jax.ShapeDtypeStruct((2, batch, n_kv, HEAD_DIM, n_chunks), BF16)],
        compiler_params=_cparams(("parallel", "parallel", "parallel")),
        name="nsa_compress",
    )(flat, pe_rows, w1, w2, jnp.swapaxes(w2, 1, 2))


def _cmp_attn_t_kernel(qt_ref, kc_ref, vct_ref, inter_ref, gate_ref, o_ref, selt_ref, ot_sc, *, tq):
    iq = pl.program_id(1)
    n_chunks = kc_ref.shape[3]
    n_slc = inter_ref.shape[0]
    t = iq * tq + lax.broadcasted_iota(jnp.int32, (n_chunks, tq), 1)
    n = lax.broadcasted_iota(jnp.int32, (n_chunks, tq), 0)
    rel = t - (n * CMP_STRIDE + CMP_LEN - 1)
    mask = rel >= 0
    relf = rel.astype(F32)
    t_s = iq * tq + lax.broadcasted_iota(jnp.int32, (n_slc, tq), 1)
    jj = lax.broadcasted_iota(jnp.int32, (n_slc, tq), 0)
    cur = t_s >> (SEL_BLOCK.bit_length() - 1)
    valid = jj * SEL_BLOCK <= t_s
    forced = (jj == 0) | (jj == cur) | (jj == cur - 1)
    jf = jj.astype(F32)
    inter = inter_ref[...]
    def raw_scores(hd):
        return _dot(kc_ref[0, 0, hd // B_GROUP], qt_ref[hd * HEAD_DIM:(hd + 1) * HEAD_DIM, :])

    raw_next = raw_scores(0)
    for h in range(B_KV_HEADS):
        vct = vct_ref[0, 0, h]
        p_sum = jnp.zeros((n_chunks, tq), F32)
        for g in range(B_GROUP):
            hd = h * B_GROUP + g
            rows = slice(hd * HEAD_DIM, (hd + 1) * HEAD_DIM)
            raw, raw_next = raw_next, (raw_scores(hd + 1) if hd + 1 < B_HEADS else None)
            s = jnp.where(mask, raw - (SLOPES_B[hd] * LOG2E) * relf, NEG_INF)
            m = jnp.max(s, axis=0, keepdims=True)
            e = jnp.where(mask, jnp.exp2(s - m), 0.0)
            l = jnp.sum(e, axis=0, keepdims=True)
            p = e / jnp.where(l > 0.0, l, 1.0)
            p_sum = p_sum + p
            ot_sc[rows, :] = _dot(vct, p.astype(BF16)) * _sigmoid(gate_ref[3 * hd:3 * hd + 1, :])
        hi, mid, lo = _split3(p_sum)
        importance = _dot(inter, hi) + _dot(inter, mid) + _dot(inter, lo)
        score = jnp.where(valid, importance + jnp.where(forced, FORCE_BONUS, 0.0), NEG_INF)
        chosen = jnp.zeros((n_slc, tq), F32)
        for _ in range(min(N_SELECT, n_slc)):
            best = jnp.max(score, axis=0, keepdims=True)
            first = jnp.min(jnp.where(score == best, jf, float(n_slc)), axis=0, keepdims=True)
            pick = jf == first
            chosen = jnp.where(pick, 1.0, chosen)
            score = jnp.where(pick, -jnp.inf, score)
        selt_ref[0, h] = jnp.where(valid, chosen, 0.0).astype(selt_ref.dtype)
    for pair in range(B_HEADS // 2):
        rows = slice(pair * LANES, (pair + 1) * LANES)
        o_ref[:, rows] = ot_sc[rows, :].T


def nsa_compressed_attention(proj_t, kvc, kvct, inter_t, gates_t, batch, seq, tq=512):
    nq = seq // tq
    n_chunks = kvc.shape[3]
    n_slc = seq // SEL_BLOCK
    return pl.pallas_call(
        functools.partial(_cmp_attn_t_kernel, tq=tq),
        grid=(batch, nq),
        in_specs=[pl.BlockSpec((B_WIDTH, tq), lambda b, i: (OFF_QBT // B_WIDTH, b * nq + i)),
                  pl.BlockSpec((1, 1, B_KV_HEADS, n_chunks, HEAD_DIM), lambda b, i: (0, b, 0, 0, 0)),
                  pl.BlockSpec((1, 1, B_KV_HEADS, HEAD_DIM, n_chunks), lambda b, i: (1, b, 0, 0, 0)),
                  pl.BlockSpec((n_slc, n_chunks), lambda b, i: (0, 0)),
                  pl.BlockSpec((LANES, tq), lambda b, i: (0, b * nq + i))],
        out_specs=[pl.BlockSpec((tq, B_WIDTH), lambda b, i: (b * nq + i, 0)),
                   pl.BlockSpec((1, B_KV_HEADS, n_slc, tq), lambda b, i: (b, 0, 0, i))],
        out_shape=[jax.ShapeDtypeStruct((batch * seq, B_WIDTH), F32),
                   jax.ShapeDtypeStruct((batch, B_KV_HEADS, n_slc, seq), BF16)],
        scratch_shapes=[pltpu.VMEM((B_WIDTH, tq), F32)],
        compiler_params=_cparams(("parallel", "parallel")),
        name="nsa_cmp_attention",
    )(proj_t, kvc, kvct, inter_t, gates_t)


def _mix_out_kernel(oa_ref, ocmp_ref, oslc_ref, owin_ref, oc_ref, g_ref, w_ref, x_ref, gn_ref, o_ref, h_ref):
    o_b = ocmp_ref[...] + oslc_ref[...] + owin_ref[...]
    b0, c0 = A_WIDTH, A_WIDTH + B_WIDTH
    mixed = jnp.concatenate([_rms(oa_ref[...], g_ref[:, 0:b0]).astype(BF16),
                             _rms(o_b, g_ref[:, b0:c0]).astype(BF16),
                             _rms(oc_ref[...], g_ref[:, c0:]).astype(BF16)], axis=1)
    x_new = x_ref[...] + _dot(mixed, w_ref[...])
    o_ref[...] = x_new
    h_ref[...] = _rms(x_new, gn_ref[...]).astype(h_ref.dtype)


def mix_out(o_a, o_cmp, o_slc, o_win, o_c, gain, w_out, x, next_gain, tm=256):
    n, d = x.shape
    width = w_out.shape[0]

    def rows(w):
        return pl.BlockSpec((tm, w), lambda i: (i, 0))

    return pl.pallas_call(
        _mix_out_kernel,
        grid=(n // tm,),
        in_specs=[rows(A_WIDTH), rows(B_WIDTH), rows(B_WIDTH), rows(B_WIDTH), rows(C_WIDTH),
                  pl.BlockSpec((1, width), lambda i: (0, 0)),
                  pl.BlockSpec((width, d), lambda i: (0, 0)),
                  rows(d),
                  pl.BlockSpec((1, d), lambda i: (0, 0))],
        out_specs=[rows(d), rows(d)],
        out_shape=[jax.ShapeDtypeStruct((n, d), F32), jax.ShapeDtypeStruct((n, d), BF16)],
        compiler_params=_cparams(("parallel",)),
        name="mix_out",
    )(o_a, o_cmp, o_slc, o_win, o_c, gain.reshape(1, width), w_out, x, next_gain.reshape(1, d))


PART_FULL, PART_FIRST, PART_SECOND, PART_EMPTY, PART_IDLE = range(5)


def _stream_weights(exp_ref, nxt_ref, wrap_ref, weights, stages, casts, sems, run_sc):
    j, t = pl.program_id(0), pl.program_id(1)
    tile = stages[0].shape[2]

    def copies(expert, sweep, slot):
        cols = pl.ds(pl.multiple_of(sweep * tile, tile), tile)
        return [pltpu.make_async_copy(w.at[expert, :, cols], st.at[slot], sems.at[slot])
                for w, st in zip(weights, stages)]

    @pl.when((j == 0) & (t == 0))
    def _():
        run_sc[0] = 0
        for c in copies(exp_ref[0], 0, 0):
            c.start()

    @pl.when((t == 0) | (exp_ref[t] != exp_ref[jnp.maximum(t - 1, 0)]))
    def _():
        slot = lax.rem(run_sc[0], 2)
        for c in copies(exp_ref[t], j, slot):
            c.wait()
        wraps = wrap_ref[t] == 1

        @pl.when(jnp.logical_not(wraps & (j == pl.num_programs(0) - 1)))
        def _():
            for c in copies(nxt_ref[t], j + wraps.astype(jnp.int32), 1 - slot):
                c.start()

        for st, dst in zip(stages, casts):
            dst[...] = st[slot].astype(BF16)
        run_sc[0] = run_sc[0] + 1


def _with_run_links(items):
    blk, expert, part = items
    n = expert.shape[0]
    t = jnp.arange(n, dtype=jnp.int32)
    starts = jnp.concatenate([jnp.ones((1,), jnp.bool_), expert[1:] != expert[:-1]])
    following = jnp.min(jnp.where(starts[None, :] & (t[None, :] > t[:, None]), t[None, :], n), axis=1)
    wrap = following == n
    nxt = jnp.where(wrap, expert[0], expert[jnp.minimum(following, n - 1)])
    return blk, expert, part, nxt.astype(jnp.int32), wrap.astype(jnp.int32)


def _for_each_part(part, o_ref, compute):
    half = o_ref.shape[0] // 2

    @pl.when(part == PART_FULL)
    def _():
        o_ref[...] = compute(slice(None))

    @pl.when(part == PART_FIRST)
    def _():
        o_ref[0:half, :] = compute(slice(0, half))
        o_ref[half:, :] = jnp.zeros((half, o_ref.shape[1]), o_ref.dtype)

    @pl.when(part == PART_SECOND)
    def _():
        o_ref[half:, :] = compute(slice(half, 2 * half))

    @pl.when(part == PART_EMPTY)
    def _():
        o_ref[...] = jnp.zeros_like(o_ref)


def _ffn_up_kernel(blk_ref, exp_ref, part_ref, nxt_ref, wrap_ref, x_ref, wg_ref, wu_ref, o_ref,
                   wg_stage, wu_stage, wg_sc, wu_sc, sems, run_sc):
    _stream_weights(exp_ref, nxt_ref, wrap_ref, (wg_ref, wu_ref), (wg_stage, wu_stage), (wg_sc, wu_sc), sems, run_sc)

    def compute(rows):
        if x_ref.dtype == jnp.int32:
            c = x_ref.shape[1]
            x_lo, x_hi = _unpack_bf16_pairs(x_ref[rows, :])
            gate = _dot(x_lo, wg_sc[0:c, :]) + _dot(x_hi, wg_sc[c:, :])
            up = _dot(x_lo, wu_sc[0:c, :]) + _dot(x_hi, wu_sc[c:, :])
        else:
            x = x_ref[rows, :]
            gate = _dot(x, wg_sc[...])
            up = _dot(x, wu_sc[...])
        return (gate * _sigmoid(gate) * up).astype(o_ref.dtype)

    _for_each_part(part_ref[pl.program_id(1)], o_ref, compute)


def _ffn_down_kernel(blk_ref, exp_ref, part_ref, nxt_ref, wrap_ref, h_ref, wd_ref, *rest):
    o_ref, wd_stage, wd_sc, sems, run_sc = rest[-5:]
    _stream_weights(exp_ref, nxt_ref, wrap_ref, (wd_ref,), (wd_stage,), (wd_sc,), sems, run_sc)

    def compute(rows):
        y = _dot(h_ref[rows, :], wd_sc[...])
        return y + rest[0][rows, :] if len(rest) == 6 else y

    _for_each_part(part_ref[pl.program_id(1)], o_ref, compute)


def _whole_block_items(n_blocks, expert):
    return (jnp.arange(n_blocks, dtype=jnp.int32), jnp.full((n_blocks,), expert, jnp.int32),
            jnp.full((n_blocks,), PART_FULL, jnp.int32))


def grouped_swiglu(xs, items_up, items_down, w_gate, w_up, w_down, residual=None, tm=512, tf=512, tm_down=512,
                   tn=512):
    rows, x_width = xs.shape
    _, d, f = w_gate.shape
    hbm = pl.BlockSpec(memory_space=pl.ANY)

    def weight_scratch(k, tile, n_weights):
        return ([pltpu.VMEM((2, k, tile), F32)] * n_weights + [pltpu.VMEM((k, tile), BF16)] * n_weights
                + [pltpu.SemaphoreType.DMA((2,)), pltpu.SMEM((1,), jnp.int32)])

    hidden = pl.pallas_call(
        _ffn_up_kernel,
        grid_spec=pltpu.PrefetchScalarGridSpec(
            num_scalar_prefetch=5,
            grid=(f // tf, items_up[0].shape[0]),
            in_specs=[pl.BlockSpec((tm, x_width), lambda j, t, blk, *_: (blk[t], 0)), hbm, hbm],
            out_specs=pl.BlockSpec((tm, tf), lambda j, t, blk, *_: (blk[t], j)),
            scratch_shapes=weight_scratch(d, tf, 2)),
        out_shape=jax.ShapeDtypeStruct((rows, f), BF16),
        compiler_params=_cparams(("arbitrary", "arbitrary"), VMEM_LIMIT_BIG),
        name="ffn_up",
    )(*_with_run_links(items_up), xs, w_gate, w_up)

    in_specs = [pl.BlockSpec((tm_down, f), lambda j, t, blk, *_: (blk[t], 0)), hbm]
    args = [hidden, w_down]
    if residual is not None:
        in_specs.append(pl.BlockSpec((tm_down, tn), lambda j, t, blk, *_: (blk[t], j)))
        args.append(residual)
    return pl.pallas_call(
        _ffn_down_kernel,
        grid_spec=pltpu.PrefetchScalarGridSpec(
            num_scalar_prefetch=5,
            grid=(d // tn, items_down[0].shape[0]),
            in_specs=in_specs,
            out_specs=pl.BlockSpec((tm_down, tn), lambda j, t, blk, *_: (blk[t], j)),
            scratch_shapes=weight_scratch(f, tn, 1)),
        out_shape=jax.ShapeDtypeStruct((rows, d), F32),
        compiler_params=_cparams(("arbitrary", "arbitrary"), VMEM_LIMIT_BIG),
        name="ffn_down",
    )(*_with_run_links(items_down), *args)


def _router_kernel(x_ref, g_ref, wr_ref, h_ref, ri_ref, rf_ref, cnt_ref, carry_sc):
    @pl.when(pl.program_id(0) == 0)
    def _():
        carry_sc[...] = jnp.zeros_like(carry_sc)

    tm = x_ref.shape[0]
    h = _rms(x_ref[...], g_ref[...])
    h_ref[...] = _pack_bf16_pairs(h)
    h1, h2, _ = _split3(h)
    w1, w2 = wr_ref[0], wr_ref[1]
    logits = _dot(h1, w1) + (_dot(h1, w2) + _dot(h2, w1))
    lane_i = lax.broadcasted_iota(jnp.int32, (tm, LANES), 1)
    lane = lane_i.astype(F32)
    logits = jnp.where(lane_i < N_EXPERTS, logits, -jnp.inf)
    v1 = jnp.max(logits, axis=-1, keepdims=True)
    e1 = jnp.min(jnp.where(logits == v1, lane, float(LANES)), axis=-1, keepdims=True)
    rest = jnp.where(lane == e1, -jnp.inf, logits)
    v2 = jnp.max(rest, axis=-1, keepdims=True)
    e2 = jnp.min(jnp.where(rest == v2, lane, float(LANES)), axis=-1, keepdims=True)
    z = jnp.exp(v2 - v1)
    g1 = 1.0 / (1.0 + z)
    g2 = z / (1.0 + z)
    chosen = (lane == e1) | (lane == e2)
    onehot = jnp.where(chosen, 1.0, 0.0)
    r = lax.broadcasted_iota(jnp.int32, (tm, tm), 0)
    c = lax.broadcasted_iota(jnp.int32, (tm, tm), 1)
    before = _dot((c < r).astype(BF16), onehot.astype(BF16)) + carry_sc[0:1, :]
    r1 = jnp.sum(jnp.where(lane == e1, before, 0.0), axis=-1, keepdims=True)
    r2 = jnp.sum(jnp.where(lane == e2, before, 0.0), axis=-1, keepdims=True)
    carry = carry_sc[0:1, :] + jnp.sum(onehot, axis=0, keepdims=True)
    carry_sc[0:1, :] = carry
    cnt_ref[...] = jnp.broadcast_to(carry, cnt_ref.shape)
    packed = jnp.where(lane_i == 0, e1, jnp.where(lane_i == 1, e2, jnp.where(
        lane_i == 2, r1, jnp.where(lane_i == 3, r2, 0.0))))
    ri_ref[...] = packed.astype(jnp.int32)
    rf_ref[...] = jnp.where(lane_i == 0, g1, jnp.where(lane_i == 1, g2, 0.0))


def _pack_bf16_pairs(h):
    bits = lax.bitcast_convert_type(h, jnp.int32)
    rounded = bits + 0x7FFF + (lax.shift_right_logical(bits, 16) & 1)
    c = h.shape[1] // 2
    return lax.shift_right_logical(rounded[:, :c], 16) | (rounded[:, c:] & -65536)


def _unpack_bf16_pairs(words):
    lo = lax.bitcast_convert_type(lax.shift_left(words, 16), F32)
    hi = lax.bitcast_convert_type(words & -65536, F32)
    return lo.astype(BF16), hi.astype(BF16)


def moe_route(x, gain, w_router3, tm=512):
    n, d = x.shape
    return pl.pallas_call(
        _router_kernel,
        grid=(n // tm,),
        in_specs=[pl.BlockSpec((tm, d), lambda i: (i, 0)),
                  pl.BlockSpec((1, d), lambda i: (0, 0)),
                  pl.BlockSpec((2, d, LANES), lambda i: (0, 0, 0))],
        out_specs=[pl.BlockSpec((tm, d // 2), lambda i: (i, 0)),
                   pl.BlockSpec((tm, LANES), lambda i: (i, 0)),
                   pl.BlockSpec((tm, LANES), lambda i: (i, 0)),
                   pl.BlockSpec((8, LANES), lambda i: (0, 0))],
        out_shape=[jax.ShapeDtypeStruct((n, d // 2), jnp.int32),
                   jax.ShapeDtypeStruct((n, LANES), jnp.int32),
                   jax.ShapeDtypeStruct((n, LANES), F32),
                   jax.ShapeDtypeStruct((8, LANES), F32)],
        scratch_shapes=[pltpu.VMEM((8, LANES), F32)],
        compiler_params=_cparams(("arbitrary",)),
        name="moe_route",
    )(x, gain.reshape(1, d), w_router3)


def _row_copy(src_ref, src_row, dst_ref, dst_row, sem):
    return pltpu.make_async_copy(src_ref.at[pl.ds(src_row, 1)], dst_ref.at[pl.ds(dst_row, 1)], sem)


def _dispatch_kernel(dest_ref, h_ref, zeros_ref, xs_ref, sem):
    del zeros_ref
    td = h_ref.shape[0]
    base = pl.program_id(0) * td

    def start(r, carry):
        for k in range(2):
            _row_copy(h_ref, r, xs_ref, dest_ref[2 * (base + r) + k], sem).start(priority=k)
        return carry

    def wait(r, carry):
        for k in range(2):
            _row_copy(h_ref, r, xs_ref, dest_ref[2 * (base + r) + k], sem).wait()
        return carry

    lax.fori_loop(0, td, start, 0, unroll=DMA_LOOP_UNROLL)
    lax.fori_loop(0, td, wait, 0, unroll=DMA_LOOP_UNROLL)


def moe_dispatch(h, dest, cap, td=256):
    n, d = h.shape
    return pl.pallas_call(
        _dispatch_kernel,
        grid_spec=pltpu.PrefetchScalarGridSpec(
            num_scalar_prefetch=1,
            grid=(n // td,),
            in_specs=[pl.BlockSpec((td, d), lambda i, dest: (i, 0)),
                      pl.BlockSpec(memory_space=pl.ANY)],
            out_specs=pl.BlockSpec(memory_space=pl.ANY),
            scratch_shapes=[pltpu.SemaphoreType.DMA(())]),
        out_shape=jax.ShapeDtypeStruct((cap, d), h.dtype),
        input_output_aliases={2: 0},
        compiler_params=_cparams(("arbitrary",)),
        name="moe_dispatch",
    )(dest, h, jnp.zeros((cap, d), h.dtype))


def _combine_kernel(dest_ref, x_ref, gate_ref, y_ref, *rest):
    out_gain_ref = rest[0] if len(rest) == 5 else None
    o_ref, ya_sc, yb_sc, sem = rest[-4:]
    tc = x_ref.shape[0]
    base = pl.program_id(0) * tc

    def start(r, carry):
        _row_copy(y_ref, dest_ref[2 * (base + r)], ya_sc, r, sem).start(priority=0)
        _row_copy(y_ref, dest_ref[2 * (base + r) + 1], yb_sc, r, sem).start(priority=1)
        return carry

    def wait(r, carry):
        _row_copy(y_ref, dest_ref[2 * (base + r)], ya_sc, r, sem).wait()
        _row_copy(y_ref, dest_ref[2 * (base + r) + 1], yb_sc, r, sem).wait()
        return carry

    lax.fori_loop(0, tc, start, 0, unroll=DMA_LOOP_UNROLL)
    lax.fori_loop(0, tc, wait, 0, unroll=DMA_LOOP_UNROLL)
    gates = gate_ref[...]
    x_new = x_ref[...] + (gates[:, 0:1] * ya_sc[...] + gates[:, 1:2] * yb_sc[...])
    o_ref[...] = x_new if out_gain_ref is None else _rms(x_new, out_gain_ref[...])


def moe_combine(x, gates, y, dest, out_gain=None, tc=256):
    n, d = x.shape
    in_specs = [pl.BlockSpec((tc, d), lambda i, dest: (i, 0)),
                pl.BlockSpec((tc, LANES), lambda i, dest: (i, 0)),
                pl.BlockSpec(memory_space=pl.ANY)]
    args = [dest, x, gates, y]
    if out_gain is not None:
        in_specs.append(pl.BlockSpec((1, d), lambda i, dest: (0, 0)))
        args.append(out_gain.reshape(1, d).astype(F32))
    return pl.pallas_call(
        _combine_kernel,
        grid_spec=pltpu.PrefetchScalarGridSpec(
            num_scalar_prefetch=1,
            grid=(n // tc,),
            in_specs=in_specs,
            out_specs=pl.BlockSpec((tc, d), lambda i, dest: (i, 0)),
            scratch_shapes=[pltpu.VMEM((tc, d), F32), pltpu.VMEM((tc, d), F32),
                            pltpu.SemaphoreType.DMA(())]),
        out_shape=jax.ShapeDtypeStruct((n, d), F32),
        compiler_params=_cparams(("arbitrary",)),
        name="moe_combine",
    )(*args)


def moe_layer(x, gain, w_router, w_gate, w_up, w_down, first_expert, out_gain=None, tm=512):
    n, d = x.shape
    wr = jnp.pad(w_router.astype(F32), ((0, 0), (0, LANES - N_EXPERTS)))
    h, info, gates, counts = moe_route(x, gain, jnp.stack(_split3(wr)[:2]))
    half = tm // 2
    counts = counts[0, :N_EXPERTS].astype(jnp.int32)
    padded = (counts + half - 1) // half * half
    pad_ends = jnp.cumsum(padded)
    pad_starts = pad_ends - padded
    experts, ranks = info[:, 0:2], info[:, 2:4]
    dest = (pad_starts[experts] + ranks).reshape(-1).astype(jnp.int32)
    n_halves = (2 * n) // half + N_EXPERTS
    assert n_halves % 2 == 0
    items = _expert_items(pad_ends, n_halves, half, first_expert)
    xs = moe_dispatch(h, dest, n_halves * half)
    y = grouped_swiglu(xs, items, items, w_gate, w_up, w_down, tm=tm, tf=1024, tm_down=tm)
    return moe_combine(x, gates, y, dest, out_gain)


def _expert_items(pad_ends, n_halves, half, first_expert):
    n_blocks = n_halves // 2
    start = jnp.arange(n_halves, dtype=jnp.int32) * half
    used = start < pad_ends[-1]
    owner = jnp.minimum(jnp.sum(start[:, None] >= pad_ends[None, :], axis=1), N_EXPERTS - 1).astype(jnp.int32)
    e0, e1, u0, u1 = owner[0::2], owner[1::2], used[0::2], used[1::2]
    straddle = u0 & u1 & (e0 != e1)
    n_items = 1 + straddle.astype(jnp.int32)
    ends = jnp.cumsum(n_items)
    t = jnp.arange(n_blocks + N_EXPERTS, dtype=jnp.int32)
    blk = jnp.minimum(jnp.sum(ends[None, :] <= t[:, None], axis=1), n_blocks - 1).astype(jnp.int32)
    second = straddle[blk] & (t - (ends - n_items)[blk] == 1)
    whole = jnp.where(u0, jnp.where(u1, PART_FULL, PART_FIRST), PART_EMPTY)
    part = jnp.where(straddle[blk], jnp.where(second, PART_SECOND, PART_FIRST), whole[blk])
    part = jnp.where(t < ends[-1], part, PART_IDLE).astype(jnp.int32)
    expert = jnp.where(second, e1[blk], e0[blk]) + first_expert
    real = part < PART_EMPTY
    expert = jnp.where(real, expert, expert[jnp.maximum(jnp.sum(real) - 1, 0)])
    return blk, expert.astype(jnp.int32), part


def _project_weights(w):
    def cols(a, n):
        return w[:, a:a + n]

    qa, ka, va, qb = cols(0, 512), cols(512, 128), cols(640, 128), cols(768, 768)
    kbc, vbc, kbs, vbs, kbw, vbw = (cols(1536 + 128 * i, 128) for i in range(6))
    gb, qc, kc, vc, fc = cols(2304, 36), cols(2340, 768), cols(3108, 768), cols(3876, 768), cols(4644, 12)
    w_rows = jnp.concatenate([kc, ka, va, qa, kbc, vbc, kbs, kbw], axis=1).astype(BF16)
    q_fold = Q_SCALE * LOG2E
    w_t = jnp.concatenate([qb * q_fold, qc * q_fold, vc, vbs, vbw], axis=1).T.astype(BF16)
    tail = jnp.concatenate([gb, fc], axis=1)
    tail = jnp.pad(tail, ((0, 0), (0, LANES - tail.shape[1]))).astype(BF16)
    return w_rows, w_t, tail


def _chunk_blocks(cols, batch, seq):
    n_chunks = seq // CMP_STRIDE
    t = cols.reshape(batch, seq, B_KV_HEADS, HEAD_DIM).transpose(0, 2, 1, 3)
    t = t.reshape(batch, B_KV_HEADS, n_chunks, CMP_STRIDE * HEAD_DIM)
    nxt = jnp.concatenate([t[:, :, 1:], jnp.zeros_like(t[:, :, :1])], axis=2)
    return jnp.concatenate([t, nxt], axis=-1)


def _overlap_matrix_t(seq):
    n_chunks, n_slc = seq // CMP_STRIDE, seq // SEL_BLOCK
    c_start = np.arange(n_chunks) * CMP_STRIDE
    s_start = np.arange(n_slc) * SEL_BLOCK
    inter = np.maximum(np.minimum(c_start[:, None] + CMP_LEN, s_start[None, :] + SEL_BLOCK)
                       - np.maximum(c_start[:, None], s_start[None, :]), 0) / CMP_LEN
    inter[(seq - CMP_LEN) // CMP_STRIDE + 1:] = 0.0
    return jnp.asarray(inter.T, BF16)


def mixer_layer(x, batch, seq, w_in, w_out, norm_mix, mix_out_norm, sinks, cmp_pe, cmp_w1, cmp_w2, f_bias, norm_ffn):
    w_rows, w_t, w_tail = _project_weights(w_in)
    proj_r, proj_t, tail, tail_t = input_projection(x, norm_mix, w_rows, w_t, w_tail)
    o_a = swa_attention(proj_r, sinks.astype(F32), batch, seq)

    bias_row = jnp.zeros((1, LANES), F32).at[0, TAIL_FC:TAIL_FC + C_HEADS].set(f_bias.astype(F32))
    fox_ka, fox_qat = fox_augmentation(tail, bias_row, batch, seq)
    o_c = flash_attention_t("fox", proj_r, proj_t, fox_ka, fox_qat, batch, seq)

    flat = jnp.stack([_chunk_blocks(proj_r[:, OFF_KBC:OFF_KBC + KV_WIDTH], batch, seq),
                      _chunk_blocks(proj_r[:, OFF_VBC:OFF_VBC + KV_WIDTH], batch, seq)])
    pe_rows = jnp.broadcast_to(cmp_pe.reshape(2, 1, CMP_LEN * HEAD_DIM), (2, 8, CMP_LEN * HEAD_DIM)).astype(BF16)
    kvc, kvct = nsa_compress(flat, pe_rows, cmp_w1.astype(BF16), cmp_w2.astype(BF16))
    o_cmp, sel_t = nsa_compressed_attention(proj_t, kvc, kvct, _overlap_matrix_t(seq), tail_t, batch, seq)
    tk = 512
    pos_ka, pos_qat = _alibi_augmentation(seq, tk)
    o_slc = flash_attention_t("slc", proj_r, proj_t, pos_ka, pos_qat, batch, seq, gates_t=tail_t, sel_t=sel_t, tk=tk)
    o_win = flash_attention_t("win", proj_r, proj_t, pos_ka, pos_qat, batch, seq, gates_t=tail_t, tk=tk)
    return mix_out(o_a, o_cmp, o_slc, o_win, o_c, mix_out_norm, w_out.astype(BF16), x, norm_ffn)


def dense_layer(x, h, w_gate, w_up, w_down, index, tm=1024):
    n = x.shape[0]
    tm_down = 512
    return grouped_swiglu(h, _whole_block_items(n // tm, index), _whole_block_items(n // tm_down, index),
                          w_gate, w_up, w_down, residual=x, tm=tm, tm_down=tm_down)


def kernel(x, w_in, w_out, norm_mix, mix_out_norm, attn_sinks, nsa_cmp_pe, nsa_cmp_w1, nsa_cmp_w2, fox_f_bias,
           norm_ffn, ffn_w_gate, ffn_w_up, ffn_w_down, moe_router, moe_w_gate, moe_w_up, moe_w_down, norm_final):
    batch, seq, d = x.shape
    depth = w_in.shape[0]
    f = ffn_w_gate.shape[-1]
    moe_gate, moe_up = moe_w_gate.reshape(-1, d, f), moe_w_up.reshape(-1, d, f)
    moe_down = moe_w_down.reshape(-1, f, d)
    xf = x.reshape(batch * seq, d).astype(F32)
    for layer in range(depth):
        xf, hf = mixer_layer(xf, batch, seq, w_in[layer], w_out[layer], norm_mix[layer], mix_out_norm[layer],
                             attn_sinks[layer], nsa_cmp_pe[layer], nsa_cmp_w1[layer], nsa_cmp_w2[layer],
                             fox_f_bias[layer], norm_ffn[layer])
        i = layer // 2
        if layer % 2 == 0:
            xf = dense_layer(xf, hf, ffn_w_gate, ffn_w_up, ffn_w_down, i)
        else:
            out_gain = norm_final if layer == depth - 1 else None
            xf = moe_layer(xf, norm_ffn[layer], moe_router[i], moe_gate, moe_up, moe_down, i * N_EXPERTS, out_gain)
    if depth % 2 == 1:
        xf = rmsnorm_rows(xf, norm_final, F32)
    return xf.astype(x.dtype).reshape(batch, seq, d)
```

```python
import functools

import jax
import jax.numpy as jnp
import numpy as np
from jax import lax
from jax.experimental import pallas as pl
from jax.experimental.pallas import tpu as pltpu

F32 = jnp.float32
BF16 = jnp.bfloat16

HEAD_DIM = 64
A_HEADS, A_KV_HEADS = 8, 2
B_HEADS, B_KV_HEADS = 12, 2
C_HEADS = 12
B_GROUP = B_HEADS // B_KV_HEADS
A_WIDTH, B_WIDTH, C_WIDTH = A_HEADS * HEAD_DIM, B_HEADS * HEAD_DIM, C_HEADS * HEAD_DIM
KV_WIDTH = 2 * HEAD_DIM
N_GATE_COLS = B_HEADS * 3
WINDOW_A = 128
WINDOW_B = 512
CMP_LEN, CMP_STRIDE = 32, 16
SEL_BLOCK, N_SELECT = 64, 16
N_EXPERTS = 8
RMS_EPS = 1e-6
NEG_INF = -1e30
M_INIT = -1e20
FORCE_BONUS = 1e4
LANES = 128
VMEM_LIMIT = 56 * 1024 * 1024
Q_SCALE = HEAD_DIM ** -0.5

OFF_KC, OFF_KA, OFF_VA, OFF_QA = 0, 768, 896, 1024
OFF_KBC, OFF_VBC, OFF_KBS, OFF_KBW = 1536, 1664, 1792, 1920
ROW_WIDTH = 2048
OFF_QBT, OFF_QCT, OFF_VCT, OFF_VBST, OFF_VBWT = 0, 768, 1536, 2304, 2432
T_WIDTH = 2560
TAIL_FC = N_GATE_COLS
ONE_LANE = LANES - 1
N_AUG = HEAD_DIM
Q_AUG_ROWS = 16
LOG2E = 1.4426950408889634
MASK_PEN = -1e30
VMEM_LIMIT_BIG = 60 * 1024 * 1024
DMA_LOOP_UNROLL = 8


def _alibi(n):
    return [float(2.0 ** (-8.0 * i / n)) for i in range(1, n + 1)]


SLOPES_A = _alibi(A_HEADS)
SLOPES_B = _alibi(B_HEADS)


def _cparams(sem, vmem=VMEM_LIMIT):
    return pltpu.CompilerParams(dimension_semantics=sem, vmem_limit_bytes=vmem)


def _dot(a, b):
    return jnp.dot(a, b, preferred_element_type=F32)


def _dot_nt(a, b):
    return lax.dot_general(a, b, (((1,), (1,)), ((), ())), preferred_element_type=F32)


def _split3(x):
    hi = x.astype(BF16)
    r1 = x - hi.astype(F32)
    mid = r1.astype(BF16)
    lo = (r1 - mid.astype(F32)).astype(BF16)
    return hi, mid, lo


def _rms(x, gain):
    return x * lax.rsqrt(jnp.mean(x * x, axis=-1, keepdims=True) + RMS_EPS) * gain


def _sigmoid(x):
    return 1.0 / (1.0 + jnp.exp(-x))


def _norm_kernel(x_ref, g_ref, o_ref):
    o_ref[...] = _rms(x_ref[...], g_ref[...]).astype(o_ref.dtype)


def rmsnorm_rows(x, gain, out_dtype, tm=512):
    n, d = x.shape
    return pl.pallas_call(
        _norm_kernel,
        grid=(n // tm,),
        in_specs=[pl.BlockSpec((tm, d), lambda i: (i, 0)), pl.BlockSpec((1, d), lambda i: (0, 0))],
        out_specs=pl.BlockSpec((tm, d), lambda i: (i, 0)),
        out_shape=jax.ShapeDtypeStruct((n, d), out_dtype),
        compiler_params=_cparams(("parallel",)),
        name="rmsnorm",
    )(x, gain.reshape(1, d))


def _projection_kernel(x_ref, g_ref, wr_ref, wt_ref, wtail_ref, wtailt_ref, pr_ref, pt_ref, tail_ref, tailt_ref, h_sc,
                       *, n_row_tiles, n_t_tiles):
    j = pl.program_id(1)

    @pl.when(j == 0)
    def _():
        h_sc[...] = _rms(x_ref[...], g_ref[...]).astype(BF16)
        tail_ref[...] = _dot(h_sc[...], wtail_ref[...])
        tailt_ref[...] = _dot_nt(wtailt_ref[...], h_sc[...])

    @pl.when((j >= 1) & (j <= n_row_tiles))
    def _():
        pr_ref[...] = _dot(h_sc[...], wr_ref[...]).astype(pr_ref.dtype)

    @pl.when(j > n_row_tiles)
    def _():
        pt_ref[...] = _dot_nt(wt_ref[...], h_sc[...]).astype(pt_ref.dtype)


def input_projection(x, gain, w_rows, w_t, w_tail, tm=1024, tn=512):
    n, d = x.shape
    n_row_tiles, n_t_tiles = w_rows.shape[1] // tn, w_t.shape[0] // tn

    def row_tile(j):
        return jnp.clip(j - 1, 0, n_row_tiles - 1)

    def t_tile(j):
        return jnp.clip(j - 1 - n_row_tiles, 0, n_t_tiles - 1)

    return pl.pallas_call(
        functools.partial(_projection_kernel, n_row_tiles=n_row_tiles, n_t_tiles=n_t_tiles),
        grid=(n // tm, n_row_tiles + n_t_tiles + 1),
        in_specs=[pl.BlockSpec((tm, d), lambda i, j: (i, 0)),
                  pl.BlockSpec((1, d), lambda i, j: (0, 0)),
                  pl.BlockSpec((d, tn), lambda i, j: (0, row_tile(j))),
                  pl.BlockSpec((tn, d), lambda i, j: (t_tile(j), 0)),
                  pl.BlockSpec((d, LANES), lambda i, j: (0, 0)),
                  pl.BlockSpec((LANES, d), lambda i, j: (0, 0))],
        out_specs=[pl.BlockSpec((tm, tn), lambda i, j: (i, row_tile(j))),
                   pl.BlockSpec((tn, tm), lambda i, j: (t_tile(j), i)),
                   pl.BlockSpec((tm, LANES), lambda i, j: (i, 0)),
                   pl.BlockSpec((LANES, tm), lambda i, j: (0, i))],
        out_shape=[jax.ShapeDtypeStruct((n, w_rows.shape[1]), BF16),
                   jax.ShapeDtypeStruct((w_t.shape[0], n), BF16),
                   jax.ShapeDtypeStruct((n, LANES), F32),
                   jax.ShapeDtypeStruct((LANES, n), F32)],
        scratch_shapes=[pltpu.VMEM((tm, d), BF16)],
        compiler_params=_cparams(("parallel", "arbitrary")),
        name="input_projection",
    )(x, gain.reshape(1, d), w_rows, w_t, w_tail, w_tail.T)


def _swa_kernel(sink_ref, q_ref, kp_ref, kc_ref, vp_ref, vc_ref, o_ref):
    n = pl.program_id(1)
    tq = q_ref.shape[0]
    q = q_ref[...]
    k = jnp.concatenate([kp_ref[...], kc_ref[...]], axis=0)
    v = jnp.concatenate([vp_ref[...], vc_ref[...]], axis=0)
    i = lax.broadcasted_iota(jnp.int32, (tq, 2 * tq), 0)
    j = lax.broadcasted_iota(jnp.int32, (tq, 2 * tq), 1)
    rel = tq + i - j
    mask = (rel >= 0) & (rel < WINDOW_A) & ((j >= tq) | (n > 0))
    relf = rel.astype(F32)
    group = A_HEADS // A_KV_HEADS
    for h in range(A_HEADS):
        kv = h // group
        qh = q[:, h * HEAD_DIM:(h + 1) * HEAD_DIM]
        kh = k[:, kv * HEAD_DIM:(kv + 1) * HEAD_DIM]
        vh = v[:, kv * HEAD_DIM:(kv + 1) * HEAD_DIM]
        s = _dot_nt(qh, kh) * Q_SCALE - SLOPES_A[h] * relf
        s = jnp.where(mask, s, NEG_INF)
        sink = sink_ref[h]
        m = jnp.maximum(jnp.max(s, axis=-1, keepdims=True), sink)
        p = jnp.exp(s - m)
        l = jnp.sum(p, axis=-1, keepdims=True) + jnp.exp(sink - m)
        o_ref[:, h * HEAD_DIM:(h + 1) * HEAD_DIM] = _dot(p.astype(BF16), vh) / l


def swa_attention(proj, sinks, batch, seq):
    tq = WINDOW_A
    nb = seq // tq
    kcol, vcol = OFF_KA // KV_WIDTH, OFF_VA // KV_WIDTH

    def cur(col):
        return pl.BlockSpec((tq, KV_WIDTH), lambda b, n: (b * nb + n, col))

    def prev(col):
        return pl.BlockSpec((tq, KV_WIDTH), lambda b, n: (b * nb + jnp.maximum(n - 1, 0), col))

    return pl.pallas_call(
        _swa_kernel,
        grid=(batch, nb),
        in_specs=[pl.BlockSpec(memory_space=pltpu.SMEM),
                  pl.BlockSpec((tq, A_WIDTH), lambda b, n: (b * nb + n, OFF_QA // A_WIDTH)),
                  prev(kcol), cur(kcol), prev(vcol), cur(vcol)],
        out_specs=pl.BlockSpec((tq, A_WIDTH), lambda b, n: (b * nb + n, 0)),
        out_shape=jax.ShapeDtypeStruct((batch * seq, A_WIDTH), F32),
        compiler_params=_cparams(("parallel", "parallel")),
        name="swa_attention",
    )(sinks, proj, proj, proj, proj, proj)


def _fox_aug_kernel(z_ref, b_ref, pk_ref, pq_ref, ka_ref, qa_ref, carry_sc):
    @pl.when(pl.program_id(1) == 0)
    def _():
        carry_sc[...] = jnp.zeros_like(carry_sc)

    z = z_ref[...] + b_ref[...]
    log_f = -(jnp.maximum(-z, 0.0) + jnp.log1p(jnp.exp(-jnp.abs(z))))
    ts = z.shape[0]
    r = lax.broadcasted_iota(jnp.int32, (ts, ts), 0)
    c = lax.broadcasted_iota(jnp.int32, (ts, ts), 1)
    tri = (c <= r).astype(BF16)
    hi, mid, lo = _split3(log_f)
    cum = _dot(tri, hi) + _dot(tri, mid) + _dot(tri, lo) + carry_sc[0:1, :]
    carry_sc[0:1, :] = cum[ts - 1:ts, :]
    hi, mid, lo = _split3(cum * LOG2E)
    lane = lax.broadcasted_iota(jnp.int32, (ts, LANES), 1)
    hi = jnp.where(lane == ONE_LANE, 1.0, hi).astype(BF16)
    ka_ref[...] = (_dot(hi, pk_ref[0]) + _dot(mid, pk_ref[1]) + _dot(lo, pk_ref[2])).astype(BF16)
    qa_ref[...] = (_dot_nt(pq_ref[0], hi) + _dot_nt(pq_ref[1], mid) + _dot_nt(pq_ref[2], lo)).astype(BF16)


def _aug_lane(h):
    return HEAD_DIM if h % 2 == 0 else 0


def _fox_placement():
    pk = np.zeros((3, LANES, C_HEADS * LANES), np.float32)
    pq = np.zeros((3, C_HEADS * Q_AUG_ROWS, LANES), np.float32)
    for h in range(C_HEADS):
        src, kbase, qbase = TAIL_FC + h, h * LANES + _aug_lane(h), h * Q_AUG_ROWS
        for piece in range(3):
            pk[piece, src, kbase + piece] = -1.0
            pq[piece, qbase + 3 + piece, src] = 1.0
            pk[0, ONE_LANE, kbase + 3 + piece] = 1.0
            pq[0, qbase + piece, ONE_LANE] = 1.0
    return jnp.asarray(pk, BF16), jnp.asarray(pq, BF16)


def fox_augmentation(tail, bias_row, batch, seq, ts=512):
    nt = seq // ts
    pk, pq = _fox_placement()
    kw, qw = C_HEADS * LANES, C_HEADS * Q_AUG_ROWS
    return pl.pallas_call(
        _fox_aug_kernel,
        grid=(batch, nt),
        in_specs=[pl.BlockSpec((ts, LANES), lambda b, i: (b * nt + i, 0)),
                  pl.BlockSpec((1, LANES), lambda b, i: (0, 0)),
                  pl.BlockSpec((3, LANES, kw), lambda b, i: (0, 0, 0)),
                  pl.BlockSpec((3, qw, LANES), lambda b, i: (0, 0, 0))],
        out_specs=[pl.BlockSpec((ts, kw), lambda b, i: (b * nt + i, 0)),
                   pl.BlockSpec((qw, ts), lambda b, i: (0, b * nt + i))],
        out_shape=[jax.ShapeDtypeStruct((batch * seq, kw), BF16),
                   jax.ShapeDtypeStruct((qw, batch * seq), BF16)],
        scratch_shapes=[pltpu.VMEM((8, LANES), F32)],
        compiler_params=_cparams(("parallel", "arbitrary")),
        name="fox_augmentation",
    )(tail, bias_row, pk, pq)


def _alibi_augmentation(seq, tk):
    pos = np.arange(seq)
    ka = np.zeros((seq, LANES), np.float32)
    for base in (0, HEAD_DIM):
        ka[:, base:base + 3] = 1.0
        ka[:, base + 3:base + 6] = ((pos >> 8) << 8)[:, None]
        ka[:, base + 6:base + 9] = (pos & 255)[:, None]
        ka[pos, base + Q_AUG_ROWS + (pos % tk) // SEL_BLOCK] = 1.0
    def split3_np(x):
        pieces, rest = [], x.astype(np.float32)
        for _ in range(3):
            piece = rest.astype(jnp.bfloat16).astype(np.float32)
            pieces.append(piece)
            rest = rest - piece
        return pieces

    slopes = (np.asarray(SLOPES_B, np.float32) * np.float32(LOG2E)).astype(np.float32)
    st = split3_np(-(slopes[:, None] * pos.astype(np.float32)[None, :]))
    sl = split3_np(slopes)
    qa = np.zeros((B_HEADS, Q_AUG_ROWS, seq), np.float32)
    for piece in range(3):
        qa[:, piece, :] = st[piece]
        qa[:, 3 + piece, :] = sl[piece][:, None]
        qa[:, 6 + piece, :] = sl[piece][:, None]
    return jnp.asarray(ka, BF16), jnp.asarray(qa.reshape(B_HEADS * Q_AUG_ROWS, seq), BF16)


def _flash_t_kernel(*refs, mode, tq, tk, n_steps, seq):
    pen_sc = None
    if mode == "slc":
        (iq_ref, ik_ref, live_ref, k_ref, ka_ref, qt_ref, qat_ref, vt_ref, gate_ref, selt_ref, o_ref,
         m_sc, l_sc, acc_sc, pen_sc) = refs
    elif mode == "win":
        k_ref, ka_ref, qt_ref, qat_ref, vt_ref, gate_ref, o_ref, m_sc, l_sc, acc_sc = refs
    else:
        iq_ref, ik_ref, k_ref, ka_ref, qt_ref, qat_ref, vt_ref, o_ref, m_sc, l_sc, acc_sc = refs
    n_heads = qt_ref.shape[0] // HEAD_DIM
    group = 1 if mode == "fox" else B_GROUP
    blocks_per_tile = tk // SEL_BLOCK
    if mode == "win":
        iq, j = pl.program_id(1), pl.program_id(2)
        ik = iq - (n_steps - 1) + j
        first, last = j == 0, j == n_steps - 1
        active = ik >= 0
    else:
        iq, ik = iq_ref[pl.program_id(1)], ik_ref[pl.program_id(1)]
        first, last = ik == 0, ik == (iq * tq + tq - 1) // tk
        active = True

    @pl.when(first)
    def _():
        m_sc[...] = jnp.full_like(m_sc, M_INIT)
        l_sc[...] = jnp.zeros_like(l_sc)
        acc_sc[...] = jnp.zeros_like(acc_sc)
        if mode == "slc":
            pen_sc[...] = (1.0 - selt_ref[0].astype(F32)) * MASK_PEN

    def step(masked, heads, dead=None):
        hk, hq = tk // 2, tq // 2
        rows_a, rows_b, lanes_b = {None: (slice(0, tk), None, None),
                                   "late": (slice(0, hk), slice(hk, tk), slice(hq, tq)),
                                   "early": (slice(hk, tk), slice(0, hk), slice(0, hq))}[dead]

        def widen(x, fill):
            pad = jnp.full(x.shape, fill, x.dtype)
            return jnp.concatenate([pad, x] if dead == "late" else [x, pad], axis=1)

        ok = None
        if masked:
            rel = ((iq * tq + lax.broadcasted_iota(jnp.int32, (tk, tq), 1))
                   - (ik * tk + lax.broadcasted_iota(jnp.int32, (tk, tq), 0)))
            ok = rel >= 0
            if mode == "win":
                ok = ok & (rel < WINDOW_B)
        lane = lax.broadcasted_iota(jnp.int32, (tk, LANES), 1)
        ones_rows = jnp.ones((Q_AUG_ROWS, tk), BF16)
        per_kv = {}

        def kv_operands(kv):
            if kv not in per_kv:
                pair = slice((kv // 2) * LANES, (kv // 2 + 1) * LANES)
                aug = ka_ref[:, kv * LANES:(kv + 1) * LANES] if mode == "fox" else ka_ref[...]
                own = (lane < HEAD_DIM) if kv % 2 == 0 else (lane >= HEAD_DIM)
                ka = jnp.where(own, k_ref[:, pair], aug)
                v_aug = jnp.concatenate([vt_ref[kv * HEAD_DIM:(kv + 1) * HEAD_DIM, :], ones_rows], axis=0)
                pen = None
                if mode == "slc":
                    pen8 = pen_sc[kv, pl.ds(pl.multiple_of(ik * blocks_per_tile, blocks_per_tile), blocks_per_tile), :]
                    pen = jnp.concatenate(
                        [pen8, jnp.zeros((Q_AUG_ROWS - blocks_per_tile, tq), F32)], axis=0).astype(BF16)
                per_kv[kv] = (ka, v_aug, pen)
            return per_kv[kv]

        def scores(h):
            kv = h // group
            ka, _, pen = kv_operands(kv)
            aug_rows = [qat_ref[h * Q_AUG_ROWS:(h + 1) * Q_AUG_ROWS, :]]
            if mode == "slc":
                aug_rows.append(pen)
            aug_rows.append(jnp.zeros((N_AUG - Q_AUG_ROWS * len(aug_rows), tq), BF16))
            halves = [[qt_ref[h * HEAD_DIM:(h + 1) * HEAD_DIM, :]], aug_rows]
            qa = jnp.concatenate(sum(halves if kv % 2 == 0 else halves[::-1], []), axis=0)
            s_a = _dot(ka[rows_a, :], qa)
            s_b = None if dead is None else _dot(ka[rows_b, :], qa[:, lanes_b])
            if ok is not None:
                s_a = jnp.where(ok[rows_a, :], s_a, NEG_INF)
                s_b = None if dead is None else jnp.where(ok[rows_b, lanes_b], s_b, NEG_INF)
            return s_a, s_b

        def probabilities(h, s):
            s_a, s_b = s
            m_prev = m_sc[h:h + 1, :]
            m_new = jnp.maximum(m_prev, jnp.max(s_a, axis=0, keepdims=True))
            if dead is not None:
                m_new = jnp.maximum(m_new, widen(jnp.max(s_b, axis=0, keepdims=True), NEG_INF))
            m_sc[h:h + 1, :] = m_new
            p_a = jnp.exp2(s_a - m_new).astype(BF16)
            p_b = None if dead is None else jnp.exp2(s_b - m_new[:, lanes_b]).astype(BF16)
            return (p_a, p_b), jnp.exp2(m_prev - m_new)

        def accumulate(h, p, alpha):
            p_a, p_b = p
            rows = slice(h * HEAD_DIM, (h + 1) * HEAD_DIM)
            v_aug = kv_operands(h // group)[1]
            pv = _dot(v_aug[:, rows_a], p_a)
            if dead is not None:
                pv = pv + widen(_dot(v_aug[:, rows_b], p_b), 0.0)
            l_sc[h:h + 1, :] = alpha * l_sc[h:h + 1, :] + pv[HEAD_DIM:HEAD_DIM + 1, :]
            acc_sc[rows, :] = alpha * acc_sc[rows, :] + pv[0:HEAD_DIM, :]

        s_cur, pending = scores(heads[0]), None
        for i, h in enumerate(heads):
            s_next = scores(heads[i + 1]) if i + 1 < len(heads) else None
            p_alpha = probabilities(h, s_cur)
            if pending is not None:
                accumulate(*pending)
            pending = (h,) + p_alpha
            s_cur = s_next
        accumulate(*pending)

    all_heads = list(range(n_heads))
    on_diagonal = ik == iq
    if mode == "win":
        pl.when(active & jnp.logical_not(on_diagonal))(lambda: step(True, all_heads, "early"))
        pl.when(on_diagonal)(lambda: step(True, all_heads, "late"))
    elif mode == "fox":
        pl.when(on_diagonal)(lambda: step(True, all_heads, "late"))
        pl.when(jnp.logical_not(on_diagonal))(lambda: step(False, all_heads))
    else:
        tile = (pl.program_id(0) * (seq // tq) + iq) * (seq // tk) + ik
        for kv in range(B_KV_HEADS):
            heads = all_heads[kv * group:(kv + 1) * group]
            live = live_ref[tile * B_KV_HEADS + kv] > 0
            pl.when(live & on_diagonal)(functools.partial(step, True, heads, "late"))
            pl.when(live & jnp.logical_not(on_diagonal))(functools.partial(step, False, heads))

    @pl.when(last)
    def _():
        branch = {"fox": None, "slc": 1, "win": 2}[mode]
        for pair in range(n_heads // 2):
            rows = slice(pair * LANES, (pair + 1) * LANES)
            heads = (2 * pair, 2 * pair + 1)
            denom = jnp.concatenate(
                [jnp.broadcast_to(l_sc[h:h + 1, :], (HEAD_DIM, tq)) for h in heads], axis=0)
            out = acc_sc[rows, :] / denom
            if branch is not None:
                out = out * jnp.concatenate(
                    [jnp.broadcast_to(_sigmoid(gate_ref[3 * h + branch:3 * h + branch + 1, :]), (HEAD_DIM, tq))
                     for h in heads], axis=0)
            o_ref[:, rows] = out.T


def flash_attention_t(mode, proj_r, proj_t, k_aug, q_aug_t, batch, seq, gates_t=None, sel_t=None, tq=512, tk=512):
    nq, nk = seq // tq, seq // tk
    assert tq == tk
    if mode == "win":
        assert WINDOW_B == tk
        n_steps = WINDOW_B // tk + 1
        grid = (batch, nq, n_steps)
        prefetch = []

        def tiles(iq, j):
            return iq, jnp.maximum(iq - (n_steps - 1) + j, 0)
    else:
        n_steps = nk
        pairs = [(iq, ik) for iq in range(nq) for ik in range((iq * tq + tq - 1) // tk + 1)]
        grid = (batch, len(pairs))
        prefetch = [jnp.asarray([p[0] for p in pairs], jnp.int32), jnp.asarray([p[1] for p in pairs], jnp.int32)]

        def tiles(p, iq_tab, ik_tab, *_):
            return iq_tab[p], ik_tab[p]

    def kv_rows(col):
        return lambda b, *g: (b * nk + tiles(*g)[1], col)

    def kv_cols(row):
        return lambda b, *g: (row, b * nk + tiles(*g)[1])

    def q_cols(row, per_batch=True):
        return lambda b, *g: (row, (b * nq if per_batch else 0) + tiles(*g)[0])

    if mode == "fox":
        in_specs = [pl.BlockSpec((tk, C_WIDTH), kv_rows(OFF_KC // C_WIDTH)),
                    pl.BlockSpec((tk, C_HEADS * LANES), kv_rows(0)),
                    pl.BlockSpec((C_WIDTH, tq), q_cols(OFF_QCT // C_WIDTH)),
                    pl.BlockSpec((C_HEADS * Q_AUG_ROWS, tq), q_cols(0)),
                    pl.BlockSpec((C_WIDTH, tk), kv_cols(OFF_VCT // C_WIDTH))]
    else:
        kcol = (OFF_KBS if mode == "slc" else OFF_KBW) // KV_WIDTH
        vrow = (OFF_VBST if mode == "slc" else OFF_VBWT) // KV_WIDTH
        in_specs = [pl.BlockSpec((tk, KV_WIDTH), kv_rows(kcol)),
                    pl.BlockSpec((tk, LANES), lambda b, *g: (tiles(*g)[1], 0)),
                    pl.BlockSpec((B_WIDTH, tq), q_cols(OFF_QBT // B_WIDTH)),
                    pl.BlockSpec((B_HEADS * Q_AUG_ROWS, tq), q_cols(0, per_batch=False)),
                    pl.BlockSpec((KV_WIDTH, tk), kv_cols(vrow))]
    args = [proj_r, k_aug, proj_t, q_aug_t, proj_t]
    width = C_WIDTH if mode == "fox" else B_WIDTH
    n_heads = width // HEAD_DIM
    scratch = [pltpu.VMEM((16, tq), F32), pltpu.VMEM((16, tq), F32), pltpu.VMEM((n_heads * HEAD_DIM, tq), F32)]
    if mode != "fox":
        in_specs.append(pl.BlockSpec((LANES, tq), q_cols(0)))
        args.append(gates_t)
    if mode == "slc":
        n_slc = seq // SEL_BLOCK
        per_tile = tk // SEL_BLOCK
        assert per_tile <= Q_AUG_ROWS and tk % SEL_BLOCK == 0
        in_specs.append(pl.BlockSpec((1, B_KV_HEADS, n_slc, tq), lambda b, *g: (b, 0, 0, tiles(*g)[0])))
        args.append(sel_t)
        scratch.append(pltpu.VMEM((B_KV_HEADS, n_slc, tq), F32))
        live = sel_t.reshape(batch, B_KV_HEADS, nk, per_tile, nq, tq).max(axis=(3, 5)) > 0
        prefetch.append(live.transpose(0, 3, 2, 1).reshape(-1).astype(jnp.int32))
    return pl.pallas_call(
        functools.partial(_flash_t_kernel, mode=mode, tq=tq, tk=tk, n_steps=n_steps, seq=seq),
        grid_spec=pltpu.PrefetchScalarGridSpec(
            num_scalar_prefetch=len(prefetch),
            grid=grid,
            in_specs=in_specs,
            out_specs=pl.BlockSpec((tq, width), lambda b, *g: (b * nq + tiles(*g)[0], 0)),
            scratch_shapes=scratch),
        out_shape=jax.ShapeDtypeStruct((batch * seq, width), F32),
        compiler_params=_cparams(("parallel",) * (len(grid) - 1) + ("arbitrary",)),
        name=mode + "_attention",
    )(*prefetch, *args)


def _compress_kernel(t_ref, pe_ref, w1_ref, w2_ref, w2t_ref, o_ref, ot_ref):
    w1 = w1_ref[0]
    hid = _dot(t_ref[0, 0, 0], w1) + _dot(pe_ref[0], w1)[0:1, :]
    act = (hid * _sigmoid(hid)).astype(BF16)
    o_ref[0, 0, 0] = _dot(act, w2_ref[0]).astype(o_ref.dtype)
    ot_ref[0, 0, 0] = _dot_nt(w2t_ref[0], act).astype(ot_ref.dtype)


def nsa_compress(flat, pe_rows, w1, w2):
    _, batch, n_kv, n_chunks, width = flat.shape
    hidden = w1.shape[-1]
    return pl.pallas_call(
        _compress_kernel,
        grid=(2, batch, n_kv),
        in_specs=[pl.BlockSpec((1, 1, 1, n_chunks, width), lambda s, b, h: (s, b, h, 0, 0)),
                  pl.BlockSpec((1, 8, width), lambda s, b, h: (s, 0, 0)),
                  pl.BlockSpec((1, width, hidden), lambda s, b, h: (s, 0, 0)),
                  pl.BlockSpec((1, hidden, HEAD_DIM), lambda s, b, h: (s, 0, 0)),
                  pl.BlockSpec((1, HEAD_DIM, hidden), lambda s, b, h: (s, 0, 0))],
        out_specs=[pl.BlockSpec((1, 1, 1, n_chunks, HEAD_DIM), lambda s, b, h: (s, b, h, 0, 0)),
                   pl.BlockSpec((1, 1, 1, HEAD_DIM, n_chunks), lambda s, b, h: (s, b, h, 0, 0))],
        out_shape=[jax.ShapeDtypeStruct((2, batch, n_kv, n_chunks, HEAD_DIM), BF16),
                   jax.ShapeDtypeStruct((2, batch, n_kv, HEAD_DIM, n_chunks), BF16)],
        compiler_params=_cparams(("parallel", "parallel", "parallel")),
        name="nsa_compress",
    )(flat, pe_rows, w1, w2, jnp.swapaxes(w2, 1, 2))


def _cmp_attn_t_kernel(qt_ref, kc_ref, vct_ref, inter_ref, gate_ref, o_ref, selt_ref, ot_sc, *, tq):
    iq = pl.program_id(1)
    all_chunks = kc_ref.shape[3]
    few = (iq + 1) * tq <= (all_chunks // 2) * CMP_STRIDE
    pl.when(few)(lambda: _cmp_attn_body(all_chunks // 2, iq, tq, qt_ref, kc_ref, vct_ref, inter_ref, gate_ref,
                                        selt_ref, ot_sc))
    pl.when(jnp.logical_not(few))(lambda: _cmp_attn_body(all_chunks, iq, tq, qt_ref, kc_ref, vct_ref, inter_ref,
                                                         gate_ref, selt_ref, ot_sc))
    for pair in range(B_HEADS // 2):
        rows = slice(pair * LANES, (pair + 1) * LANES)
        o_ref[:, rows] = ot_sc[rows, :].T


def _cmp_attn_body(n_chunks, iq, tq, qt_ref, kc_ref, vct_ref, inter_ref, gate_ref, selt_ref, ot_sc):
    n_slc = inter_ref.shape[0]
    t = iq * tq + lax.broadcasted_iota(jnp.int32, (n_chunks, tq), 1)
    n = lax.broadcasted_iota(jnp.int32, (n_chunks, tq), 0)
    rel = t - (n * CMP_STRIDE + CMP_LEN - 1)
    mask = rel >= 0
    relf = rel.astype(F32)
    t_s = iq * tq + lax.broadcasted_iota(jnp.int32, (n_slc, tq), 1)
    jj = lax.broadcasted_iota(jnp.int32, (n_slc, tq), 0)
    cur = t_s >> (SEL_BLOCK.bit_length() - 1)
    valid = jj * SEL_BLOCK <= t_s
    forced = (jj == 0) | (jj == cur) | (jj == cur - 1)
    jf = jj.astype(F32)
    inter = inter_ref[:, 0:n_chunks]

    def raw_scores(hd):
        return _dot(kc_ref[0, 0, hd // B_GROUP, 0:n_chunks, :], qt_ref[hd * HEAD_DIM:(hd + 1) * HEAD_DIM, :])

    raw_next = raw_scores(0)
    for h in range(B_KV_HEADS):
        vct = vct_ref[0, 0, h, :, 0:n_chunks]
        p_sum = jnp.zeros((n_chunks, tq), F32)
        for g in range(B_GROUP):
            hd = h * B_GROUP + g
            rows = slice(hd * HEAD_DIM, (hd + 1) * HEAD_DIM)
            raw, raw_next = raw_next, (raw_scores(hd + 1) if hd + 1 < B_HEADS else None)
            s = jnp.where(mask, raw - (SLOPES_B[hd] * LOG2E) * relf, NEG_INF)
            m = jnp.max(s, axis=0, keepdims=True)
            e = jnp.where(mask, jnp.exp2(s - m), 0.0)
            l = jnp.sum(e, axis=0, keepdims=True)
            p = e / jnp.where(l > 0.0, l, 1.0)
            p_sum = p_sum + p
            ot_sc[rows, :] = _dot(vct, p.astype(BF16)) * _sigmoid(gate_ref[3 * hd:3 * hd + 1, :])
        hi, mid, lo = _split3(p_sum)
        importance = _dot(inter, hi) + _dot(inter, mid) + _dot(inter, lo)
        score = jnp.where(valid, importance + jnp.where(forced, FORCE_BONUS, 0.0), NEG_INF)
        chosen = jnp.zeros((n_slc, tq), F32)
        for _ in range(min(N_SELECT, n_slc)):
            best = jnp.max(score, axis=0, keepdims=True)
            first = jnp.min(jnp.where(score == best, jf, float(n_slc)), axis=0, keepdims=True)
            pick = jf == first
            chosen = jnp.where(pick, 1.0, chosen)
            score = jnp.where(pick, -jnp.inf, score)
        selt_ref[0, h] = jnp.where(valid, chosen, 0.0).astype(selt_ref.dtype)


def nsa_compressed_attention(proj_t, kvc, kvct, inter_t, gates_t, batch, seq, tq=512):
    nq = seq // tq
    n_chunks = kvc.shape[3]
    n_slc = seq // SEL_BLOCK
    return pl.pallas_call(
        functools.partial(_cmp_attn_t_kernel, tq=tq),
        grid=(batch, nq),
        in_specs=[pl.BlockSpec((B_WIDTH, tq), lambda b, i: (OFF_QBT // B_WIDTH, b * nq + i)),
                  pl.BlockSpec((1, 1, B_KV_HEADS, n_chunks, HEAD_DIM), lambda b, i: (0, b, 0, 0, 0)),
                  pl.BlockSpec((1, 1, B_KV_HEADS, HEAD_DIM, n_chunks), lambda b, i: (1, b, 0, 0, 0)),
                  pl.BlockSpec((n_slc, n_chunks), lambda b, i: (0, 0)),
                  pl.BlockSpec((LANES, tq), lambda b, i: (0, b * nq + i))],
        out_specs=[pl.BlockSpec((tq, B_WIDTH), lambda b, i: (b * nq + i, 0)),
                   pl.BlockSpec((1, B_KV_HEADS, n_slc, tq), lambda b, i: (b, 0, 0, i))],
        out_shape=[jax.ShapeDtypeStruct((batch * seq, B_WIDTH), F32),
                   jax.ShapeDtypeStruct((batch, B_KV_HEADS, n_slc, seq), BF16)],
        scratch_shapes=[pltpu.VMEM((B_WIDTH, tq), F32)],
        compiler_params=_cparams(("parallel", "parallel")),
        name="nsa_cmp_attention",
    )(proj_t, kvc, kvct, inter_t, gates_t)


def _mix_out_kernel(oa_ref, ocmp_ref, oslc_ref, owin_ref, oc_ref, g_ref, w_ref, x_ref, gn_ref, o_ref, h_ref):
    o_b = ocmp_ref[...] + oslc_ref[...] + owin_ref[...]
    b0, c0 = A_WIDTH, A_WIDTH + B_WIDTH
    mixed = jnp.concatenate([_rms(oa_ref[...], g_ref[:, 0:b0]).astype(BF16),
                             _rms(o_b, g_ref[:, b0:c0]).astype(BF16),
                             _rms(oc_ref[...], g_ref[:, c0:]).astype(BF16)], axis=1)
    x_new = x_ref[...] + _dot(mixed, w_ref[...])
    o_ref[...] = x_new
    h_ref[...] = _rms(x_new, gn_ref[...]).astype(h_ref.dtype)


def mix_out(o_a, o_cmp, o_slc, o_win, o_c, gain, w_out, x, next_gain, tm=256):
    n, d = x.shape
    width = w_out.shape[0]

    def rows(w):
        return pl.BlockSpec((tm, w), lambda i: (i, 0))

    return pl.pallas_call(
        _mix_out_kernel,
        grid=(n // tm,),
        in_specs=[rows(A_WIDTH), rows(B_WIDTH), rows(B_WIDTH), rows(B_WIDTH), rows(C_WIDTH),
                  pl.BlockSpec((1, width), lambda i: (0, 0)),
                  pl.BlockSpec((width, d), lambda i: (0, 0)),
                  rows(d),
                  pl.BlockSpec((1, d), lambda i: (0, 0))],
        out_specs=[rows(d), rows(d)],
        out_shape=[jax.ShapeDtypeStruct((n, d), F32), jax.ShapeDtypeStruct((n, d), BF16)],
        compiler_params=_cparams(("parallel",)),
        name="mix_out",
    )(o_a, o_cmp, o_slc, o_win, o_c, gain.reshape(1, width), w_out, x, next_gain.reshape(1, d))


PART_FULL, PART_FIRST, PART_SECOND, PART_EMPTY, PART_IDLE = range(5)


def _stream_weights(exp_ref, nxt_ref, wrap_ref, weights, stages, casts, sems, run_sc):
    j, t = pl.program_id(0), pl.program_id(1)
    tile = stages[0].shape[2]

    def copies(expert, sweep, slot):
        cols = pl.ds(pl.multiple_of(sweep * tile, tile), tile)
        return [pltpu.make_async_copy(w.at[expert, :, cols], st.at[slot], sems.at[slot])
                for w, st in zip(weights, stages)]

    @pl.when((j == 0) & (t == 0))
    def _():
        run_sc[0] = 0
        for c in copies(exp_ref[0], 0, 0):
            c.start()

    @pl.when((t == 0) | (exp_ref[t] != exp_ref[jnp.maximum(t - 1, 0)]))
    def _():
        slot = lax.rem(run_sc[0], 2)
        for c in copies(exp_ref[t], j, slot):
            c.wait()
        wraps = wrap_ref[t] == 1

        @pl.when(jnp.logical_not(wraps & (j == pl.num_programs(0) - 1)))
        def _():
            for c in copies(nxt_ref[t], j + wraps.astype(jnp.int32), 1 - slot):
                c.start()

        for st, dst in zip(stages, casts):
            dst[...] = st[slot].astype(BF16)
        run_sc[0] = run_sc[0] + 1


def _with_run_links(items):
    blk, expert, part = items
    n = expert.shape[0]
    t = jnp.arange(n, dtype=jnp.int32)
    starts = jnp.concatenate([jnp.ones((1,), jnp.bool_), expert[1:] != expert[:-1]])
    following = jnp.min(jnp.where(starts[None, :] & (t[None, :] > t[:, None]), t[None, :], n), axis=1)
    wrap = following == n
    nxt = jnp.where(wrap, expert[0], expert[jnp.minimum(following, n - 1)])
    return blk, expert, part, nxt.astype(jnp.int32), wrap.astype(jnp.int32)


def _for_each_part(part, o_ref, compute):
    half = o_ref.shape[0] // 2

    @pl.when(part == PART_FULL)
    def _():
        o_ref[...] = compute(slice(None))

    @pl.when(part == PART_FIRST)
    def _():
        o_ref[0:half, :] = compute(slice(0, half))
        o_ref[half:, :] = jnp.zeros((half, o_ref.shape[1]), o_ref.dtype)

    @pl.when(part == PART_SECOND)
    def _():
        o_ref[half:, :] = compute(slice(half, 2 * half))

    @pl.when(part == PART_EMPTY)
    def _():
        o_ref[...] = jnp.zeros_like(o_ref)


def _ffn_up_kernel(blk_ref, exp_ref, part_ref, nxt_ref, wrap_ref, x_ref, wg_ref, wu_ref, o_ref,
                   wg_stage, wu_stage, wg_sc, wu_sc, sems, run_sc):
    _stream_weights(exp_ref, nxt_ref, wrap_ref, (wg_ref, wu_ref), (wg_stage, wu_stage), (wg_sc, wu_sc), sems, run_sc)

    def compute(rows):
        if x_ref.dtype == jnp.int32:
            c = x_ref.shape[1]
            x_lo, x_hi = _unpack_bf16_pairs(x_ref[rows, :])
            gate = _dot(x_lo, wg_sc[0:c, :]) + _dot(x_hi, wg_sc[c:, :])
            up = _dot(x_lo, wu_sc[0:c, :]) + _dot(x_hi, wu_sc[c:, :])
        else:
            x = x_ref[rows, :]
            gate = _dot(x, wg_sc[...])
            up = _dot(x, wu_sc[...])
        return (gate * _sigmoid(gate) * up).astype(o_ref.dtype)

    _for_each_part(part_ref[pl.program_id(1)], o_ref, compute)


def _ffn_down_kernel(blk_ref, exp_ref, part_ref, nxt_ref, wrap_ref, h_ref, wd_ref, *rest):
    o_ref, wd_stage, wd_sc, sems, run_sc = rest[-5:]
    _stream_weights(exp_ref, nxt_ref, wrap_ref, (wd_ref,), (wd_stage,), (wd_sc,), sems, run_sc)

    def compute(rows):
        y = _dot(h_ref[rows, :], wd_sc[...])
        return y + rest[0][rows, :] if len(rest) == 6 else y

    _for_each_part(part_ref[pl.program_id(1)], o_ref, compute)


def _whole_block_items(n_blocks, expert):
    return (jnp.arange(n_blocks, dtype=jnp.int32), jnp.full((n_blocks,), expert, jnp.int32),
            jnp.full((n_blocks,), PART_FULL, jnp.int32))


def grouped_swiglu(xs, items_up, items_down, w_gate, w_up, w_down, residual=None, tm=512, tf=512, tm_down=512,
                   tn=512):
    rows, x_width = xs.shape
    _, d, f = w_gate.shape
    hbm = pl.BlockSpec(memory_space=pl.ANY)

    def weight_scratch(k, tile, n_weights):
        return ([pltpu.VMEM((2, k, tile), F32)] * n_weights + [pltpu.VMEM((k, tile), BF16)] * n_weights
                + [pltpu.SemaphoreType.DMA((2,)), pltpu.SMEM((1,), jnp.int32)])

    hidden = pl.pallas_call(
        _ffn_up_kernel,
        grid_spec=pltpu.PrefetchScalarGridSpec(
            num_scalar_prefetch=5,
            grid=(f // tf, items_up[0].shape[0]),
            in_specs=[pl.BlockSpec((tm, x_width), lambda j, t, blk, *_: (blk[t], 0)), hbm, hbm],
            out_specs=pl.BlockSpec((tm, tf), lambda j, t, blk, *_: (blk[t], j)),
            scratch_shapes=weight_scratch(d, tf, 2)),
        out_shape=jax.ShapeDtypeStruct((rows, f), BF16),
        compiler_params=_cparams(("arbitrary", "arbitrary"), VMEM_LIMIT_BIG),
        name="ffn_up",
    )(*_with_run_links(items_up), xs, w_gate, w_up)

    in_specs = [pl.BlockSpec((tm_down, f), lambda j, t, blk, *_: (blk[t], 0)), hbm]
    args = [hidden, w_down]
    if residual is not None:
        in_specs.append(pl.BlockSpec((tm_down, tn), lambda j, t, blk, *_: (blk[t], j)))
        args.append(residual)
    return pl.pallas_call(
        _ffn_down_kernel,
        grid_spec=pltpu.PrefetchScalarGridSpec(
            num_scalar_prefetch=5,
            grid=(d // tn, items_down[0].shape[0]),
            in_specs=in_specs,
            out_specs=pl.BlockSpec((tm_down, tn), lambda j, t, blk, *_: (blk[t], j)),
            scratch_shapes=weight_scratch(f, tn, 1)),
        out_shape=jax.ShapeDtypeStruct((rows, d), F32),
        compiler_params=_cparams(("arbitrary", "arbitrary"), VMEM_LIMIT_BIG),
        name="ffn_down",
    )(*_with_run_links(items_down), *args)


def _router_kernel(x_ref, g_ref, wr_ref, h_ref, ri_ref, rf_ref, cnt_ref, carry_sc):
    @pl.when(pl.program_id(0) == 0)
    def _():
        carry_sc[...] = jnp.zeros_like(carry_sc)

    tm = x_ref.shape[0]
    h = _rms(x_ref[...], g_ref[...])
    h_ref[...] = _pack_bf16_pairs(h)
    h1, h2, _ = _split3(h)
    w1, w2 = wr_ref[0], wr_ref[1]
    logits = _dot(h1, w1) + (_dot(h1, w2) + _dot(h2, w1))
    lane_i = lax.broadcasted_iota(jnp.int32, (tm, LANES), 1)
    lane = lane_i.astype(F32)
    logits = jnp.where(lane_i < N_EXPERTS, logits, -jnp.inf)
    v1 = jnp.max(logits, axis=-1, keepdims=True)
    e1 = jnp.min(jnp.where(logits == v1, lane, float(LANES)), axis=-1, keepdims=True)
    rest = jnp.where(lane == e1, -jnp.inf, logits)
    v2 = jnp.max(rest, axis=-1, keepdims=True)
    e2 = jnp.min(jnp.where(rest == v2, lane, float(LANES)), axis=-1, keepdims=True)
    z = jnp.exp(v2 - v1)
    g1 = 1.0 / (1.0 + z)
    g2 = z / (1.0 + z)
    chosen = (lane == e1) | (lane == e2)
    onehot = jnp.where(chosen, 1.0, 0.0)
    r = lax.broadcasted_iota(jnp.int32, (tm, tm), 0)
    c = lax.broadcasted_iota(jnp.int32, (tm, tm), 1)
    before = _dot((c < r).astype(BF16), onehot.astype(BF16)) + carry_sc[0:1, :]
    r1 = jnp.sum(jnp.where(lane == e1, before, 0.0), axis=-1, keepdims=True)
    r2 = jnp.sum(jnp.where(lane == e2, before, 0.0), axis=-1, keepdims=True)
    carry = carry_sc[0:1, :] + jnp.sum(onehot, axis=0, keepdims=True)
    carry_sc[0:1, :] = carry
    cnt_ref[...] = jnp.broadcast_to(carry, cnt_ref.shape)
    packed = jnp.where(lane_i == 0, e1, jnp.where(lane_i == 1, e2, jnp.where(
        lane_i == 2, r1, jnp.where(lane_i == 3, r2, 0.0))))
    ri_ref[...] = packed.astype(jnp.int32)
    rf_ref[...] = jnp.where(lane_i == 0, g1, jnp.where(lane_i == 1, g2, 0.0))


def _pack_bf16_pairs(h):
    bits = lax.bitcast_convert_type(h, jnp.int32)
    rounded = bits + 0x7FFF + (lax.shift_right_logical(bits, 16) & 1)
    c = h.shape[1] // 2
    return lax.shift_right_logical(rounded[:, :c], 16) | (rounded[:, c:] & -65536)


def _unpack_bf16_pairs(words):
    lo = lax.bitcast_convert_type(lax.shift_left(words, 16), F32)
    hi = lax.bitcast_convert_type(words & -65536, F32)
    return lo.astype(BF16), hi.astype(BF16)


def moe_route(x, gain, w_router3, tm=512):
    n, d = x.shape
    return pl.pallas_call(
        _router_kernel,
        grid=(n // tm,),
        in_specs=[pl.BlockSpec((tm, d), lambda i: (i, 0)),
                  pl.BlockSpec((1, d), lambda i: (0, 0)),
                  pl.BlockSpec((2, d, LANES), lambda i: (0, 0, 0))],
        out_specs=[pl.BlockSpec((tm, d // 2), lambda i: (i, 0)),
                   pl.BlockSpec((tm, LANES), lambda i: (i, 0)),
                   pl.BlockSpec((tm, LANES), lambda i: (i, 0)),
                   pl.BlockSpec((8, LANES), lambda i: (0, 0))],
        out_shape=[jax.ShapeDtypeStruct((n, d // 2), jnp.int32),
                   jax.ShapeDtypeStruct((n, LANES), jnp.int32),
                   jax.ShapeDtypeStruct((n, LANES), F32),
                   jax.ShapeDtypeStruct((8, LANES), F32)],
        scratch_shapes=[pltpu.VMEM((8, LANES), F32)],
        compiler_params=_cparams(("arbitrary",)),
        name="moe_route",
    )(x, gain.reshape(1, d), w_router3)


def _row_copy(src_ref, src_row, dst_ref, dst_row, sem):
    return pltpu.make_async_copy(src_ref.at[pl.ds(src_row, 1)], dst_ref.at[pl.ds(dst_row, 1)], sem)


def _dispatch_kernel(dest_ref, h_ref, zeros_ref, xs_ref, sem):
    del zeros_ref
    td = h_ref.shape[0]
    base = pl.program_id(0) * td

    def start(r, carry):
        for k in range(2):
            _row_copy(h_ref, r, xs_ref, dest_ref[2 * (base + r) + k], sem).start(priority=k)
        return carry

    def wait(r, carry):
        for k in range(2):
            _row_copy(h_ref, r, xs_ref, dest_ref[2 * (base + r) + k], sem).wait()
        return carry

    lax.fori_loop(0, td, start, 0, unroll=DMA_LOOP_UNROLL)
    lax.fori_loop(0, td, wait, 0, unroll=DMA_LOOP_UNROLL)


def moe_dispatch(h, dest, cap, td=256):
    n, d = h.shape
    return pl.pallas_call(
        _dispatch_kernel,
        grid_spec=pltpu.PrefetchScalarGridSpec(
            num_scalar_prefetch=1,
            grid=(n // td,),
            in_specs=[pl.BlockSpec((td, d), lambda i, dest: (i, 0)),
                      pl.BlockSpec(memory_space=pl.ANY)],
            out_specs=pl.BlockSpec(memory_space=pl.ANY),
            scratch_shapes=[pltpu.SemaphoreType.DMA(())]),
        out_shape=jax.ShapeDtypeStruct((cap, d), h.dtype),
        input_output_aliases={2: 0},
        compiler_params=_cparams(("arbitrary",)),
        name="moe_dispatch",
    )(dest, h, jnp.zeros((cap, d), h.dtype))


def _combine_kernel(dest_ref, x_ref, gate_ref, y_ref, *rest):
    out_gain_ref = rest[0] if len(rest) == 5 else None
    o_ref, ya_sc, yb_sc, sem = rest[-4:]
    tc = x_ref.shape[0]
    base = pl.program_id(0) * tc

    def start(r, carry):
        _row_copy(y_ref, dest_ref[2 * (base + r)], ya_sc, r, sem).start(priority=0)
        _row_copy(y_ref, dest_ref[2 * (base + r) + 1], yb_sc, r, sem).start(priority=1)
        return carry

    def wait(r, carry):
        _row_copy(y_ref, dest_ref[2 * (base + r)], ya_sc, r, sem).wait()
        _row_copy(y_ref, dest_ref[2 * (base + r) + 1], yb_sc, r, sem).wait()
        return carry

    lax.fori_loop(0, tc, start, 0, unroll=DMA_LOOP_UNROLL)
    lax.fori_loop(0, tc, wait, 0, unroll=DMA_LOOP_UNROLL)
    gates = gate_ref[...]
    x_new = x_ref[...] + (gates[:, 0:1] * ya_sc[...] + gates[:, 1:2] * yb_sc[...])
    o_ref[...] = x_new if out_gain_ref is None else _rms(x_new, out_gain_ref[...])


def moe_combine(x, gates, y, dest, out_gain=None, tc=256):
    n, d = x.shape
    in_specs = [pl.BlockSpec((tc, d), lambda i, dest: (i, 0)),
                pl.BlockSpec((tc, LANES), lambda i, dest: (i, 0)),
                pl.BlockSpec(memory_space=pl.ANY)]
    args = [dest, x, gates, y]
    if out_gain is not None:
        in_specs.append(pl.BlockSpec((1, d), lambda i, dest: (0, 0)))
        args.append(out_gain.reshape(1, d).astype(F32))
    return pl.pallas_call(
        _combine_kernel,
        grid_spec=pltpu.PrefetchScalarGridSpec(
            num_scalar_prefetch=1,
            grid=(n // tc,),
            in_specs=in_specs,
            out_specs=pl.BlockSpec((tc, d), lambda i, dest: (i, 0)),
            scratch_shapes=[pltpu.VMEM((tc, d), F32), pltpu.VMEM((tc, d), F32),
                            pltpu.SemaphoreType.DMA(())]),
        out_shape=jax.ShapeDtypeStruct((n, d), F32),
        compiler_params=_cparams(("arbitrary",)),
        name="moe_combine",
    )(*args)


def moe_layer(x, gain, w_router, w_gate, w_up, w_down, first_expert, out_gain=None, tm=512):
    n, d = x.shape
    wr = jnp.pad(w_router.astype(F32), ((0, 0), (0, LANES - N_EXPERTS)))
    h, info, gates, counts = moe_route(x, gain, jnp.stack(_split3(wr)[:2]))
    half = tm // 2
    counts = counts[0, :N_EXPERTS].astype(jnp.int32)
    padded = (counts + half - 1) // half * half
    pad_ends = jnp.cumsum(padded)
    pad_starts = pad_ends - padded
    experts, ranks = info[:, 0:2], info[:, 2:4]
    dest = (pad_starts[experts] + ranks).reshape(-1).astype(jnp.int32)
    n_halves = (2 * n) // half + N_EXPERTS
    assert n_halves % 2 == 0
    items = _expert_items(pad_ends, n_halves, half, first_expert)
    xs = moe_dispatch(h, dest, n_halves * half)
    y = grouped_swiglu(xs, items, items, w_gate, w_up, w_down, tm=tm, tf=1024, tm_down=tm)
    return moe_combine(x, gates, y, dest, out_gain)


def _expert_items(pad_ends, n_halves, half, first_expert):
    n_blocks = n_halves // 2
    start = jnp.arange(n_halves, dtype=jnp.int32) * half
    used = start < pad_ends[-1]
    owner = jnp.minimum(jnp.sum(start[:, None] >= pad_ends[None, :], axis=1), N_EXPERTS - 1).astype(jnp.int32)
    e0, e1, u0, u1 = owner[0::2], owner[1::2], used[0::2], used[1::2]
    straddle = u0 & u1 & (e0 != e1)
    n_items = 1 + straddle.astype(jnp.int32)
    ends = jnp.cumsum(n_items)
    t = jnp.arange(n_blocks + N_EXPERTS, dtype=jnp.int32)
    blk = jnp.minimum(jnp.sum(ends[None, :] <= t[:, None], axis=1), n_blocks - 1).astype(jnp.int32)
    second = straddle[blk] & (t - (ends - n_items)[blk] == 1)
    whole = jnp.where(u0, jnp.where(u1, PART_FULL, PART_FIRST), PART_EMPTY)
    part = jnp.where(straddle[blk], jnp.where(second, PART_SECOND, PART_FIRST), whole[blk])
    part = jnp.where(t < ends[-1], part, PART_IDLE).astype(jnp.int32)
    expert = jnp.where(second, e1[blk], e0[blk]) + first_expert
    real = part < PART_EMPTY
    expert = jnp.where(real, expert, expert[jnp.maximum(jnp.sum(real) - 1, 0)])
    return blk, expert.astype(jnp.int32), part


def _project_weights(w):
    def cols(a, n):
        return w[:, a:a + n]

    qa, ka, va, qb = cols(0, 512), cols(512, 128), cols(640, 128), cols(768, 768)
    kbc, vbc, kbs, vbs, kbw, vbw = (cols(1536 + 128 * i, 128) for i in range(6))
    gb, qc, kc, vc, fc = cols(2304, 36), cols(2340, 768), cols(3108, 768), cols(3876, 768), cols(4644, 12)
    w_rows = jnp.concatenate([kc, ka, va, qa, kbc, vbc, kbs, kbw], axis=1).astype(BF16)
    q_fold = Q_SCALE * LOG2E
    w_t = jnp.concatenate([qb * q_fold, qc * q_fold, vc, vbs, vbw], axis=1).T.astype(BF16)
    tail = jnp.concatenate([gb, fc], axis=1)
    tail = jnp.pad(tail, ((0, 0), (0, LANES - tail.shape[1]))).astype(BF16)
    return w_rows, w_t, tail


def _chunk_blocks(cols, batch, seq):
    n_chunks = seq // CMP_STRIDE
    t = cols.reshape(batch, seq, B_KV_HEADS, HEAD_DIM).transpose(0, 2, 1, 3)
    t = t.reshape(batch, B_KV_HEADS, n_chunks, CMP_STRIDE * HEAD_DIM)
    nxt = jnp.concatenate([t[:, :, 1:], jnp.zeros_like(t[:, :, :1])], axis=2)
    return jnp.concatenate([t, nxt], axis=-1)


def _overlap_matrix_t(seq):
    n_chunks, n_slc = seq // CMP_STRIDE, seq // SEL_BLOCK
    c_start = np.arange(n_chunks) * CMP_STRIDE
    s_start = np.arange(n_slc) * SEL_BLOCK
    inter = np.maximum(np.minimum(c_start[:, None] + CMP_LEN, s_start[None, :] + SEL_BLOCK)
                       - np.maximum(c_start[:, None], s_start[None, :]), 0) / CMP_LEN
    inter[(seq - CMP_LEN) // CMP_STRIDE + 1:] = 0.0
    return jnp.asarray(inter.T, BF16)


def mixer_layer(x, batch, seq, w_in, w_out, norm_mix, mix_out_norm, sinks, cmp_pe, cmp_w1, cmp_w2, f_bias, norm_ffn):
    w_rows, w_t, w_tail = _project_weights(w_in)
    proj_r, proj_t, tail, tail_t = input_projection(x, norm_mix, w_rows, w_t, w_tail)
    o_a = swa_attention(proj_r, sinks.astype(F32), batch, seq)

    bias_row = jnp.zeros((1, LANES), F32).at[0, TAIL_FC:TAIL_FC + C_HEADS].set(f_bias.astype(F32))
    fox_ka, fox_qat = fox_augmentation(tail, bias_row, batch, seq)
    o_c = flash_attention_t("fox", proj_r, proj_t, fox_ka, fox_qat, batch, seq)

    flat = jnp.stack([_chunk_blocks(proj_r[:, OFF_KBC:OFF_KBC + KV_WIDTH], batch, seq),
                      _chunk_blocks(proj_r[:, OFF_VBC:OFF_VBC + KV_WIDTH], batch, seq)])
    pe_rows = jnp.broadcast_to(cmp_pe.reshape(2, 1, CMP_LEN * HEAD_DIM), (2, 8, CMP_LEN * HEAD_DIM)).astype(BF16)
    kvc, kvct = nsa_compress(flat, pe_rows, cmp_w1.astype(BF16), cmp_w2.astype(BF16))
    o_cmp, sel_t = nsa_compressed_attention(proj_t, kvc, kvct, _overlap_matrix_t(seq), tail_t, batch, seq)
    tk = 512
    pos_ka, pos_qat = _alibi_augmentation(seq, tk)
    o_slc = flash_attention_t("slc", proj_r, proj_t, pos_ka, pos_qat, batch, seq, gates_t=tail_t, sel_t=sel_t, tk=tk)
    o_win = flash_attention_t("win", proj_r, proj_t, pos_ka, pos_qat, batch, seq, gates_t=tail_t, tk=tk)
    return mix_out(o_a, o_cmp, o_slc, o_win, o_c, mix_out_norm, w_out.astype(BF16), x, norm_ffn)


def dense_layer(x, h, w_gate, w_up, w_down, index, tm=1024):
    n = x.shape[0]
    tm_down = 512
    return grouped_swiglu(h, _whole_block_items(n // tm, index), _whole_block_items(n // tm_down, index),
                          w_gate, w_up, w_down, residual=x, tm=tm, tm_down=tm_down)


def kernel(x, w_in, w_out, norm_mix, mix_out_norm, attn_sinks, nsa_cmp_pe, nsa_cmp_w1, nsa_cmp_w2, fox_f_bias,
           norm_ffn, ffn_w_gate, ffn_w_up, ffn_w_down, moe_router, moe_w_gate, moe_w_up, moe_w_down, norm_final):
    batch, seq, d = x.shape
    depth = w_in.shape[0]
    f = ffn_w_gate.shape[-1]
    moe_gate, moe_up = moe_w_gate.reshape(-1, d, f), moe_w_up.reshape(-1, d, f)
    moe_down = moe_w_down.reshape(-1, f, d)
    xf = x.reshape(batch * seq, d).astype(F32)
    for layer in range(depth):
        xf, hf = mixer_layer(xf, batch, seq, w_in[layer], w_out[layer], norm_mix[layer], mix_out_norm[layer],
                             attn_sinks[layer], nsa_cmp_pe[layer], nsa_cmp_w1[layer], nsa_cmp_w2[layer],
                             fox_f_bias[layer], norm_ffn[layer])
        i = layer // 2
        if layer % 2 == 0:
            xf = dense_layer(xf, hf, ffn_w_gate, ffn_w_up, ffn_w_down, i)
        else:
            out_gain = norm_final if layer == depth - 1 else None
            xf = moe_layer(xf, norm_ffn[layer], moe_router[i], moe_gate, moe_up, moe_down, i * N_EXPERTS, out_gain)
    if depth % 2 == 1:
        xf = rmsnorm_rows(xf, norm_final, F32)
    return xf.astype(x.dtype).reshape(batch, seq, d)
```

```python
import functools

import jax
import jax.numpy as jnp
import numpy as np
from jax import lax
from jax.experimental import pallas as pl
from jax.experimental.pallas import tpu as pltpu

F32 = jnp.float32
BF16 = jnp.bfloat16

HEAD_DIM = 64
A_HEADS, A_KV_HEADS = 8, 2
B_HEADS, B_KV_HEADS = 12, 2
C_HEADS = 12
B_GROUP = B_HEADS // B_KV_HEADS
A_WIDTH, B_WIDTH, C_WIDTH = A_HEADS * HEAD_DIM, B_HEADS * HEAD_DIM, C_HEADS * HEAD_DIM
KV_WIDTH = 2 * HEAD_DIM
N_GATE_COLS = B_HEADS * 3
WINDOW_A = 128
WINDOW_B = 512
CMP_LEN, CMP_STRIDE = 32, 16
SEL_BLOCK, N_SELECT = 64, 16
N_EXPERTS = 8
RMS_EPS = 1e-6
NEG_INF = -1e30
M_INIT = -1e20
FORCE_BONUS = 1e4
LANES = 128
VMEM_LIMIT = 56 * 1024 * 1024
Q_SCALE = HEAD_DIM ** -0.5

OFF_KC, OFF_KA, OFF_VA, OFF_QA = 0, 768, 896, 1024
OFF_KBC, OFF_VBC, OFF_KBS, OFF_KBW = 1536, 1664, 1792, 1920
ROW_WIDTH = 2048
OFF_QBT, OFF_QCT, OFF_VCT, OFF_VBST, OFF_VBWT = 0, 768, 1536, 2304, 2432
T_WIDTH = 2560
TAIL_FC = N_GATE_COLS
ONE_LANE = LANES - 1
N_AUG = HEAD_DIM
Q_AUG_ROWS = 16
LOG2E = 1.4426950408889634
MASK_PEN = -1e30
VMEM_LIMIT_BIG = 60 * 1024 * 1024
DMA_LOOP_UNROLL = 8


def _alibi(n):
    return [float(2.0 ** (-8.0 * i / n)) for i in range(1, n + 1)]


SLOPES_A = _alibi(A_HEADS)
SLOPES_B = _alibi(B_HEADS)


def _cparams(sem, vmem=VMEM_LIMIT):
    return pltpu.CompilerParams(dimension_semantics=sem, vmem_limit_bytes=vmem)


def _dot(a, b):
    return jnp.dot(a, b, preferred_element_type=F32)


def _dot_nt(a, b):
    return lax.dot_general(a, b, (((1,), (1,)), ((), ())), preferred_element_type=F32)


def _split3(x):
    hi = x.astype(BF16)
    r1 = x - hi.astype(F32)
    mid = r1.astype(BF16)
    lo = (r1 - mid.astype(F32)).astype(BF16)
    return hi, mid, lo


def _rms(x, gain):
    return x * lax.rsqrt(jnp.mean(x * x, axis=-1, keepdims=True) + RMS_EPS) * gain


def _sigmoid(x):
    return 1.0 / (1.0 + jnp.exp(-x))


def _norm_kernel(x_ref, g_ref, o_ref):
    o_ref[...] = _rms(x_ref[...], g_ref[...]).astype(o_ref.dtype)


def rmsnorm_rows(x, gain, out_dtype, tm=512):
    n, d = x.shape
    return pl.pallas_call(
        _norm_kernel,
        grid=(n // tm,),
        in_specs=[pl.BlockSpec((tm, d), lambda i: (i, 0)), pl.BlockSpec((1, d), lambda i: (0, 0))],
        out_specs=pl.BlockSpec((tm, d), lambda i: (i, 0)),
        out_shape=jax.ShapeDtypeStruct((n, d), out_dtype),
        compiler_params=_cparams(("parallel",)),
        name="rmsnorm",
    )(x, gain.reshape(1, d))


def _projection_kernel(x_ref, g_ref, wr_ref, wt_ref, wtail_ref, wtailt_ref, pr_ref, pt_ref, tail_ref, tailt_ref, h_sc,
                       *, n_row_tiles, n_t_tiles):
    j = pl.program_id(1)

    @pl.when(j == 0)
    def _():
        h_sc[...] = _rms(x_ref[...], g_ref[...]).astype(BF16)
        tail_ref[...] = _dot(h_sc[...], wtail_ref[...])
        tailt_ref[...] = _dot_nt(wtailt_ref[...], h_sc[...])

    @pl.when((j >= 1) & (j <= n_row_tiles))
    def _():
        pr_ref[...] = _dot(h_sc[...], wr_ref[...]).astype(pr_ref.dtype)

    @pl.when(j > n_row_tiles)
    def _():
        pt_ref[...] = _dot_nt(wt_ref[...], h_sc[...]).astype(pt_ref.dtype)


def input_projection(x, gain, w_rows, w_t, w_tail, tm=1024, tn=512):
    n, d = x.shape
    n_row_tiles, n_t_tiles = w_rows.shape[1] // tn, w_t.shape[0] // tn

    def row_tile(j):
        return jnp.clip(j - 1, 0, n_row_tiles - 1)

    def t_tile(j):
        return jnp.clip(j - 1 - n_row_tiles, 0, n_t_tiles - 1)

    return pl.pallas_call(
        functools.partial(_projection_kernel, n_row_tiles=n_row_tiles, n_t_tiles=n_t_tiles),
        grid=(n // tm, n_row_tiles + n_t_tiles + 1),
        in_specs=[pl.BlockSpec((tm, d), lambda i, j: (i, 0)),
                  pl.BlockSpec((1, d), lambda i, j: (0, 0)),
                  pl.BlockSpec((d, tn), lambda i, j: (0, row_tile(j))),
                  pl.BlockSpec((tn, d), lambda i, j: (t_tile(j), 0)),
                  pl.BlockSpec((d, LANES), lambda i, j: (0, 0)),
                  pl.BlockSpec((LANES, d), lambda i, j: (0, 0))],
        out_specs=[pl.BlockSpec((tm, tn), lambda i, j: (i, row_tile(j))),
                   pl.BlockSpec((tn, tm), lambda i, j: (t_tile(j), i)),
                   pl.BlockSpec((tm, LANES), lambda i, j: (i, 0)),
                   pl.BlockSpec((LANES, tm), lambda i, j: (0, i))],
        out_shape=[jax.ShapeDtypeStruct((n, w_rows.shape[1]), BF16),
                   jax.ShapeDtypeStruct((w_t.shape[0], n), BF16),
                   jax.ShapeDtypeStruct((n, LANES), F32),
                   jax.ShapeDtypeStruct((LANES, n), F32)],
        scratch_shapes=[pltpu.VMEM((tm, d), BF16)],
        compiler_params=_cparams(("parallel", "arbitrary")),
        name="input_projection",
    )(x, gain.reshape(1, d), w_rows, w_t, w_tail, w_tail.T)


def _swa_kernel(sink_ref, q_ref, kp_ref, kc_ref, vp_ref, vc_ref, o_ref):
    n = pl.program_id(1)
    tq = q_ref.shape[0]
    q = q_ref[...]
    k = jnp.concatenate([kp_ref[...], kc_ref[...]], axis=0)
    v = jnp.concatenate([vp_ref[...], vc_ref[...]], axis=0)
    i = lax.broadcasted_iota(jnp.int32, (tq, 2 * tq), 0)
    j = lax.broadcasted_iota(jnp.int32, (tq, 2 * tq), 1)
    rel = tq + i - j
    mask = (rel >= 0) & (rel < WINDOW_A) & ((j >= tq) | (n > 0))
    relf = rel.astype(F32)
    group = A_HEADS // A_KV_HEADS
    for h in range(A_HEADS):
        kv = h // group
        qh = q[:, h * HEAD_DIM:(h + 1) * HEAD_DIM]
        kh = k[:, kv * HEAD_DIM:(kv + 1) * HEAD_DIM]
        vh = v[:, kv * HEAD_DIM:(kv + 1) * HEAD_DIM]
        s = _dot_nt(qh, kh) * Q_SCALE - SLOPES_A[h] * relf
        s = jnp.where(mask, s, NEG_INF)
        sink = sink_ref[h]
        m = jnp.maximum(jnp.max(s, axis=-1, keepdims=True), sink)
        p = jnp.exp(s - m)
        l = jnp.sum(p, axis=-1, keepdims=True) + jnp.exp(sink - m)
        o_ref[:, h * HEAD_DIM:(h + 1) * HEAD_DIM] = _dot(p.astype(BF16), vh) / l


def swa_attention(proj, sinks, batch, seq):
    tq = WINDOW_A
    nb = seq // tq
    kcol, vcol = OFF_KA // KV_WIDTH, OFF_VA // KV_WIDTH

    def cur(col):
        return pl.BlockSpec((tq, KV_WIDTH), lambda b, n: (b * nb + n, col))

    def prev(col):
        return pl.BlockSpec((tq, KV_WIDTH), lambda b, n: (b * nb + jnp.maximum(n - 1, 0), col))

    return pl.pallas_call(
        _swa_kernel,
        grid=(batch, nb),
        in_specs=[pl.BlockSpec(memory_space=pltpu.SMEM),
                  pl.BlockSpec((tq, A_WIDTH), lambda b, n: (b * nb + n, OFF_QA // A_WIDTH)),
                  prev(kcol), cur(kcol), prev(vcol), cur(vcol)],
        out_specs=pl.BlockSpec((tq, A_WIDTH), lambda b, n: (b * nb + n, 0)),
        out_shape=jax.ShapeDtypeStruct((batch * seq, A_WIDTH), F32),
        compiler_params=_cparams(("parallel", "parallel")),
        name="swa_attention",
    )(sinks, proj, proj, proj, proj, proj)


def _fox_aug_kernel(z_ref, b_ref, pk_ref, pq_ref, ka_ref, qa_ref, carry_sc):
    @pl.when(pl.program_id(1) == 0)
    def _():
        carry_sc[...] = jnp.zeros_like(carry_sc)

    z = z_ref[...] + b_ref[...]
    log_f = -(jnp.maximum(-z, 0.0) + jnp.log1p(jnp.exp(-jnp.abs(z))))
    ts = z.shape[0]
    r = lax.broadcasted_iota(jnp.int32, (ts, ts), 0)
    c = lax.broadcasted_iota(jnp.int32, (ts, ts), 1)
    tri = (c <= r).astype(BF16)
    hi, mid, lo = _split3(log_f)
    cum = _dot(tri, hi) + _dot(tri, mid) + _dot(tri, lo) + carry_sc[0:1, :]
    carry_sc[0:1, :] = cum[ts - 1:ts, :]
    hi, mid, lo = _split3(cum * LOG2E)
    lane = lax.broadcasted_iota(jnp.int32, (ts, LANES), 1)
    hi = jnp.where(lane == ONE_LANE, 1.0, hi).astype(BF16)
    ka_ref[...] = (_dot(hi, pk_ref[0]) + _dot(mid, pk_ref[1]) + _dot(lo, pk_ref[2])).astype(BF16)
    qa_ref[...] = (_dot_nt(pq_ref[0], hi) + _dot_nt(pq_ref[1], mid) + _dot_nt(pq_ref[2], lo)).astype(BF16)


def _aug_lane(h):
    return HEAD_DIM if h % 2 == 0 else 0


def _fox_placement():
    pk = np.zeros((3, LANES, C_HEADS * LANES), np.float32)
    pq = np.zeros((3, C_HEADS * Q_AUG_ROWS, LANES), np.float32)
    for h in range(C_HEADS):
        src, kbase, qbase = TAIL_FC + h, h * LANES + _aug_lane(h), h * Q_AUG_ROWS
        for piece in range(3):
            pk[piece, src, kbase + piece] = -1.0
            pq[piece, qbase + 3 + piece, src] = 1.0
            pk[0, ONE_LANE, kbase + 3 + piece] = 1.0
            pq[0, qbase + piece, ONE_LANE] = 1.0
    return jnp.asarray(pk, BF16), jnp.asarray(pq, BF16)


def fox_augmentation(tail, bias_row, batch, seq, ts=512):
    nt = seq // ts
    pk, pq = _fox_placement()
    kw, qw = C_HEADS * LANES, C_HEADS * Q_AUG_ROWS
    return pl.pallas_call(
        _fox_aug_kernel,
        grid=(batch, nt),
        in_specs=[pl.BlockSpec((ts, LANES), lambda b, i: (b * nt + i, 0)),
                  pl.BlockSpec((1, LANES), lambda b, i: (0, 0)),
                  pl.BlockSpec((3, LANES, kw), lambda b, i: (0, 0, 0)),
                  pl.BlockSpec((3, qw, LANES), lambda b, i: (0, 0, 0))],
        out_specs=[pl.BlockSpec((ts, kw), lambda b, i: (b * nt + i, 0)),
                   pl.BlockSpec((qw, ts), lambda b, i: (0, b * nt + i))],
        out_shape=[jax.ShapeDtypeStruct((batch * seq, kw), BF16),
                   jax.ShapeDtypeStruct((qw, batch * seq), BF16)],
        scratch_shapes=[pltpu.VMEM((8, LANES), F32)],
        compiler_params=_cparams(("parallel", "arbitrary")),
        name="fox_augmentation",
    )(tail, bias_row, pk, pq)


def _alibi_augmentation(seq, tk):
    pos = np.arange(seq)
    ka = np.zeros((seq, LANES), np.float32)
    for base in (0, HEAD_DIM):
        ka[:, base:base + 3] = 1.0
        ka[:, base + 3:base + 6] = ((pos >> 8) << 8)[:, None]
        ka[:, base + 6:base + 9] = (pos & 255)[:, None]
        ka[pos, base + Q_AUG_ROWS + (pos % tk) // SEL_BLOCK] = 1.0
    def split3_np(x):
        pieces, rest = [], x.astype(np.float32)
        for _ in range(3):
            piece = rest.astype(jnp.bfloat16).astype(np.float32)
            pieces.append(piece)
            rest = rest - piece
        return pieces

    slopes = (np.asarray(SLOPES_B, np.float32) * np.float32(LOG2E)).astype(np.float32)
    st = split3_np(-(slopes[:, None] * pos.astype(np.float32)[None, :]))
    sl = split3_np(slopes)
    qa = np.zeros((B_HEADS, Q_AUG_ROWS, seq), np.float32)
    for piece in range(3):
        qa[:, piece, :] = st[piece]
        qa[:, 3 + piece, :] = sl[piece][:, None]
        qa[:, 6 + piece, :] = sl[piece][:, None]
    return jnp.asarray(ka, BF16), jnp.asarray(qa.reshape(B_HEADS * Q_AUG_ROWS, seq), BF16)


def _flash_t_kernel(*refs, mode, tq, tk, n_steps, seq):
    pen_sc = None
    if mode == "slc":
        (iq_ref, ik_ref, live_ref, k_ref, ka_ref, qt_ref, qat_ref, vt_ref, gate_ref, selt_ref, o_ref,
         m_sc, l_sc, acc_sc, pen_sc) = refs
    elif mode == "win":
        k_ref, ka_ref, qt_ref, qat_ref, vt_ref, gate_ref, o_ref, m_sc, l_sc, acc_sc = refs
    else:
        iq_ref, ik_ref, k_ref, ka_ref, qt_ref, qat_ref, vt_ref, o_ref, m_sc, l_sc, acc_sc = refs
    n_heads = qt_ref.shape[0] // HEAD_DIM
    group = 1 if mode == "fox" else B_GROUP
    blocks_per_tile = tk // SEL_BLOCK
    if mode == "win":
        iq, j = pl.program_id(1), pl.program_id(2)
        ik = iq - (n_steps - 1) + j
        first, last = j == 0, j == n_steps - 1
        active = ik >= 0
    else:
        iq, ik = iq_ref[pl.program_id(1)], ik_ref[pl.program_id(1)]
        first, last = ik == 0, ik == (iq * tq + tq - 1) // tk
        active = True

    @pl.when(first)
    def _():
        m_sc[...] = jnp.full_like(m_sc, M_INIT)
        l_sc[...] = jnp.zeros_like(l_sc)
        acc_sc[...] = jnp.zeros_like(acc_sc)
        if mode == "slc":
            pen_sc[...] = (1.0 - selt_ref[0].astype(F32)) * MASK_PEN

    def step(masked, heads, dead=None):
        hk, hq = tk // 2, tq // 2
        rows_a, rows_b, lanes_b = {None: (slice(0, tk), None, None),
                                   "late": (slice(0, hk), slice(hk, tk), slice(hq, tq)),
                                   "early": (slice(hk, tk), slice(0, hk), slice(0, hq))}[dead]

        def widen(x, fill):
            pad = jnp.full(x.shape, fill, x.dtype)
            return jnp.concatenate([pad, x] if dead == "late" else [x, pad], axis=1)

        ok = None
        if masked:
            rel = ((iq * tq + lax.broadcasted_iota(jnp.int32, (tk, tq), 1))
                   - (ik * tk + lax.broadcasted_iota(jnp.int32, (tk, tq), 0)))
            ok = rel >= 0
            if mode == "win":
                ok = ok & (rel < WINDOW_B)
        lane = lax.broadcasted_iota(jnp.int32, (tk, LANES), 1)
        ones_rows = jnp.ones((Q_AUG_ROWS, tk), BF16)
        per_kv = {}

        def kv_operands(kv):
            if kv not in per_kv:
                pair = slice((kv // 2) * LANES, (kv // 2 + 1) * LANES)
                aug = ka_ref[:, kv * LANES:(kv + 1) * LANES] if mode == "fox" else ka_ref[...]
                own = (lane < HEAD_DIM) if kv % 2 == 0 else (lane >= HEAD_DIM)
                ka = jnp.where(own, k_ref[:, pair], aug)
                v_aug = jnp.concatenate([vt_ref[kv * HEAD_DIM:(kv + 1) * HEAD_DIM, :], ones_rows], axis=0)
                pen = None
                if mode == "slc":
                    pen8 = pen_sc[kv, pl.ds(pl.multiple_of(ik * blocks_per_tile, blocks_per_tile), blocks_per_tile), :]
                    pen = jnp.concatenate(
                        [pen8, jnp.zeros((Q_AUG_ROWS - blocks_per_tile, tq), F32)], axis=0).astype(BF16)
                per_kv[kv] = (ka, v_aug, pen)
            return per_kv[kv]

        def scores(h):
            kv = h // group
            ka, _, pen = kv_operands(kv)
            aug_rows = [qat_ref[h * Q_AUG_ROWS:(h + 1) * Q_AUG_ROWS, :]]
            if mode == "slc":
                aug_rows.append(pen)
            aug_rows.append(jnp.zeros((N_AUG - Q_AUG_ROWS * len(aug_rows), tq), BF16))
            halves = [[qt_ref[h * HEAD_DIM:(h + 1) * HEAD_DIM, :]], aug_rows]
            qa = jnp.concatenate(sum(halves if kv % 2 == 0 else halves[::-1], []), axis=0)
            s_a = _dot(ka[rows_a, :], qa)
            s_b = None if dead is None else _dot(ka[rows_b, :], qa[:, lanes_b])
            if ok is not None:
                s_a = jnp.where(ok[rows_a, :], s_a, NEG_INF)
                s_b = None if dead is None else jnp.where(ok[rows_b, lanes_b], s_b, NEG_INF)
            return s_a, s_b

        def probabilities(h, s):
            s_a, s_b = s
            m_prev = m_sc[h:h + 1, :]
            m_new = jnp.maximum(m_prev, jnp.max(s_a, axis=0, keepdims=True))
            if dead is not None:
                m_new = jnp.maximum(m_new, widen(jnp.max(s_b, axis=0, keepdims=True), NEG_INF))
            m_sc[h:h + 1, :] = m_new
            p_a = jnp.exp2(s_a - m_new).astype(BF16)
            p_b = None if dead is None else jnp.exp2(s_b - m_new[:, lanes_b]).astype(BF16)
            return (p_a, p_b), jnp.exp2(m_prev - m_new)

        def accumulate(h, p, alpha):
            p_a, p_b = p
            rows = slice(h * HEAD_DIM, (h + 1) * HEAD_DIM)
            v_aug = kv_operands(h // group)[1]
            pv = _dot(v_aug[:, rows_a], p_a)
            if dead is not None:
                pv = pv + widen(_dot(v_aug[:, rows_b], p_b), 0.0)
            l_sc[h:h + 1, :] = alpha * l_sc[h:h + 1, :] + pv[HEAD_DIM:HEAD_DIM + 1, :]
            acc_sc[rows, :] = alpha * acc_sc[rows, :] + pv[0:HEAD_DIM, :]

        s_cur, pending = scores(heads[0]), None
        for i, h in enumerate(heads):
            s_next = scores(heads[i + 1]) if i + 1 < len(heads) else None
            p_alpha = probabilities(h, s_cur)
            if pending is not None:
                accumulate(*pending)
            pending = (h,) + p_alpha
            s_cur = s_next
        accumulate(*pending)

    all_heads = list(range(n_heads))
    on_diagonal = ik == iq
    if mode == "win":
        pl.when(active & jnp.logical_not(on_diagonal))(lambda: step(True, all_heads, "early"))
        pl.when(on_diagonal)(lambda: step(True, all_heads, "late"))
    elif mode == "fox":
        pl.when(on_diagonal)(lambda: step(True, all_heads, "late"))
        pl.when(jnp.logical_not(on_diagonal))(lambda: step(False, all_heads))
    else:
        tile = (pl.program_id(0) * (seq // tq) + iq) * (seq // tk) + ik
        for kv in range(B_KV_HEADS):
            heads = all_heads[kv * group:(kv + 1) * group]
            live = live_ref[tile * B_KV_HEADS + kv] > 0
            pl.when(live & on_diagonal)(functools.partial(step, True, heads, "late"))
            pl.when(live & jnp.logical_not(on_diagonal))(functools.partial(step, False, heads))

    @pl.when(last)
    def _():
        branch = {"fox": None, "slc": 1, "win": 2}[mode]
        for pair in range(n_heads // 2):
            rows = slice(pair * LANES, (pair + 1) * LANES)
            heads = (2 * pair, 2 * pair + 1)
            denom = jnp.concatenate(
                [jnp.broadcast_to(l_sc[h:h + 1, :], (HEAD_DIM, tq)) for h in heads], axis=0)
            out = acc_sc[rows, :] / denom
            if branch is not None:
                out = out * jnp.concatenate(
                    [jnp.broadcast_to(_sigmoid(gate_ref[3 * h + branch:3 * h + branch + 1, :]), (HEAD_DIM, tq))
                     for h in heads], axis=0)
            o_ref[:, rows] = out.T


def flash_attention_t(mode, proj_r, proj_t, k_aug, q_aug_t, batch, seq, gates_t=None, sel_t=None, tq=512, tk=512):
    nq, nk = seq // tq, seq // tk
    assert tq == tk
    if mode == "win":
        assert WINDOW_B == tk
        n_steps = WINDOW_B // tk + 1
        grid = (batch, nq, n_steps)
        prefetch = []

        def tiles(iq, j):
            return iq, jnp.maximum(iq - (n_steps - 1) + j, 0)
    else:
        n_steps = nk
        pairs = [(iq, ik) for iq in range(nq) for ik in range((iq * tq + tq - 1) // tk + 1)]
        grid = (batch, len(pairs))
        prefetch = [jnp.asarray([p[0] for p in pairs], jnp.int32), jnp.asarray([p[1] for p in pairs], jnp.int32)]

        def tiles(p, iq_tab, ik_tab, *_):
            return iq_tab[p], ik_tab[p]

    def kv_rows(col):
        return lambda b, *g: (b * nk + tiles(*g)[1], col)

    def kv_cols(row):
        return lambda b, *g: (row, b * nk + tiles(*g)[1])

    def q_cols(row, per_batch=True):
        return lambda b, *g: (row, (b * nq if per_batch else 0) + tiles(*g)[0])

    if mode == "fox":
        in_specs = [pl.BlockSpec((tk, C_WIDTH), kv_rows(OFF_KC // C_WIDTH)),
                    pl.BlockSpec((tk, C_HEADS * LANES), kv_rows(0)),
                    pl.BlockSpec((C_WIDTH, tq), q_cols(OFF_QCT // C_WIDTH)),
                    pl.BlockSpec((C_HEADS * Q_AUG_ROWS, tq), q_cols(0)),
                    pl.BlockSpec((C_WIDTH, tk), kv_cols(OFF_VCT // C_WIDTH))]
    else:
        kcol = (OFF_KBS if mode == "slc" else OFF_KBW) // KV_WIDTH
        vrow = (OFF_VBST if mode == "slc" else OFF_VBWT) // KV_WIDTH
        in_specs = [pl.BlockSpec((tk, KV_WIDTH), kv_rows(kcol)),
                    pl.BlockSpec((tk, LANES), lambda b, *g: (tiles(*g)[1], 0)),
                    pl.BlockSpec((B_WIDTH, tq), q_cols(OFF_QBT // B_WIDTH)),
                    pl.BlockSpec((B_HEADS * Q_AUG_ROWS, tq), q_cols(0, per_batch=False)),
                    pl.BlockSpec((KV_WIDTH, tk), kv_cols(vrow))]
    args = [proj_r, k_aug, proj_t, q_aug_t, proj_t]
    width = C_WIDTH if mode == "fox" else B_WIDTH
    n_heads = width // HEAD_DIM
    scratch = [pltpu.VMEM((16, tq), F32), pltpu.VMEM((16, tq), F32), pltpu.VMEM((n_heads * HEAD_DIM, tq), F32)]
    if mode != "fox":
        in_specs.append(pl.BlockSpec((LANES, tq), q_cols(0)))
        args.append(gates_t)
    if mode == "slc":
        n_slc = seq // SEL_BLOCK
        per_tile = tk // SEL_BLOCK
        assert per_tile <= Q_AUG_ROWS and tk % SEL_BLOCK == 0
        in_specs.append(pl.BlockSpec((1, B_KV_HEADS, n_slc, tq), lambda b, *g: (b, 0, 0, tiles(*g)[0])))
        args.append(sel_t)
        scratch.append(pltpu.VMEM((B_KV_HEADS, n_slc, tq), F32))
        live = sel_t.reshape(batch, B_KV_HEADS, nk, per_tile, nq, tq).max(axis=(3, 5)) > 0
        prefetch.append(live.transpose(0, 3, 2, 1).reshape(-1).astype(jnp.int32))
    return pl.pallas_call(
        functools.partial(_flash_t_kernel, mode=mode, tq=tq, tk=tk, n_steps=n_steps, seq=seq),
        grid_spec=pltpu.PrefetchScalarGridSpec(
            num_scalar_prefetch=len(prefetch),
            grid=grid,
            in_specs=in_specs,
            out_specs=pl.BlockSpec((tq, width), lambda b, *g: (b * nq + tiles(*g)[0], 0)),
            scratch_shapes=scratch),
        out_shape=jax.ShapeDtypeStruct((batch * seq, width), F32),
        compiler_params=_cparams(("parallel",) * (len(grid) - 1) + ("arbitrary",)),
        name=mode + "_attention",
    )(*prefetch, *args)


def _compress_kernel(t_ref, pe_ref, w1_ref, w2_ref, w2t_ref, o_ref, ot_ref):
    w1 = w1_ref[0]
    hid = _dot(t_ref[0, 0, 0], w1) + _dot(pe_ref[0], w1)[0:1, :]
    act = (hid * _sigmoid(hid)).astype(BF16)
    o_ref[0, 0, 0] = _dot(act, w2_ref[0]).astype(o_ref.dtype)
    ot_ref[0, 0, 0] = _dot_nt(w2t_ref[0], act).astype(ot_ref.dtype)


def nsa_compress(flat, pe_rows, w1, w2):
    _, batch, n_kv, n_chunks, width = flat.shape
    hidden = w1.shape[-1]
    return pl.pallas_call(
        _compress_kernel,
        grid=(2, batch, n_kv),
        in_specs=[pl.BlockSpec((1, 1, 1, n_chunks, width), lambda s, b, h: (s, b, h, 0, 0)),
                  pl.BlockSpec((1, 8, width), lambda s, b, h: (s, 0, 0)),
                  pl.BlockSpec((1, width, hidden), lambda s, b, h: (s, 0, 0)),
                  pl.BlockSpec((1, hidden, HEAD_DIM), lambda s, b, h: (s, 0, 0)),
                  pl.BlockSpec((1, HEAD_DIM, hidden), lambda s, b, h: (s, 0, 0))],
        out_specs=[pl.BlockSpec((1, 1, 1, n_chunks, HEAD_DIM), lambda s, b, h: (s, b, h, 0, 0)),
                   pl.BlockSpec((1, 1, 1, HEAD_DIM, n_chunks), lambda s, b, h: (s, b, h, 0, 0))],
        out_shape=[jax.ShapeDtypeStruct((2, batch, n_kv, n_chunks, HEAD_DIM), BF16),
                   jax.ShapeDtypeStruct((2, batch, n_kv, HEAD_DIM, n_chunks), BF16)],
        compiler_params=_cparams(("parallel", "parallel", "parallel")),
        name="nsa_compress",
    )(flat, pe_rows, w1, w2, jnp.swapaxes(w2, 1, 2))


def _cmp_attn_t_kernel(qt_ref, kc_ref, vct_ref, inter_ref, gate_ref, o_ref, selt_ref, ot_sc, *, tq):
    iq = pl.program_id(1)
    all_chunks = kc_ref.shape[3]
    few = (iq + 1) * tq <= (all_chunks // 2) * CMP_STRIDE
    pl.when(few)(lambda: _cmp_attn_body(all_chunks // 2, iq, tq, qt_ref, kc_ref, vct_ref, inter_ref, gate_ref,
                                        selt_ref, ot_sc))
    pl.when(jnp.logical_not(few))(lambda: _cmp_attn_body(all_chunks, iq, tq, qt_ref, kc_ref, vct_ref, inter_ref,
                                                         gate_ref, selt_ref, ot_sc))
    for pair in range(B_HEADS // 2):
        rows = slice(pair * LANES, (pair + 1) * LANES)
        o_ref[:, rows] = ot_sc[rows, :].T


def _cmp_attn_body(n_chunks, iq, tq, qt_ref, kc_ref, vct_ref, inter_ref, gate_ref, selt_ref, ot_sc):
    n_slc = inter_ref.shape[0]
    t = iq * tq + lax.broadcasted_iota(jnp.int32, (n_chunks, tq), 1)
    n = lax.broadcasted_iota(jnp.int32, (n_chunks, tq), 0)
    rel = t - (n * CMP_STRIDE + CMP_LEN - 1)
    mask = rel >= 0
    relf = rel.astype(F32)
    t_s = iq * tq + lax.broadcasted_iota(jnp.int32, (n_slc, tq), 1)
    jj = lax.broadcasted_iota(jnp.int32, (n_slc, tq), 0)
    cur = t_s >> (SEL_BLOCK.bit_length() - 1)
    valid = jj * SEL_BLOCK <= t_s
    forced = (jj == 0) | (jj == cur) | (jj == cur - 1)
    jf = jj.astype(F32)
    inter = inter_ref[:, 0:n_chunks]

    def raw_scores(hd):
        return _dot(kc_ref[0, 0, hd // B_GROUP, 0:n_chunks, :], qt_ref[hd * HEAD_DIM:(hd + 1) * HEAD_DIM, :])

    raw_next = raw_scores(0)
    for h in range(B_KV_HEADS):
        vct = vct_ref[0, 0, h, :, 0:n_chunks]
        p_sum = jnp.zeros((n_chunks, tq), F32)
        for g in range(B_GROUP):
            hd = h * B_GROUP + g
            rows = slice(hd * HEAD_DIM, (hd + 1) * HEAD_DIM)
            raw, raw_next = raw_next, (raw_scores(hd + 1) if hd + 1 < B_HEADS else None)
            s = jnp.where(mask, raw - (SLOPES_B[hd] * LOG2E) * relf, NEG_INF)
            m = jnp.max(s, axis=0, keepdims=True)
            e = jnp.where(mask, jnp.exp2(s - m), 0.0)
            l = jnp.sum(e, axis=0, keepdims=True)
            p = e / jnp.where(l > 0.0, l, 1.0)
            p_sum = p_sum + p
            ot_sc[rows, :] = _dot(vct, p.astype(BF16)) * _sigmoid(gate_ref[3 * hd:3 * hd + 1, :])
        hi, mid, lo = _split3(p_sum)
        importance = _dot(inter, hi) + _dot(inter, mid) + _dot(inter, lo)
        score = jnp.where(valid, importance + jnp.where(forced, FORCE_BONUS, 0.0), NEG_INF)
        chosen = jnp.zeros((n_slc, tq), F32)
        for _ in range(min(N_SELECT, n_slc)):
            best = jnp.max(score, axis=0, keepdims=True)
            first = jnp.min(jnp.where(score == best, jf, float(n_slc)), axis=0, keepdims=True)
            pick = jf == first
            chosen = jnp.where(pick, 1.0, chosen)
            score = jnp.where(pick, -jnp.inf, score)
        selt_ref[0, h] = jnp.where(valid, chosen, 0.0).astype(selt_ref.dtype)


def nsa_compressed_attention(proj_t, kvc, kvct, inter_t, gates_t, batch, seq, tq=512):
    nq = seq // tq
    n_chunks = kvc.shape[3]
    n_slc = seq // SEL_BLOCK
    return pl.pallas_call(
        functools.partial(_cmp_attn_t_kernel, tq=tq),
        grid=(batch, nq),
        in_specs=[pl.BlockSpec((B_WIDTH, tq), lambda b, i: (OFF_QBT // B_WIDTH, b * nq + i)),
                  pl.BlockSpec((1, 1, B_KV_HEADS, n_chunks, HEAD_DIM), lambda b, i: (0, b, 0, 0, 0)),
                  pl.BlockSpec((1, 1, B_KV_HEADS, HEAD_DIM, n_chunks), lambda b, i: (1, b, 0, 0, 0)),
                  pl.BlockSpec((n_slc, n_chunks), lambda b, i: (0, 0)),
                  pl.BlockSpec((LANES, tq), lambda b, i: (0, b * nq + i))],
        out_specs=[pl.BlockSpec((tq, B_WIDTH), lambda b, i: (b * nq + i, 0)),
                   pl.BlockSpec((1, B_KV_HEADS, n_slc, tq), lambda b, i: (b, 0, 0, i))],
        out_shape=[jax.ShapeDtypeStruct((batch * seq, B_WIDTH), F32),
                   jax.ShapeDtypeStruct((batch, B_KV_HEADS, n_slc, seq), BF16)],
        scratch_shapes=[pltpu.VMEM((B_WIDTH, tq), F32)],
        compiler_params=_cparams(("parallel", "parallel")),
        name="nsa_cmp_attention",
    )(proj_t, kvc, kvct, inter_t, gates_t)


def _mix_out_kernel(oa_ref, ocmp_ref, oslc_ref, owin_ref, oc_ref, g_ref, w_ref, x_ref, gn_ref, o_ref, h_ref):
    o_b = ocmp_ref[...] + oslc_ref[...] + owin_ref[...]
    b0, c0 = A_WIDTH, A_WIDTH + B_WIDTH
    mixed = jnp.concatenate([_rms(oa_ref[...], g_ref[:, 0:b0]).astype(BF16),
                             _rms(o_b, g_ref[:, b0:c0]).astype(BF16),
                             _rms(oc_ref[...], g_ref[:, c0:]).astype(BF16)], axis=1)
    x_new = x_ref[...] + _dot(mixed, w_ref[...])
    o_ref[...] = x_new
    h_ref[...] = _rms(x_new, gn_ref[...]).astype(h_ref.dtype)


def mix_out(o_a, o_cmp, o_slc, o_win, o_c, gain, w_out, x, next_gain, tm=256):
    n, d = x.shape
    width = w_out.shape[0]

    def rows(w):
        return pl.BlockSpec((tm, w), lambda i: (i, 0))

    return pl.pallas_call(
        _mix_out_kernel,
        grid=(n // tm,),
        in_specs=[rows(A_WIDTH), rows(B_WIDTH), rows(B_WIDTH), rows(B_WIDTH), rows(C_WIDTH),
                  pl.BlockSpec((1, width), lambda i: (0, 0)),
                  pl.BlockSpec((width, d), lambda i: (0, 0)),
                  rows(d),
                  pl.BlockSpec((1, d), lambda i: (0, 0))],
        out_specs=[rows(d), rows(d)],
        out_shape=[jax.ShapeDtypeStruct((n, d), F32), jax.ShapeDtypeStruct((n, d), BF16)],
        compiler_params=_cparams(("parallel",)),
        name="mix_out",
    )(o_a, o_cmp, o_slc, o_win, o_c, gain.reshape(1, width), w_out, x, next_gain.reshape(1, d))


PART_FULL, PART_FIRST, PART_SECOND, PART_EMPTY, PART_IDLE = range(5)


def _stream_weights(exp_ref, nxt_ref, wrap_ref, weights, stages, casts, sems, run_sc):
    j, t = pl.program_id(0), pl.program_id(1)
    tile = stages[0].shape[2]

    def copies(expert, sweep, slot):
        cols = pl.ds(pl.multiple_of(sweep * tile, tile), tile)
        return [pltpu.make_async_copy(w.at[expert, :, cols], st.at[slot], sems.at[slot])
                for w, st in zip(weights, stages)]

    @pl.when((j == 0) & (t == 0))
    def _():
        run_sc[0] = 0
        for c in copies(exp_ref[0], 0, 0):
            c.start()

    @pl.when((t == 0) | (exp_ref[t] != exp_ref[jnp.maximum(t - 1, 0)]))
    def _():
        slot = lax.rem(run_sc[0], 2)
        for c in copies(exp_ref[t], j, slot):
            c.wait()
        wraps = wrap_ref[t] == 1

        @pl.when(jnp.logical_not(wraps & (j == pl.num_programs(0) - 1)))
        def _():
            for c in copies(nxt_ref[t], j + wraps.astype(jnp.int32), 1 - slot):
                c.start()

        for st, dst in zip(stages, casts):
            dst[...] = st[slot].astype(BF16)
        run_sc[0] = run_sc[0] + 1


def _with_run_links(items):
    blk, expert, part = items
    n = expert.shape[0]
    t = jnp.arange(n, dtype=jnp.int32)
    starts = jnp.concatenate([jnp.ones((1,), jnp.bool_), expert[1:] != expert[:-1]])
    following = jnp.min(jnp.where(starts[None, :] & (t[None, :] > t[:, None]), t[None, :], n), axis=1)
    wrap = following == n
    nxt = jnp.where(wrap, expert[0], expert[jnp.minimum(following, n - 1)])
    return blk, expert, part, nxt.astype(jnp.int32), wrap.astype(jnp.int32)


def _for_each_part(part, o_ref, compute):
    half = o_ref.shape[0] // 2

    @pl.when(part == PART_FULL)
    def _():
        o_ref[...] = compute(slice(None))

    @pl.when(part == PART_FIRST)
    def _():
        o_ref[0:half, :] = compute(slice(0, half))
        o_ref[half:, :] = jnp.zeros((half, o_ref.shape[1]), o_ref.dtype)

    @pl.when(part == PART_SECOND)
    def _():
        o_ref[half:, :] = compute(slice(half, 2 * half))

    @pl.when(part == PART_EMPTY)
    def _():
        o_ref[...] = jnp.zeros_like(o_ref)


def _ffn_up_kernel(blk_ref, exp_ref, part_ref, nxt_ref, wrap_ref, x_ref, wg_ref, wu_ref, o_ref,
                   wg_stage, wu_stage, wg_sc, wu_sc, sems, run_sc):
    _stream_weights(exp_ref, nxt_ref, wrap_ref, (wg_ref, wu_ref), (wg_stage, wu_stage), (wg_sc, wu_sc), sems, run_sc)

    def compute(rows):
        if x_ref.dtype == jnp.int32:
            c = x_ref.shape[1]
            x_lo, x_hi = _unpack_bf16_pairs(x_ref[rows, :])
            gate = _dot(x_lo, wg_sc[0:c, :]) + _dot(x_hi, wg_sc[c:, :])
            up = _dot(x_lo, wu_sc[0:c, :]) + _dot(x_hi, wu_sc[c:, :])
        else:
            x = x_ref[rows, :]
            gate = _dot(x, wg_sc[...])
            up = _dot(x, wu_sc[...])
        return (gate * _sigmoid(gate) * up).astype(o_ref.dtype)

    _for_each_part(part_ref[pl.program_id(1)], o_ref, compute)


def _ffn_down_kernel(blk_ref, exp_ref, part_ref, nxt_ref, wrap_ref, h_ref, wd_ref, *rest):
    o_ref, wd_stage, wd_sc, sems, run_sc = rest[-5:]
    _stream_weights(exp_ref, nxt_ref, wrap_ref, (wd_ref,), (wd_stage,), (wd_sc,), sems, run_sc)

    def compute(rows):
        y = _dot(h_ref[rows, :], wd_sc[...])
        return y + rest[0][rows, :] if len(rest) == 6 else y

    _for_each_part(part_ref[pl.program_id(1)], o_ref, compute)


def _whole_block_items(n_blocks, expert):
    return (jnp.arange(n_blocks, dtype=jnp.int32), jnp.full((n_blocks,), expert, jnp.int32),
            jnp.full((n_blocks,), PART_FULL, jnp.int32))


def grouped_swiglu(xs, items_up, items_down, w_gate, w_up, w_down, residual=None, tm=512, tf=512, tm_down=512,
                   tn=512):
    rows, x_width = xs.shape
    _, d, f = w_gate.shape
    hbm = pl.BlockSpec(memory_space=pl.ANY)

    def weight_scratch(k, tile, n_weights):
        return ([pltpu.VMEM((2, k, tile), F32)] * n_weights + [pltpu.VMEM((k, tile), BF16)] * n_weights
                + [pltpu.SemaphoreType.DMA((2,)), pltpu.SMEM((1,), jnp.int32)])

    hidden = pl.pallas_call(
        _ffn_up_kernel,
        grid_spec=pltpu.PrefetchScalarGridSpec(
            num_scalar_prefetch=5,
            grid=(f // tf, items_up[0].shape[0]),
            in_specs=[pl.BlockSpec((tm, x_width), lambda j, t, blk, *_: (blk[t], 0)), hbm, hbm],
            out_specs=pl.BlockSpec((tm, tf), lambda j, t, blk, *_: (blk[t], j)),
            scratch_shapes=weight_scratch(d, tf, 2)),
        out_shape=jax.ShapeDtypeStruct((rows, f), BF16),
        compiler_params=_cparams(("arbitrary", "arbitrary"), VMEM_LIMIT_BIG),
        name="ffn_up",
    )(*_with_run_links(items_up), xs, w_gate, w_up)

    in_specs = [pl.BlockSpec((tm_down, f), lambda j, t, blk, *_: (blk[t], 0)), hbm]
    args = [hidden, w_down]
    if residual is not None:
        in_specs.append(pl.BlockSpec((tm_down, tn), lambda j, t, blk, *_: (blk[t], j)))
        args.append(residual)
    return pl.pallas_call(
        _ffn_down_kernel,
        grid_spec=pltpu.PrefetchScalarGridSpec(
            num_scalar_prefetch=5,
            grid=(d // tn, items_down[0].shape[0]),
            in_specs=in_specs,
            out_specs=pl.BlockSpec((tm_down, tn), lambda j, t, blk, *_: (blk[t], j)),
            scratch_shapes=weight_scratch(f, tn, 1)),
        out_shape=jax.ShapeDtypeStruct((rows, d), F32),
        compiler_params=_cparams(("arbitrary", "arbitrary"), VMEM_LIMIT_BIG),
        name="ffn_down",
    )(*_with_run_links(items_down), *args)


def _router_kernel(x_ref, g_ref, wr_ref, h_ref, ri_ref, rf_ref, cnt_ref, carry_sc):
    @pl.when(pl.program_id(0) == 0)
    def _():
        carry_sc[...] = jnp.zeros_like(carry_sc)

    tm = x_ref.shape[0]
    h = _rms(x_ref[...], g_ref[...])
    h_ref[...] = _pack_bf16_pairs(h)
    h1, h2, _ = _split3(h)
    w1, w2 = wr_ref[0], wr_ref[1]
    logits = _dot(h1, w1) + (_dot(h1, w2) + _dot(h2, w1))
    lane_i = lax.broadcasted_iota(jnp.int32, (tm, LANES), 1)
    lane = lane_i.astype(F32)
    logits = jnp.where(lane_i < N_EXPERTS, logits, -jnp.inf)
    v1 = jnp.max(logits, axis=-1, keepdims=True)
    e1 = jnp.min(jnp.where(logits == v1, lane, float(LANES)), axis=-1, keepdims=True)
    rest = jnp.where(lane == e1, -jnp.inf, logits)
    v2 = jnp.max(rest, axis=-1, keepdims=True)
    e2 = jnp.min(jnp.where(rest == v2, lane, float(LANES)), axis=-1, keepdims=True)
    z = jnp.exp(v2 - v1)
    g1 = 1.0 / (1.0 + z)
    g2 = z / (1.0 + z)
    chosen = (lane == e1) | (lane == e2)
    onehot = jnp.where(chosen, 1.0, 0.0)
    r = lax.broadcasted_iota(jnp.int32, (tm, tm), 0)
    c = lax.broadcasted_iota(jnp.int32, (tm, tm), 1)
    before = _dot((c < r).astype(BF16), onehot.astype(BF16)) + carry_sc[0:1, :]
    r1 = jnp.sum(jnp.where(lane == e1, before, 0.0), axis=-1, keepdims=True)
    r2 = jnp.sum(jnp.where(lane == e2, before, 0.0), axis=-1, keepdims=True)
    carry = carry_sc[0:1, :] + jnp.sum(onehot, axis=0, keepdims=True)
    carry_sc[0:1, :] = carry
    cnt_ref[...] = jnp.broadcast_to(carry, cnt_ref.shape)
    packed = jnp.where(lane_i == 0, e1, jnp.where(lane_i == 1, e2, jnp.where(
        lane_i == 2, r1, jnp.where(lane_i == 3, r2, 0.0))))
    ri_ref[...] = packed.astype(jnp.int32)
    rf_ref[...] = jnp.where(lane_i == 0, g1, jnp.where(lane_i == 1, g2, 0.0))


def _pack_bf16_pairs(h):
    bits = lax.bitcast_convert_type(h, jnp.int32)
    rounded = bits + 0x7FFF + (lax.shift_right_logical(bits, 16) & 1)
    c = h.shape[1] // 2
    return lax.shift_right_logical(rounded[:, :c], 16) | (rounded[:, c:] & -65536)


def _unpack_bf16_pairs(words):
    lo = lax.bitcast_convert_type(lax.shift_left(words, 16), F32)
    hi = lax.bitcast_convert_type(words & -65536, F32)
    return lo.astype(BF16), hi.astype(BF16)


def moe_route(x, gain, w_router3, tm=512):
    n, d = x.shape
    return pl.pallas_call(
        _router_kernel,
        grid=(n // tm,),
        in_specs=[pl.BlockSpec((tm, d), lambda i: (i, 0)),
                  pl.BlockSpec((1, d), lambda i: (0, 0)),
                  pl.BlockSpec((2, d, LANES), lambda i: (0, 0, 0))],
        out_specs=[pl.BlockSpec((tm, d // 2), lambda i: (i, 0)),
                   pl.BlockSpec((tm, LANES), lambda i: (i, 0)),
                   pl.BlockSpec((tm, LANES), lambda i: (i, 0)),
                   pl.BlockSpec((8, LANES), lambda i: (0, 0))],
        out_shape=[jax.ShapeDtypeStruct((n, d // 2), jnp.int32),
                   jax.ShapeDtypeStruct((n, LANES), jnp.int32),
                   jax.ShapeDtypeStruct((n, LANES), F32),
                   jax.ShapeDtypeStruct((8, LANES), F32)],
        scratch_shapes=[pltpu.VMEM((8, LANES), F32)],
        compiler_params=_cparams(("arbitrary",)),
        name="moe_route",
    )(x, gain.reshape(1, d), w_router3)


def _row_copy(src_ref, src_row, dst_ref, dst_row, sem):
    return pltpu.make_async_copy(src_ref.at[pl.ds(src_row, 1)], dst_ref.at[pl.ds(dst_row, 1)], sem)


def _dispatch_kernel(dest_ref, h_ref, zeros_ref, xs_ref, sem):
    del zeros_ref
    td = h_ref.shape[0]
    base = pl.program_id(0) * td

    def start(r, carry):
        for k in range(2):
            _row_copy(h_ref, r, xs_ref, dest_ref[2 * (base + r) + k], sem).start(priority=k)
        return carry

    def wait(r, carry):
        for k in range(2):
            _row_copy(h_ref, r, xs_ref, dest_ref[2 * (base + r) + k], sem).wait()
        return carry

    lax.fori_loop(0, td, start, 0, unroll=DMA_LOOP_UNROLL)
    lax.fori_loop(0, td, wait, 0, unroll=DMA_LOOP_UNROLL)


def moe_dispatch(h, dest, cap, td=256):
    n, d = h.shape
    return pl.pallas_call(
        _dispatch_kernel,
        grid_spec=pltpu.PrefetchScalarGridSpec(
            num_scalar_prefetch=1,
            grid=(n // td,),
            in_specs=[pl.BlockSpec((td, d), lambda i, dest: (i, 0)),
                      pl.BlockSpec(memory_space=pl.ANY)],
            out_specs=pl.BlockSpec(memory_space=pl.ANY),
            scratch_shapes=[pltpu.SemaphoreType.DMA(())]),
        out_shape=jax.ShapeDtypeStruct((cap, d), h.dtype),
        input_output_aliases={2: 0},
        compiler_params=_cparams(("arbitrary",)),
        name="moe_dispatch",
    )(dest, h, jnp.zeros((cap, d), h.dtype))


def _combine_kernel(dest_ref, x_ref, gate_ref, y_ref, *rest):
    out_gain_ref = rest[0] if len(rest) == 5 else None
    o_ref, ya_sc, yb_sc, sem = rest[-4:]
    tc = x_ref.shape[0]
    base = pl.program_id(0) * tc

    def start(r, carry):
        _row_copy(y_ref, dest_ref[2 * (base + r)], ya_sc, r, sem).start(priority=0)
        _row_copy(y_ref, dest_ref[2 * (base + r) + 1], yb_sc, r, sem).start(priority=1)
        return carry

    def wait(r, carry):
        _row_copy(y_ref, dest_ref[2 * (base + r)], ya_sc, r, sem).wait()
        _row_copy(y_ref, dest_ref[2 * (base + r) + 1], yb_sc, r, sem).wait()
        return carry

    lax.fori_loop(0, tc, start, 0, unroll=DMA_LOOP_UNROLL)
    lax.fori_loop(0, tc, wait, 0, unroll=DMA_LOOP_UNROLL)
    gates = gate_ref[...]
    x_new = x_ref[...] + (gates[:, 0:1] * ya_sc[...] + gates[:, 1:2] * yb_sc[...])
    o_ref[...] = x_new if out_gain_ref is None else _rms(x_new, out_gain_ref[...])


def moe_combine(x, gates, y, dest, out_gain=None, tc=512):
    n, d = x.shape
    in_specs = [pl.BlockSpec((tc, d), lambda i, dest: (i, 0)),
                pl.BlockSpec((tc, LANES), lambda i, dest: (i, 0)),
                pl.BlockSpec(memory_space=pl.ANY)]
    args = [dest, x, gates, y]
    if out_gain is not None:
        in_specs.append(pl.BlockSpec((1, d), lambda i, dest: (0, 0)))
        args.append(out_gain.reshape(1, d).astype(F32))
    return pl.pallas_call(
        _combine_kernel,
        grid_spec=pltpu.PrefetchScalarGridSpec(
            num_scalar_prefetch=1,
            grid=(n // tc,),
            in_specs=in_specs,
            out_specs=pl.BlockSpec((tc, d), lambda i, dest: (i, 0)),
            scratch_shapes=[pltpu.VMEM((tc, d), F32), pltpu.VMEM((tc, d), F32),
                            pltpu.SemaphoreType.DMA(())]),
        out_shape=jax.ShapeDtypeStruct((n, d), F32),
        compiler_params=_cparams(("arbitrary",)),
        name="moe_combine",
    )(*args)


def moe_layer(x, gain, w_router, w_gate, w_up, w_down, first_expert, out_gain=None, tm=512):
    n, d = x.shape
    wr = jnp.pad(w_router.astype(F32), ((0, 0), (0, LANES - N_EXPERTS)))
    h, info, gates, counts = moe_route(x, gain, jnp.stack(_split3(wr)[:2]))
    half = tm // 2
    counts = counts[0, :N_EXPERTS].astype(jnp.int32)
    padded = (counts + half - 1) // half * half
    pad_ends = jnp.cumsum(padded)
    pad_starts = pad_ends - padded
    experts, ranks = info[:, 0:2], info[:, 2:4]
    dest = (pad_starts[experts] + ranks).reshape(-1).astype(jnp.int32)
    n_halves = (2 * n) // half + N_EXPERTS
    assert n_halves % 2 == 0
    items = _expert_items(pad_ends, n_halves, half, first_expert)
    xs = moe_dispatch(h, dest, n_halves * half)
    y = grouped_swiglu(xs, items, items, w_gate, w_up, w_down, tm=tm, tf=1024, tm_down=tm)
    return moe_combine(x, gates, y, dest, out_gain)


def _expert_items(pad_ends, n_halves, half, first_expert):
    n_blocks = n_halves // 2
    start = jnp.arange(n_halves, dtype=jnp.int32) * half
    used = start < pad_ends[-1]
    owner = jnp.minimum(jnp.sum(start[:, None] >= pad_ends[None, :], axis=1), N_EXPERTS - 1).astype(jnp.int32)
    e0, e1, u0, u1 = owner[0::2], owner[1::2], used[0::2], used[1::2]
    straddle = u0 & u1 & (e0 != e1)
    n_items = 1 + straddle.astype(jnp.int32)
    ends = jnp.cumsum(n_items)
    t = jnp.arange(n_blocks + N_EXPERTS, dtype=jnp.int32)
    blk = jnp.minimum(jnp.sum(ends[None, :] <= t[:, None], axis=1), n_blocks - 1).astype(jnp.int32)
    second = straddle[blk] & (t - (ends - n_items)[blk] == 1)
    whole = jnp.where(u0, jnp.where(u1, PART_FULL, PART_FIRST), PART_EMPTY)
    part = jnp.where(straddle[blk], jnp.where(second, PART_SECOND, PART_FIRST), whole[blk])
    part = jnp.where(t < ends[-1], part, PART_IDLE).astype(jnp.int32)
    expert = jnp.where(second, e1[blk], e0[blk]) + first_expert
    real = part < PART_EMPTY
    expert = jnp.where(real, expert, expert[jnp.maximum(jnp.sum(real) - 1, 0)])
    return blk, expert.astype(jnp.int32), part


def _project_weights(w):
    def cols(a, n):
        return w[:, a:a + n]

    qa, ka, va, qb = cols(0, 512), cols(512, 128), cols(640, 128), cols(768, 768)
    kbc, vbc, kbs, vbs, kbw, vbw = (cols(1536 + 128 * i, 128) for i in range(6))
    gb, qc, kc, vc, fc = cols(2304, 36), cols(2340, 768), cols(3108, 768), cols(3876, 768), cols(4644, 12)
    w_rows = jnp.concatenate([kc, ka, va, qa, kbc, vbc, kbs, kbw], axis=1).astype(BF16)
    q_fold = Q_SCALE * LOG2E
    w_t = jnp.concatenate([qb * q_fold, qc * q_fold, vc, vbs, vbw], axis=1).T.astype(BF16)
    tail = jnp.concatenate([gb, fc], axis=1)
    tail = jnp.pad(tail, ((0, 0), (0, LANES - tail.shape[1]))).astype(BF16)
    return w_rows, w_t, tail


def _chunk_blocks(cols, batch, seq):
    n_chunks = seq // CMP_STRIDE
    t = cols.reshape(batch, seq, B_KV_HEADS, HEAD_DIM).transpose(0, 2, 1, 3)
    t = t.reshape(batch, B_KV_HEADS, n_chunks, CMP_STRIDE * HEAD_DIM)
    nxt = jnp.concatenate([t[:, :, 1:], jnp.zeros_like(t[:, :, :1])], axis=2)
    return jnp.concatenate([t, nxt], axis=-1)


def _overlap_matrix_t(seq):
    n_chunks, n_slc = seq // CMP_STRIDE, seq // SEL_BLOCK
    c_start = np.arange(n_chunks) * CMP_STRIDE
    s_start = np.arange(n_slc) * SEL_BLOCK
    inter = np.maximum(np.minimum(c_start[:, None] + CMP_LEN, s_start[None, :] + SEL_BLOCK)
                       - np.maximum(c_start[:, None], s_start[None, :]), 0) / CMP_LEN
    inter[(seq - CMP_LEN) // CMP_STRIDE + 1:] = 0.0
    return jnp.asarray(inter.T, BF16)


def mixer_layer(x, batch, seq, w_in, w_out, norm_mix, mix_out_norm, sinks, cmp_pe, cmp_w1, cmp_w2, f_bias, norm_ffn):
    w_rows, w_t, w_tail = _project_weights(w_in)
    proj_r, proj_t, tail, tail_t = input_projection(x, norm_mix, w_rows, w_t, w_tail)
    o_a = swa_attention(proj_r, sinks.astype(F32), batch, seq)

    bias_row = jnp.zeros((1, LANES), F32).at[0, TAIL_FC:TAIL_FC + C_HEADS].set(f_bias.astype(F32))
    fox_ka, fox_qat = fox_augmentation(tail, bias_row, batch, seq)
    o_c = flash_attention_t("fox", proj_r, proj_t, fox_ka, fox_qat, batch, seq)

    flat = jnp.stack([_chunk_blocks(proj_r[:, OFF_KBC:OFF_KBC + KV_WIDTH], batch, seq),
                      _chunk_blocks(proj_r[:, OFF_VBC:OFF_VBC + KV_WIDTH], batch, seq)])
    pe_rows = jnp.broadcast_to(cmp_pe.reshape(2, 1, CMP_LEN * HEAD_DIM), (2, 8, CMP_LEN * HEAD_DIM)).astype(BF16)
    kvc, kvct = nsa_compress(flat, pe_rows, cmp_w1.astype(BF16), cmp_w2.astype(BF16))
    o_cmp, sel_t = nsa_compressed_attention(proj_t, kvc, kvct, _overlap_matrix_t(seq), tail_t, batch, seq)
    tk = 512
    pos_ka, pos_qat = _alibi_augmentation(seq, tk)
    o_slc = flash_attention_t("slc", proj_r, proj_t, pos_ka, pos_qat, batch, seq, gates_t=tail_t, sel_t=sel_t, tk=tk)
    o_win = flash_attention_t("win", proj_r, proj_t, pos_ka, pos_qat, batch, seq, gates_t=tail_t, tk=tk)
    return mix_out(o_a, o_cmp, o_slc, o_win, o_c, mix_out_norm, w_out.astype(BF16), x, norm_ffn)


def dense_layer(x, h, w_gate, w_up, w_down, index, tm=1024):
    n = x.shape[0]
    tm_down = 512
    return grouped_swiglu(h, _whole_block_items(n // tm, index), _whole_block_items(n // tm_down, index),
                          w_gate, w_up, w_down, residual=x, tm=tm, tm_down=tm_down)


def kernel(x, w_in, w_out, norm_mix, mix_out_norm, attn_sinks, nsa_cmp_pe, nsa_cmp_w1, nsa_cmp_w2, fox_f_bias,
           norm_ffn, ffn_w_gate, ffn_w_up, ffn_w_down, moe_router, moe_w_gate, moe_w_up, moe_w_down, norm_final):
    batch, seq, d = x.shape
    depth = w_in.shape[0]
    f = ffn_w_gate.shape[-1]
    moe_gate, moe_up = moe_w_gate.reshape(-1, d, f), moe_w_up.reshape(-1, d, f)
    moe_down = moe_w_down.reshape(-1, f, d)
    xf = x.reshape(batch * seq, d).astype(F32)
    for layer in range(depth):
        xf, hf = mixer_layer(xf, batch, seq, w_in[layer], w_out[layer], norm_mix[layer], mix_out_norm[layer],
                             attn_sinks[layer], nsa_cmp_pe[layer], nsa_cmp_w1[layer], nsa_cmp_w2[layer],
                             fox_f_bias[layer], norm_ffn[layer])
        i = layer // 2
        if layer % 2 == 0:
            xf = dense_layer(xf, hf, ffn_w_gate, ffn_w_up, ffn_w_down, i)
        else:
            out_gain = norm_final if layer == depth - 1 else None
            xf = moe_layer(xf, norm_ffn[layer], moe_router[i], moe_gate, moe_up, moe_down, i * N_EXPERTS, out_gain)
    if depth % 2 == 1:
        xf = rmsnorm_rows(xf, norm_final, F32)
    return xf.astype(x.dtype).reshape(batch, seq, d)
```
